```python
import math
import jax
import jax.numpy as jnp
from jax import lax
import numpy as np

D_MODEL = 2048
BATCH = 4
SEQ = 8192
DEPTH = 2
DEC_BATCH = 8
DEC_SEQ = 32
PAST_LEN = 4096

CHUNK = 64
MIX_WIDTH = D_MODEL
N_GROUPS = 4
GROUP_WIDTH = MIX_WIDTH // N_GROUPS
HEAD_DIM = 128
N_HEADS = GROUP_WIDTH // HEAD_DIM
CONV_W = 4
IDX_HEADS = 16
IDX_DIM = 64
TOPK_MAX = 256
N_MEM = 256
Q_BLOCK = 128
ROPE_THETA = 10000.0
EPS = 1e-6
SPLIT_SIZES = (GROUP_WIDTH,) * 4 + (N_HEADS, N_HEADS) + (GROUP_WIDTH,) * 8 + (IDX_HEADS * IDX_DIM, IDX_DIM, IDX_HEADS) + (GROUP_WIDTH,) * 2
IN_COLS = 14 * GROUP_WIDTH + 2 * N_HEADS + IDX_HEADS * IDX_DIM + IDX_DIM + IDX_HEADS

kernel_name = 'hybrid_streaming_encoder_step'


def _split_points():
    return [int(p) for p in np.cumsum(SPLIT_SIZES)[:-1]]


def rmsnorm(x, g):
    xf = x.astype(jnp.float32)
    y = xf * lax.rsqrt(jnp.mean(xf * xf, axis=-1, keepdims=True) + EPS)
    return (y * g.astype(jnp.float32)).astype(x.dtype)


def head_layernorm(x, g, b):
    xf = x.astype(jnp.float32)
    mu = jnp.mean(xf, axis=-1, keepdims=True)
    xc = xf - mu
    var = jnp.mean(xc * xc, axis=-1, keepdims=True)
    return xc * lax.rsqrt(var + EPS) * g.astype(jnp.float32) + b.astype(jnp.float32)


def l2norm(x):
    xf = x.astype(jnp.float32)
    return xf * lax.rsqrt(jnp.sum(xf * xf, axis=-1, keepdims=True) + EPS)


def heads(t):
    return t.reshape(t.shape[:2] + (N_HEADS, HEAD_DIM))


def rotary(x, pos):
    half = x.shape[-1] // 2
    inv = ROPE_THETA ** (-jnp.arange(half, dtype=jnp.float32) / half)
    ang = pos.astype(jnp.float32)[:, None] * inv[None, :]
    cos = jnp.cos(ang)[None, :, None, :]
    sin = jnp.sin(ang)[None, :, None, :]
    xf = x.astype(jnp.float32)
    x1, x2 = xf[..., :half], xf[..., half:]
    return jnp.concatenate([x1 * cos - x2 * sin, x1 * sin + x2 * cos], axis=-1)


def causal_conv(x, buf, w):
    l = x.shape[1]
    xp = jnp.concatenate([buf.astype(x.dtype), x], axis=1)
    y = xp[:, 0:l] * w[0]
    for j in range(1, CONV_W):
        y = y + xp[:, j:j + l] * w[j]
    return jax.nn.silu(y), xp[:, -(CONV_W - 1):]


def _to_chunks(t, c):
    b, l, h = t.shape[:3]
    t = t.astype(jnp.float32).reshape((b, l // c, c, h) + t.shape[3:])
    return jnp.moveaxis(t, (1, 3), (0, 2))


def _from_chunks(o):
    n, b, h, c, d = o.shape
    return jnp.moveaxis(o, (0, 2), (1, 3)).reshape(b, n * c, h, d)


def gated_delta_rule(q, k, v, g, beta, s0):
    l, dk = q.shape[1], q.shape[-1]
    c = min(CHUNK, l)
    qc = _to_chunks(q, c) * dk ** -0.5
    kc = _to_chunks(k, c)
    vc = _to_chunks(v, c)
    bc = _to_chunks(beta, c)
    gcum = jnp.cumsum(_to_chunks(g, c), axis=-1)
    causal = jnp.tril(jnp.ones((c, c), dtype=bool))
    strict = jnp.tril(jnp.ones((c, c), dtype=bool), -1)
    diff = jnp.where(causal, gcum[..., :, None] - gcum[..., None, :], 0.0)
    decay = jnp.where(causal, jnp.exp(diff), 0.0)
    kb = kc * bc[..., None]
    a_mat = jnp.where(strict, jnp.einsum('nbhid,nbhjd->nbhij', kb, kc) * decay, 0.0)
    eye = jnp.eye(c, dtype=jnp.float32)
    t_mat = lax.linalg.triangular_solve(eye + a_mat, jnp.broadcast_to(eye, a_mat.shape),
                                        left_side=True, lower=True, unit_diagonal=True)
    u = jnp.einsum('nbhij,nbhje->nbhie', t_mat, vc * bc[..., None])
    w = jnp.einsum('nbhij,nbhjd->nbhid', t_mat, kb * jnp.exp(gcum)[..., None])
    qk = jnp.where(causal, jnp.einsum('nbhid,nbhjd->nbhij', qc, kc) * decay, 0.0)

    def step(s, inp):
        q_i, k_i, u_i, w_i, qk_i, g_i = inp
        v_new = u_i - jnp.einsum('bhid,bhde->bhie', w_i, s)
        o = (jnp.einsum('bhid,bhde->bhie', q_i * jnp.exp(g_i)[..., None], s)
             + jnp.einsum('bhij,bhje->bhie', qk_i, v_new))
        g_last = g_i[..., -1:]
        s = (s * jnp.exp(g_last)[..., None]
             + jnp.einsum('bhid,bhie->bhde', k_i * jnp.exp(g_last - g_i)[..., None], v_new))
        return s, o

    s, o = lax.scan(step, s0.astype(jnp.float32), (qc, kc, u, w, qk, gcum))
    return _from_chunks(o), s


def retention(q, k, v, log_gamma, s0):
    l, dk = q.shape[1], q.shape[-1]
    c = min(CHUNK, l)
    qc = _to_chunks(q, c)
    kc = _to_chunks(k, c) * dk ** -0.5
    vc = _to_chunks(v, c)
    idx = jnp.arange(c, dtype=jnp.float32)
    lg = log_gamma[:, None]
    causal = jnp.tril(jnp.ones((c, c), dtype=bool))
    rel = jnp.where(causal, idx[:, None] - idx[None, :], 0.0)
    d_mat = jnp.where(causal, jnp.exp(lg[..., None] * rel), 0.0)
    cross_decay = jnp.exp(lg * (idx + 1.0))[:, :, None]
    state_decay = jnp.exp(lg * (c - 1.0 - idx))[:, :, None]
    chunk_decay = jnp.exp(lg * c)[:, :, None]
    o_intra = jnp.einsum('nbhqk,nbhke->nbhqe', jnp.einsum('nbhqd,nbhkd->nbhqk', qc, kc) * d_mat, vc)

    def step(s, inp):
        q_i, k_i, v_i = inp
        o_cross = jnp.einsum('bhqd,bhde->bhqe', q_i, s) * cross_decay
        s = s * chunk_decay + jnp.einsum('bhkd,bhke->bhde', k_i * state_decay, v_i)
        return s, o_cross

    s, o_cross = lax.scan(step, s0.astype(jnp.float32), (qc, kc, vc))
    return _from_chunks(o_intra + o_cross), s


def dsa_attend(q, qi, wi, q_pos, k, v, ki, k_pos):
    b, l = q.shape[:2]
    qb = min(Q_BLOCK, l)
    nb = l // qb
    topk = min(TOPK_MAX, k.shape[1] // 4)
    gather = jax.vmap(lambda t, i: t[i])
    ki32 = ki.astype(jnp.float32)
    k_chunk = k_pos // CHUNK

    def blocks(t):
        return jnp.moveaxis(t.reshape((b, nb, qb) + t.shape[2:]), 1, 0)

    def one_block(args):
        q_b, qi_b, wi_b, pos_b = args
        score = jnp.einsum('bthi,bsi->bths', qi_b.astype(jnp.float32), ki32) * IDX_DIM ** -0.5
        index = jnp.einsum('bth,bths->bts', wi_b.astype(jnp.float32), jax.nn.relu(score))
        admissible = k_chunk[None, :] <= (pos_b // CHUNK)[:, None]
        index = jnp.where(admissible[None], index, -jnp.inf)
        _, sel = lax.top_k(index, topk)
        k_sel = gather(k, sel).astype(jnp.float32)
        v_sel = gather(v, sel).astype(jnp.float32)
        valid = k_chunk[sel] <= (pos_b // CHUNK)[None, :, None]
        logits = jnp.einsum('bthd,btkhd->bhtk', q_b.astype(jnp.float32), k_sel) * HEAD_DIM ** -0.5
        logits = jnp.where(valid[:, None], logits, -jnp.inf)
        p = jax.nn.softmax(logits, axis=-1)
        return jnp.einsum('bhtk,btkhd->bthd', p, v_sel)

    o = lax.map(one_block, (blocks(q), blocks(qi), blocks(wi), q_pos.reshape(nb, qb)))
    return jnp.moveaxis(o, 0, 1).reshape(q.shape)


def mem_attend(q, mk, mv):
    logits = jnp.einsum('bthd,bmhd->bhtm', q.astype(jnp.float32), mk.astype(jnp.float32)) * HEAD_DIM ** -0.5
    p = jax.nn.softmax(logits, axis=-1)
    return jnp.einsum('bhtm,bmhd->bthd', p, mv.astype(jnp.float32))


def memory_kv(mem, mem_norm_g, w_mem_kv, mem_k_norm_g):
    m = rmsnorm(mem, mem_norm_g)
    mk, mv = jnp.split(jnp.einsum('bmd,dc->bmc', m, w_mem_kv), 2, axis=-1)
    return rmsnorm(heads(mk), mem_k_norm_g), heads(mv)


def mixer_layer(x, conv_buf, s_gdn, s_ret, past_k, past_v, past_ki, mem_k, mem_v,
                norm_g, w_in, conv_w, a_log, dt_bias, gdn_norm_g, ret_norm_g, ret_norm_b,
                dsa_q_norm_g, dsa_k_norm_g, idx_k_norm_g, mem_q_norm_g, w_out):
    b, l, _ = x.shape
    offset = 0 if past_k is None else past_k.shape[1]
    pos = offset + jnp.arange(l, dtype=jnp.int32)
    h = rmsnorm(x, norm_g)
    proj = jnp.einsum('bld,dc->blc', h, w_in)
    (qa, ka, va, za, ba, aa, qb, kb, vb, zb, qc, kc, vc, zc,
     qi, ki, wi, qd, zd) = jnp.split(proj, _split_points(), axis=-1)

    qkv, conv_new = causal_conv(jnp.concatenate([qa, ka, va], axis=-1), conv_buf, conv_w)
    qa, ka, va = jnp.split(qkv, 3, axis=-1)
    beta = jax.nn.sigmoid(ba.astype(jnp.float32))
    g = -jnp.exp(a_log.astype(jnp.float32)) * jax.nn.softplus(aa.astype(jnp.float32) + dt_bias.astype(jnp.float32))
    o_a, s_gdn_new = gated_delta_rule(l2norm(heads(qa)), l2norm(heads(ka)), heads(va), g, beta, s_gdn)
    o_a = rmsnorm(o_a, gdn_norm_g) * jax.nn.silu(heads(za).astype(jnp.float32))

    log_gamma = jnp.log(1.0 - 2.0 ** (-5.0 - jnp.arange(N_HEADS, dtype=jnp.float32)))
    o_b, s_ret_new = retention(rotary(heads(qb), pos), rotary(heads(kb), pos), heads(vb), log_gamma, s_ret)
    o_b = head_layernorm(o_b, ret_norm_g, ret_norm_b) * jax.nn.silu(heads(zb).astype(jnp.float32))

    qc = rmsnorm(heads(qc), dsa_q_norm_g)
    kc = rmsnorm(heads(kc), dsa_k_norm_g)
    vc = heads(vc)
    qi = qi.reshape(b, l, IDX_HEADS, IDX_DIM)
    ki = rmsnorm(ki, idx_k_norm_g)
    wi = wi * IDX_HEADS ** -0.5
    if past_k is None:
        k_all, v_all, ki_all = kc, vc, ki
    else:
        k_all = jnp.concatenate([past_k.astype(kc.dtype), kc], axis=1)
        v_all = jnp.concatenate([past_v.astype(vc.dtype), vc], axis=1)
        ki_all = jnp.concatenate([past_ki.astype(ki.dtype), ki], axis=1)
    k_pos = jnp.arange(k_all.shape[1], dtype=jnp.int32)
    o_c = dsa_attend(qc, qi, wi, pos, k_all, v_all, ki_all, k_pos) * jax.nn.silu(heads(zc).astype(jnp.float32))

    qd = rmsnorm(heads(qd), mem_q_norm_g)
    o_d = mem_attend(qd, mem_k, mem_v) * jax.nn.silu(heads(zd).astype(jnp.float32))

    mix = jnp.concatenate([o.reshape(b, l, GROUP_WIDTH) for o in (o_a, o_b, o_c, o_d)], axis=-1).astype(x.dtype)
    y = x + jnp.einsum('blc,cd->bld', mix, w_out).astype(x.dtype)
    return y, (conv_new, s_gdn_new.astype(x.dtype), s_ret_new.astype(x.dtype), kc, vc, ki)


def setup_inputs(seed: int = 0) -> dict:
    key = jax.random.key(seed)
    ks = jax.random.split(key, 32)
    f32 = jnp.float32

    def nrm(k, shape, s=1.0):
        return jax.random.normal(k, shape, f32) * s

    def gain(k, n):
        return 1.0 + 0.02 * jax.random.normal(k, (DEPTH, n), f32)

    dt = jnp.exp(jax.random.uniform(ks[24], (DEPTH, N_HEADS), f32, math.log(1e-3), math.log(1e-1)))
    return {
        'x_prompt': nrm(ks[0], (BATCH, SEQ, D_MODEL)),
        'x_sample': nrm(ks[1], (DEC_BATCH, DEC_SEQ, D_MODEL)),
        'cache_gdn_conv': nrm(ks[2], (DEPTH, DEC_BATCH, CONV_W - 1, 3 * GROUP_WIDTH)),
        'state_gdn': nrm(ks[3], (DEPTH, DEC_BATCH, N_HEADS, HEAD_DIM, HEAD_DIM), 0.1),
        'state_ret': nrm(ks[4], (DEPTH, DEC_BATCH, N_HEADS, HEAD_DIM, HEAD_DIM), 0.1),
        'cache_dsa_k': nrm(ks[5], (DEPTH, DEC_BATCH, PAST_LEN, N_HEADS, HEAD_DIM)),
        'cache_dsa_v': nrm(ks[6], (DEPTH, DEC_BATCH, PAST_LEN, N_HEADS, HEAD_DIM)),
        'cache_idx_k': nrm(ks[7], (DEPTH, DEC_BATCH, PAST_LEN, IDX_DIM)),
        'cache_mem_k': nrm(ks[8], (DEPTH, DEC_BATCH, N_MEM, N_HEADS, HEAD_DIM)),
        'cache_mem_v': nrm(ks[9], (DEPTH, DEC_BATCH, N_MEM, N_HEADS, HEAD_DIM)),
        'mem_prompt': nrm(ks[10], (BATCH, N_MEM, D_MODEL)),
        'norm_g': gain(ks[11], D_MODEL),
        'w_in': nrm(ks[12], (DEPTH, D_MODEL, IN_COLS), D_MODEL ** -0.5),
        'gdn_conv_w': nrm(ks[13], (DEPTH, CONV_W, 3 * GROUP_WIDTH), CONV_W ** -0.5),
        'gdn_a_log': jnp.log(jax.random.uniform(ks[14], (DEPTH, N_HEADS), f32, 1.0, 16.0)),
        'gdn_dt_bias': dt + jnp.log(-jnp.expm1(-dt)),
        'gdn_norm_g': gain(ks[15], HEAD_DIM),
        'ret_norm_g': gain(ks[16], HEAD_DIM),
        'ret_norm_b': nrm(ks[17], (DEPTH, HEAD_DIM), 0.02),
        'dsa_q_norm_g': gain(ks[18], HEAD_DIM),
        'dsa_k_norm_g': gain(ks[19], HEAD_DIM),
        'idx_k_norm_g': gain(ks[20], IDX_DIM),
        'mem_norm_g': gain(ks[21], D_MODEL),
        'w_mem_kv': nrm(ks[22], (DEPTH, D_MODEL, 2 * GROUP_WIDTH), D_MODEL ** -0.5),
        'mem_q_norm_g': gain(ks[23], HEAD_DIM),
        'mem_k_norm_g': gain(ks[25], HEAD_DIM),
        'w_out': nrm(ks[26], (DEPTH, MIX_WIDTH, D_MODEL), MIX_WIDTH ** -0.5),
    }


def reference(x_prompt, x_sample, cache_gdn_conv, state_gdn, state_ret, cache_dsa_k, cache_dsa_v,
              cache_idx_k, cache_mem_k, cache_mem_v, mem_prompt, norm_g, w_in, gdn_conv_w, gdn_a_log,
              gdn_dt_bias, gdn_norm_g, ret_norm_g, ret_norm_b, dsa_q_norm_g, dsa_k_norm_g, idx_k_norm_g,
              mem_norm_g, w_mem_kv, mem_q_norm_g, mem_k_norm_g, w_out):
    b = x_prompt.shape[0]
    y_prompt, y_sample = x_prompt, x_sample
    st_p, st_s, mem_p = [], [], []
    for l in range(DEPTH):
        weights = (norm_g[l], w_in[l], gdn_conv_w[l], gdn_a_log[l], gdn_dt_bias[l], gdn_norm_g[l],
                   ret_norm_g[l], ret_norm_b[l], dsa_q_norm_g[l], dsa_k_norm_g[l], idx_k_norm_g[l],
                   mem_q_norm_g[l], w_out[l])
        mk, mv = memory_kv(mem_prompt, mem_norm_g[l], w_mem_kv[l], mem_k_norm_g[l])
        conv0 = jnp.zeros((b, CONV_W - 1, 3 * GROUP_WIDTH), x_prompt.dtype)
        s0 = jnp.zeros((b, N_HEADS, HEAD_DIM, HEAD_DIM), jnp.float32)
        y_prompt, sp = mixer_layer(y_prompt, conv0, s0, s0, None, None, None, mk, mv, *weights)
        st_p.append(sp)
        mem_p.append((mk, mv))
        y_sample, ss = mixer_layer(y_sample, cache_gdn_conv[l], state_gdn[l], state_ret[l],
                                   cache_dsa_k[l], cache_dsa_v[l], cache_idx_k[l],
                                   cache_mem_k[l], cache_mem_v[l], *weights)
        st_s.append(ss)
    new_conv_p = jnp.stack([s[0] for s in st_p])
    new_gdn_p = jnp.stack([s[1] for s in st_p])
    new_ret_p = jnp.stack([s[2] for s in st_p])
    new_dsa_k_p = jnp.stack([s[3] for s in st_p])
    new_dsa_v_p = jnp.stack([s[4] for s in st_p])
    new_idx_k_p = jnp.stack([s[5] for s in st_p])
    new_mem_k_p = jnp.stack([m[0] for m in mem_p])
    new_mem_v_p = jnp.stack([m[1] for m in mem_p])
    new_conv_s = jnp.stack([s[0] for s in st_s])
    new_gdn_s = jnp.stack([s[1] for s in st_s])
    new_ret_s = jnp.stack([s[2] for s in st_s])
    new_dsa_k_s = jnp.stack([s[3] for s in st_s])
    new_dsa_v_s = jnp.stack([s[4] for s in st_s])
    new_idx_k_s = jnp.stack([s[5] for s in st_s])
    return (y_prompt, y_sample, new_conv_p, new_gdn_p, new_ret_p, new_dsa_k_p, new_dsa_v_p, new_idx_k_p,
            new_mem_k_p, new_mem_v_p, new_conv_s, new_gdn_s, new_ret_s, new_dsa_k_s, new_dsa_v_s, new_idx_k_s)
```

```python
import functools
import math

import jax
import jax.numpy as jnp
from jax import lax
from jax.experimental import pallas as pl
from jax.experimental.pallas import tpu as pltpu

F32 = jnp.float32
BF16 = jnp.bfloat16
I32 = jnp.int32

HEAD_DIM = 128
N_HEADS = 4
GROUP_WIDTH = N_HEADS * HEAD_DIM
CHUNK = 64
CONV_W = 4
IDX_HEADS = 16
IDX_DIM = 64
TOPK_MAX = 256
ROPE_THETA = 10000.0
EPS = 1e-6

COL_QA, COL_KA, COL_VA = 0, 512, 1024
COL_QB, COL_KB, COL_VB = 1536, 2048, 2560
COL_ZA, COL_ZB = 3072, 3584
COL_QC, COL_KC, COL_VC, COL_ZC = 4096, 4608, 5120, 5632
COL_QI = 6144
COL_QD, COL_ZD = 7168, 7680
MAIN_COLS = 8192
LANE_KI, LANE_WI, LANE_BETA, LANE_ALPHA = 0, 64, 80, 84
SMALL_COLS = 128

INT_MIN = -2 ** 31
NEG_BIG = -1e30
VMEM_LIMIT = 56 * 1024 * 1024
HI = lax.Precision.HIGHEST


def _cparams(sem):
    return pltpu.CompilerParams(dimension_semantics=sem, vmem_limit_bytes=VMEM_LIMIT)


def _dot(a, b):
    return jnp.dot(a, b, preferred_element_type=F32)


def _dot_nt(a, b):
    return lax.dot_general(a, b, (((1,), (1,)), ((), ())), preferred_element_type=F32)


def _silu(x):
    return x * jax.nn.sigmoid(x)


def _inproj_kernel(x_ref, g_ref, w_ref, ws_ref, o_ref, os_ref, h_ref):
    @pl.when(pl.program_id(1) == 0)
    def _():
        x = x_ref[...]
        y = x * lax.rsqrt(jnp.mean(x * x, axis=-1, keepdims=True) + EPS) * g_ref[...]
        hb = y.astype(BF16)
        h_ref[...] = hb
        os_ref[...] = _dot(hb, ws_ref[...])

    o_ref[...] = _dot(h_ref[...], w_ref[...])


def _in_proj(x2d, g_row, w_main, w_small):
    n, d = x2d.shape
    tm = min(512, n)
    tn = 1024
    return pl.pallas_call(
        _inproj_kernel,
        grid=(n // tm, MAIN_COLS // tn),
        in_specs=[
            pl.BlockSpec((tm, d), lambda i, j: (i, 0)),
            pl.BlockSpec((1, d), lambda i, j: (0, 0)),
            pl.BlockSpec((d, tn), lambda i, j: (0, j)),
            pl.BlockSpec((d, SMALL_COLS), lambda i, j: (0, 0)),
        ],
        out_specs=[
            pl.BlockSpec((tm, tn), lambda i, j: (i, j)),
            pl.BlockSpec((tm, SMALL_COLS), lambda i, j: (i, 0)),
        ],
        out_shape=[jax.ShapeDtypeStruct((n, MAIN_COLS), F32),
                   jax.ShapeDtypeStruct((n, SMALL_COLS), F32)],
        scratch_shapes=[pltpu.VMEM((tm, d), BF16)],
        compiler_params=_cparams(("parallel", "arbitrary")),
        name="in_proj",
    )(x2d, g_row, w_main, w_small)


def _memkv_kernel(x_ref, g_ref, w_ref, gk_ref, mk_ref, mv_ref):
    x = x_ref[...]
    y = x * lax.rsqrt(jnp.mean(x * x, axis=-1, keepdims=True) + EPS) * g_ref[...]
    kv = _dot(y.astype(BF16), w_ref[...])
    gk = gk_ref[...]
    for h in range(N_HEADS):
        sl = slice(h * HEAD_DIM, (h + 1) * HEAD_DIM)
        kh = kv[:, sl]
        mk_ref[:, sl] = kh * lax.rsqrt(jnp.mean(kh * kh, axis=-1, keepdims=True) + EPS) * gk
    mv_ref[...] = kv[:, GROUP_WIDTH:]


def _memory_kv(mem2d, g_row, w_kv, gk_row):
    n, d = mem2d.shape
    tm = min(256, n)
    return pl.pallas_call(
        _memkv_kernel,
        grid=(n // tm,),
        in_specs=[
            pl.BlockSpec((tm, d), lambda i: (i, 0)),
            pl.BlockSpec((1, d), lambda i: (0, 0)),
            pl.BlockSpec((d, 2 * GROUP_WIDTH), lambda i: (0, 0)),
            pl.BlockSpec((1, HEAD_DIM), lambda i: (0, 0)),
        ],
        out_specs=[pl.BlockSpec((tm, GROUP_WIDTH), lambda i: (i, 0)),
                   pl.BlockSpec((tm, GROUP_WIDTH), lambda i: (i, 0))],
        out_shape=[jax.ShapeDtypeStruct((n, GROUP_WIDTH), F32),
                   jax.ShapeDtypeStruct((n, GROUP_WIDTH), F32)],
        compiler_params=_cparams(("parallel",)),
        name="memory_kv",
    )(mem2d, g_row, w_kv, gk_row)


def _gdn_kernel(qkv_ref, z_ref, sm_ref, cw_ref, cb_ref, s0_ref, alog_ref, dtb_ref, gn_ref,
                o_ref, conv_ref, st_ref, xbuf_ref, s_ref, *, t_blk, chunk):
    j = pl.program_id(1)
    nj = pl.num_programs(1)
    gw3 = 3 * GROUP_WIDTH

    @pl.when(j == 0)
    def _():
        xbuf_ref[0:8, :] = jnp.zeros((8, gw3), F32)
        xbuf_ref[5:8, :] = cb_ref[0]
        s_ref[...] = s0_ref[0]

    @pl.when(j > 0)
    def _():
        xbuf_ref[0:8, :] = xbuf_ref[t_blk:t_blk + 8, :]

    xbuf_ref[8:8 + t_blk, :] = qkv_ref[...]
    conv_ref[0] = xbuf_ref[t_blk + 5:t_blk + 8, :]

    cw = cw_ref[...]
    y = xbuf_ref[5:5 + t_blk, :] * cw[0:1, :]
    for jj in range(1, CONV_W):
        y = y + xbuf_ref[5 + jj:5 + jj + t_blk, :] * cw[jj:jj + 1, :]
    y = _silu(y)

    sm = sm_ref[...]
    lane = lax.broadcasted_iota(I32, sm.shape, 1)
    beta_all = jax.nn.sigmoid(sm)
    xs = sm + dtb_ref[...]
    softplus = jnp.maximum(xs, 0.0) + jnp.log1p(jnp.exp(-jnp.abs(xs)))
    g_all = -jnp.exp(alog_ref[...]) * softplus
    g_all = jnp.where((lane >= LANE_ALPHA) & (lane < LANE_ALPHA + N_HEADS), g_all, 0.0)
    ri = lax.broadcasted_iota(I32, (t_blk, t_blk), 0)
    ci = lax.broadcasted_iota(I32, (t_blk, t_blk), 1)
    tri = jnp.where((ri // chunk == ci // chunk) & (ci <= ri), 1.0, 0.0).astype(F32)
    gcum = jnp.dot(tri, g_all, preferred_element_type=F32, precision=HI)
    gcum_t = gcum.T

    r64 = lax.broadcasted_iota(I32, (chunk, chunk), 0)
    c64 = lax.broadcasted_iota(I32, (chunk, chunk), 1)
    causal = c64 <= r64
    strict = c64 < r64
    eye = jnp.where(r64 == c64, 1.0, 0.0).astype(F32)
    n_dbl = max(int(math.log2(chunk)) - 1, 0)
    gn = gn_ref[...]
    z_all = z_ref[...]

    for h in range(N_HEADS):
        sl = slice(h * HEAD_DIM, (h + 1) * HEAD_DIM)
        qh = y[:, COL_QA + h * HEAD_DIM:COL_QA + (h + 1) * HEAD_DIM]
        kh = y[:, COL_KA + h * HEAD_DIM:COL_KA + (h + 1) * HEAD_DIM]
        vh = y[:, COL_VA + h * HEAD_DIM:COL_VA + (h + 1) * HEAD_DIM]
        qh = qh * lax.rsqrt(jnp.sum(qh * qh, axis=-1, keepdims=True) + EPS) * (HEAD_DIM ** -0.5)
        kh = kh * lax.rsqrt(jnp.sum(kh * kh, axis=-1, keepdims=True) + EPS)
        s = s_ref[h]
        for c in range(t_blk // chunk):
            rs = slice(c * chunk, (c + 1) * chunk)
            gcol = gcum[rs, LANE_ALPHA + h:LANE_ALPHA + h + 1]
            grow = gcum_t[LANE_ALPHA + h:LANE_ALPHA + h + 1, rs]
            bcol = beta_all[rs, LANE_BETA + h:LANE_BETA + h + 1]
            decay = jnp.where(causal, jnp.exp(jnp.where(causal, gcol - grow, 0.0)), 0.0)
            qc, kc, vc = qh[rs], kh[rs], vh[rs]
            kb = kc * bcol
            kc16 = kc.astype(BF16)
            a_mat = jnp.where(strict, _dot_nt(kb.astype(BF16), kc16) * decay, 0.0)
            p = -a_mat
            x = eye + p
            for _ in range(n_dbl):
                p = jnp.dot(p, p, preferred_element_type=F32, precision=HI)
                x = x + jnp.dot(x, p, preferred_element_type=F32, precision=HI)
            x16 = x.astype(BF16)
            u = _dot(x16, (vc * bcol).astype(BF16))
            w = _dot(x16, (kb * jnp.exp(gcol)).astype(BF16))
            qk = jnp.where(causal, _dot_nt(qc.astype(BF16), kc16) * decay, 0.0)
            s16 = s.astype(BF16)
            v_new = u - _dot(w.astype(BF16), s16)
            v16 = v_new.astype(BF16)
            o = _dot((qc * jnp.exp(gcol)).astype(BF16), s16) + _dot(qk.astype(BF16), v16)
            glast = gcum[(c + 1) * chunk - 1:(c + 1) * chunk, LANE_ALPHA + h:LANE_ALPHA + h + 1]
            kd_t = (kc * jnp.exp(glast - gcol)).T.astype(BF16)
            s = s * jnp.exp(glast) + _dot(kd_t, v16)
            on = o * lax.rsqrt(jnp.mean(o * o, axis=-1, keepdims=True) + EPS) * gn
            o_ref[rs, sl] = (on * _silu(z_all[rs, sl])).astype(o_ref.dtype)
        s_ref[h] = s

    @pl.when(j == nj - 1)
    def _():
        st_ref[0] = s_ref[...]


def _gdn(proj, small, conv_w, conv_buf, s0, alog_row, dtb_row, gn_row, batch, seqlen):
    n = batch * seqlen
    t_blk = min(256, seqlen)
    chunk = min(CHUNK, seqlen)
    nj = seqlen // t_blk
    gw3 = 3 * GROUP_WIDTH
    kern = functools.partial(_gdn_kernel, t_blk=t_blk, chunk=chunk)
    return pl.pallas_call(
        kern,
        grid=(batch, nj),
        in_specs=[
            pl.BlockSpec((t_blk, gw3), lambda b, j: (b * nj + j, COL_QA // gw3)),
            pl.BlockSpec((t_blk, GROUP_WIDTH), lambda b, j: (b * nj + j, COL_ZA // GROUP_WIDTH)),
            pl.BlockSpec((t_blk, SMALL_COLS), lambda b, j: (b * nj + j, 0)),
            pl.BlockSpec((CONV_W, gw3), lambda b, j: (0, 0)),
            pl.BlockSpec((1, CONV_W - 1, gw3), lambda b, j: (b, 0, 0)),
            pl.BlockSpec((1, N_HEADS, HEAD_DIM, HEAD_DIM), lambda b, j: (b, 0, 0, 0)),
            pl.BlockSpec((1, SMALL_COLS), lambda b, j: (0, 0)),
            pl.BlockSpec((1, SMALL_COLS), lambda b, j: (0, 0)),
            pl.BlockSpec((1, HEAD_DIM), lambda b, j: (0, 0)),
        ],
        out_specs=[
            pl.BlockSpec((t_blk, GROUP_WIDTH), lambda b, j: (b * nj + j, 0)),
            pl.BlockSpec((1, CONV_W - 1, gw3), lambda b, j: (b, 0, 0)),
            pl.BlockSpec((1, N_HEADS, HEAD_DIM, HEAD_DIM), lambda b, j: (b, 0, 0, 0)),
        ],
        out_shape=[
            jax.ShapeDtypeStruct((n, GROUP_WIDTH), BF16),
            jax.ShapeDtypeStruct((batch, CONV_W - 1, gw3), F32),
            jax.ShapeDtypeStruct((batch, N_HEADS, HEAD_DIM, HEAD_DIM), F32),
        ],
        scratch_shapes=[pltpu.VMEM((t_blk + 8, gw3), F32),
                        pltpu.VMEM((N_HEADS, HEAD_DIM, HEAD_DIM), F32)],
        compiler_params=_cparams(("parallel", "arbitrary")),
        name="gdn",
    )(proj, proj, small, conv_w, conv_buf, s0, alog_row, dtb_row, gn_row)


def _ret_kernel(q_ref, k_ref, v_ref, z_ref, cos_ref, sin_ref, s0_ref, g_ref, b_ref,
                o_ref, st_ref, s_ref, *, t_blk):
    j = pl.program_id(1)
    nj = pl.num_programs(1)

    @pl.when(j == 0)
    def _():
        s_ref[...] = s0_ref[0]

    cos = cos_ref[...]
    sin = sin_ref[...]
    ri = lax.broadcasted_iota(I32, (t_blk, t_blk), 0)
    ci = lax.broadcasted_iota(I32, (t_blk, t_blk), 1)
    causal = ci <= ri
    rel = jnp.where(causal, ri - ci, 0).astype(F32)
    idx_col = lax.broadcasted_iota(I32, (t_blk, 1), 0).astype(F32)
    gamma_g = g_ref[...]
    gamma_b = b_ref[...]
    q_all, k_all, v_all, z_all = q_ref[...], k_ref[...], v_ref[...], z_ref[...]

    for h in range(N_HEADS):
        sl = slice(h * HEAD_DIM, (h + 1) * HEAD_DIM)
        lg = math.log(1.0 - 2.0 ** (-5.0 - h))
        qh, kh, vh = q_all[:, sl], k_all[:, sl], v_all[:, sl]
        qh = qh * cos + pltpu.roll(qh, HEAD_DIM // 2, 1) * sin
        kh = (kh * cos + pltpu.roll(kh, HEAD_DIM // 2, 1) * sin) * (HEAD_DIM ** -0.5)
        d_mat = jnp.where(causal, jnp.exp(lg * rel), 0.0)
        q16, k16, v16 = qh.astype(BF16), kh.astype(BF16), vh.astype(BF16)
        o_intra = _dot((_dot_nt(q16, k16) * d_mat).astype(BF16), v16)
        s = s_ref[h]
        o_cross = _dot(q16, s.astype(BF16)) * jnp.exp(lg * (idx_col + 1.0))
        kd_t = (kh * jnp.exp(lg * (t_blk - 1.0 - idx_col))).T.astype(BF16)
        s_ref[h] = s * math.exp(lg * t_blk) + _dot(kd_t, v16)
        o = o_intra + o_cross
        mu = jnp.mean(o, axis=-1, keepdims=True)
        oc = o - mu
        var = jnp.mean(oc * oc, axis=-1, keepdims=True)
        on = oc * lax.rsqrt(var + EPS) * gamma_g + gamma_b
        o_ref[:, sl] = (on * _silu(z_all[:, sl])).astype(o_ref.dtype)

    @pl.when(j == nj - 1)
    def _():
        st_ref[0] = s_ref[...]


def _retention(proj, cos_t, sin_t, s0, g_row, b_row, batch, seqlen):
    n = batch * seqlen
    t_blk = min(256, seqlen)
    nj = seqlen // t_blk
    gw = GROUP_WIDTH
    kern = functools.partial(_ret_kernel, t_blk=t_blk)

    def col(c):
        return pl.BlockSpec((t_blk, gw), lambda b, j: (b * nj + j, c // gw))

    return pl.pallas_call(
        kern,
        grid=(batch, nj),
        in_specs=[
            col(COL_QB), col(COL_KB), col(COL_VB), col(COL_ZB),
            pl.BlockSpec((t_blk, HEAD_DIM), lambda b, j: (j, 0)),
            pl.BlockSpec((t_blk, HEAD_DIM), lambda b, j: (j, 0)),
            pl.BlockSpec((1, N_HEADS, HEAD_DIM, HEAD_DIM), lambda b, j: (b, 0, 0, 0)),
            pl.BlockSpec((1, HEAD_DIM), lambda b, j: (0, 0)),
            pl.BlockSpec((1, HEAD_DIM), lambda b, j: (0, 0)),
        ],
        out_specs=[
            pl.BlockSpec((t_blk, gw), lambda b, j: (b * nj + j, 0)),
            pl.BlockSpec((1, N_HEADS, HEAD_DIM, HEAD_DIM), lambda b, j: (b, 0, 0, 0)),
        ],
        out_shape=[
            jax.ShapeDtypeStruct((n, gw), BF16),
            jax.ShapeDtypeStruct((batch, N_HEADS, HEAD_DIM, HEAD_DIM), F32),
        ],
        scratch_shapes=[pltpu.VMEM((N_HEADS, HEAD_DIM, HEAD_DIM), F32)],
        compiler_params=_cparams(("parallel", "arbitrary")),
        name="retention",
    )(proj, proj, proj, proj, cos_t, sin_t, s0, g_row, b_row)


def _dsa_prep_kernel(k_ref, v_ref, sm_ref, gk_ref, gi_ref, ko_ref, vo_ref, kio_ref):
    k = k_ref[...]
    gk = gk_ref[...]
    for h in range(N_HEADS):
        sl = slice(h * HEAD_DIM, (h + 1) * HEAD_DIM)
        kh = k[:, sl]
        ko_ref[:, sl] = kh * lax.rsqrt(jnp.mean(kh * kh, axis=-1, keepdims=True) + EPS) * gk
    vo_ref[...] = v_ref[...]
    ki = sm_ref[...][:, LANE_KI:LANE_KI + IDX_DIM]
    kio_ref[...] = ki * lax.rsqrt(jnp.mean(ki * ki, axis=-1, keepdims=True) + EPS) * gi_ref[...]


def _dsa_prep(proj, small, gk_row, gi_row):
    n = proj.shape[0]
    tm = min(512, n)
    gw = GROUP_WIDTH
    return pl.pallas_call(
        _dsa_prep_kernel,
        grid=(n // tm,),
        in_specs=[
            pl.BlockSpec((tm, gw), lambda i: (i, COL_KC // gw)),
            pl.BlockSpec((tm, gw), lambda i: (i, COL_VC // gw)),
            pl.BlockSpec((tm, SMALL_COLS), lambda i: (i, 0)),
            pl.BlockSpec((1, HEAD_DIM), lambda i: (0, 0)),
            pl.BlockSpec((1, IDX_DIM), lambda i: (0, 0)),
        ],
        out_specs=[pl.BlockSpec((tm, gw), lambda i: (i, 0)),
                   pl.BlockSpec((tm, gw), lambda i: (i, 0)),
                   pl.BlockSpec((tm, IDX_DIM), lambda i: (i, 0))],
        out_shape=[jax.ShapeDtypeStruct((n, gw), F32),
                   jax.ShapeDtypeStruct((n, gw), F32),
                   jax.ShapeDtypeStruct((n, IDX_DIM), F32)],
        compiler_params=_cparams(("parallel",)),
        name="dsa_prep",
    )(proj, proj, small, gk_row, gi_row)


KT_I = 128
KT_A = 512


def _dsa_kernel(q_ref, qi_ref, z_ref, sm_ref, k_ref, vt_ref, ki_ref, gq_ref, o_ref,
                key_ref, qit_ref, *, tq, offset, s_valid, topk, rows_per_batch):
    i = pl.program_id(1)
    pos0 = offset + i * tq
    t_pos = pos0 + lax.broadcasted_iota(I32, (1, tq), 1)
    t_chunk = t_pos // CHUNK
    n_adm_row = jnp.minimum((t_chunk + 1) * CHUNK, s_valid)
    n_keys = jnp.minimum(((pos0 + tq - 1) // CHUNK + 1) * CHUNK, s_valid)
    n_it = (n_keys + KT_I - 1) // KT_I
    n_at = (n_keys + KT_A - 1) // KT_A
    n_it_full = n_at * (KT_A // KT_I)

    qit_ref[...] = qi_ref[...].T.astype(BF16)
    w_t = sm_ref[...].T * (IDX_HEADS ** -0.5 * IDX_DIM ** -0.5)

    def index_step(kt, carry):
        r0 = pl.multiple_of(kt * KT_I, KT_I)
        ki_t = ki_ref[0, pl.ds(r0, KT_I), :]
        acc = jnp.zeros((KT_I, tq), F32)
        for h in range(IDX_HEADS):
            sc = _dot(ki_t, qit_ref[h * IDX_DIM:(h + 1) * IDX_DIM, :])
            acc = acc + w_t[LANE_WI + h:LANE_WI + h + 1, :] * jnp.maximum(sc, 0.0)
        bits = lax.bitcast_convert_type(acc, I32)
        key = bits ^ ((bits >> 31) & 0x7FFFFFFF)
        s_pos = r0 + lax.broadcasted_iota(I32, (KT_I, 1), 0)
        adm = (s_pos // CHUNK <= t_chunk) & (s_pos < s_valid)
        key_ref[pl.ds(r0, KT_I), :] = jnp.where(adm, key, INT_MIN)
        return carry

    lax.fori_loop(0, n_it_full, index_step, 0)

    def count(pred_fn):
        def body(kt, acc):
            r0 = pl.multiple_of(kt * KT_I, KT_I)
            key = key_ref[pl.ds(r0, KT_I), :]
            s_pos = r0 + lax.broadcasted_iota(I32, (KT_I, 1), 0)
            m = jnp.where(pred_fn(key, s_pos), 1, 0).astype(I32)
            return acc + jnp.sum(m.reshape(KT_I // 8, 8, tq), axis=0)
        acc = lax.fori_loop(0, n_it, body, jnp.zeros((8, tq), I32))
        return jnp.sum(acc, axis=0, keepdims=True)

    small = jnp.where(n_adm_row <= topk, 1, 0).astype(I32)

    def bit_step(b, carry):
        v, done, thr = carry
        cand_u = v | jnp.left_shift(jnp.int32(1), 31 - b)
        cand_s = cand_u ^ INT_MIN
        cnt = count(lambda key, s_pos: key >= cand_s)
        v = jnp.where(cnt >= topk, cand_u, v)
        newly = (cnt == topk) & (done == 0)
        thr = jnp.where(newly, cand_s, thr)
        return v, jnp.where(newly, 1, done), thr

    v0 = jnp.zeros((1, tq), I32)
    thr0 = jnp.full((1, tq), INT_MIN + 1, I32)
    v_u, done_i, thr = lax.fori_loop(0, 32, bit_step, (v0, small, thr0))
    done = done_i != 0
    v_s = v_u ^ INT_MIN

    n_gt = count(lambda key, s_pos: key > v_s)
    need = topk - n_gt
    pos_bits = max(int(math.ceil(math.log2(max(k_ref.shape[1], 2)))), 1) + 1

    def pos_step(b, jv):
        cand = jv | jnp.left_shift(jnp.int32(1), pos_bits - 1 - b)
        cnt = count(lambda key, s_pos: (key == v_s) & (s_pos < cand))
        return jnp.where(cnt <= need, cand, jv)

    j_lim = lax.fori_loop(0, pos_bits, pos_step, jnp.zeros((1, tq), I32))

    v_eff = jnp.where(done, thr - 1, v_s)
    j_eff = jnp.where(done, 0, j_lim)

    def bias_step(kt, carry):
        r0 = pl.multiple_of(kt * KT_I, KT_I)
        key = key_ref[pl.ds(r0, KT_I), :]
        s_pos = r0 + lax.broadcasted_iota(I32, (KT_I, 1), 0)
        sel = ((key > v_eff) | ((key == v_eff) & (s_pos < j_eff))) & (key != INT_MIN)
        bias = jnp.where(sel, 0.0, NEG_BIG).astype(F32)
        key_ref[pl.ds(r0, KT_I), :] = lax.bitcast_convert_type(bias, I32)
        return carry

    lax.fori_loop(0, n_it_full, bias_step, 0)

    gq = gq_ref[...]
    q_all = q_ref[...]
    z_all = z_ref[...]
    for h in range(N_HEADS):
        sl = slice(h * HEAD_DIM, (h + 1) * HEAD_DIM)
        qh = q_all[:, sl]
        qh = qh * lax.rsqrt(jnp.mean(qh * qh, axis=-1, keepdims=True) + EPS) * gq * (HEAD_DIM ** -0.5)
        qt = qh.T.astype(BF16)

        def att_step(kt, carry, h=h, qt=qt):
            m, l, acc = carry
            r0 = pl.multiple_of(kt * KT_A, KT_A)
            k_t = k_ref[0, pl.ds(r0, KT_A), h * HEAD_DIM:(h + 1) * HEAD_DIM]
            bias = lax.bitcast_convert_type(key_ref[pl.ds(r0, KT_A), :], F32)
            logit = _dot(k_t, qt) + bias
            m_new = jnp.maximum(m, jnp.max(logit, axis=0, keepdims=True))
            alpha = jnp.exp(m - m_new)
            p = jnp.exp(logit - m_new)
            l = alpha * l + jnp.sum(p, axis=0, keepdims=True)
            v_t = vt_ref[0, kt, h * HEAD_DIM:(h + 1) * HEAD_DIM, :]
            acc = alpha * acc + _dot(v_t, p.astype(BF16))
            return m_new, l, acc

        m0 = jnp.full((1, tq), NEG_BIG, F32)
        l0 = jnp.zeros((1, tq), F32)
        a0 = jnp.zeros((HEAD_DIM, tq), F32)
        _, l, acc = lax.fori_loop(0, n_at, att_step, (m0, l0, a0))
        oh = (acc / l).T
        o_ref[:, sl] = (oh * _silu(z_all[:, sl])).astype(o_ref.dtype)


def _dsa(proj, small, k_all16, vt16, ki16, gq_row, batch, q_len, tq, offset, s_valid):
    n = batch * q_len
    nq = q_len // tq
    s_pad = k_all16.shape[1]
    topk = min(TOPK_MAX, s_valid // 4)
    gw = GROUP_WIDTH
    qi_w = IDX_HEADS * IDX_DIM
    kern = functools.partial(_dsa_kernel, tq=tq, offset=offset, s_valid=s_valid, topk=topk,
                             rows_per_batch=q_len)
    return pl.pallas_call(
        kern,
        grid=(batch, nq),
        in_specs=[
            pl.BlockSpec((tq, gw), lambda b, i: (b * nq + i, COL_QC // gw)),
            pl.BlockSpec((tq, qi_w), lambda b, i: (b * nq + i, COL_QI // qi_w)),
            pl.BlockSpec((tq, gw), lambda b, i: (b * nq + i, COL_ZC // gw)),
            pl.BlockSpec((tq, SMALL_COLS), lambda b, i: (b * nq + i, 0)),
            pl.BlockSpec((1, s_pad, gw), lambda b, i: (b, 0, 0)),
            pl.BlockSpec((1, s_pad // KT_A, gw, KT_A), lambda b, i: (b, 0, 0, 0)),
            pl.BlockSpec((1, s_pad, IDX_DIM), lambda b, i: (b, 0, 0)),
            pl.BlockSpec((1, HEAD_DIM), lambda b, i: (0, 0)),
        ],
        out_specs=pl.BlockSpec((tq, gw), lambda b, i: (b * nq + i, 0)),
        out_shape=jax.ShapeDtypeStruct((n, gw), BF16),
        scratch_shapes=[pltpu.VMEM((s_pad, tq), I32),
                        pltpu.VMEM((qi_w, tq), BF16)],
        compiler_params=_cparams(("parallel", "arbitrary")),
        name="dsa",
    )(proj, proj, proj, small, k_all16, vt16, ki16, gq_row)


def _mem_kernel(q_ref, z_ref, mk_ref, mv_ref, gq_ref, o_ref):
    gq = gq_ref[...]
    q_all, z_all = q_ref[...], z_ref[...]
    mk = mk_ref[0].astype(BF16)
    mv = mv_ref[0].astype(BF16)
    for h in range(N_HEADS):
        sl = slice(h * HEAD_DIM, (h + 1) * HEAD_DIM)
        qh = q_all[:, sl]
        qh = qh * lax.rsqrt(jnp.mean(qh * qh, axis=-1, keepdims=True) + EPS) * gq * (HEAD_DIM ** -0.5)
        logit = _dot_nt(qh.astype(BF16), mk[:, sl])
        m = jnp.max(logit, axis=-1, keepdims=True)
        p = jnp.exp(logit - m)
        l = jnp.sum(p, axis=-1, keepdims=True)
        oh = _dot(p.astype(BF16), mv[:, sl]) / l
        o_ref[:, sl] = (oh * _silu(z_all[:, sl])).astype(o_ref.dtype)


def _mem_attend(proj, mk, mv, gq_row, batch, seqlen):
    n = batch * seqlen
    tm = min(512, seqlen)
    nj = seqlen // tm
    gw = GROUP_WIDTH
    n_mem = mk.shape[1]
    return pl.pallas_call(
        _mem_kernel,
        grid=(batch, nj),
        in_specs=[
            pl.BlockSpec((tm, gw), lambda b, j: (b * nj + j, COL_QD // gw)),
            pl.BlockSpec((tm, gw), lambda b, j: (b * nj + j, COL_ZD // gw)),
            pl.BlockSpec((1, n_mem, gw), lambda b, j: (b, 0, 0)),
            pl.BlockSpec((1, n_mem, gw), lambda b, j: (b, 0, 0)),
            pl.BlockSpec((1, HEAD_DIM), lambda b, j: (0, 0)),
        ],
        out_specs=pl.BlockSpec((tm, gw), lambda b, j: (b * nj + j, 0)),
        out_shape=jax.ShapeDtypeStruct((n, gw), BF16),
        compiler_params=_cparams(("parallel", "arbitrary")),
        name="mem_attend",
    )(proj, proj, mk, mv, gq_row)


def _outproj_kernel(x_ref, a_ref, b_ref, c_ref, d_ref, w_ref, y_ref):
    gw = GROUP_WIDTH
    acc = x_ref[...] + _dot(a_ref[...], w_ref[0:gw, :])
    acc = acc + _dot(b_ref[...], w_ref[gw:2 * gw, :])
    acc = acc + _dot(c_ref[...], w_ref[2 * gw:3 * gw, :])
    acc = acc + _dot(d_ref[...], w_ref[3 * gw:4 * gw, :])
    y_ref[...] = acc


def _out_proj(x2d, oa, ob, oc, od, w16):
    n, d = x2d.shape
    tm = min(512, n)
    gw = GROUP_WIDTH
    grp = pl.BlockSpec((tm, gw), lambda i: (i, 0))
    return pl.pallas_call(
        _outproj_kernel,
        grid=(n // tm,),
        in_specs=[pl.BlockSpec((tm, d), lambda i: (i, 0)), grp, grp, grp, grp,
                  pl.BlockSpec((4 * gw, d), lambda i: (0, 0))],
        out_specs=pl.BlockSpec((tm, d), lambda i: (i, 0)),
        out_shape=jax.ShapeDtypeStruct((n, d), F32),
        compiler_params=_cparams(("parallel",)),
        name="out_proj",
    )(x2d, oa, ob, oc, od, w16)


def _reorder_w_in(w):
    gw = GROUP_WIDTH
    o = 0
    seg = {}
    for name, width in (("qa", gw), ("ka", gw), ("va", gw), ("za", gw), ("ba", N_HEADS), ("aa", N_HEADS),
                        ("qb", gw), ("kb", gw), ("vb", gw), ("zb", gw),
                        ("qc", gw), ("kc", gw), ("vc", gw), ("zc", gw),
                        ("qi", IDX_HEADS * IDX_DIM), ("ki", IDX_DIM), ("wi", IDX_HEADS),
                        ("qd", gw), ("zd", gw)):
        seg[name] = w[:, o:o + width]
        o += width
    main = jnp.concatenate([seg[k] for k in ("qa", "ka", "va", "qb", "kb", "vb", "za", "zb",
                                             "qc", "kc", "vc", "zc", "qi", "qd", "zd")], axis=1)
    pad = jnp.zeros((w.shape[0], SMALL_COLS - IDX_DIM - IDX_HEADS - 2 * N_HEADS), w.dtype)
    small = jnp.concatenate([seg["ki"], seg["wi"], seg["ba"], seg["aa"], pad], axis=1)
    return main.astype(BF16), small.astype(BF16)


def _lane_row(vals, lane0):
    row = jnp.zeros((1, SMALL_COLS), F32)
    return row.at[0, lane0:lane0 + vals.shape[0]].set(vals.astype(F32))


def _rope_tables(pos):
    half = HEAD_DIM // 2
    inv = ROPE_THETA ** (-jnp.arange(half, dtype=F32) / half)
    ang = pos.astype(F32)[:, None] * inv[None, :]
    cos, sin = jnp.cos(ang), jnp.sin(ang)
    return jnp.concatenate([cos, cos], axis=-1), jnp.concatenate([-sin, sin], axis=-1)


def _round_up(x, m):
    return (x + m - 1) // m * m


def _mixer_layer(x, conv_buf, s_gdn, s_ret, past_k, past_v, past_ki, mem_k, mem_v, wts):
    b, l, d = x.shape
    n = b * l
    offset = 0 if past_k is None else past_k.shape[1]
    x2d = x.reshape(n, d)
    proj, small = _in_proj(x2d, wts["norm_g"], wts["w_main"], wts["w_small"])

    o_a, conv_new, s_gdn_new = _gdn(proj, small, wts["conv_w"], conv_buf, s_gdn, wts["alog_row"],
                                    wts["dtb_row"], wts["gdn_norm_g"], b, l)

    cos_t, sin_t = _rope_tables(offset + jnp.arange(l, dtype=I32))
    o_b, s_ret_new = _retention(proj, cos_t, sin_t, s_ret, wts["ret_norm_g"], wts["ret_norm_b"], b, l)

    kc, vc, ki = _dsa_prep(proj, small, wts["dsa_k_norm_g"], wts["idx_k_norm_g"])
    kc3, vc3, ki3 = kc.reshape(b, l, GROUP_WIDTH), vc.reshape(b, l, GROUP_WIDTH), ki.reshape(b, l, IDX_DIM)
    if past_k is None:
        k_all, v_all, ki_all = kc3, vc3, ki3
    else:
        k_all = jnp.concatenate([past_k.reshape(b, offset, GROUP_WIDTH), kc3], axis=1)
        v_all = jnp.concatenate([past_v.reshape(b, offset, GROUP_WIDTH), vc3], axis=1)
        ki_all = jnp.concatenate([past_ki, ki3], axis=1)
    s_valid = k_all.shape[1]
    s_pad = _round_up(s_valid, KT_A)
    padw = ((0, 0), (0, s_pad - s_valid), (0, 0))
    k16 = jnp.pad(k_all.astype(BF16), padw)
    v16 = jnp.pad(v_all.astype(BF16), padw)
    ki16 = jnp.pad(ki_all.astype(BF16), padw)
    vt16 = v16.reshape(b, s_pad // KT_A, KT_A, GROUP_WIDTH).transpose(0, 1, 3, 2)

    tq = 256 if l % 256 == 0 else 128
    q_len = _round_up(l, tq)
    if q_len != l:
        proj_q = jnp.pad(proj.reshape(b, l, MAIN_COLS), ((0, 0), (0, q_len - l), (0, 0))).reshape(b * q_len, MAIN_COLS)
        small_q = jnp.pad(small.reshape(b, l, SMALL_COLS), ((0, 0), (0, q_len - l), (0, 0))).reshape(b * q_len, SMALL_COLS)
    else:
        proj_q, small_q = proj, small
    o_c = _dsa(proj_q, small_q, k16, vt16, ki16, wts["dsa_q_norm_g"], b, q_len, tq, offset, s_valid)
    if q_len != l:
        o_c = o_c.reshape(b, q_len, GROUP_WIDTH)[:, :l].reshape(n, GROUP_WIDTH)

    o_d = _mem_attend(proj, mem_k, mem_v, wts["mem_q_norm_g"], b, l)

    y = _out_proj(x2d, o_a, o_b, o_c, o_d, wts["w_out"]).reshape(b, l, d)
    new = (conv_new, s_gdn_new, s_ret_new,
           kc.reshape(b, l, N_HEADS, HEAD_DIM), vc.reshape(b, l, N_HEADS, HEAD_DIM), ki3)
    return y, new


def kernel(x_prompt, x_sample, cache_gdn_conv, state_gdn, state_ret, cache_dsa_k, cache_dsa_v, cache_idx_k, cache_mem_k, cache_mem_v, mem_prompt, norm_g, w_in, gdn_conv_w, gdn_a_log, gdn_dt_bias, gdn_norm_g, ret_norm_g, ret_norm_b, dsa_q_norm_g, dsa_k_norm_g, idx_k_norm_g, mem_norm_g, w_mem_kv, mem_q_norm_g, mem_k_norm_g, w_out):
    depth = w_in.shape[0]
    b = x_prompt.shape[0]
    n_mem = mem_prompt.shape[1]
    d = x_prompt.shape[-1]
    y_p, y_s = x_prompt, x_sample
    st_p, st_s, mem_p = [], [], []
    for li in range(depth):
        w_main, w_small = _reorder_w_in(w_in[li])
        wts = dict(
            norm_g=norm_g[li][None, :], w_main=w_main, w_small=w_small, conv_w=gdn_conv_w[li],
            alog_row=_lane_row(gdn_a_log[li], LANE_ALPHA), dtb_row=_lane_row(gdn_dt_bias[li], LANE_ALPHA),
            gdn_norm_g=gdn_norm_g[li][None, :], ret_norm_g=ret_norm_g[li][None, :],
            ret_norm_b=ret_norm_b[li][None, :], dsa_q_norm_g=dsa_q_norm_g[li][None, :],
            dsa_k_norm_g=dsa_k_norm_g[li][None, :], idx_k_norm_g=idx_k_norm_g[li][None, :],
            mem_q_norm_g=mem_q_norm_g[li][None, :], w_out=w_out[li].astype(BF16),
        )
        mk, mv = _memory_kv(mem_prompt.reshape(b * n_mem, d), mem_norm_g[li][None, :],
                            w_mem_kv[li].astype(BF16), mem_k_norm_g[li][None, :])
        mk = mk.reshape(b, n_mem, GROUP_WIDTH)
        mv = mv.reshape(b, n_mem, GROUP_WIDTH)
        conv0 = jnp.zeros((b, CONV_W - 1, 3 * GROUP_WIDTH), F32)
        s0 = jnp.zeros((b, N_HEADS, HEAD_DIM, HEAD_DIM), F32)
        y_p, sp = _mixer_layer(y_p, conv0, s0, s0, None, None, None, mk, mv, wts)
        st_p.append(sp)
        mem_p.append((mk.reshape(b, n_mem, N_HEADS, HEAD_DIM), mv.reshape(b, n_mem, N_HEADS, HEAD_DIM)))
        bs = x_sample.shape[0]
        y_s, ss = _mixer_layer(y_s, cache_gdn_conv[li], state_gdn[li], state_ret[li],
                               cache_dsa_k[li], cache_dsa_v[li], cache_idx_k[li],
                               cache_mem_k[li].reshape(bs, n_mem, GROUP_WIDTH),
                               cache_mem_v[li].reshape(bs, n_mem, GROUP_WIDTH), wts)
        st_s.append(ss)

    def stack(lst, k):
        return jnp.stack([s[k] for s in lst])

    return (y_p, y_s,
            stack(st_p, 0), stack(st_p, 1), stack(st_p, 2), stack(st_p, 3), stack(st_p, 4), stack(st_p, 5),
            stack(mem_p, 0), stack(mem_p, 1),
            stack(st_s, 0), stack(st_s, 1), stack(st_s, 2), stack(st_s, 3), stack(st_s, 4), stack(st_s, 5))
```

```python
import functools
import math

import jax
import jax.numpy as jnp
from jax import lax
from jax.experimental import pallas as pl
from jax.experimental.pallas import tpu as pltpu

F32 = jnp.float32
BF16 = jnp.bfloat16
I32 = jnp.int32

HEAD_DIM = 128
N_HEADS = 4
GROUP_WIDTH = N_HEADS * HEAD_DIM
CHUNK = 64
CONV_W = 4
IDX_HEADS = 16
IDX_DIM = 64
TOPK_MAX = 256
ROPE_THETA = 10000.0
EPS = 1e-6

COL_QA, COL_KA, COL_VA = 0, 512, 1024
COL_QB, COL_KB, COL_VB = 1536, 2048, 2560
COL_ZA, COL_ZB = 3072, 3584
COL_QC, COL_KC, COL_VC, COL_ZC = 4096, 4608, 5120, 5632
COL_QI = 6144
COL_QD, COL_ZD = 7168, 7680
MAIN_COLS = 8192
LANE_KI, LANE_WI, LANE_BETA, LANE_ALPHA = 0, 64, 80, 84
SMALL_COLS = 128

INT_MIN = -2 ** 31
NEG_BIG = -1e30
LOG2_E = 1.4426950408889634
VMEM_LIMIT = 56 * 1024 * 1024
HI = lax.Precision.HIGHEST


def _cparams(sem):
    return pltpu.CompilerParams(dimension_semantics=sem, vmem_limit_bytes=VMEM_LIMIT)


def _dot(a, b):
    return jnp.dot(a, b, preferred_element_type=F32)


def _dot_nt(a, b):
    return lax.dot_general(a, b, (((1,), (1,)), ((), ())), preferred_element_type=F32)


def _silu(x):
    return x * jax.nn.sigmoid(x)


def _inproj_kernel(x_ref, g_ref, w_ref, ws_ref, o_ref, os_ref, h_ref):
    @pl.when(pl.program_id(1) == 0)
    def _():
        x = x_ref[...]
        y = x * lax.rsqrt(jnp.mean(x * x, axis=-1, keepdims=True) + EPS) * g_ref[...]
        hb = y.astype(BF16)
        h_ref[...] = hb
        os_ref[...] = _dot(hb, ws_ref[...])

    o_ref[...] = _dot(h_ref[...], w_ref[...])


def _in_proj(x2d, g_row, w_main, w_small):
    n, d = x2d.shape
    tm = min(512, n)
    tn = 1024
    return pl.pallas_call(
        _inproj_kernel,
        grid=(n // tm, MAIN_COLS // tn),
        in_specs=[
            pl.BlockSpec((tm, d), lambda i, j: (i, 0)),
            pl.BlockSpec((1, d), lambda i, j: (0, 0)),
            pl.BlockSpec((d, tn), lambda i, j: (0, j)),
            pl.BlockSpec((d, SMALL_COLS), lambda i, j: (0, 0)),
        ],
        out_specs=[
            pl.BlockSpec((tm, tn), lambda i, j: (i, j)),
            pl.BlockSpec((tm, SMALL_COLS), lambda i, j: (i, 0)),
        ],
        out_shape=[jax.ShapeDtypeStruct((n, MAIN_COLS), F32),
                   jax.ShapeDtypeStruct((n, SMALL_COLS), F32)],
        scratch_shapes=[pltpu.VMEM((tm, d), BF16)],
        compiler_params=_cparams(("parallel", "arbitrary")),
        name="in_proj",
    )(x2d, g_row, w_main, w_small)


def _memkv_kernel(x_ref, g_ref, w_ref, gk_ref, mk_ref, mv_ref):
    x = x_ref[...]
    y = x * lax.rsqrt(jnp.mean(x * x, axis=-1, keepdims=True) + EPS) * g_ref[...]
    kv = _dot(y.astype(BF16), w_ref[...])
    gk = gk_ref[...]
    for h in range(N_HEADS):
        sl = slice(h * HEAD_DIM, (h + 1) * HEAD_DIM)
        kh = kv[:, sl]
        mk_ref[:, sl] = kh * lax.rsqrt(jnp.mean(kh * kh, axis=-1, keepdims=True) + EPS) * gk
    mv_ref[...] = kv[:, GROUP_WIDTH:]


def _memory_kv(mem2d, g_row, w_kv, gk_row):
    n, d = mem2d.shape
    tm = min(256, n)
    return pl.pallas_call(
        _memkv_kernel,
        grid=(n // tm,),
        in_specs=[
            pl.BlockSpec((tm, d), lambda i: (i, 0)),
            pl.BlockSpec((1, d), lambda i: (0, 0)),
            pl.BlockSpec((d, 2 * GROUP_WIDTH), lambda i: (0, 0)),
            pl.BlockSpec((1, HEAD_DIM), lambda i: (0, 0)),
        ],
        out_specs=[pl.BlockSpec((tm, GROUP_WIDTH), lambda i: (i, 0)),
                   pl.BlockSpec((tm, GROUP_WIDTH), lambda i: (i, 0))],
        out_shape=[jax.ShapeDtypeStruct((n, GROUP_WIDTH), F32),
                   jax.ShapeDtypeStruct((n, GROUP_WIDTH), F32)],
        compiler_params=_cparams(("parallel",)),
        name="memory_kv",
    )(mem2d, g_row, w_kv, gk_row)


def _gdn_kernel(qkv_ref, z_ref, sm_ref, cw_ref, cb_ref, s0_ref, alog_ref, dtb_ref, gn_ref,
                o_ref, conv_ref, st_ref, xbuf_ref, s_ref, *, t_blk, chunk):
    j = pl.program_id(1)
    nj = pl.num_programs(1)
    gw3 = 3 * GROUP_WIDTH

    @pl.when(j == 0)
    def _():
        xbuf_ref[0:8, :] = jnp.zeros((8, gw3), F32)
        xbuf_ref[5:8, :] = cb_ref[0]
        s_ref[...] = s0_ref[0]

    @pl.when(j > 0)
    def _():
        xbuf_ref[0:8, :] = xbuf_ref[t_blk:t_blk + 8, :]

    xbuf_ref[8:8 + t_blk, :] = qkv_ref[...]
    conv_ref[0] = xbuf_ref[t_blk + 5:t_blk + 8, :]

    cw = cw_ref[...]
    y = xbuf_ref[5:5 + t_blk, :] * cw[0:1, :]
    for jj in range(1, CONV_W):
        y = y + xbuf_ref[5 + jj:5 + jj + t_blk, :] * cw[jj:jj + 1, :]
    y = _silu(y)

    sm = sm_ref[...]
    lane = lax.broadcasted_iota(I32, sm.shape, 1)
    beta_all = jax.nn.sigmoid(sm)
    xs = sm + dtb_ref[...]
    softplus = jnp.maximum(xs, 0.0) + jnp.log1p(jnp.exp(-jnp.abs(xs)))
    g_all = -jnp.exp(alog_ref[...]) * softplus
    g_all = jnp.where((lane >= LANE_ALPHA) & (lane < LANE_ALPHA + N_HEADS), g_all, 0.0)
    ri = lax.broadcasted_iota(I32, (t_blk, t_blk), 0)
    ci = lax.broadcasted_iota(I32, (t_blk, t_blk), 1)
    tri = jnp.where((ri // chunk == ci // chunk) & (ci <= ri), 1.0, 0.0).astype(F32)
    gcum = jnp.dot(tri, g_all, preferred_element_type=F32, precision=HI)
    gcum_t = gcum.T

    r64 = lax.broadcasted_iota(I32, (chunk, chunk), 0)
    c64 = lax.broadcasted_iota(I32, (chunk, chunk), 1)
    causal = c64 <= r64
    strict = c64 < r64
    eye = jnp.where(r64 == c64, 1.0, 0.0).astype(F32)
    n_dbl = max(int(math.log2(chunk)) - 1, 0)
    gn = gn_ref[...]
    z_all = z_ref[...]

    for h in range(N_HEADS):
        sl = slice(h * HEAD_DIM, (h + 1) * HEAD_DIM)
        qh = y[:, COL_QA + h * HEAD_DIM:COL_QA + (h + 1) * HEAD_DIM]
        kh = y[:, COL_KA + h * HEAD_DIM:COL_KA + (h + 1) * HEAD_DIM]
        vh = y[:, COL_VA + h * HEAD_DIM:COL_VA + (h + 1) * HEAD_DIM]
        qh = qh * lax.rsqrt(jnp.sum(qh * qh, axis=-1, keepdims=True) + EPS) * (HEAD_DIM ** -0.5)
        kh = kh * lax.rsqrt(jnp.sum(kh * kh, axis=-1, keepdims=True) + EPS)
        s = s_ref[h]
        for c in range(t_blk // chunk):
            rs = slice(c * chunk, (c + 1) * chunk)
            gcol = gcum[rs, LANE_ALPHA + h:LANE_ALPHA + h + 1]
            grow = gcum_t[LANE_ALPHA + h:LANE_ALPHA + h + 1, rs]
            bcol = beta_all[rs, LANE_BETA + h:LANE_BETA + h + 1]
            decay = jnp.where(causal, jnp.exp(jnp.where(causal, gcol - grow, 0.0)), 0.0)
            qc, kc, vc = qh[rs], kh[rs], vh[rs]
            kb = kc * bcol
            kc16 = kc.astype(BF16)
            a_mat = jnp.where(strict, _dot_nt(kb.astype(BF16), kc16) * decay, 0.0)
            p = -a_mat
            x = eye + p
            for _ in range(n_dbl):
                p = jnp.dot(p, p, preferred_element_type=F32, precision=HI)
                x = x + jnp.dot(x, p, preferred_element_type=F32, precision=HI)
            x16 = x.astype(BF16)
            u = _dot(x16, (vc * bcol).astype(BF16))
            w = _dot(x16, (kb * jnp.exp(gcol)).astype(BF16))
            qk = jnp.where(causal, _dot_nt(qc.astype(BF16), kc16) * decay, 0.0)
            s16 = s.astype(BF16)
            v_new = u - _dot(w.astype(BF16), s16)
            v16 = v_new.astype(BF16)
            o = _dot((qc * jnp.exp(gcol)).astype(BF16), s16) + _dot(qk.astype(BF16), v16)
            glast = gcum[(c + 1) * chunk - 1:(c + 1) * chunk, LANE_ALPHA + h:LANE_ALPHA + h + 1]
            kd_t = (kc * jnp.exp(glast - gcol)).T.astype(BF16)
            s = s * jnp.exp(glast) + _dot(kd_t, v16)
            on = o * lax.rsqrt(jnp.mean(o * o, axis=-1, keepdims=True) + EPS) * gn
            o_ref[rs, sl] = (on * _silu(z_all[rs, sl])).astype(o_ref.dtype)
        s_ref[h] = s

    @pl.when(j == nj - 1)
    def _():
        st_ref[0] = s_ref[...]


def _gdn(proj, small, conv_w, conv_buf, s0, alog_row, dtb_row, gn_row, batch, seqlen):
    n = batch * seqlen
    t_blk = min(256, seqlen)
    chunk = min(CHUNK, seqlen)
    nj = seqlen // t_blk
    gw3 = 3 * GROUP_WIDTH
    kern = functools.partial(_gdn_kernel, t_blk=t_blk, chunk=chunk)
    return pl.pallas_call(
        kern,
        grid=(batch, nj),
        in_specs=[
            pl.BlockSpec((t_blk, gw3), lambda b, j: (b * nj + j, COL_QA // gw3)),
            pl.BlockSpec((t_blk, GROUP_WIDTH), lambda b, j: (b * nj + j, COL_ZA // GROUP_WIDTH)),
            pl.BlockSpec((t_blk, SMALL_COLS), lambda b, j: (b * nj + j, 0)),
            pl.BlockSpec((CONV_W, gw3), lambda b, j: (0, 0)),
            pl.BlockSpec((1, CONV_W - 1, gw3), lambda b, j: (b, 0, 0)),
            pl.BlockSpec((1, N_HEADS, HEAD_DIM, HEAD_DIM), lambda b, j: (b, 0, 0, 0)),
            pl.BlockSpec((1, SMALL_COLS), lambda b, j: (0, 0)),
            pl.BlockSpec((1, SMALL_COLS), lambda b, j: (0, 0)),
            pl.BlockSpec((1, HEAD_DIM), lambda b, j: (0, 0)),
        ],
        out_specs=[
            pl.BlockSpec((t_blk, GROUP_WIDTH), lambda b, j: (b * nj + j, 0)),
            pl.BlockSpec((1, CONV_W - 1, gw3), lambda b, j: (b, 0, 0)),
            pl.BlockSpec((1, N_HEADS, HEAD_DIM, HEAD_DIM), lambda b, j: (b, 0, 0, 0)),
        ],
        out_shape=[
            jax.ShapeDtypeStruct((n, GROUP_WIDTH), BF16),
            jax.ShapeDtypeStruct((batch, CONV_W - 1, gw3), F32),
            jax.ShapeDtypeStruct((batch, N_HEADS, HEAD_DIM, HEAD_DIM), F32),
        ],
        scratch_shapes=[pltpu.VMEM((t_blk + 8, gw3), F32),
                        pltpu.VMEM((N_HEADS, HEAD_DIM, HEAD_DIM), F32)],
        compiler_params=_cparams(("parallel", "arbitrary")),
        name="gdn",
    )(proj, proj, small, conv_w, conv_buf, s0, alog_row, dtb_row, gn_row)


def _ret_kernel(q_ref, k_ref, v_ref, z_ref, cos_ref, sin_ref, s0_ref, g_ref, b_ref,
                o_ref, st_ref, s_ref, *, t_blk):
    j = pl.program_id(1)
    nj = pl.num_programs(1)

    @pl.when(j == 0)
    def _():
        s_ref[...] = s0_ref[0]

    cos = cos_ref[...]
    sin = sin_ref[...]
    ri = lax.broadcasted_iota(I32, (t_blk, t_blk), 0)
    ci = lax.broadcasted_iota(I32, (t_blk, t_blk), 1)
    causal = ci <= ri
    rel = jnp.where(causal, ri - ci, 0).astype(F32)
    idx_col = lax.broadcasted_iota(I32, (t_blk, 1), 0).astype(F32)
    gamma_g = g_ref[...]
    gamma_b = b_ref[...]
    q_all, k_all, v_all, z_all = q_ref[...], k_ref[...], v_ref[...], z_ref[...]

    for h in range(N_HEADS):
        sl = slice(h * HEAD_DIM, (h + 1) * HEAD_DIM)
        lg = math.log(1.0 - 2.0 ** (-5.0 - h))
        qh, kh, vh = q_all[:, sl], k_all[:, sl], v_all[:, sl]
        qh = qh * cos + pltpu.roll(qh, HEAD_DIM // 2, 1) * sin
        kh = (kh * cos + pltpu.roll(kh, HEAD_DIM // 2, 1) * sin) * (HEAD_DIM ** -0.5)
        d_mat = jnp.where(causal, jnp.exp(lg * rel), 0.0)
        q16, k16, v16 = qh.astype(BF16), kh.astype(BF16), vh.astype(BF16)
        o_intra = _dot((_dot_nt(q16, k16) * d_mat).astype(BF16), v16)
        s = s_ref[h]
        o_cross = _dot(q16, s.astype(BF16)) * jnp.exp(lg * (idx_col + 1.0))
        kd_t = (kh * jnp.exp(lg * (t_blk - 1.0 - idx_col))).T.astype(BF16)
        s_ref[h] = s * math.exp(lg * t_blk) + _dot(kd_t, v16)
        o = o_intra + o_cross
        mu = jnp.mean(o, axis=-1, keepdims=True)
        oc = o - mu
        var = jnp.mean(oc * oc, axis=-1, keepdims=True)
        on = oc * lax.rsqrt(var + EPS) * gamma_g + gamma_b
        o_ref[:, sl] = (on * _silu(z_all[:, sl])).astype(o_ref.dtype)

    @pl.when(j == nj - 1)
    def _():
        st_ref[0] = s_ref[...]


def _retention(proj, cos_t, sin_t, s0, g_row, b_row, batch, seqlen):
    n = batch * seqlen
    t_blk = min(256, seqlen)
    nj = seqlen // t_blk
    gw = GROUP_WIDTH
    kern = functools.partial(_ret_kernel, t_blk=t_blk)

    def col(c):
        return pl.BlockSpec((t_blk, gw), lambda b, j: (b * nj + j, c // gw))

    return pl.pallas_call(
        kern,
        grid=(batch, nj),
        in_specs=[
            col(COL_QB), col(COL_KB), col(COL_VB), col(COL_ZB),
            pl.BlockSpec((t_blk, HEAD_DIM), lambda b, j: (j, 0)),
            pl.BlockSpec((t_blk, HEAD_DIM), lambda b, j: (j, 0)),
            pl.BlockSpec((1, N_HEADS, HEAD_DIM, HEAD_DIM), lambda b, j: (b, 0, 0, 0)),
            pl.BlockSpec((1, HEAD_DIM), lambda b, j: (0, 0)),
            pl.BlockSpec((1, HEAD_DIM), lambda b, j: (0, 0)),
        ],
        out_specs=[
            pl.BlockSpec((t_blk, gw), lambda b, j: (b * nj + j, 0)),
            pl.BlockSpec((1, N_HEADS, HEAD_DIM, HEAD_DIM), lambda b, j: (b, 0, 0, 0)),
        ],
        out_shape=[
            jax.ShapeDtypeStruct((n, gw), BF16),
            jax.ShapeDtypeStruct((batch, N_HEADS, HEAD_DIM, HEAD_DIM), F32),
        ],
        scratch_shapes=[pltpu.VMEM((N_HEADS, HEAD_DIM, HEAD_DIM), F32)],
        compiler_params=_cparams(("parallel", "arbitrary")),
        name="retention",
    )(proj, proj, proj, proj, cos_t, sin_t, s0, g_row, b_row)


def _dsa_prep_kernel(k_ref, v_ref, sm_ref, gk_ref, gi_ref, ko_ref, vo_ref, kio_ref):
    k = k_ref[...]
    gk = gk_ref[...]
    for h in range(N_HEADS):
        sl = slice(h * HEAD_DIM, (h + 1) * HEAD_DIM)
        kh = k[:, sl]
        ko_ref[:, sl] = kh * lax.rsqrt(jnp.mean(kh * kh, axis=-1, keepdims=True) + EPS) * gk
    vo_ref[...] = v_ref[...]
    ki = sm_ref[...][:, LANE_KI:LANE_KI + IDX_DIM]
    kio_ref[...] = ki * lax.rsqrt(jnp.mean(ki * ki, axis=-1, keepdims=True) + EPS) * gi_ref[...]


def _dsa_prep(proj, small, gk_row, gi_row):
    n = proj.shape[0]
    tm = min(512, n)
    gw = GROUP_WIDTH
    return pl.pallas_call(
        _dsa_prep_kernel,
        grid=(n // tm,),
        in_specs=[
            pl.BlockSpec((tm, gw), lambda i: (i, COL_KC // gw)),
            pl.BlockSpec((tm, gw), lambda i: (i, COL_VC // gw)),
            pl.BlockSpec((tm, SMALL_COLS), lambda i: (i, 0)),
            pl.BlockSpec((1, HEAD_DIM), lambda i: (0, 0)),
            pl.BlockSpec((1, IDX_DIM), lambda i: (0, 0)),
        ],
        out_specs=[pl.BlockSpec((tm, gw), lambda i: (i, 0)),
                   pl.BlockSpec((tm, gw), lambda i: (i, 0)),
                   pl.BlockSpec((tm, IDX_DIM), lambda i: (i, 0))],
        out_shape=[jax.ShapeDtypeStruct((n, gw), F32),
                   jax.ShapeDtypeStruct((n, gw), F32),
                   jax.ShapeDtypeStruct((n, IDX_DIM), F32)],
        compiler_params=_cparams(("parallel",)),
        name="dsa_prep",
    )(proj, proj, small, gk_row, gi_row)


KT_I = 128
KT_A = 512


def _dsa_kernel(q_ref, qi_ref, z_ref, sm_ref, k_ref, vt_ref, ki_ref, gq_ref, o_ref,
                key_ref, qit_ref, wt_ref, jlim_ref, qt_ref, acc_ref, lg_ref, p_ref, *, tq, offset, s_valid, topk):
    i = pl.program_id(1)
    pos0 = offset + i * tq
    t_pos = pos0 + lax.broadcasted_iota(I32, (1, tq), 1)
    t_chunk = t_pos // CHUNK
    n_adm_row = jnp.minimum((t_chunk + 1) * CHUNK, s_valid)
    n_keys = jnp.minimum(((pos0 + tq - 1) // CHUNK + 1) * CHUNK, s_valid)
    n_it = (n_keys + KT_I - 1) // KT_I
    n_at = (n_keys + KT_A - 1) // KT_A
    n_it_full = n_at * (KT_A // KT_I)

    qit_ref[...] = qi_ref[...].T.astype(BF16)
    wt_ref[...] = sm_ref[...].T * (IDX_HEADS ** -0.5 * IDX_DIM ** -0.5)

    def index_step(kt, carry):
        r0 = pl.multiple_of(kt * KT_I, KT_I)
        ki_t = ki_ref[0, pl.ds(r0, KT_I), :]
        acc = jnp.zeros((KT_I, tq), F32)
        for h in range(IDX_HEADS):
            sc = _dot(ki_t, qit_ref[h * IDX_DIM:(h + 1) * IDX_DIM, :])
            acc = acc + wt_ref[LANE_WI + h:LANE_WI + h + 1, :] * jnp.maximum(sc, 0.0)
        bits = lax.bitcast_convert_type(acc, I32)
        key = bits ^ ((bits >> 31) & 0x7FFFFFFF)
        s_pos = r0 + lax.broadcasted_iota(I32, (KT_I, 1), 0)
        adm = (s_pos // CHUNK <= t_chunk) & (s_pos < s_valid)
        key_ref[pl.ds(r0, KT_I), :] = jnp.where(adm, key, INT_MIN)
        return carry

    lax.fori_loop(0, n_it_full // 2, lambda t, c: index_step(2 * t + 1, index_step(2 * t, c)), 0)

    def count(pred_fn):
        def body(kt, acc):
            r0 = pl.multiple_of(kt * KT_I, KT_I)
            key = key_ref[pl.ds(r0, KT_I), :]
            s_pos = r0 + lax.broadcasted_iota(I32, (KT_I, 1), 0)
            m = jnp.where(pred_fn(key, s_pos), 1, 0).astype(I32)
            return acc + jnp.sum(m.reshape(KT_I // 8, 8, tq), axis=0)
        acc = lax.fori_loop(0, n_it, body, jnp.zeros((8, tq), I32))
        return jnp.sum(acc, axis=0, keepdims=True)

    small = jnp.where(n_adm_row <= topk, 1, 0).astype(I32)

    def bit_cond(carry):
        b, _, done, _ = carry
        return (b < 32) & (jnp.min(done) == 0)

    def bit_step(carry):
        b, v, done, thr = carry
        cand_u = v | jnp.left_shift(jnp.int32(1), 31 - b)
        cand_s = cand_u ^ INT_MIN
        cnt = count(lambda key, s_pos: key >= cand_s)
        v = jnp.where(cnt >= topk, cand_u, v)
        newly = (cnt == topk) & (done == 0)
        thr = jnp.where(newly, cand_s, thr)
        return b + 1, v, jnp.where(newly, 1, done), thr

    v0 = jnp.zeros((1, tq), I32)
    thr0 = jnp.full((1, tq), INT_MIN + 1, I32)
    _, v_u, done_i, thr = lax.while_loop(bit_cond, bit_step, (jnp.int32(0), v0, small, thr0))
    done = done_i != 0
    v_s = v_u ^ INT_MIN

    jlim_ref[...] = jnp.zeros(jlim_ref.shape, I32)

    @pl.when(jnp.min(done_i) == 0)
    def _():
        n_gt = count(lambda key, s_pos: key > v_s)
        need = topk - n_gt
        pos_bits = max(int(math.ceil(math.log2(max(k_ref.shape[1], 2)))), 1) + 1

        def pos_step(b, jv):
            cand = jv | jnp.left_shift(jnp.int32(1), pos_bits - 1 - b)
            cnt = count(lambda key, s_pos: (key == v_s) & (s_pos < cand))
            return jnp.where(cnt <= need, cand, jv)

        jlim_ref[0:1, :] = lax.fori_loop(0, pos_bits, pos_step, jnp.zeros((1, tq), I32))

    j_lim = jlim_ref[0:1, :]

    v_eff = jnp.where(done, thr - 1, v_s)
    j_eff = jnp.where(done, 0, j_lim)

    def bias_step(kt, carry):
        r0 = pl.multiple_of(kt * KT_I, KT_I)
        key = key_ref[pl.ds(r0, KT_I), :]
        s_pos = r0 + lax.broadcasted_iota(I32, (KT_I, 1), 0)
        sel = ((key > v_eff) | ((key == v_eff) & (s_pos < j_eff))) & (key != INT_MIN)
        bias = jnp.where(sel, 0.0, NEG_BIG).astype(F32)
        key_ref[pl.ds(r0, KT_I), :] = lax.bitcast_convert_type(bias, I32)
        return carry

    lax.fori_loop(0, n_it_full, bias_step, 0)

    gq = gq_ref[...]
    q_all = q_ref[...]
    for h in range(N_HEADS):
        sl = slice(h * HEAD_DIM, (h + 1) * HEAD_DIM)
        qh = q_all[:, sl]
        qh = qh * lax.rsqrt(jnp.mean(qh * qh, axis=-1, keepdims=True) + EPS) * gq * (HEAD_DIM ** -0.5 * LOG2_E)
        qt_ref[sl, :] = qh.T.astype(BF16)
    acc_ref[...] = jnp.zeros(acc_ref.shape, F32)
    n_sub = KT_A // KT_I

    def att_step(kt, carry):
        ms, ls = carry
        r0 = pl.multiple_of(kt * KT_A, KT_A)
        for h in range(N_HEADS):
            sl = slice(h * HEAD_DIM, (h + 1) * HEAD_DIM)
            lg_ref[h] = _dot(k_ref[0, pl.ds(r0, KT_A), sl], qt_ref[sl, :])
        new_ms, new_ls, alphas = [], [], []
        for h in range(N_HEADS):
            mx = None
            for s in range(n_sub):
                rs = slice(s * KT_I, (s + 1) * KT_I)
                bias = lax.bitcast_convert_type(key_ref[pl.ds(r0 + s * KT_I, KT_I), :], F32)
                lg = lg_ref[h, rs, :] + bias
                lg_ref[h, rs, :] = lg
                part = jnp.max(lg.reshape(KT_I // 8, 8, tq), axis=0)
                mx = part if mx is None else jnp.maximum(mx, part)
            m_new = jnp.maximum(ms[h], jnp.max(mx, axis=0, keepdims=True))
            alphas.append(jnp.exp2(ms[h] - m_new))
            new_ms.append(m_new)
        for h in range(N_HEADS):
            lsum = None
            for s in range(n_sub):
                rs = slice(s * KT_I, (s + 1) * KT_I)
                p = jnp.exp2(lg_ref[h, rs, :] - new_ms[h])
                p_ref[h, rs, :] = p.astype(BF16)
                part = jnp.sum(p.reshape(KT_I // 8, 8, tq), axis=0)
                lsum = part if lsum is None else lsum + part
            new_ls.append(alphas[h] * ls[h] + jnp.sum(lsum, axis=0, keepdims=True))
        for h in range(N_HEADS):
            sl = slice(h * HEAD_DIM, (h + 1) * HEAD_DIM)
            acc_ref[h] = alphas[h] * acc_ref[h] + _dot(vt_ref[0, kt, sl, :], p_ref[h])
        return tuple(new_ms), tuple(new_ls)

    m0 = tuple(jnp.full((1, tq), NEG_BIG, F32) for _ in range(N_HEADS))
    l0 = tuple(jnp.zeros((1, tq), F32) for _ in range(N_HEADS))
    _, ls = lax.fori_loop(0, n_at, att_step, (m0, l0))
    z_all = z_ref[...]
    for h in range(N_HEADS):
        sl = slice(h * HEAD_DIM, (h + 1) * HEAD_DIM)
        oh = (acc_ref[h] / ls[h]).T
        o_ref[:, sl] = (oh * _silu(z_all[:, sl])).astype(o_ref.dtype)


def _dsa(proj, small, k_all16, vt16, ki16, gq_row, batch, q_len, tq, offset, s_valid):
    n = batch * q_len
    nq = q_len // tq
    s_pad = k_all16.shape[1]
    topk = min(TOPK_MAX, s_valid // 4)
    gw = GROUP_WIDTH
    qi_w = IDX_HEADS * IDX_DIM
    kern = functools.partial(_dsa_kernel, tq=tq, offset=offset, s_valid=s_valid, topk=topk)
    once = pl.Buffered(1)
    return pl.pallas_call(
        kern,
        grid=(batch, nq),
        in_specs=[
            pl.BlockSpec((tq, gw), lambda b, i: (b * nq + i, COL_QC // gw)),
            pl.BlockSpec((tq, qi_w), lambda b, i: (b * nq + i, COL_QI // qi_w)),
            pl.BlockSpec((tq, gw), lambda b, i: (b * nq + i, COL_ZC // gw)),
            pl.BlockSpec((tq, SMALL_COLS), lambda b, i: (b * nq + i, 0)),
            pl.BlockSpec((1, s_pad, gw), lambda b, i: (b, 0, 0), pipeline_mode=once),
            pl.BlockSpec((1, s_pad // KT_A, gw, KT_A), lambda b, i: (b, 0, 0, 0), pipeline_mode=once),
            pl.BlockSpec((1, s_pad, IDX_DIM), lambda b, i: (b, 0, 0), pipeline_mode=once),
            pl.BlockSpec((1, HEAD_DIM), lambda b, i: (0, 0)),
        ],
        out_specs=pl.BlockSpec((tq, gw), lambda b, i: (b * nq + i, 0)),
        out_shape=jax.ShapeDtypeStruct((n, gw), BF16),
        scratch_shapes=[pltpu.VMEM((s_pad, tq), I32),
                        pltpu.VMEM((qi_w, tq), BF16),
                        pltpu.VMEM((SMALL_COLS, tq), F32),
                        pltpu.VMEM((8, tq), I32),
                        pltpu.VMEM((gw, tq), BF16),
                        pltpu.VMEM((N_HEADS, HEAD_DIM, tq), F32),
                        pltpu.VMEM((N_HEADS, KT_A, tq), F32),
                        pltpu.VMEM((N_HEADS, KT_A, tq), BF16)],
        compiler_params=_cparams(("parallel", "arbitrary")),
        name="dsa",
    )(proj, proj, proj, small, k_all16, vt16, ki16, gq_row)


def _mem_kernel(q_ref, z_ref, mk_ref, mv_ref, gq_ref, o_ref):
    gq = gq_ref[...]
    q_all, z_all = q_ref[...], z_ref[...]
    mk = mk_ref[0].astype(BF16)
    mv = mv_ref[0].astype(BF16)
    for h in range(N_HEADS):
        sl = slice(h * HEAD_DIM, (h + 1) * HEAD_DIM)
        qh = q_all[:, sl]
        qh = qh * lax.rsqrt(jnp.mean(qh * qh, axis=-1, keepdims=True) + EPS) * gq * (HEAD_DIM ** -0.5)
        logit = _dot_nt(qh.astype(BF16), mk[:, sl])
        m = jnp.max(logit, axis=-1, keepdims=True)
        p = jnp.exp(logit - m)
        l = jnp.sum(p, axis=-1, keepdims=True)
        oh = _dot(p.astype(BF16), mv[:, sl]) / l
        o_ref[:, sl] = (oh * _silu(z_all[:, sl])).astype(o_ref.dtype)


def _mem_attend(proj, mk, mv, gq_row, batch, seqlen):
    n = batch * seqlen
    tm = min(512, seqlen)
    nj = seqlen // tm
    gw = GROUP_WIDTH
    n_mem = mk.shape[1]
    return pl.pallas_call(
        _mem_kernel,
        grid=(batch, nj),
        in_specs=[
            pl.BlockSpec((tm, gw), lambda b, j: (b * nj + j, COL_QD // gw)),
            pl.BlockSpec((tm, gw), lambda b, j: (b * nj + j, COL_ZD // gw)),
            pl.BlockSpec((1, n_mem, gw), lambda b, j: (b, 0, 0)),
            pl.BlockSpec((1, n_mem, gw), lambda b, j: (b, 0, 0)),
            pl.BlockSpec((1, HEAD_DIM), lambda b, j: (0, 0)),
        ],
        out_specs=pl.BlockSpec((tm, gw), lambda b, j: (b * nj + j, 0)),
        out_shape=jax.ShapeDtypeStruct((n, gw), BF16),
        compiler_params=_cparams(("parallel", "arbitrary")),
        name="mem_attend",
    )(proj, proj, mk, mv, gq_row)


def _outproj_kernel(x_ref, a_ref, b_ref, c_ref, d_ref, w_ref, y_ref):
    gw = GROUP_WIDTH
    acc = x_ref[...] + _dot(a_ref[...], w_ref[0:gw, :])
    acc = acc + _dot(b_ref[...], w_ref[gw:2 * gw, :])
    acc = acc + _dot(c_ref[...], w_ref[2 * gw:3 * gw, :])
    acc = acc + _dot(d_ref[...], w_ref[3 * gw:4 * gw, :])
    y_ref[...] = acc


def _out_proj(x2d, oa, ob, oc, od, w16):
    n, d = x2d.shape
    tm = min(512, n)
    gw = GROUP_WIDTH
    grp = pl.BlockSpec((tm, gw), lambda i: (i, 0))
    return pl.pallas_call(
        _outproj_kernel,
        grid=(n // tm,),
        in_specs=[pl.BlockSpec((tm, d), lambda i: (i, 0)), grp, grp, grp, grp,
                  pl.BlockSpec((4 * gw, d), lambda i: (0, 0))],
        out_specs=pl.BlockSpec((tm, d), lambda i: (i, 0)),
        out_shape=jax.ShapeDtypeStruct((n, d), F32),
        compiler_params=_cparams(("parallel",)),
        name="out_proj",
    )(x2d, oa, ob, oc, od, w16)


def _reorder_w_in(w):
    gw = GROUP_WIDTH
    o = 0
    seg = {}
    for name, width in (("qa", gw), ("ka", gw), ("va", gw), ("za", gw), ("ba", N_HEADS), ("aa", N_HEADS),
                        ("qb", gw), ("kb", gw), ("vb", gw), ("zb", gw),
                        ("qc", gw), ("kc", gw), ("vc", gw), ("zc", gw),
                        ("qi", IDX_HEADS * IDX_DIM), ("ki", IDX_DIM), ("wi", IDX_HEADS),
                        ("qd", gw), ("zd", gw)):
        seg[name] = w[:, o:o + width]
        o += width
    main = jnp.concatenate([seg[k] for k in ("qa", "ka", "va", "qb", "kb", "vb", "za", "zb",
                                             "qc", "kc", "vc", "zc", "qi", "qd", "zd")], axis=1)
    pad = jnp.zeros((w.shape[0], SMALL_COLS - IDX_DIM - IDX_HEADS - 2 * N_HEADS), w.dtype)
    small = jnp.concatenate([seg["ki"], seg["wi"], seg["ba"], seg["aa"], pad], axis=1)
    return main.astype(BF16), small.astype(BF16)


def _lane_row(vals, lane0):
    row = jnp.zeros((1, SMALL_COLS), F32)
    return row.at[0, lane0:lane0 + vals.shape[0]].set(vals.astype(F32))


def _rope_tables(pos):
    half = HEAD_DIM // 2
    inv = ROPE_THETA ** (-jnp.arange(half, dtype=F32) / half)
    ang = pos.astype(F32)[:, None] * inv[None, :]
    cos, sin = jnp.cos(ang), jnp.sin(ang)
    return jnp.concatenate([cos, cos], axis=-1), jnp.concatenate([-sin, sin], axis=-1)


def _round_up(x, m):
    return (x + m - 1) // m * m


def _mixer_layer(x, conv_buf, s_gdn, s_ret, past_k, past_v, past_ki, mem_k, mem_v, wts):
    b, l, d = x.shape
    n = b * l
    offset = 0 if past_k is None else past_k.shape[1]
    x2d = x.reshape(n, d)
    proj, small = _in_proj(x2d, wts["norm_g"], wts["w_main"], wts["w_small"])

    o_a, conv_new, s_gdn_new = _gdn(proj, small, wts["conv_w"], conv_buf, s_gdn, wts["alog_row"],
                                    wts["dtb_row"], wts["gdn_norm_g"], b, l)

    cos_t, sin_t = _rope_tables(offset + jnp.arange(l, dtype=I32))
    o_b, s_ret_new = _retention(proj, cos_t, sin_t, s_ret, wts["ret_norm_g"], wts["ret_norm_b"], b, l)

    kc, vc, ki = _dsa_prep(proj, small, wts["dsa_k_norm_g"], wts["idx_k_norm_g"])
    kc3, vc3, ki3 = kc.reshape(b, l, GROUP_WIDTH), vc.reshape(b, l, GROUP_WIDTH), ki.reshape(b, l, IDX_DIM)
    if past_k is None:
        k_all, v_all, ki_all = kc3, vc3, ki3
    else:
        k_all = jnp.concatenate([past_k.reshape(b, offset, GROUP_WIDTH), kc3], axis=1)
        v_all = jnp.concatenate([past_v.reshape(b, offset, GROUP_WIDTH), vc3], axis=1)
        ki_all = jnp.concatenate([past_ki, ki3], axis=1)
    s_valid = k_all.shape[1]
    s_pad = _round_up(s_valid, KT_A)
    padw = ((0, 0), (0, s_pad - s_valid), (0, 0))
    k16 = jnp.pad(k_all.astype(BF16), padw)
    v16 = jnp.pad(v_all.astype(BF16), padw)
    ki16 = jnp.pad(ki_all.astype(BF16), padw)
    vt16 = v16.reshape(b, s_pad // KT_A, KT_A, GROUP_WIDTH).transpose(0, 1, 3, 2)

    tq = 256 if l % 256 == 0 else 128
    q_len = _round_up(l, tq)
    if q_len != l:
        proj_q = jnp.pad(proj.reshape(b, l, MAIN_COLS), ((0, 0), (0, q_len - l), (0, 0))).reshape(b * q_len, MAIN_COLS)
        small_q = jnp.pad(small.reshape(b, l, SMALL_COLS), ((0, 0), (0, q_len - l), (0, 0))).reshape(b * q_len, SMALL_COLS)
    else:
        proj_q, small_q = proj, small
    o_c = _dsa(proj_q, small_q, k16, vt16, ki16, wts["dsa_q_norm_g"], b, q_len, tq, offset, s_valid)
    if q_len != l:
        o_c = o_c.reshape(b, q_len, GROUP_WIDTH)[:, :l].reshape(n, GROUP_WIDTH)

    o_d = _mem_attend(proj, mem_k, mem_v, wts["mem_q_norm_g"], b, l)

    y = _out_proj(x2d, o_a, o_b, o_c, o_d, wts["w_out"]).reshape(b, l, d)
    new = (conv_new, s_gdn_new, s_ret_new,
           kc.reshape(b, l, N_HEADS, HEAD_DIM), vc.reshape(b, l, N_HEADS, HEAD_DIM), ki3)
    return y, new


def kernel(x_prompt, x_sample, cache_gdn_conv, state_gdn, state_ret, cache_dsa_k, cache_dsa_v, cache_idx_k, cache_mem_k, cache_mem_v, mem_prompt, norm_g, w_in, gdn_conv_w, gdn_a_log, gdn_dt_bias, gdn_norm_g, ret_norm_g, ret_norm_b, dsa_q_norm_g, dsa_k_norm_g, idx_k_norm_g, mem_norm_g, w_mem_kv, mem_q_norm_g, mem_k_norm_g, w_out):
    depth = w_in.shape[0]
    b = x_prompt.shape[0]
    n_mem = mem_prompt.shape[1]
    d = x_prompt.shape[-1]
    y_p, y_s = x_prompt, x_sample
    st_p, st_s, mem_p = [], [], []
    for li in range(depth):
        w_main, w_small = _reorder_w_in(w_in[li])
        wts = dict(
            norm_g=norm_g[li][None, :], w_main=w_main, w_small=w_small, conv_w=gdn_conv_w[li],
            alog_row=_lane_row(gdn_a_log[li], LANE_ALPHA), dtb_row=_lane_row(gdn_dt_bias[li], LANE_ALPHA),
            gdn_norm_g=gdn_norm_g[li][None, :], ret_norm_g=ret_norm_g[li][None, :],
            ret_norm_b=ret_norm_b[li][None, :], dsa_q_norm_g=dsa_q_norm_g[li][None, :],
            dsa_k_norm_g=dsa_k_norm_g[li][None, :], idx_k_norm_g=idx_k_norm_g[li][None, :],
            mem_q_norm_g=mem_q_norm_g[li][None, :], w_out=w_out[li].astype(BF16),
        )
        mk, mv = _memory_kv(mem_prompt.reshape(b * n_mem, d), mem_norm_g[li][None, :],
                            w_mem_kv[li].astype(BF16), mem_k_norm_g[li][None, :])
        mk = mk.reshape(b, n_mem, GROUP_WIDTH)
        mv = mv.reshape(b, n_mem, GROUP_WIDTH)
        conv0 = jnp.zeros((b, CONV_W - 1, 3 * GROUP_WIDTH), F32)
        s0 = jnp.zeros((b, N_HEADS, HEAD_DIM, HEAD_DIM), F32)
        y_p, sp = _mixer_layer(y_p, conv0, s0, s0, None, None, None, mk, mv, wts)
        st_p.append(sp)
        mem_p.append((mk.reshape(b, n_mem, N_HEADS, HEAD_DIM), mv.reshape(b, n_mem, N_HEADS, HEAD_DIM)))
        bs = x_sample.shape[0]
        y_s, ss = _mixer_layer(y_s, cache_gdn_conv[li], state_gdn[li], state_ret[li],
                               cache_dsa_k[li], cache_dsa_v[li], cache_idx_k[li],
                               cache_mem_k[li].reshape(bs, n_mem, GROUP_WIDTH),
                               cache_mem_v[li].reshape(bs, n_mem, GROUP_WIDTH), wts)
        st_s.append(ss)

    def stack(lst, k):
        return jnp.stack([s[k] for s in lst])

    return (y_p, y_s,
            stack(st_p, 0), stack(st_p, 1), stack(st_p, 2), stack(st_p, 3), stack(st_p, 4), stack(st_p, 5),
            stack(mem_p, 0), stack(mem_p, 1),
            stack(st_s, 0), stack(st_s, 1), stack(st_s, 2), stack(st_s, 3), stack(st_s, 4), stack(st_s, 5))
```

```python
import functools
import math

import jax
import jax.numpy as jnp
from jax import lax
from jax.experimental import pallas as pl
from jax.experimental.pallas import tpu as pltpu

F32 = jnp.float32
BF16 = jnp.bfloat16
I32 = jnp.int32

HEAD_DIM = 128
N_HEADS = 4
GROUP_WIDTH = N_HEADS * HEAD_DIM
CHUNK = 64
CONV_W = 4
IDX_HEADS = 16
IDX_DIM = 64
TOPK_MAX = 256
ROPE_THETA = 10000.0
EPS = 1e-6

COL_QA, COL_KA, COL_VA, COL_ZA = 0, 512, 1024, 1536
COL_QB, COL_KB, COL_VB, COL_ZB = 2048, 2560, 3072, 3584
COL_QC, COL_KC, COL_VC, COL_ZC = 4096, 4608, 5120, 5632
COL_QI = 6144
COL_QD, COL_ZD = 7168, 7680
MAIN_COLS = 8192
LANE_KI, LANE_WI, LANE_BETA, LANE_ALPHA = 0, 64, 80, 84
SMALL_COLS = 128

INT_MIN = -2 ** 31
NEG_BIG = -1e30
LOG2_E = 1.4426950408889634
VMEM_LIMIT = 56 * 1024 * 1024
HI = lax.Precision.HIGHEST


def _cparams(sem):
    return pltpu.CompilerParams(dimension_semantics=sem, vmem_limit_bytes=VMEM_LIMIT)


def _dot(a, b):
    return jnp.dot(a, b, preferred_element_type=F32)


def _dot_nt(a, b):
    return lax.dot_general(a, b, (((1,), (1,)), ((), ())), preferred_element_type=F32)


def _dot3(a, b):
    a_hi = a.astype(BF16)
    b_hi = b.astype(BF16)
    a_lo = (a - a_hi.astype(F32)).astype(BF16)
    b_lo = (b - b_hi.astype(F32)).astype(BF16)
    return _dot(a_hi, b_hi) + (_dot(a_hi, b_lo) + _dot(a_lo, b_hi))


def _silu(x):
    return x * jax.nn.sigmoid(x)


def _inproj_kernel(x_ref, g_ref, w_ref, ws_ref, o_ref, os_ref, h_ref):
    @pl.when(pl.program_id(1) == 0)
    def _():
        x = x_ref[...]
        y = x * lax.rsqrt(jnp.mean(x * x, axis=-1, keepdims=True) + EPS) * g_ref[...]
        hb = y.astype(BF16)
        h_ref[...] = hb
        os_ref[...] = _dot(hb, ws_ref[...])

    o_ref[...] = _dot(h_ref[...], w_ref[...])


def _in_proj(x2d, g_row, w_main, w_small):
    n, d = x2d.shape
    tm = min(512, n)
    tn = 1024
    return pl.pallas_call(
        _inproj_kernel,
        grid=(n // tm, MAIN_COLS // tn),
        in_specs=[
            pl.BlockSpec((tm, d), lambda i, j: (i, 0)),
            pl.BlockSpec((1, d), lambda i, j: (0, 0)),
            pl.BlockSpec((d, tn), lambda i, j: (0, j)),
            pl.BlockSpec((d, SMALL_COLS), lambda i, j: (0, 0)),
        ],
        out_specs=[
            pl.BlockSpec((tm, tn), lambda i, j: (i, j)),
            pl.BlockSpec((tm, SMALL_COLS), lambda i, j: (i, 0)),
        ],
        out_shape=[jax.ShapeDtypeStruct((n, MAIN_COLS), F32),
                   jax.ShapeDtypeStruct((n, SMALL_COLS), F32)],
        scratch_shapes=[pltpu.VMEM((tm, d), BF16)],
        compiler_params=_cparams(("parallel", "arbitrary")),
        name="in_proj",
    )(x2d, g_row, w_main, w_small)


def _memkv_kernel(x_ref, g_ref, w_ref, gk_ref, mk_ref, mv_ref):
    x = x_ref[...]
    y = x * lax.rsqrt(jnp.mean(x * x, axis=-1, keepdims=True) + EPS) * g_ref[...]
    kv = _dot(y.astype(BF16), w_ref[...])
    gk = gk_ref[...]
    for h in range(N_HEADS):
        sl = slice(h * HEAD_DIM, (h + 1) * HEAD_DIM)
        kh = kv[:, sl]
        mk_ref[:, sl] = kh * lax.rsqrt(jnp.mean(kh * kh, axis=-1, keepdims=True) + EPS) * gk
    mv_ref[...] = kv[:, GROUP_WIDTH:]


def _memory_kv(mem2d, g_row, w_kv, gk_row):
    n, d = mem2d.shape
    tm = min(256, n)
    return pl.pallas_call(
        _memkv_kernel,
        grid=(n // tm,),
        in_specs=[
            pl.BlockSpec((tm, d), lambda i: (i, 0)),
            pl.BlockSpec((1, d), lambda i: (0, 0)),
            pl.BlockSpec((d, 2 * GROUP_WIDTH), lambda i: (0, 0)),
            pl.BlockSpec((1, HEAD_DIM), lambda i: (0, 0)),
        ],
        out_specs=[pl.BlockSpec((tm, GROUP_WIDTH), lambda i: (i, 0)),
                   pl.BlockSpec((tm, GROUP_WIDTH), lambda i: (i, 0))],
        out_shape=[jax.ShapeDtypeStruct((n, GROUP_WIDTH), F32),
                   jax.ShapeDtypeStruct((n, GROUP_WIDTH), F32)],
        compiler_params=_cparams(("parallel",)),
        name="memory_kv",
    )(mem2d, g_row, w_kv, gk_row)


def _gdn_kernel(qkv_ref, z_ref, sm_ref, cw_ref, cb_ref, s0_ref, alog_ref, dtb_ref, gn_ref,
                o_ref, conv_ref, st_ref, xbuf_ref, s_ref, *, t_blk, chunk):
    j = pl.program_id(1)
    nj = pl.num_programs(1)
    gw3 = 3 * GROUP_WIDTH

    @pl.when(j == 0)
    def _():
        xbuf_ref[0:8, :] = jnp.zeros((8, gw3), F32)
        xbuf_ref[5:8, :] = cb_ref[0]
        s_ref[...] = s0_ref[0]

    @pl.when(j > 0)
    def _():
        xbuf_ref[0:8, :] = xbuf_ref[t_blk:t_blk + 8, :]

    xbuf_ref[8:8 + t_blk, :] = qkv_ref[...]
    conv_ref[0] = xbuf_ref[t_blk + 5:t_blk + 8, :]

    cw = cw_ref[...]
    y = xbuf_ref[5:5 + t_blk, :] * cw[0:1, :]
    for jj in range(1, CONV_W):
        y = y + xbuf_ref[5 + jj:5 + jj + t_blk, :] * cw[jj:jj + 1, :]
    y = _silu(y)

    sm = sm_ref[...]
    lane = lax.broadcasted_iota(I32, sm.shape, 1)
    beta_all = jax.nn.sigmoid(sm)
    xs = sm + dtb_ref[...]
    softplus = jnp.maximum(xs, 0.0) + jnp.log1p(jnp.exp(-jnp.abs(xs)))
    g_all = -jnp.exp(alog_ref[...]) * softplus
    g_all = jnp.where((lane >= LANE_ALPHA) & (lane < LANE_ALPHA + N_HEADS), g_all, 0.0)
    ri = lax.broadcasted_iota(I32, (t_blk, t_blk), 0)
    ci = lax.broadcasted_iota(I32, (t_blk, t_blk), 1)
    tri = jnp.where((ri // chunk == ci // chunk) & (ci <= ri), 1.0, 0.0).astype(F32)
    gcum = jnp.dot(tri, g_all, preferred_element_type=F32, precision=HI)
    gcum_t = gcum.T

    blk = min(2 * chunk, t_blk)
    n_blk = t_blk // blk
    cpb = blk // chunk
    rb = lax.broadcasted_iota(I32, (blk, blk), 0)
    cb = lax.broadcasted_iota(I32, (blk, blk), 1)
    same = (rb // chunk) == (cb // chunk)
    causal = same & (cb <= rb)
    strict = same & (cb < rb)
    eye = jnp.where(rb == cb, 1.0, 0.0).astype(F32)
    n_dbl = max(int(math.log2(chunk)) - 1, 0)
    gn = gn_ref[...]
    z_all = z_ref[...]

    units = [(h, b) for h in range(N_HEADS) for b in range(n_blk)]
    qs, ks, gcols, p_mats, x_mats, qk_mats, vb_mats, kg_mats = {}, {}, {}, {}, {}, {}, {}, {}
    for h in range(N_HEADS):
        qh = y[:, COL_QA + h * HEAD_DIM:COL_QA + (h + 1) * HEAD_DIM]
        kh = y[:, COL_KA + h * HEAD_DIM:COL_KA + (h + 1) * HEAD_DIM]
        vh = y[:, COL_VA + h * HEAD_DIM:COL_VA + (h + 1) * HEAD_DIM]
        qh = qh * lax.rsqrt(jnp.sum(qh * qh, axis=-1, keepdims=True) + EPS) * (HEAD_DIM ** -0.5)
        kh = kh * lax.rsqrt(jnp.sum(kh * kh, axis=-1, keepdims=True) + EPS)
        for b in range(n_blk):
            rs = slice(b * blk, (b + 1) * blk)
            gcol = gcum[rs, LANE_ALPHA + h:LANE_ALPHA + h + 1]
            grow = gcum_t[LANE_ALPHA + h:LANE_ALPHA + h + 1, rs]
            bcol = beta_all[rs, LANE_BETA + h:LANE_BETA + h + 1]
            decay = jnp.where(causal, jnp.exp(jnp.where(causal, gcol - grow, 0.0)), 0.0)
            qc, kc, vc = qh[rs], kh[rs], vh[rs]
            kb = kc * bcol
            kc16 = kc.astype(BF16)
            a_mat = jnp.where(strict, _dot_nt(kb.astype(BF16), kc16) * decay, 0.0)
            qk_mats[h, b] = jnp.where(causal, _dot_nt(qc.astype(BF16), kc16) * decay, 0.0).astype(BF16)
            p_mats[h, b] = -a_mat
            x_mats[h, b] = eye - a_mat
            vb_mats[h, b] = (vc * bcol).astype(BF16)
            kg_mats[h, b] = (kb * jnp.exp(gcol)).astype(BF16)
            qs[h, b], ks[h, b], gcols[h, b] = qc, kc, gcol

    for _ in range(n_dbl):
        for u_ in units:
            p_mats[u_] = _dot3(p_mats[u_], p_mats[u_])
        for u_ in units:
            x_mats[u_] = x_mats[u_] + _dot3(x_mats[u_], p_mats[u_])

    u_mats, w_mats = {}, {}
    for u_ in units:
        x16 = x_mats[u_].astype(BF16)
        u_mats[u_] = _dot(x16, vb_mats[u_])
        w_mats[u_] = _dot(x16, kg_mats[u_]).astype(BF16)

    states = [s_ref[h] for h in range(N_HEADS)]
    for b in range(n_blk):
        o_state = {h: [] for h in range(N_HEADS)}
        v_new = {h: [] for h in range(N_HEADS)}
        for c in range(cpb):
            cs = slice(c * chunk, (c + 1) * chunk)
            for h in range(N_HEADS):
                s = states[h]
                s16 = s.astype(BF16)
                gcol = gcols[h, b][cs]
                glast = gcols[h, b][(c + 1) * chunk - 1:(c + 1) * chunk]
                vn = u_mats[h, b][cs] - _dot(w_mats[h, b][cs], s16)
                vn16 = vn.astype(BF16)
                kd_t = (ks[h, b][cs] * jnp.exp(glast - gcol)).T.astype(BF16)
                states[h] = s * jnp.exp(glast) + _dot(kd_t, vn16)
                o_state[h].append(_dot((qs[h, b][cs] * jnp.exp(gcol)).astype(BF16), s16))
                v_new[h].append(vn16)
        for h in range(N_HEADS):
            sl = slice(h * HEAD_DIM, (h + 1) * HEAD_DIM)
            rs = slice(b * blk, (b + 1) * blk)
            o = jnp.concatenate(o_state[h], axis=0) + _dot(qk_mats[h, b], jnp.concatenate(v_new[h], axis=0))
            on = o * lax.rsqrt(jnp.mean(o * o, axis=-1, keepdims=True) + EPS) * gn
            o_ref[rs, sl] = (on * _silu(z_all[rs, sl])).astype(o_ref.dtype)
    for h in range(N_HEADS):
        s_ref[h] = states[h]

    @pl.when(j == nj - 1)
    def _():
        st_ref[0] = s_ref[...]


def _gdn(proj, small, conv_w, conv_buf, s0, alog_row, dtb_row, gn_row, batch, seqlen):
    n = batch * seqlen
    t_blk = min(256, seqlen)
    chunk = min(CHUNK, seqlen)
    nj = seqlen // t_blk
    gw3 = 3 * GROUP_WIDTH
    kern = functools.partial(_gdn_kernel, t_blk=t_blk, chunk=chunk)
    return pl.pallas_call(
        kern,
        grid=(batch, nj),
        in_specs=[
            pl.BlockSpec((t_blk, gw3), lambda b, j: (b * nj + j, COL_QA // gw3)),
            pl.BlockSpec((t_blk, GROUP_WIDTH), lambda b, j: (b * nj + j, COL_ZA // GROUP_WIDTH)),
            pl.BlockSpec((t_blk, SMALL_COLS), lambda b, j: (b * nj + j, 0)),
            pl.BlockSpec((CONV_W, gw3), lambda b, j: (0, 0)),
            pl.BlockSpec((1, CONV_W - 1, gw3), lambda b, j: (b, 0, 0)),
            pl.BlockSpec((1, N_HEADS, HEAD_DIM, HEAD_DIM), lambda b, j: (b, 0, 0, 0)),
            pl.BlockSpec((1, SMALL_COLS), lambda b, j: (0, 0)),
            pl.BlockSpec((1, SMALL_COLS), lambda b, j: (0, 0)),
            pl.BlockSpec((1, HEAD_DIM), lambda b, j: (0, 0)),
        ],
        out_specs=[
            pl.BlockSpec((t_blk, GROUP_WIDTH), lambda b, j: (b * nj + j, 0)),
            pl.BlockSpec((1, CONV_W - 1, gw3), lambda b, j: (b, 0, 0)),
            pl.BlockSpec((1, N_HEADS, HEAD_DIM, HEAD_DIM), lambda b, j: (b, 0, 0, 0)),
        ],
        out_shape=[
            jax.ShapeDtypeStruct((n, GROUP_WIDTH), BF16),
            jax.ShapeDtypeStruct((batch, CONV_W - 1, gw3), F32),
            jax.ShapeDtypeStruct((batch, N_HEADS, HEAD_DIM, HEAD_DIM), F32),
        ],
        scratch_shapes=[pltpu.VMEM((t_blk + 8, gw3), F32),
                        pltpu.VMEM((N_HEADS, HEAD_DIM, HEAD_DIM), F32)],
        compiler_params=_cparams(("parallel", "arbitrary")),
        name="gdn",
    )(proj, proj, small, conv_w, conv_buf, s0, alog_row, dtb_row, gn_row)


def _ret_kernel(q_ref, k_ref, v_ref, z_ref, cos_ref, sin_ref, s0_ref, g_ref, b_ref,
                o_ref, st_ref, s_ref, *, t_blk):
    j = pl.program_id(1)
    nj = pl.num_programs(1)

    @pl.when(j == 0)
    def _():
        s_ref[...] = s0_ref[0]

    cos = cos_ref[...]
    sin = sin_ref[...]
    ri = lax.broadcasted_iota(I32, (t_blk, t_blk), 0)
    ci = lax.broadcasted_iota(I32, (t_blk, t_blk), 1)
    causal = ci <= ri
    rel = jnp.where(causal, ri - ci, 0).astype(F32)
    idx_col = lax.broadcasted_iota(I32, (t_blk, 1), 0).astype(F32)
    gamma_g = g_ref[...]
    gamma_b = b_ref[...]
    q_all, k_all, v_all, z_all = q_ref[...], k_ref[...], v_ref[...], z_ref[...]

    for h in range(N_HEADS):
        sl = slice(h * HEAD_DIM, (h + 1) * HEAD_DIM)
        lg = math.log(1.0 - 2.0 ** (-5.0 - h))
        qh, kh, vh = q_all[:, sl], k_all[:, sl], v_all[:, sl]
        qh = qh * cos + pltpu.roll(qh, HEAD_DIM // 2, 1) * sin
        kh = (kh * cos + pltpu.roll(kh, HEAD_DIM // 2, 1) * sin) * (HEAD_DIM ** -0.5)
        d_mat = jnp.where(causal, jnp.exp(lg * rel), 0.0)
        q16, k16, v16 = qh.astype(BF16), kh.astype(BF16), vh.astype(BF16)
        o_intra = _dot((_dot_nt(q16, k16) * d_mat).astype(BF16), v16)
        s = s_ref[h]
        o_cross = _dot(q16, s.astype(BF16)) * jnp.exp(lg * (idx_col + 1.0))
        kd_t = (kh * jnp.exp(lg * (t_blk - 1.0 - idx_col))).T.astype(BF16)
        s_ref[h] = s * math.exp(lg * t_blk) + _dot(kd_t, v16)
        o = o_intra + o_cross
        mu = jnp.mean(o, axis=-1, keepdims=True)
        oc = o - mu
        var = jnp.mean(oc * oc, axis=-1, keepdims=True)
        on = oc * lax.rsqrt(var + EPS) * gamma_g + gamma_b
        o_ref[:, sl] = (on * _silu(z_all[:, sl])).astype(o_ref.dtype)

    @pl.when(j == nj - 1)
    def _():
        st_ref[0] = s_ref[...]


def _retention(proj, cos_t, sin_t, s0, g_row, b_row, batch, seqlen):
    n = batch * seqlen
    t_blk = min(256, seqlen)
    nj = seqlen // t_blk
    gw = GROUP_WIDTH
    kern = functools.partial(_ret_kernel, t_blk=t_blk)

    def col(c):
        return pl.BlockSpec((t_blk, gw), lambda b, j: (b * nj + j, c // gw))

    return pl.pallas_call(
        kern,
        grid=(batch, nj),
        in_specs=[
            col(COL_QB), col(COL_KB), col(COL_VB), col(COL_ZB),
            pl.BlockSpec((t_blk, HEAD_DIM), lambda b, j: (j, 0)),
            pl.BlockSpec((t_blk, HEAD_DIM), lambda b, j: (j, 0)),
            pl.BlockSpec((1, N_HEADS, HEAD_DIM, HEAD_DIM), lambda b, j: (b, 0, 0, 0)),
            pl.BlockSpec((1, HEAD_DIM), lambda b, j: (0, 0)),
            pl.BlockSpec((1, HEAD_DIM), lambda b, j: (0, 0)),
        ],
        out_specs=[
            pl.BlockSpec((t_blk, gw), lambda b, j: (b * nj + j, 0)),
            pl.BlockSpec((1, N_HEADS, HEAD_DIM, HEAD_DIM), lambda b, j: (b, 0, 0, 0)),
        ],
        out_shape=[
            jax.ShapeDtypeStruct((n, gw), BF16),
            jax.ShapeDtypeStruct((batch, N_HEADS, HEAD_DIM, HEAD_DIM), F32),
        ],
        scratch_shapes=[pltpu.VMEM((N_HEADS, HEAD_DIM, HEAD_DIM), F32)],
        compiler_params=_cparams(("parallel", "arbitrary")),
        name="retention",
    )(proj, proj, proj, proj, cos_t, sin_t, s0, g_row, b_row)


def _dsa_prep_kernel(k_ref, v_ref, sm_ref, gk_ref, gi_ref, *rest, n_prev, with_vt):
    outs = rest[n_prev:]
    ko_ref, vo_ref, kio_ref, k16_ref, ki16_ref = outs[:5]
    k = k_ref[...]
    gk = gk_ref[...]
    for h in range(N_HEADS):
        sl = slice(h * HEAD_DIM, (h + 1) * HEAD_DIM)
        kh = k[:, sl]
        kn = kh * lax.rsqrt(jnp.mean(kh * kh, axis=-1, keepdims=True) + EPS) * gk
        ko_ref[:, sl] = kn
        k16_ref[:, sl] = kn.astype(BF16)
    v = v_ref[...]
    vo_ref[...] = v
    if with_vt:
        outs[5][...] = v.T.astype(BF16)
    ki = sm_ref[...][:, LANE_KI:LANE_KI + IDX_DIM]
    kin = ki * lax.rsqrt(jnp.mean(ki * ki, axis=-1, keepdims=True) + EPS) * gi_ref[...]
    kio_ref[...] = kin
    ki16_ref[...] = kin.astype(BF16)


def _dsa_prep(proj, small, gk_row, gi_row, layer, depth, prev, with_vt):
    n = proj.shape[0]
    tm = min(KT_A, n)
    gw = GROUP_WIDTH
    n_prev = 0 if prev is None else 3
    kern = functools.partial(_dsa_prep_kernel, n_prev=n_prev, with_vt=with_vt)
    in_specs = [
        pl.BlockSpec((tm, gw), lambda i: (i, COL_KC // gw)),
        pl.BlockSpec((tm, gw), lambda i: (i, COL_VC // gw)),
        pl.BlockSpec((tm, SMALL_COLS), lambda i: (i, 0)),
        pl.BlockSpec((1, HEAD_DIM), lambda i: (0, 0)),
        pl.BlockSpec((1, IDX_DIM), lambda i: (0, 0)),
    ] + [pl.BlockSpec(memory_space=pl.ANY)] * n_prev
    out_specs = [pl.BlockSpec((None, tm, gw), lambda i: (layer, i, 0)),
                 pl.BlockSpec((None, tm, gw), lambda i: (layer, i, 0)),
                 pl.BlockSpec((None, tm, IDX_DIM), lambda i: (layer, i, 0)),
                 pl.BlockSpec((tm, gw), lambda i: (i, 0)),
                 pl.BlockSpec((tm, IDX_DIM), lambda i: (i, 0))]
    out_shape = [jax.ShapeDtypeStruct((depth, n, gw), F32),
                 jax.ShapeDtypeStruct((depth, n, gw), F32),
                 jax.ShapeDtypeStruct((depth, n, IDX_DIM), F32),
                 jax.ShapeDtypeStruct((n, gw), BF16),
                 jax.ShapeDtypeStruct((n, IDX_DIM), BF16)]
    if with_vt:
        out_specs.append(pl.BlockSpec((None, gw, tm), lambda i: (i, 0, 0)))
        out_shape.append(jax.ShapeDtypeStruct((n // tm, gw, tm), BF16))
    args = (proj, proj, small, gk_row, gi_row) + (() if prev is None else tuple(prev))
    return pl.pallas_call(
        kern,
        grid=(n // tm,),
        in_specs=in_specs,
        out_specs=out_specs,
        out_shape=out_shape,
        input_output_aliases={5 + t: t for t in range(n_prev)},
        compiler_params=_cparams(("parallel",)),
        name="dsa_prep",
    )(*args)


KT_I = 128
KT_A = 512


def _dsa_kernel(q_ref, qi_ref, z_ref, sm_ref, k_ref, vt_ref, ki_ref, gq_ref, o_ref,
                key_ref, qit_ref, wt_ref, jlim_ref, qt_ref, acc_ref, lg_ref, p_ref, key16_ref, *, tq, offset, s_valid, topk):
    i = pl.program_id(1)
    pos0 = offset + i * tq
    t_pos = pos0 + lax.broadcasted_iota(I32, (1, tq), 1)
    t_chunk = t_pos // CHUNK
    n_adm_row = jnp.minimum((t_chunk + 1) * CHUNK, s_valid)
    n_keys = jnp.minimum(((pos0 + tq - 1) // CHUNK + 1) * CHUNK, s_valid)
    n_at = (n_keys + KT_A - 1) // KT_A
    n_sub = KT_A // KT_I
    n_it_full = n_at * n_sub

    qit_ref[...] = qi_ref[...].T.astype(BF16)
    wt_ref[...] = sm_ref[...].T * (IDX_HEADS ** -0.5 * IDX_DIM ** -0.5)

    def index_step(kt, carry):
        r0 = pl.multiple_of(kt * KT_I, KT_I)
        ki_t = ki_ref[0, pl.ds(r0, KT_I), :]
        acc = jnp.zeros((KT_I, tq), F32)
        for h in range(IDX_HEADS):
            sc = _dot(ki_t, qit_ref[h * IDX_DIM:(h + 1) * IDX_DIM, :])
            acc = acc + wt_ref[LANE_WI + h:LANE_WI + h + 1, :] * jnp.maximum(sc, 0.0)
        bits = lax.bitcast_convert_type(acc, I32)
        key = bits ^ ((bits >> 31) & 0x7FFFFFFF)
        s_pos = r0 + lax.broadcasted_iota(I32, (KT_I, 1), 0)
        adm = (s_pos // CHUNK <= t_chunk) & (s_pos < s_valid)
        key = jnp.where(adm, key, INT_MIN)
        key_ref[pl.ds(r0, KT_I), :] = key
        key16_ref[pl.ds(r0, KT_I), :] = (key >> 16).astype(jnp.int16)
        return carry

    lax.fori_loop(0, n_it_full // 2, lambda t, c: index_step(2 * t + 1, index_step(2 * t, c)), 0)

    def count(pred_fn):
        def body(kt, accs):
            out = []
            for s in range(n_sub):
                r0 = pl.multiple_of(kt * KT_A + s * KT_I, KT_I)
                key = key_ref[pl.ds(r0, KT_I), :]
                s_pos = r0 + lax.broadcasted_iota(I32, (KT_I, 1), 0)
                m = jnp.where(pred_fn(key, s_pos), 1, 0).astype(I32)
                out.append(accs[s] + jnp.sum(m.reshape(KT_I // 8, 8, tq), axis=0))
            return tuple(out)
        accs = lax.fori_loop(0, n_at, body, tuple(jnp.zeros((8, tq), I32) for _ in range(n_sub)))
        return jnp.sum(sum(accs[1:], accs[0]), axis=0, keepdims=True)

    def count16(cand16):
        def body(kt, accs):
            out = []
            for s in range(n_sub):
                r0 = pl.multiple_of(kt * KT_A + s * KT_I, KT_I)
                m = jnp.where(key16_ref[pl.ds(r0, KT_I), :] >= cand16, jnp.int16(1), jnp.int16(0))
                m = m.reshape(KT_I // 16, 16, tq)
                acc = accs[s]
                for q in range(KT_I // 16):
                    acc = acc + m[q]
                out.append(acc)
            return tuple(out)
        accs = lax.fori_loop(0, n_at, body, tuple(jnp.zeros((16, tq), jnp.int16) for _ in range(n_sub)))
        tot = sum((a.astype(I32) for a in accs[1:]), accs[0].astype(I32))
        return jnp.sum(tot, axis=0, keepdims=True)

    small = jnp.where(n_adm_row <= topk, 1, 0).astype(I32)

    def bit_cond(carry):
        b, _, done, _ = carry
        return (b < 32) & (jnp.min(done) == 0)

    def bit_step(carry):
        b, v, done, thr = carry
        cand_u = v | jnp.left_shift(jnp.int32(1), 31 - b)
        cand_s = cand_u ^ INT_MIN
        cnt = lax.cond(b < 16,
                       lambda: count16((cand_s >> 16).astype(jnp.int16)),
                       lambda: count(lambda key, s_pos: key >= cand_s))
        v = jnp.where(cnt >= topk, cand_u, v)
        newly = (cnt == topk) & (done == 0)
        thr = jnp.where(newly, cand_s, thr)
        return b + 1, v, jnp.where(newly, 1, done), thr

    v0 = jnp.zeros((1, tq), I32)
    thr0 = jnp.full((1, tq), INT_MIN + 1, I32)
    _, v_u, done_i, thr = lax.while_loop(bit_cond, bit_step, (jnp.int32(0), v0, small, thr0))
    done = done_i != 0
    v_s = v_u ^ INT_MIN

    jlim_ref[...] = jnp.zeros(jlim_ref.shape, I32)

    @pl.when(jnp.min(done_i) == 0)
    def _():
        n_gt = count(lambda key, s_pos: key > v_s)
        need = topk - n_gt
        pos_bits = max(int(math.ceil(math.log2(max(k_ref.shape[1], 2)))), 1) + 1

        def pos_step(b, jv):
            cand = jv | jnp.left_shift(jnp.int32(1), pos_bits - 1 - b)
            cnt = count(lambda key, s_pos: (key == v_s) & (s_pos < cand))
            return jnp.where(cnt <= need, cand, jv)

        jlim_ref[0:1, :] = lax.fori_loop(0, pos_bits, pos_step, jnp.zeros((1, tq), I32))

    j_lim = jlim_ref[0:1, :]

    v_eff = jnp.where(done, thr - 1, v_s)
    j_eff = jnp.where(done, 0, j_lim)

    def bias_loop(with_ties):
        def bias_step(kt, carry):
            for s in range(n_sub):
                r0 = pl.multiple_of(kt * KT_A + s * KT_I, KT_I)
                key = key_ref[pl.ds(r0, KT_I), :]
                sel = key > v_eff
                if with_ties:
                    s_pos = r0 + lax.broadcasted_iota(I32, (KT_I, 1), 0)
                    sel = sel | ((key == v_eff) & (s_pos < j_eff) & (key != INT_MIN))
                bias = jnp.where(sel, 0.0, NEG_BIG).astype(F32)
                key_ref[pl.ds(r0, KT_I), :] = lax.bitcast_convert_type(bias, I32)
            return carry

        lax.fori_loop(0, n_at, bias_step, 0)

    any_tie = jnp.min(done_i) == 0
    pl.when(any_tie)(lambda: bias_loop(True))
    pl.when(jnp.logical_not(any_tie))(lambda: bias_loop(False))

    gq = gq_ref[...]
    q_all = q_ref[...]
    for h in range(N_HEADS):
        sl = slice(h * HEAD_DIM, (h + 1) * HEAD_DIM)
        qh = q_all[:, sl]
        qh = qh * lax.rsqrt(jnp.mean(qh * qh, axis=-1, keepdims=True) + EPS) * gq * (HEAD_DIM ** -0.5 * LOG2_E)
        qt_ref[sl, :] = qh.T.astype(BF16)
    acc_ref[...] = jnp.zeros(acc_ref.shape, F32)

    def att_step(kt, carry):
        ms, ls = carry
        r0 = pl.multiple_of(kt * KT_A, KT_A)
        for h in range(N_HEADS):
            sl = slice(h * HEAD_DIM, (h + 1) * HEAD_DIM)
            lg_ref[h] = _dot(k_ref[0, pl.ds(r0, KT_A), sl], qt_ref[sl, :])
        new_ms, new_ls, alphas = [], [], []
        for h in range(N_HEADS):
            mx = None
            for s in range(n_sub):
                rs = slice(s * KT_I, (s + 1) * KT_I)
                bias = lax.bitcast_convert_type(key_ref[pl.ds(r0 + s * KT_I, KT_I), :], F32)
                lg = lg_ref[h, rs, :] + bias
                lg_ref[h, rs, :] = lg
                part = jnp.max(lg.reshape(KT_I // 8, 8, tq), axis=0)
                mx = part if mx is None else jnp.maximum(mx, part)
            m_new = jnp.maximum(ms[h], jnp.max(mx, axis=0, keepdims=True))
            alphas.append(jnp.exp2(ms[h] - m_new))
            new_ms.append(m_new)
        for h in range(N_HEADS):
            lsum = None
            for s in range(n_sub):
                rs = slice(s * KT_I, (s + 1) * KT_I)
                p = jnp.exp2(lg_ref[h, rs, :] - new_ms[h])
                p_ref[h, rs, :] = p.astype(BF16)
                part = jnp.sum(p.reshape(KT_I // 8, 8, tq), axis=0)
                lsum = part if lsum is None else lsum + part
            new_ls.append(alphas[h] * ls[h] + jnp.sum(lsum, axis=0, keepdims=True))
        for h in range(N_HEADS):
            sl = slice(h * HEAD_DIM, (h + 1) * HEAD_DIM)
            acc_ref[h] = alphas[h] * acc_ref[h] + _dot(vt_ref[0, kt, sl, :], p_ref[h])
        return tuple(new_ms), tuple(new_ls)

    m0 = tuple(jnp.full((1, tq), NEG_BIG, F32) for _ in range(N_HEADS))
    l0 = tuple(jnp.zeros((1, tq), F32) for _ in range(N_HEADS))
    _, ls = lax.fori_loop(0, n_at, att_step, (m0, l0))
    z_all = z_ref[...]
    for h in range(N_HEADS):
        sl = slice(h * HEAD_DIM, (h + 1) * HEAD_DIM)
        oh = (acc_ref[h] / ls[h]).T
        o_ref[:, sl] = (oh * _silu(z_all[:, sl])).astype(o_ref.dtype)


def _dsa(proj, small, k_all16, vt16, ki16, gq_row, batch, q_len, tq, offset, s_valid):
    n = batch * q_len
    nq = q_len // tq
    s_pad = k_all16.shape[1]
    topk = min(TOPK_MAX, s_valid // 4)
    gw = GROUP_WIDTH
    qi_w = IDX_HEADS * IDX_DIM
    kern = functools.partial(_dsa_kernel, tq=tq, offset=offset, s_valid=s_valid, topk=topk)
    once = pl.Buffered(1)
    return pl.pallas_call(
        kern,
        grid=(batch, nq),
        in_specs=[
            pl.BlockSpec((tq, gw), lambda b, i: (b * nq + i, COL_QC // gw)),
            pl.BlockSpec((tq, qi_w), lambda b, i: (b * nq + i, COL_QI // qi_w)),
            pl.BlockSpec((tq, gw), lambda b, i: (b * nq + i, COL_ZC // gw)),
            pl.BlockSpec((tq, SMALL_COLS), lambda b, i: (b * nq + i, 0)),
            pl.BlockSpec((1, s_pad, gw), lambda b, i: (b, 0, 0), pipeline_mode=once),
            pl.BlockSpec((1, s_pad // KT_A, gw, KT_A), lambda b, i: (b, 0, 0, 0), pipeline_mode=once),
            pl.BlockSpec((1, s_pad, IDX_DIM), lambda b, i: (b, 0, 0), pipeline_mode=once),
            pl.BlockSpec((1, HEAD_DIM), lambda b, i: (0, 0)),
        ],
        out_specs=pl.BlockSpec((tq, gw), lambda b, i: (b * nq + i, 0)),
        out_shape=jax.ShapeDtypeStruct((n, gw), BF16),
        scratch_shapes=[pltpu.VMEM((s_pad, tq), I32),
                        pltpu.VMEM((qi_w, tq), BF16),
                        pltpu.VMEM((SMALL_COLS, tq), F32),
                        pltpu.VMEM((8, tq), I32),
                        pltpu.VMEM((gw, tq), BF16),
                        pltpu.VMEM((N_HEADS, HEAD_DIM, tq), F32),
                        pltpu.VMEM((N_HEADS, KT_A, tq), F32),
                        pltpu.VMEM((N_HEADS, KT_A, tq), BF16),
                        pltpu.VMEM((s_pad, tq), jnp.int16)],
        compiler_params=_cparams(("parallel", "arbitrary")),
        name="dsa",
    )(proj, proj, proj, small, k_all16, vt16, ki16, gq_row)


def _mem_kernel(q_ref, z_ref, mk_ref, mv_ref, gq_ref, o_ref):
    gq = gq_ref[...]
    q_all, z_all = q_ref[...], z_ref[...]
    mk = mk_ref[0].astype(BF16)
    mv = mv_ref[0].astype(BF16)
    for h in range(N_HEADS):
        sl = slice(h * HEAD_DIM, (h + 1) * HEAD_DIM)
        qh = q_all[:, sl]
        qh = qh * lax.rsqrt(jnp.mean(qh * qh, axis=-1, keepdims=True) + EPS) * gq * (HEAD_DIM ** -0.5)
        logit = _dot_nt(qh.astype(BF16), mk[:, sl])
        m = jnp.max(logit, axis=-1, keepdims=True)
        p = jnp.exp(logit - m)
        l = jnp.sum(p, axis=-1, keepdims=True)
        oh = _dot(p.astype(BF16), mv[:, sl]) / l
        o_ref[:, sl] = (oh * _silu(z_all[:, sl])).astype(o_ref.dtype)


def _mem_attend(proj, mk, mv, gq_row, batch, seqlen):
    n = batch * seqlen
    tm = min(512, seqlen)
    nj = seqlen // tm
    gw = GROUP_WIDTH
    n_mem = mk.shape[1]
    return pl.pallas_call(
        _mem_kernel,
        grid=(batch, nj),
        in_specs=[
            pl.BlockSpec((tm, gw), lambda b, j: (b * nj + j, COL_QD // gw)),
            pl.BlockSpec((tm, gw), lambda b, j: (b * nj + j, COL_ZD // gw)),
            pl.BlockSpec((1, n_mem, gw), lambda b, j: (b, 0, 0)),
            pl.BlockSpec((1, n_mem, gw), lambda b, j: (b, 0, 0)),
            pl.BlockSpec((1, HEAD_DIM), lambda b, j: (0, 0)),
        ],
        out_specs=pl.BlockSpec((tm, gw), lambda b, j: (b * nj + j, 0)),
        out_shape=jax.ShapeDtypeStruct((n, gw), BF16),
        compiler_params=_cparams(("parallel", "arbitrary")),
        name="mem_attend",
    )(proj, proj, mk, mv, gq_row)


def _outproj_kernel(x_ref, a_ref, b_ref, c_ref, d_ref, w_ref, y_ref):
    gw = GROUP_WIDTH
    acc = x_ref[...] + _dot(a_ref[...], w_ref[0:gw, :])
    acc = acc + _dot(b_ref[...], w_ref[gw:2 * gw, :])
    acc = acc + _dot(c_ref[...], w_ref[2 * gw:3 * gw, :])
    acc = acc + _dot(d_ref[...], w_ref[3 * gw:4 * gw, :])
    y_ref[...] = acc


def _out_proj(x2d, oa, ob, oc, od, w16):
    n, d = x2d.shape
    tm = min(512, n)
    gw = GROUP_WIDTH
    grp = pl.BlockSpec((tm, gw), lambda i: (i, 0))
    return pl.pallas_call(
        _outproj_kernel,
        grid=(n // tm,),
        in_specs=[pl.BlockSpec((tm, d), lambda i: (i, 0)), grp, grp, grp, grp,
                  pl.BlockSpec((4 * gw, d), lambda i: (0, 0))],
        out_specs=pl.BlockSpec((tm, d), lambda i: (i, 0)),
        out_shape=jax.ShapeDtypeStruct((n, d), F32),
        compiler_params=_cparams(("parallel",)),
        name="out_proj",
    )(x2d, oa, ob, oc, od, w16)


def _reorder_w_in(w):
    gw = GROUP_WIDTH
    o = 0
    seg = {}
    for name, width in (("qa", gw), ("ka", gw), ("va", gw), ("za", gw), ("ba", N_HEADS), ("aa", N_HEADS),
                        ("qb", gw), ("kb", gw), ("vb", gw), ("zb", gw),
                        ("qc", gw), ("kc", gw), ("vc", gw), ("zc", gw),
                        ("qi", IDX_HEADS * IDX_DIM), ("ki", IDX_DIM), ("wi", IDX_HEADS),
                        ("qd", gw), ("zd", gw)):
        seg[name] = w[:, o:o + width]
        o += width
    a_end = 4 * gw
    b_start = a_end + 2 * N_HEADS
    b_end = b_start + 8 * gw + IDX_HEADS * IDX_DIM
    d_start = b_end + IDX_DIM + IDX_HEADS
    w16 = w.astype(BF16)
    main = jnp.concatenate([w16[:, :a_end], w16[:, b_start:b_end], w16[:, d_start:]], axis=1)
    pad = jnp.zeros((w.shape[0], SMALL_COLS - IDX_DIM - IDX_HEADS - 2 * N_HEADS), BF16)
    small = jnp.concatenate([seg["ki"].astype(BF16), seg["wi"].astype(BF16), seg["ba"].astype(BF16),
                             seg["aa"].astype(BF16), pad], axis=1)
    return main, small


def _lane_row(vals, lane0):
    row = jnp.zeros((1, SMALL_COLS), F32)
    return row.at[0, lane0:lane0 + vals.shape[0]].set(vals.astype(F32))


def _rope_tables(pos):
    half = HEAD_DIM // 2
    inv = ROPE_THETA ** (-jnp.arange(half, dtype=F32) / half)
    ang = pos.astype(F32)[:, None] * inv[None, :]
    cos, sin = jnp.cos(ang), jnp.sin(ang)
    return jnp.concatenate([cos, cos], axis=-1), jnp.concatenate([-sin, sin], axis=-1)


def _round_up(x, m):
    return (x + m - 1) // m * m


def _mixer_layer(x, conv_buf, s_gdn, s_ret, past_k, past_v, past_ki, mem_k, mem_v, wts, layer, depth, prev_cache):
    b, l, d = x.shape
    n = b * l
    offset = 0 if past_k is None else past_k.shape[1]
    x2d = x.reshape(n, d)
    proj, small = _in_proj(x2d, wts["norm_g"], wts["w_main"], wts["w_small"])

    o_a, conv_new, s_gdn_new = _gdn(proj, small, wts["conv_w"], conv_buf, s_gdn, wts["alog_row"],
                                    wts["dtb_row"], wts["gdn_norm_g"], b, l)

    cos_t, sin_t = _rope_tables(offset + jnp.arange(l, dtype=I32))
    o_b, s_ret_new = _retention(proj, cos_t, sin_t, s_ret, wts["ret_norm_g"], wts["ret_norm_b"], b, l)

    direct = past_k is None and l % KT_A == 0
    prep = _dsa_prep(proj, small, wts["dsa_k_norm_g"], wts["idx_k_norm_g"], layer, depth, prev_cache, direct)
    cache = tuple(prep[:3])
    kc16, kic16 = prep[3].reshape(b, l, GROUP_WIDTH), prep[4].reshape(b, l, IDX_DIM)
    if direct:
        s_valid = l
        k16, ki16 = kc16, kic16
        vt16 = prep[5].reshape(b, l // KT_A, GROUP_WIDTH, KT_A)
    else:
        vc16 = cache[1][layer].astype(BF16).reshape(b, l, GROUP_WIDTH)
        if past_k is None:
            k_all, v_all, ki_all = kc16, vc16, kic16
        else:
            k_all = jnp.concatenate([past_k.reshape(b, offset, GROUP_WIDTH).astype(BF16), kc16], axis=1)
            v_all = jnp.concatenate([past_v.reshape(b, offset, GROUP_WIDTH).astype(BF16), vc16], axis=1)
            ki_all = jnp.concatenate([past_ki.astype(BF16), kic16], axis=1)
        s_valid = k_all.shape[1]
        s_pad = _round_up(s_valid, KT_A)
        padw = ((0, 0), (0, s_pad - s_valid), (0, 0))
        k16 = jnp.pad(k_all, padw)
        ki16 = jnp.pad(ki_all, padw)
        vt16 = jnp.pad(v_all, padw).reshape(b, s_pad // KT_A, KT_A, GROUP_WIDTH).transpose(0, 1, 3, 2)

    tq = 256 if l % 256 == 0 else 128
    q_len = _round_up(l, tq)
    if q_len != l:
        proj_q = jnp.pad(proj.reshape(b, l, MAIN_COLS), ((0, 0), (0, q_len - l), (0, 0))).reshape(b * q_len, MAIN_COLS)
        small_q = jnp.pad(small.reshape(b, l, SMALL_COLS), ((0, 0), (0, q_len - l), (0, 0))).reshape(b * q_len, SMALL_COLS)
    else:
        proj_q, small_q = proj, small
    o_c = _dsa(proj_q, small_q, k16, vt16, ki16, wts["dsa_q_norm_g"], b, q_len, tq, offset, s_valid)
    if q_len != l:
        o_c = o_c.reshape(b, q_len, GROUP_WIDTH)[:, :l].reshape(n, GROUP_WIDTH)

    o_d = _mem_attend(proj, mem_k, mem_v, wts["mem_q_norm_g"], b, l)

    y = _out_proj(x2d, o_a, o_b, o_c, o_d, wts["w_out"]).reshape(b, l, d)
    return y, (conv_new, s_gdn_new, s_ret_new), cache


def kernel(x_prompt, x_sample, cache_gdn_conv, state_gdn, state_ret, cache_dsa_k, cache_dsa_v, cache_idx_k, cache_mem_k, cache_mem_v, mem_prompt, norm_g, w_in, gdn_conv_w, gdn_a_log, gdn_dt_bias, gdn_norm_g, ret_norm_g, ret_norm_b, dsa_q_norm_g, dsa_k_norm_g, idx_k_norm_g, mem_norm_g, w_mem_kv, mem_q_norm_g, mem_k_norm_g, w_out):
    depth = w_in.shape[0]
    b = x_prompt.shape[0]
    n_mem = mem_prompt.shape[1]
    d = x_prompt.shape[-1]
    y_p, y_s = x_prompt, x_sample
    st_p, st_s, mem_p = [], [], []
    cache_p = cache_s = None
    for li in range(depth):
        w_main, w_small = _reorder_w_in(w_in[li])
        wts = dict(
            norm_g=norm_g[li][None, :], w_main=w_main, w_small=w_small, conv_w=gdn_conv_w[li],
            alog_row=_lane_row(gdn_a_log[li], LANE_ALPHA), dtb_row=_lane_row(gdn_dt_bias[li], LANE_ALPHA),
            gdn_norm_g=gdn_norm_g[li][None, :], ret_norm_g=ret_norm_g[li][None, :],
            ret_norm_b=ret_norm_b[li][None, :], dsa_q_norm_g=dsa_q_norm_g[li][None, :],
            dsa_k_norm_g=dsa_k_norm_g[li][None, :], idx_k_norm_g=idx_k_norm_g[li][None, :],
            mem_q_norm_g=mem_q_norm_g[li][None, :], w_out=w_out[li].astype(BF16),
        )
        mk, mv = _memory_kv(mem_prompt.reshape(b * n_mem, d), mem_norm_g[li][None, :],
                            w_mem_kv[li].astype(BF16), mem_k_norm_g[li][None, :])
        mk = mk.reshape(b, n_mem, GROUP_WIDTH)
        mv = mv.reshape(b, n_mem, GROUP_WIDTH)
        conv0 = jnp.zeros((b, CONV_W - 1, 3 * GROUP_WIDTH), F32)
        s0 = jnp.zeros((b, N_HEADS, HEAD_DIM, HEAD_DIM), F32)
        y_p, sp, cache_p = _mixer_layer(y_p, conv0, s0, s0, None, None, None, mk, mv, wts, li, depth, cache_p)
        st_p.append(sp)
        mem_p.append((mk.reshape(b, n_mem, N_HEADS, HEAD_DIM), mv.reshape(b, n_mem, N_HEADS, HEAD_DIM)))
        bs = x_sample.shape[0]
        y_s, ss, cache_s = _mixer_layer(y_s, cache_gdn_conv[li], state_gdn[li], state_ret[li],
                                        cache_dsa_k[li], cache_dsa_v[li], cache_idx_k[li],
                                        cache_mem_k[li].reshape(bs, n_mem, GROUP_WIDTH),
                                        cache_mem_v[li].reshape(bs, n_mem, GROUP_WIDTH), wts, li, depth, cache_s)
        st_s.append(ss)

    def stack(lst, k):
        return jnp.stack([s[k] for s in lst])

    def caches(c, bb, ll):
        return (c[0].reshape(depth, bb, ll, N_HEADS, HEAD_DIM), c[1].reshape(depth, bb, ll, N_HEADS, HEAD_DIM),
                c[2].reshape(depth, bb, ll, IDX_DIM))

    return ((y_p, y_s, stack(st_p, 0), stack(st_p, 1), stack(st_p, 2))
            + caches(cache_p, b, x_prompt.shape[1])
            + (stack(mem_p, 0), stack(mem_p, 1), stack(st_s, 0), stack(st_s, 1), stack(st_s, 2))
            + caches(cache_s, x_sample.shape[0], x_sample.shape[1]))
```

```python
import functools
import math

import jax
import jax.numpy as jnp
from jax import lax
from jax.experimental import pallas as pl
from jax.experimental.pallas import tpu as pltpu

F32 = jnp.float32
BF16 = jnp.bfloat16
I32 = jnp.int32

HEAD_DIM = 128
N_HEADS = 4
GROUP_WIDTH = N_HEADS * HEAD_DIM
CHUNK = 64
CONV_W = 4
IDX_HEADS = 16
IDX_DIM = 64
TOPK_MAX = 256
ROPE_THETA = 10000.0
EPS = 1e-6

COL_QA, COL_KA, COL_VA, COL_ZA = 0, 512, 1024, 1536
COL_QB, COL_KB, COL_VB, COL_ZB = 2048, 2560, 3072, 3584
COL_QC, COL_KC, COL_VC, COL_ZC = 4096, 4608, 5120, 5632
COL_QI = 6144
COL_QD, COL_ZD = 7168, 7680
MAIN_COLS = 8192
LANE_KI, LANE_WI, LANE_BETA, LANE_ALPHA = 0, 64, 80, 84
SMALL_COLS = 128

INT_MIN = -2 ** 31
NEG_BIG = -1e30
LOG2_E = 1.4426950408889634
VMEM_LIMIT = 56 * 1024 * 1024
HI = lax.Precision.HIGHEST


def _cparams(sem):
    return pltpu.CompilerParams(dimension_semantics=sem, vmem_limit_bytes=VMEM_LIMIT)


def _dot(a, b):
    return jnp.dot(a, b, preferred_element_type=F32)


def _dot_nt(a, b):
    return lax.dot_general(a, b, (((1,), (1,)), ((), ())), preferred_element_type=F32)


def _dot3(a, b):
    a_hi = a.astype(BF16)
    b_hi = b.astype(BF16)
    a_lo = (a - a_hi.astype(F32)).astype(BF16)
    b_lo = (b - b_hi.astype(F32)).astype(BF16)
    return _dot(a_hi, b_hi) + (_dot(a_hi, b_lo) + _dot(a_lo, b_hi))


def _silu(x):
    return x * jax.nn.sigmoid(x)


def _inproj_kernel(x_ref, g_ref, w_ref, ws_ref, o_ref, os_ref, h_ref):
    @pl.when(pl.program_id(1) == 0)
    def _():
        x = x_ref[...]
        y = x * lax.rsqrt(jnp.mean(x * x, axis=-1, keepdims=True) + EPS) * g_ref[...]
        hb = y.astype(BF16)
        h_ref[...] = hb
        os_ref[...] = _dot(hb, ws_ref[...])

    o_ref[...] = _dot(h_ref[...], w_ref[...])


def _in_proj(x2d, g_row, w_main, w_small):
    n, d = x2d.shape
    tm = min(512, n)
    tn = 1024
    return pl.pallas_call(
        _inproj_kernel,
        grid=(n // tm, MAIN_COLS // tn),
        in_specs=[
            pl.BlockSpec((tm, d), lambda i, j: (i, 0)),
            pl.BlockSpec((1, d), lambda i, j: (0, 0)),
            pl.BlockSpec((d, tn), lambda i, j: (0, j)),
            pl.BlockSpec((d, SMALL_COLS), lambda i, j: (0, 0)),
        ],
        out_specs=[
            pl.BlockSpec((tm, tn), lambda i, j: (i, j)),
            pl.BlockSpec((tm, SMALL_COLS), lambda i, j: (i, 0)),
        ],
        out_shape=[jax.ShapeDtypeStruct((n, MAIN_COLS), F32),
                   jax.ShapeDtypeStruct((n, SMALL_COLS), F32)],
        scratch_shapes=[pltpu.VMEM((tm, d), BF16)],
        compiler_params=_cparams(("parallel", "arbitrary")),
        name="in_proj",
    )(x2d, g_row, w_main, w_small)


def _memkv_kernel(x_ref, g_ref, w_ref, gk_ref, mk_ref, mv_ref):
    x = x_ref[...]
    y = x * lax.rsqrt(jnp.mean(x * x, axis=-1, keepdims=True) + EPS) * g_ref[...]
    kv = _dot(y.astype(BF16), w_ref[...])
    gk = gk_ref[...]
    for h in range(N_HEADS):
        sl = slice(h * HEAD_DIM, (h + 1) * HEAD_DIM)
        kh = kv[:, sl]
        mk_ref[:, sl] = kh * lax.rsqrt(jnp.mean(kh * kh, axis=-1, keepdims=True) + EPS) * gk
    mv_ref[...] = kv[:, GROUP_WIDTH:]


def _memory_kv(mem2d, g_row, w_kv, gk_row):
    n, d = mem2d.shape
    tm = min(256, n)
    return pl.pallas_call(
        _memkv_kernel,
        grid=(n // tm,),
        in_specs=[
            pl.BlockSpec((tm, d), lambda i: (i, 0)),
            pl.BlockSpec((1, d), lambda i: (0, 0)),
            pl.BlockSpec((d, 2 * GROUP_WIDTH), lambda i: (0, 0)),
            pl.BlockSpec((1, HEAD_DIM), lambda i: (0, 0)),
        ],
        out_specs=[pl.BlockSpec((tm, GROUP_WIDTH), lambda i: (i, 0)),
                   pl.BlockSpec((tm, GROUP_WIDTH), lambda i: (i, 0))],
        out_shape=[jax.ShapeDtypeStruct((n, GROUP_WIDTH), F32),
                   jax.ShapeDtypeStruct((n, GROUP_WIDTH), F32)],
        compiler_params=_cparams(("parallel",)),
        name="memory_kv",
    )(mem2d, g_row, w_kv, gk_row)


def _gdn_kernel(qkv_ref, z_ref, sm_ref, cw_ref, cb_ref, s0_ref, alog_ref, dtb_ref, gn_ref,
                o_ref, conv_ref, st_ref, xbuf_ref, s_ref, *, t_blk, chunk):
    j = pl.program_id(1)
    nj = pl.num_programs(1)
    gw3 = 3 * GROUP_WIDTH

    @pl.when(j == 0)
    def _():
        xbuf_ref[0:8, :] = jnp.zeros((8, gw3), F32)
        xbuf_ref[5:8, :] = cb_ref[0]
        s_ref[...] = s0_ref[0]

    @pl.when(j > 0)
    def _():
        xbuf_ref[0:8, :] = xbuf_ref[t_blk:t_blk + 8, :]

    xbuf_ref[8:8 + t_blk, :] = qkv_ref[...]
    conv_ref[0] = xbuf_ref[t_blk + 5:t_blk + 8, :]

    cw = cw_ref[...]
    y = xbuf_ref[5:5 + t_blk, :] * cw[0:1, :]
    for jj in range(1, CONV_W):
        y = y + xbuf_ref[5 + jj:5 + jj + t_blk, :] * cw[jj:jj + 1, :]
    y = _silu(y)

    sm = sm_ref[...]
    lane = lax.broadcasted_iota(I32, sm.shape, 1)
    beta_all = jax.nn.sigmoid(sm)
    xs = sm + dtb_ref[...]
    softplus = jnp.maximum(xs, 0.0) + jnp.log1p(jnp.exp(-jnp.abs(xs)))
    g_all = -jnp.exp(alog_ref[...]) * softplus
    g_all = jnp.where((lane >= LANE_ALPHA) & (lane < LANE_ALPHA + N_HEADS), g_all, 0.0)
    ri = lax.broadcasted_iota(I32, (t_blk, t_blk), 0)
    ci = lax.broadcasted_iota(I32, (t_blk, t_blk), 1)
    tri = jnp.where((ri // chunk == ci // chunk) & (ci <= ri), 1.0, 0.0).astype(F32)
    gcum = jnp.dot(tri, g_all, preferred_element_type=F32, precision=HI)
    gcum_t = gcum.T

    blk = min(2 * chunk, t_blk)
    n_blk = t_blk // blk
    cpb = blk // chunk
    rb = lax.broadcasted_iota(I32, (blk, blk), 0)
    cb = lax.broadcasted_iota(I32, (blk, blk), 1)
    same = (rb // chunk) == (cb // chunk)
    causal = same & (cb <= rb)
    strict = same & (cb < rb)
    eye = jnp.where(rb == cb, 1.0, 0.0).astype(F32)
    n_dbl = max(int(math.log2(chunk)) - 1, 0)
    gn = gn_ref[...]
    z_all = z_ref[...]

    units = [(h, b) for h in range(N_HEADS) for b in range(n_blk)]
    qs, ks, gcols, p_mats, x_mats, qk_mats, vb_mats, kg_mats = {}, {}, {}, {}, {}, {}, {}, {}
    for h in range(N_HEADS):
        qh = y[:, COL_QA + h * HEAD_DIM:COL_QA + (h + 1) * HEAD_DIM]
        kh = y[:, COL_KA + h * HEAD_DIM:COL_KA + (h + 1) * HEAD_DIM]
        vh = y[:, COL_VA + h * HEAD_DIM:COL_VA + (h + 1) * HEAD_DIM]
        qh = qh * lax.rsqrt(jnp.sum(qh * qh, axis=-1, keepdims=True) + EPS) * (HEAD_DIM ** -0.5)
        kh = kh * lax.rsqrt(jnp.sum(kh * kh, axis=-1, keepdims=True) + EPS)
        for b in range(n_blk):
            rs = slice(b * blk, (b + 1) * blk)
            gcol = gcum[rs, LANE_ALPHA + h:LANE_ALPHA + h + 1]
            grow = gcum_t[LANE_ALPHA + h:LANE_ALPHA + h + 1, rs]
            bcol = beta_all[rs, LANE_BETA + h:LANE_BETA + h + 1]
            decay = jnp.where(causal, jnp.exp(jnp.where(causal, gcol - grow, 0.0)), 0.0)
            qc, kc, vc = qh[rs], kh[rs], vh[rs]
            kb = kc * bcol
            kc16 = kc.astype(BF16)
            a_mat = jnp.where(strict, _dot_nt(kb.astype(BF16), kc16) * decay, 0.0)
            qk_mats[h, b] = jnp.where(causal, _dot_nt(qc.astype(BF16), kc16) * decay, 0.0).astype(BF16)
            p_mats[h, b] = -a_mat
            x_mats[h, b] = eye - a_mat
            vb_mats[h, b] = (vc * bcol).astype(BF16)
            kg_mats[h, b] = (kb * jnp.exp(gcol)).astype(BF16)
            qs[h, b], ks[h, b], gcols[h, b] = qc, kc, gcol

    for _ in range(n_dbl):
        for u_ in units:
            p_mats[u_] = _dot3(p_mats[u_], p_mats[u_])
        for u_ in units:
            x_mats[u_] = x_mats[u_] + _dot3(x_mats[u_], p_mats[u_])

    u_mats, w_mats = {}, {}
    for u_ in units:
        x16 = x_mats[u_].astype(BF16)
        u_mats[u_] = _dot(x16, vb_mats[u_])
        w_mats[u_] = _dot(x16, kg_mats[u_]).astype(BF16)

    states = [s_ref[h] for h in range(N_HEADS)]
    for b in range(n_blk):
        o_state = {h: [] for h in range(N_HEADS)}
        v_new = {h: [] for h in range(N_HEADS)}
        for c in range(cpb):
            cs = slice(c * chunk, (c + 1) * chunk)
            for h in range(N_HEADS):
                s = states[h]
                s16 = s.astype(BF16)
                gcol = gcols[h, b][cs]
                glast = gcols[h, b][(c + 1) * chunk - 1:(c + 1) * chunk]
                vn = u_mats[h, b][cs] - _dot(w_mats[h, b][cs], s16)
                vn16 = vn.astype(BF16)
                kd_t = (ks[h, b][cs] * jnp.exp(glast - gcol)).T.astype(BF16)
                states[h] = s * jnp.exp(glast) + _dot(kd_t, vn16)
                o_state[h].append(_dot((qs[h, b][cs] * jnp.exp(gcol)).astype(BF16), s16))
                v_new[h].append(vn16)
        for h in range(N_HEADS):
            sl = slice(h * HEAD_DIM, (h + 1) * HEAD_DIM)
            rs = slice(b * blk, (b + 1) * blk)
            o = jnp.concatenate(o_state[h], axis=0) + _dot(qk_mats[h, b], jnp.concatenate(v_new[h], axis=0))
            on = o * lax.rsqrt(jnp.mean(o * o, axis=-1, keepdims=True) + EPS) * gn
            o_ref[rs, sl] = (on * _silu(z_all[rs, sl])).astype(o_ref.dtype)
    for h in range(N_HEADS):
        s_ref[h] = states[h]

    @pl.when(j == nj - 1)
    def _():
        st_ref[0] = s_ref[...]


def _gdn(proj, small, conv_w, conv_buf, s0, alog_row, dtb_row, gn_row, batch, seqlen):
    n = batch * seqlen
    t_blk = min(256, seqlen)
    chunk = min(CHUNK, seqlen)
    nj = seqlen // t_blk
    gw3 = 3 * GROUP_WIDTH
    kern = functools.partial(_gdn_kernel, t_blk=t_blk, chunk=chunk)
    return pl.pallas_call(
        kern,
        grid=(batch, nj),
        in_specs=[
            pl.BlockSpec((t_blk, gw3), lambda b, j: (b * nj + j, COL_QA // gw3)),
            pl.BlockSpec((t_blk, GROUP_WIDTH), lambda b, j: (b * nj + j, COL_ZA // GROUP_WIDTH)),
            pl.BlockSpec((t_blk, SMALL_COLS), lambda b, j: (b * nj + j, 0)),
            pl.BlockSpec((CONV_W, gw3), lambda b, j: (0, 0)),
            pl.BlockSpec((1, CONV_W - 1, gw3), lambda b, j: (b, 0, 0)),
            pl.BlockSpec((1, N_HEADS, HEAD_DIM, HEAD_DIM), lambda b, j: (b, 0, 0, 0)),
            pl.BlockSpec((1, SMALL_COLS), lambda b, j: (0, 0)),
            pl.BlockSpec((1, SMALL_COLS), lambda b, j: (0, 0)),
            pl.BlockSpec((1, HEAD_DIM), lambda b, j: (0, 0)),
        ],
        out_specs=[
            pl.BlockSpec((t_blk, GROUP_WIDTH), lambda b, j: (b * nj + j, 0)),
            pl.BlockSpec((1, CONV_W - 1, gw3), lambda b, j: (b, 0, 0)),
            pl.BlockSpec((1, N_HEADS, HEAD_DIM, HEAD_DIM), lambda b, j: (b, 0, 0, 0)),
        ],
        out_shape=[
            jax.ShapeDtypeStruct((n, GROUP_WIDTH), BF16),
            jax.ShapeDtypeStruct((batch, CONV_W - 1, gw3), F32),
            jax.ShapeDtypeStruct((batch, N_HEADS, HEAD_DIM, HEAD_DIM), F32),
        ],
        scratch_shapes=[pltpu.VMEM((t_blk + 8, gw3), F32),
                        pltpu.VMEM((N_HEADS, HEAD_DIM, HEAD_DIM), F32)],
        compiler_params=_cparams(("parallel", "arbitrary")),
        name="gdn",
    )(proj, proj, small, conv_w, conv_buf, s0, alog_row, dtb_row, gn_row)


def _ret_kernel(q_ref, k_ref, v_ref, z_ref, cos_ref, sin_ref, s0_ref, g_ref, b_ref,
                o_ref, st_ref, s_ref, *, t_blk):
    j = pl.program_id(1)
    nj = pl.num_programs(1)

    @pl.when(j == 0)
    def _():
        s_ref[...] = s0_ref[0]

    cos = cos_ref[...]
    sin = sin_ref[...]
    ri = lax.broadcasted_iota(I32, (t_blk, t_blk), 0)
    ci = lax.broadcasted_iota(I32, (t_blk, t_blk), 1)
    causal = ci <= ri
    rel = jnp.where(causal, ri - ci, 0).astype(F32)
    idx_col = lax.broadcasted_iota(I32, (t_blk, 1), 0).astype(F32)
    gamma_g = g_ref[...]
    gamma_b = b_ref[...]
    q_all, k_all, v_all, z_all = q_ref[...], k_ref[...], v_ref[...], z_ref[...]

    for h in range(N_HEADS):
        sl = slice(h * HEAD_DIM, (h + 1) * HEAD_DIM)
        lg = math.log(1.0 - 2.0 ** (-5.0 - h))
        qh, kh, vh = q_all[:, sl], k_all[:, sl], v_all[:, sl]
        qh = qh * cos + pltpu.roll(qh, HEAD_DIM // 2, 1) * sin
        kh = (kh * cos + pltpu.roll(kh, HEAD_DIM // 2, 1) * sin) * (HEAD_DIM ** -0.5)
        d_mat = jnp.where(causal, jnp.exp(lg * rel), 0.0)
        q16, k16, v16 = qh.astype(BF16), kh.astype(BF16), vh.astype(BF16)
        o_intra = _dot((_dot_nt(q16, k16) * d_mat).astype(BF16), v16)
        s = s_ref[h]
        o_cross = _dot(q16, s.astype(BF16)) * jnp.exp(lg * (idx_col + 1.0))
        kd_t = (kh * jnp.exp(lg * (t_blk - 1.0 - idx_col))).T.astype(BF16)
        s_ref[h] = s * math.exp(lg * t_blk) + _dot(kd_t, v16)
        o = o_intra + o_cross
        mu = jnp.mean(o, axis=-1, keepdims=True)
        oc = o - mu
        var = jnp.mean(oc * oc, axis=-1, keepdims=True)
        on = oc * lax.rsqrt(var + EPS) * gamma_g + gamma_b
        o_ref[:, sl] = (on * _silu(z_all[:, sl])).astype(o_ref.dtype)

    @pl.when(j == nj - 1)
    def _():
        st_ref[0] = s_ref[...]


def _retention(proj, cos_t, sin_t, s0, g_row, b_row, batch, seqlen):
    n = batch * seqlen
    t_blk = min(256, seqlen)
    nj = seqlen // t_blk
    gw = GROUP_WIDTH
    kern = functools.partial(_ret_kernel, t_blk=t_blk)

    def col(c):
        return pl.BlockSpec((t_blk, gw), lambda b, j: (b * nj + j, c // gw))

    return pl.pallas_call(
        kern,
        grid=(batch, nj),
        in_specs=[
            col(COL_QB), col(COL_KB), col(COL_VB), col(COL_ZB),
            pl.BlockSpec((t_blk, HEAD_DIM), lambda b, j: (j, 0)),
            pl.BlockSpec((t_blk, HEAD_DIM), lambda b, j: (j, 0)),
            pl.BlockSpec((1, N_HEADS, HEAD_DIM, HEAD_DIM), lambda b, j: (b, 0, 0, 0)),
            pl.BlockSpec((1, HEAD_DIM), lambda b, j: (0, 0)),
            pl.BlockSpec((1, HEAD_DIM), lambda b, j: (0, 0)),
        ],
        out_specs=[
            pl.BlockSpec((t_blk, gw), lambda b, j: (b * nj + j, 0)),
            pl.BlockSpec((1, N_HEADS, HEAD_DIM, HEAD_DIM), lambda b, j: (b, 0, 0, 0)),
        ],
        out_shape=[
            jax.ShapeDtypeStruct((n, gw), BF16),
            jax.ShapeDtypeStruct((batch, N_HEADS, HEAD_DIM, HEAD_DIM), F32),
        ],
        scratch_shapes=[pltpu.VMEM((N_HEADS, HEAD_DIM, HEAD_DIM), F32)],
        compiler_params=_cparams(("parallel", "arbitrary")),
        name="retention",
    )(proj, proj, proj, proj, cos_t, sin_t, s0, g_row, b_row)


def _dsa_prep_kernel(k_ref, v_ref, sm_ref, gk_ref, gi_ref, *rest, n_prev, with_vt):
    outs = rest[n_prev:]
    ko_ref, vo_ref, kio_ref, k16_ref, ki16_ref = outs[:5]
    k = k_ref[...]
    gk = gk_ref[...]
    for h in range(N_HEADS):
        sl = slice(h * HEAD_DIM, (h + 1) * HEAD_DIM)
        kh = k[:, sl]
        kn = kh * lax.rsqrt(jnp.mean(kh * kh, axis=-1, keepdims=True) + EPS) * gk
        ko_ref[:, sl] = kn
        k16_ref[:, sl] = kn.astype(BF16)
    v = v_ref[...]
    vo_ref[...] = v
    if with_vt:
        outs[5][...] = v.T.astype(BF16)
    ki = sm_ref[...][:, LANE_KI:LANE_KI + IDX_DIM]
    kin = ki * lax.rsqrt(jnp.mean(ki * ki, axis=-1, keepdims=True) + EPS) * gi_ref[...]
    kio_ref[...] = kin
    ki16_ref[...] = kin.astype(BF16)


def _dsa_prep(proj, small, gk_row, gi_row, layer, depth, prev, with_vt):
    n = proj.shape[0]
    tm = min(KT_A, n)
    gw = GROUP_WIDTH
    n_prev = 0 if prev is None else 3
    kern = functools.partial(_dsa_prep_kernel, n_prev=n_prev, with_vt=with_vt)
    in_specs = [
        pl.BlockSpec((tm, gw), lambda i: (i, COL_KC // gw)),
        pl.BlockSpec((tm, gw), lambda i: (i, COL_VC // gw)),
        pl.BlockSpec((tm, SMALL_COLS), lambda i: (i, 0)),
        pl.BlockSpec((1, HEAD_DIM), lambda i: (0, 0)),
        pl.BlockSpec((1, IDX_DIM), lambda i: (0, 0)),
    ] + [pl.BlockSpec(memory_space=pl.ANY)] * n_prev
    out_specs = [pl.BlockSpec((None, tm, gw), lambda i: (layer, i, 0)),
                 pl.BlockSpec((None, tm, gw), lambda i: (layer, i, 0)),
                 pl.BlockSpec((None, tm, IDX_DIM), lambda i: (layer, i, 0)),
                 pl.BlockSpec((tm, gw), lambda i: (i, 0)),
                 pl.BlockSpec((tm, IDX_DIM), lambda i: (i, 0))]
    out_shape = [jax.ShapeDtypeStruct((depth, n, gw), F32),
                 jax.ShapeDtypeStruct((depth, n, gw), F32),
                 jax.ShapeDtypeStruct((depth, n, IDX_DIM), F32),
                 jax.ShapeDtypeStruct((n, gw), BF16),
                 jax.ShapeDtypeStruct((n, IDX_DIM), BF16)]
    if with_vt:
        out_specs.append(pl.BlockSpec((None, gw, tm), lambda i: (i, 0, 0)))
        out_shape.append(jax.ShapeDtypeStruct((n // tm, gw, tm), BF16))
    args = (proj, proj, small, gk_row, gi_row) + (() if prev is None else tuple(prev))
    return pl.pallas_call(
        kern,
        grid=(n // tm,),
        in_specs=in_specs,
        out_specs=out_specs,
        out_shape=out_shape,
        input_output_aliases={5 + t: t for t in range(n_prev)},
        compiler_params=_cparams(("parallel",)),
        name="dsa_prep",
    )(*args)


KT_I = 128
KT_A = 512


def _dsa_kernel(q_ref, qi_ref, z_ref, sm_ref, k_ref, vt_ref, ki_ref, gq_ref, o_ref,
                key_ref, qit_ref, wt_ref, jlim_ref, qt_ref, acc_ref, lg_ref, p_ref, key16_ref, low16_ref, *, tq, offset, s_valid, topk):
    i = pl.program_id(1)
    pos0 = offset + i * tq
    t_pos = pos0 + lax.broadcasted_iota(I32, (1, tq), 1)
    t_chunk = t_pos // CHUNK
    n_adm_row = jnp.minimum((t_chunk + 1) * CHUNK, s_valid)
    n_keys = jnp.minimum(((pos0 + tq - 1) // CHUNK + 1) * CHUNK, s_valid)
    n_at = (n_keys + KT_A - 1) // KT_A
    n_sub = KT_A // KT_I
    n_it_full = n_at * n_sub

    qit_ref[...] = qi_ref[...].T.astype(BF16)
    wt_ref[...] = sm_ref[...].T * (IDX_HEADS ** -0.5 * IDX_DIM ** -0.5)

    def index_step(kt, carry):
        r0 = pl.multiple_of(kt * KT_I, KT_I)
        ki_t = ki_ref[0, pl.ds(r0, KT_I), :]
        acc = jnp.zeros((KT_I, tq), F32)
        for h in range(IDX_HEADS):
            sc = _dot(ki_t, qit_ref[h * IDX_DIM:(h + 1) * IDX_DIM, :])
            acc = acc + wt_ref[LANE_WI + h:LANE_WI + h + 1, :] * jnp.maximum(sc, 0.0)
        bits = lax.bitcast_convert_type(acc, I32)
        key = bits ^ ((bits >> 31) & 0x7FFFFFFF)
        s_pos = r0 + lax.broadcasted_iota(I32, (KT_I, 1), 0)
        adm = (s_pos // CHUNK <= t_chunk) & (s_pos < s_valid)
        key = jnp.where(adm, key, INT_MIN)
        key_ref[pl.ds(r0, KT_I), :] = key
        key16_ref[pl.ds(r0, KT_I), :] = (key >> 16).astype(jnp.int16)
        return carry

    lax.fori_loop(0, n_it_full // 2, lambda t, c: index_step(2 * t + 1, index_step(2 * t, c)), 0)

    def count(pred_fn):
        def body(kt, accs):
            out = []
            for s in range(n_sub):
                r0 = pl.multiple_of(kt * KT_A + s * KT_I, KT_I)
                key = key_ref[pl.ds(r0, KT_I), :]
                s_pos = r0 + lax.broadcasted_iota(I32, (KT_I, 1), 0)
                m = jnp.where(pred_fn(key, s_pos), 1, 0).astype(I32)
                out.append(accs[s] + jnp.sum(m.reshape(KT_I // 8, 8, tq), axis=0))
            return tuple(out)
        accs = lax.fori_loop(0, n_at, body, tuple(jnp.zeros((8, tq), I32) for _ in range(n_sub)))
        return jnp.sum(sum(accs[1:], accs[0]), axis=0, keepdims=True)

    def count16(ref16, pred_fn):
        def body(kt, accs):
            out = []
            for s in range(n_sub):
                r0 = pl.multiple_of(kt * KT_A + s * KT_I, KT_I)
                m = jnp.where(pred_fn(ref16[pl.ds(r0, KT_I), :]), jnp.int16(1), jnp.int16(0))
                m = m.reshape(KT_I // 16, 16, tq)
                acc = accs[s]
                for q in range(KT_I // 16):
                    acc = acc + m[q]
                out.append(acc)
            return tuple(out)
        accs = lax.fori_loop(0, n_at, body, tuple(jnp.zeros((16, tq), jnp.int16) for _ in range(n_sub)))
        tot = sum((a.astype(I32) for a in accs[1:]), accs[0].astype(I32))
        return jnp.sum(tot, axis=0, keepdims=True)

    small = jnp.where(n_adm_row <= topk, 1, 0).astype(I32)

    def all_done(done):
        return jnp.min(done.astype(F32)) > 0.0

    def search(first_bit, last_bit, count_ge, carry, early_exit):
        def step(b, c):
            v, done, thr = c
            cand_u = v | jnp.left_shift(jnp.int32(1), 31 - b)
            cand_s = cand_u ^ INT_MIN
            cnt = count_ge(cand_s)
            v = jnp.where(cnt >= topk, cand_u, v)
            newly = (cnt == topk) & (done == 0)
            thr = jnp.where(newly, cand_s, thr)
            return v, jnp.where(newly, 1, done), thr

        if not early_exit:
            return lax.fori_loop(first_bit, last_bit, step, carry)

        def cond(c):
            return (c[0] < last_bit) & jnp.logical_not(all_done(c[1][1]))

        def body(c):
            b, inner = c
            return b + 4, lax.fori_loop(b, b + 4, step, inner)

        assert (last_bit - first_bit) % 4 == 0
        return lax.while_loop(cond, body, (jnp.int32(first_bit), carry))[1]

    v0 = jnp.zeros((1, tq), I32)
    thr0 = jnp.full((1, tq), INT_MIN + 1, I32)
    carry = search(0, 16, lambda cand_s: count16(key16_ref, lambda k16: k16 >= (cand_s >> 16).astype(jnp.int16)),
                   (v0, small, thr0), early_exit=False)
    hi16 = ((carry[0] ^ INT_MIN) >> 16).astype(jnp.int16)
    n_above = count16(key16_ref, lambda k16: k16 > hi16)

    @pl.when(jnp.logical_not(all_done(carry[1])))
    def _():
        hi32 = (carry[0] ^ INT_MIN) >> 16

        def low_step(kt, c):
            for s in range(n_sub):
                r0 = pl.multiple_of(kt * KT_A + s * KT_I, KT_I)
                key = key_ref[pl.ds(r0, KT_I), :]
                low = jnp.where((key >> 16) == hi32, (key & 0xFFFF) - 32768, -32768)
                low16_ref[pl.ds(r0, KT_I), :] = low.astype(jnp.int16)
            return c

        lax.fori_loop(0, n_at, low_step, 0)

    def count_low(cand_s):
        c16 = ((cand_s & 0xFFFF) - 32768).astype(jnp.int16)
        return n_above + count16(low16_ref, lambda l16: l16 >= c16)

    v_u, done_i, thr = search(16, 32, count_low, carry, early_exit=True)
    done = done_i != 0
    v_s = v_u ^ INT_MIN
    any_tie = jnp.logical_not(all_done(done_i))

    jlim_ref[...] = jnp.zeros(jlim_ref.shape, I32)

    @pl.when(any_tie)
    def _():
        n_gt = count(lambda key, s_pos: key > v_s)
        need = topk - n_gt
        pos_bits = max(int(math.ceil(math.log2(max(k_ref.shape[1], 2)))), 1) + 1

        def pos_step(b, jv):
            cand = jv | jnp.left_shift(jnp.int32(1), pos_bits - 1 - b)
            cnt = count(lambda key, s_pos: (key == v_s) & (s_pos < cand))
            return jnp.where(cnt <= need, cand, jv)

        jlim_ref[0:1, :] = lax.fori_loop(0, pos_bits, pos_step, jnp.zeros((1, tq), I32))

    j_lim = jlim_ref[0:1, :]

    v_eff = jnp.where(done, thr - 1, v_s)
    j_eff = jnp.where(done, 0, j_lim)

    def bias_loop(with_ties):
        def bias_step(kt, carry):
            for s in range(n_sub):
                r0 = pl.multiple_of(kt * KT_A + s * KT_I, KT_I)
                key = key_ref[pl.ds(r0, KT_I), :]
                sel = key > v_eff
                if with_ties:
                    s_pos = r0 + lax.broadcasted_iota(I32, (KT_I, 1), 0)
                    sel = sel | ((key == v_eff) & (s_pos < j_eff) & (key != INT_MIN))
                bias = jnp.where(sel, 0.0, NEG_BIG).astype(F32)
                key_ref[pl.ds(r0, KT_I), :] = lax.bitcast_convert_type(bias, I32)
            return carry

        lax.fori_loop(0, n_at, bias_step, 0)

    pl.when(any_tie)(lambda: bias_loop(True))
    pl.when(jnp.logical_not(any_tie))(lambda: bias_loop(False))

    gq = gq_ref[...]
    q_all = q_ref[...]
    for h in range(N_HEADS):
        sl = slice(h * HEAD_DIM, (h + 1) * HEAD_DIM)
        qh = q_all[:, sl]
        qh = qh * lax.rsqrt(jnp.mean(qh * qh, axis=-1, keepdims=True) + EPS) * gq * (HEAD_DIM ** -0.5 * LOG2_E)
        qt_ref[sl, :] = qh.T.astype(BF16)
    acc_ref[...] = jnp.zeros(acc_ref.shape, F32)

    heads = [slice(h * HEAD_DIM, (h + 1) * HEAD_DIM) for h in range(N_HEADS)]
    kt_h = KT_A // 2
    n_sub_h = kt_h // KT_I

    def issue_scores(kt, half):
        r0 = pl.multiple_of(kt * KT_A + half * kt_h, kt_h)
        for h, sl in enumerate(heads):
            lg_ref[half, h] = _dot(k_ref[0, pl.ds(r0, kt_h), sl], qt_ref[sl, :])

    def issue_values(kt, half):
        return [_dot(vt_ref[0, kt, sl, half * kt_h:(half + 1) * kt_h], p_ref[half, h])
                for h, sl in enumerate(heads)]

    def softmax_half(kt, half, ms, ls):
        r0 = pl.multiple_of(kt * KT_A + half * kt_h, kt_h)
        new_ms, new_ls, alphas = [], [], []
        for h in range(N_HEADS):
            mx = None
            for s in range(n_sub_h):
                rs = slice(s * KT_I, (s + 1) * KT_I)
                bias = lax.bitcast_convert_type(key_ref[pl.ds(r0 + s * KT_I, KT_I), :], F32)
                lg = lg_ref[half, h, rs, :] + bias
                lg_ref[half, h, rs, :] = lg
                part = jnp.max(lg.reshape(KT_I // 8, 8, tq), axis=0)
                mx = part if mx is None else jnp.maximum(mx, part)
            m_new = jnp.maximum(ms[h], jnp.max(mx, axis=0, keepdims=True))
            alphas.append(jnp.exp2(ms[h] - m_new))
            new_ms.append(m_new)
        for h in range(N_HEADS):
            lsum = None
            for s in range(n_sub_h):
                rs = slice(s * KT_I, (s + 1) * KT_I)
                p = jnp.exp2(lg_ref[half, h, rs, :] - new_ms[h])
                p_ref[half, h, rs, :] = p.astype(BF16)
                part = jnp.sum(p.reshape(KT_I // 8, 8, tq), axis=0)
                lsum = part if lsum is None else lsum + part
            new_ls.append(alphas[h] * ls[h] + jnp.sum(lsum, axis=0, keepdims=True))
        return new_ms, new_ls, alphas

    def accumulate(alphas, pv):
        for h in range(N_HEADS):
            acc_ref[h] = alphas[h] * acc_ref[h] + pv[h]

    def att_step(kt, carry):
        ms, ls = carry
        issue_scores(kt, 0)
        issue_scores(kt, 1)
        ms, ls, alphas_a = softmax_half(kt, 0, ms, ls)
        pv_a = issue_values(kt, 0)
        ms, ls, alphas_b = softmax_half(kt, 1, ms, ls)
        accumulate(alphas_a, pv_a)
        accumulate(alphas_b, issue_values(kt, 1))
        return tuple(ms), tuple(ls)

    m0 = tuple(jnp.full((1, tq), NEG_BIG, F32) for _ in range(N_HEADS))
    l0 = tuple(jnp.zeros((1, tq), F32) for _ in range(N_HEADS))
    _, ls = lax.fori_loop(0, n_at, att_step, (m0, l0))
    z_all = z_ref[...]
    for h in range(N_HEADS):
        sl = slice(h * HEAD_DIM, (h + 1) * HEAD_DIM)
        oh = (acc_ref[h] / ls[h]).T
        o_ref[:, sl] = (oh * _silu(z_all[:, sl])).astype(o_ref.dtype)


def _dsa(proj, small, k_all16, vt16, ki16, gq_row, batch, q_len, tq, offset, s_valid):
    n = batch * q_len
    nq = q_len // tq
    s_pad = k_all16.shape[1]
    topk = min(TOPK_MAX, s_valid // 4)
    gw = GROUP_WIDTH
    qi_w = IDX_HEADS * IDX_DIM
    kern = functools.partial(_dsa_kernel, tq=tq, offset=offset, s_valid=s_valid, topk=topk)
    once = pl.Buffered(1)
    return pl.pallas_call(
        kern,
        grid=(batch, nq),
        in_specs=[
            pl.BlockSpec((tq, gw), lambda b, i: (b * nq + i, COL_QC // gw)),
            pl.BlockSpec((tq, qi_w), lambda b, i: (b * nq + i, COL_QI // qi_w)),
            pl.BlockSpec((tq, gw), lambda b, i: (b * nq + i, COL_ZC // gw)),
            pl.BlockSpec((tq, SMALL_COLS), lambda b, i: (b * nq + i, 0)),
            pl.BlockSpec((1, s_pad, gw), lambda b, i: (b, 0, 0), pipeline_mode=once),
            pl.BlockSpec((1, s_pad // KT_A, gw, KT_A), lambda b, i: (b, 0, 0, 0), pipeline_mode=once),
            pl.BlockSpec((1, s_pad, IDX_DIM), lambda b, i: (b, 0, 0), pipeline_mode=once),
            pl.BlockSpec((1, HEAD_DIM), lambda b, i: (0, 0)),
        ],
        out_specs=pl.BlockSpec((tq, gw), lambda b, i: (b * nq + i, 0)),
        out_shape=jax.ShapeDtypeStruct((n, gw), BF16),
        scratch_shapes=[pltpu.VMEM((s_pad, tq), I32),
                        pltpu.VMEM((qi_w, tq), BF16),
                        pltpu.VMEM((SMALL_COLS, tq), F32),
                        pltpu.VMEM((8, tq), I32),
                        pltpu.VMEM((gw, tq), BF16),
                        pltpu.VMEM((N_HEADS, HEAD_DIM, tq), F32),
                        pltpu.VMEM((2, N_HEADS, KT_A // 2, tq), F32),
                        pltpu.VMEM((2, N_HEADS, KT_A // 2, tq), BF16),
                        pltpu.VMEM((s_pad, tq), jnp.int16),
                        pltpu.VMEM((s_pad, tq), jnp.int16)],
        compiler_params=_cparams(("parallel", "arbitrary")),
        name="dsa",
    )(proj, proj, proj, small, k_all16, vt16, ki16, gq_row)


def _mem_kernel(q_ref, z_ref, mk_ref, mv_ref, gq_ref, o_ref):
    gq = gq_ref[...]
    q_all, z_all = q_ref[...], z_ref[...]
    mk = mk_ref[0].astype(BF16)
    mv = mv_ref[0].astype(BF16)
    for h in range(N_HEADS):
        sl = slice(h * HEAD_DIM, (h + 1) * HEAD_DIM)
        qh = q_all[:, sl]
        qh = qh * lax.rsqrt(jnp.mean(qh * qh, axis=-1, keepdims=True) + EPS) * gq * (HEAD_DIM ** -0.5)
        logit = _dot_nt(qh.astype(BF16), mk[:, sl])
        m = jnp.max(logit, axis=-1, keepdims=True)
        p = jnp.exp(logit - m)
        l = jnp.sum(p, axis=-1, keepdims=True)
        oh = _dot(p.astype(BF16), mv[:, sl]) / l
        o_ref[:, sl] = (oh * _silu(z_all[:, sl])).astype(o_ref.dtype)


def _mem_attend(proj, mk, mv, gq_row, batch, seqlen):
    n = batch * seqlen
    tm = min(512, seqlen)
    nj = seqlen // tm
    gw = GROUP_WIDTH
    n_mem = mk.shape[1]
    return pl.pallas_call(
        _mem_kernel,
        grid=(batch, nj),
        in_specs=[
            pl.BlockSpec((tm, gw), lambda b, j: (b * nj + j, COL_QD // gw)),
            pl.BlockSpec((tm, gw), lambda b, j: (b * nj + j, COL_ZD // gw)),
            pl.BlockSpec((1, n_mem, gw), lambda b, j: (b, 0, 0)),
            pl.BlockSpec((1, n_mem, gw), lambda b, j: (b, 0, 0)),
            pl.BlockSpec((1, HEAD_DIM), lambda b, j: (0, 0)),
        ],
        out_specs=pl.BlockSpec((tm, gw), lambda b, j: (b * nj + j, 0)),
        out_shape=jax.ShapeDtypeStruct((n, gw), BF16),
        compiler_params=_cparams(("parallel", "arbitrary")),
        name="mem_attend",
    )(proj, proj, mk, mv, gq_row)


def _outproj_kernel(x_ref, a_ref, b_ref, c_ref, d_ref, w_ref, y_ref):
    gw = GROUP_WIDTH
    acc = x_ref[...] + _dot(a_ref[...], w_ref[0:gw, :])
    acc = acc + _dot(b_ref[...], w_ref[gw:2 * gw, :])
    acc = acc + _dot(c_ref[...], w_ref[2 * gw:3 * gw, :])
    acc = acc + _dot(d_ref[...], w_ref[3 * gw:4 * gw, :])
    y_ref[...] = acc


def _out_proj(x2d, oa, ob, oc, od, w16):
    n, d = x2d.shape
    tm = min(512, n)
    gw = GROUP_WIDTH
    grp = pl.BlockSpec((tm, gw), lambda i: (i, 0))
    return pl.pallas_call(
        _outproj_kernel,
        grid=(n // tm,),
        in_specs=[pl.BlockSpec((tm, d), lambda i: (i, 0)), grp, grp, grp, grp,
                  pl.BlockSpec((4 * gw, d), lambda i: (0, 0))],
        out_specs=pl.BlockSpec((tm, d), lambda i: (i, 0)),
        out_shape=jax.ShapeDtypeStruct((n, d), F32),
        compiler_params=_cparams(("parallel",)),
        name="out_proj",
    )(x2d, oa, ob, oc, od, w16)


def _reorder_w_in(w):
    gw = GROUP_WIDTH
    o = 0
    seg = {}
    for name, width in (("qa", gw), ("ka", gw), ("va", gw), ("za", gw), ("ba", N_HEADS), ("aa", N_HEADS),
                        ("qb", gw), ("kb", gw), ("vb", gw), ("zb", gw),
                        ("qc", gw), ("kc", gw), ("vc", gw), ("zc", gw),
                        ("qi", IDX_HEADS * IDX_DIM), ("ki", IDX_DIM), ("wi", IDX_HEADS),
                        ("qd", gw), ("zd", gw)):
        seg[name] = w[:, o:o + width]
        o += width
    a_end = 4 * gw
    b_start = a_end + 2 * N_HEADS
    b_end = b_start + 8 * gw + IDX_HEADS * IDX_DIM
    d_start = b_end + IDX_DIM + IDX_HEADS
    w16 = w.astype(BF16)
    main = jnp.concatenate([w16[:, :a_end], w16[:, b_start:b_end], w16[:, d_start:]], axis=1)
    pad = jnp.zeros((w.shape[0], SMALL_COLS - IDX_DIM - IDX_HEADS - 2 * N_HEADS), BF16)
    small = jnp.concatenate([seg["ki"].astype(BF16), seg["wi"].astype(BF16), seg["ba"].astype(BF16),
                             seg["aa"].astype(BF16), pad], axis=1)
    return main, small


def _lane_row(vals, lane0):
    row = jnp.zeros((1, SMALL_COLS), F32)
    return row.at[0, lane0:lane0 + vals.shape[0]].set(vals.astype(F32))


def _rope_tables(pos):
    half = HEAD_DIM // 2
    inv = ROPE_THETA ** (-jnp.arange(half, dtype=F32) / half)
    ang = pos.astype(F32)[:, None] * inv[None, :]
    cos, sin = jnp.cos(ang), jnp.sin(ang)
    return jnp.concatenate([cos, cos], axis=-1), jnp.concatenate([-sin, sin], axis=-1)


def _round_up(x, m):
    return (x + m - 1) // m * m


def _mixer_layer(x, conv_buf, s_gdn, s_ret, past_k, past_v, past_ki, mem_k, mem_v, wts, layer, depth, prev_cache):
    b, l, d = x.shape
    n = b * l
    offset = 0 if past_k is None else past_k.shape[1]
    x2d = x.reshape(n, d)
    proj, small = _in_proj(x2d, wts["norm_g"], wts["w_main"], wts["w_small"])

    o_a, conv_new, s_gdn_new = _gdn(proj, small, wts["conv_w"], conv_buf, s_gdn, wts["alog_row"],
                                    wts["dtb_row"], wts["gdn_norm_g"], b, l)

    cos_t, sin_t = _rope_tables(offset + jnp.arange(l, dtype=I32))
    o_b, s_ret_new = _retention(proj, cos_t, sin_t, s_ret, wts["ret_norm_g"], wts["ret_norm_b"], b, l)

    direct = past_k is None and l % KT_A == 0
    prep = _dsa_prep(proj, small, wts["dsa_k_norm_g"], wts["idx_k_norm_g"], layer, depth, prev_cache, direct)
    cache = tuple(prep[:3])
    kc16, kic16 = prep[3].reshape(b, l, GROUP_WIDTH), prep[4].reshape(b, l, IDX_DIM)
    if direct:
        s_valid = l
        k16, ki16 = kc16, kic16
        vt16 = prep[5].reshape(b, l // KT_A, GROUP_WIDTH, KT_A)
    else:
        vc16 = cache[1][layer].astype(BF16).reshape(b, l, GROUP_WIDTH)
        if past_k is None:
            k_all, v_all, ki_all = kc16, vc16, kic16
        else:
            k_all = jnp.concatenate([past_k.reshape(b, offset, GROUP_WIDTH).astype(BF16), kc16], axis=1)
            v_all = jnp.concatenate([past_v.reshape(b, offset, GROUP_WIDTH).astype(BF16), vc16], axis=1)
            ki_all = jnp.concatenate([past_ki.astype(BF16), kic16], axis=1)
        s_valid = k_all.shape[1]
        s_pad = _round_up(s_valid, KT_A)
        padw = ((0, 0), (0, s_pad - s_valid), (0, 0))
        k16 = jnp.pad(k_all, padw)
        ki16 = jnp.pad(ki_all, padw)
        vt16 = jnp.pad(v_all, padw).reshape(b, s_pad // KT_A, KT_A, GROUP_WIDTH).transpose(0, 1, 3, 2)

    tq = 256 if l % 256 == 0 else 128
    q_len = _round_up(l, tq)
    if q_len != l:
        proj_q = jnp.pad(proj.reshape(b, l, MAIN_COLS), ((0, 0), (0, q_len - l), (0, 0))).reshape(b * q_len, MAIN_COLS)
        small_q = jnp.pad(small.reshape(b, l, SMALL_COLS), ((0, 0), (0, q_len - l), (0, 0))).reshape(b * q_len, SMALL_COLS)
    else:
        proj_q, small_q = proj, small
    o_c = _dsa(proj_q, small_q, k16, vt16, ki16, wts["dsa_q_norm_g"], b, q_len, tq, offset, s_valid)
    if q_len != l:
        o_c = o_c.reshape(b, q_len, GROUP_WIDTH)[:, :l].reshape(n, GROUP_WIDTH)

    o_d = _mem_attend(proj, mem_k, mem_v, wts["mem_q_norm_g"], b, l)

    y = _out_proj(x2d, o_a, o_b, o_c, o_d, wts["w_out"]).reshape(b, l, d)
    return y, (conv_new, s_gdn_new, s_ret_new), cache


def kernel(x_prompt, x_sample, cache_gdn_conv, state_gdn, state_ret, cache_dsa_k, cache_dsa_v, cache_idx_k, cache_mem_k, cache_mem_v, mem_prompt, norm_g, w_in, gdn_conv_w, gdn_a_log, gdn_dt_bias, gdn_norm_g, ret_norm_g, ret_norm_b, dsa_q_norm_g, dsa_k_norm_g, idx_k_norm_g, mem_norm_g, w_mem_kv, mem_q_norm_g, mem_k_norm_g, w_out):
    depth = w_in.shape[0]
    b = x_prompt.shape[0]
    n_mem = mem_prompt.shape[1]
    d = x_prompt.shape[-1]
    y_p, y_s = x_prompt, x_sample
    st_p, st_s, mem_p = [], [], []
    cache_p = cache_s = None
    for li in range(depth):
        w_main, w_small = _reorder_w_in(w_in[li])
        wts = dict(
            norm_g=norm_g[li][None, :], w_main=w_main, w_small=w_small, conv_w=gdn_conv_w[li],
            alog_row=_lane_row(gdn_a_log[li], LANE_ALPHA), dtb_row=_lane_row(gdn_dt_bias[li], LANE_ALPHA),
            gdn_norm_g=gdn_norm_g[li][None, :], ret_norm_g=ret_norm_g[li][None, :],
            ret_norm_b=ret_norm_b[li][None, :], dsa_q_norm_g=dsa_q_norm_g[li][None, :],
            dsa_k_norm_g=dsa_k_norm_g[li][None, :], idx_k_norm_g=idx_k_norm_g[li][None, :],
            mem_q_norm_g=mem_q_norm_g[li][None, :], w_out=w_out[li].astype(BF16),
        )
        mk, mv = _memory_kv(mem_prompt.reshape(b * n_mem, d), mem_norm_g[li][None, :],
                            w_mem_kv[li].astype(BF16), mem_k_norm_g[li][None, :])
        mk = mk.reshape(b, n_mem, GROUP_WIDTH)
        mv = mv.reshape(b, n_mem, GROUP_WIDTH)
        conv0 = jnp.zeros((b, CONV_W - 1, 3 * GROUP_WIDTH), F32)
        s0 = jnp.zeros((b, N_HEADS, HEAD_DIM, HEAD_DIM), F32)
        y_p, sp, cache_p = _mixer_layer(y_p, conv0, s0, s0, None, None, None, mk, mv, wts, li, depth, cache_p)
        st_p.append(sp)
        mem_p.append((mk.reshape(b, n_mem, N_HEADS, HEAD_DIM), mv.reshape(b, n_mem, N_HEADS, HEAD_DIM)))
        bs = x_sample.shape[0]
        y_s, ss, cache_s = _mixer_layer(y_s, cache_gdn_conv[li], state_gdn[li], state_ret[li],
                                        cache_dsa_k[li], cache_dsa_v[li], cache_idx_k[li],
                                        cache_mem_k[li].reshape(bs, n_mem, GROUP_WIDTH),
                                        cache_mem_v[li].reshape(bs, n_mem, GROUP_WIDTH), wts, li, depth, cache_s)
        st_s.append(ss)

    def stack(lst, k):
        return jnp.stack([s[k] for s in lst])

    def caches(c, bb, ll):
        return (c[0].reshape(depth, bb, ll, N_HEADS, HEAD_DIM), c[1].reshape(depth, bb, ll, N_HEADS, HEAD_DIM),
                c[2].reshape(depth, bb, ll, IDX_DIM))

    return ((y_p, y_s, stack(st_p, 0), stack(st_p, 1), stack(st_p, 2))
            + caches(cache_p, b, x_prompt.shape[1])
            + (stack(mem_p, 0), stack(mem_p, 1), stack(st_s, 0), stack(st_s, 1), stack(st_s, 2))
            + caches(cache_s, x_sample.shape[0], x_sample.shape[1]))
```

```python
import functools
import math

import jax
import jax.numpy as jnp
from jax import lax
from jax.experimental import pallas as pl
from jax.experimental.pallas import tpu as pltpu

F32 = jnp.float32
BF16 = jnp.bfloat16
I32 = jnp.int32

HEAD_DIM = 128
N_HEADS = 4
GROUP_WIDTH = N_HEADS * HEAD_DIM
CHUNK = 64
CONV_W = 4
IDX_HEADS = 16
IDX_DIM = 64
TOPK_MAX = 256
ROPE_THETA = 10000.0
EPS = 1e-6

COL_QA, COL_KA, COL_VA, COL_ZA = 0, 512, 1024, 1536
COL_QB, COL_KB, COL_VB, COL_ZB = 2048, 2560, 3072, 3584
COL_QC, COL_KC, COL_VC, COL_ZC = 4096, 4608, 5120, 5632
COL_QI = 6144
COL_QD, COL_ZD = 7168, 7680
MAIN_COLS = 8192
LANE_KI, LANE_WI, LANE_BETA, LANE_ALPHA = 0, 64, 80, 84
SMALL_COLS = 128

INT_MIN = -2 ** 31
NEG_BIG = -1e30
LOG2_E = 1.4426950408889634
VMEM_LIMIT = 56 * 1024 * 1024
HI = lax.Precision.HIGHEST


def _cparams(sem):
    return pltpu.CompilerParams(dimension_semantics=sem, vmem_limit_bytes=VMEM_LIMIT)


def _dot(a, b):
    return jnp.dot(a, b, preferred_element_type=F32)


def _dot_nt(a, b):
    return lax.dot_general(a, b, (((1,), (1,)), ((), ())), preferred_element_type=F32)


def _dot3(a, b):
    a_hi = a.astype(BF16)
    b_hi = b.astype(BF16)
    a_lo = (a - a_hi.astype(F32)).astype(BF16)
    b_lo = (b - b_hi.astype(F32)).astype(BF16)
    return _dot(a_hi, b_hi) + (_dot(a_hi, b_lo) + _dot(a_lo, b_hi))


def _silu(x):
    return x * jax.nn.sigmoid(x)


def _inproj_kernel(x_ref, g_ref, w_ref, ws_ref, o_ref, os_ref, h_ref):
    @pl.when(pl.program_id(1) == 0)
    def _():
        x = x_ref[...]
        y = x * lax.rsqrt(jnp.mean(x * x, axis=-1, keepdims=True) + EPS) * g_ref[...]
        hb = y.astype(BF16)
        h_ref[...] = hb
        os_ref[...] = _dot(hb, ws_ref[...])

    o_ref[...] = _dot(h_ref[...], w_ref[...])


def _in_proj(x2d, g_row, w_main, w_small):
    n, d = x2d.shape
    tm = min(1024, n)
    tn = 1024
    return pl.pallas_call(
        _inproj_kernel,
        grid=(n // tm, MAIN_COLS // tn),
        in_specs=[
            pl.BlockSpec((tm, d), lambda i, j: (i, 0)),
            pl.BlockSpec((1, d), lambda i, j: (0, 0)),
            pl.BlockSpec((d, tn), lambda i, j: (0, j)),
            pl.BlockSpec((d, SMALL_COLS), lambda i, j: (0, 0)),
        ],
        out_specs=[
            pl.BlockSpec((tm, tn), lambda i, j: (i, j)),
            pl.BlockSpec((tm, SMALL_COLS), lambda i, j: (i, 0)),
        ],
        out_shape=[jax.ShapeDtypeStruct((n, MAIN_COLS), F32),
                   jax.ShapeDtypeStruct((n, SMALL_COLS), F32)],
        scratch_shapes=[pltpu.VMEM((tm, d), BF16)],
        compiler_params=_cparams(("parallel", "arbitrary")),
        name="in_proj",
    )(x2d, g_row, w_main, w_small)


def _memkv_kernel(x_ref, g_ref, w_ref, gk_ref, mk_ref, mv_ref):
    x = x_ref[...]
    y = x * lax.rsqrt(jnp.mean(x * x, axis=-1, keepdims=True) + EPS) * g_ref[...]
    kv = _dot(y.astype(BF16), w_ref[...])
    gk = gk_ref[...]
    for h in range(N_HEADS):
        sl = slice(h * HEAD_DIM, (h + 1) * HEAD_DIM)
        kh = kv[:, sl]
        mk_ref[:, sl] = kh * lax.rsqrt(jnp.mean(kh * kh, axis=-1, keepdims=True) + EPS) * gk
    mv_ref[...] = kv[:, GROUP_WIDTH:]


def _memory_kv(mem2d, g_row, w_kv, gk_row):
    n, d = mem2d.shape
    tm = min(256, n)
    return pl.pallas_call(
        _memkv_kernel,
        grid=(n // tm,),
        in_specs=[
            pl.BlockSpec((tm, d), lambda i: (i, 0)),
            pl.BlockSpec((1, d), lambda i: (0, 0)),
            pl.BlockSpec((d, 2 * GROUP_WIDTH), lambda i: (0, 0)),
            pl.BlockSpec((1, HEAD_DIM), lambda i: (0, 0)),
        ],
        out_specs=[pl.BlockSpec((tm, GROUP_WIDTH), lambda i: (i, 0)),
                   pl.BlockSpec((tm, GROUP_WIDTH), lambda i: (i, 0))],
        out_shape=[jax.ShapeDtypeStruct((n, GROUP_WIDTH), F32),
                   jax.ShapeDtypeStruct((n, GROUP_WIDTH), F32)],
        compiler_params=_cparams(("parallel",)),
        name="memory_kv",
    )(mem2d, g_row, w_kv, gk_row)


def _gdn_kernel(qkv_ref, z_ref, sm_ref, cw_ref, cb_ref, s0_ref, alog_ref, dtb_ref, gn_ref,
                o_ref, conv_ref, st_ref, xbuf_ref, s_ref, *, t_blk, chunk):
    j = pl.program_id(1)
    nj = pl.num_programs(1)
    gw3 = 3 * GROUP_WIDTH

    @pl.when(j == 0)
    def _():
        xbuf_ref[0:8, :] = jnp.zeros((8, gw3), F32)
        xbuf_ref[5:8, :] = cb_ref[0]
        s_ref[...] = s0_ref[0]

    @pl.when(j > 0)
    def _():
        xbuf_ref[0:8, :] = xbuf_ref[t_blk:t_blk + 8, :]

    xbuf_ref[8:8 + t_blk, :] = qkv_ref[...]
    conv_ref[0] = xbuf_ref[t_blk + 5:t_blk + 8, :]

    cw = cw_ref[...]
    y = xbuf_ref[5:5 + t_blk, :] * cw[0:1, :]
    for jj in range(1, CONV_W):
        y = y + xbuf_ref[5 + jj:5 + jj + t_blk, :] * cw[jj:jj + 1, :]
    y = _silu(y)

    sm = sm_ref[...]
    lane = lax.broadcasted_iota(I32, sm.shape, 1)
    beta_all = jax.nn.sigmoid(sm)
    xs = sm + dtb_ref[...]
    softplus = jnp.maximum(xs, 0.0) + jnp.log1p(jnp.exp(-jnp.abs(xs)))
    g_all = -jnp.exp(alog_ref[...]) * softplus
    g_all = jnp.where((lane >= LANE_ALPHA) & (lane < LANE_ALPHA + N_HEADS), g_all, 0.0)
    ri = lax.broadcasted_iota(I32, (t_blk, t_blk), 0)
    ci = lax.broadcasted_iota(I32, (t_blk, t_blk), 1)
    tri = jnp.where((ri // chunk == ci // chunk) & (ci <= ri), 1.0, 0.0).astype(F32)
    gcum = jnp.dot(tri, g_all, preferred_element_type=F32, precision=HI)
    gcum_t = gcum.T

    blk = min(2 * chunk, t_blk)
    n_blk = t_blk // blk
    cpb = blk // chunk
    rb = lax.broadcasted_iota(I32, (blk, blk), 0)
    cb = lax.broadcasted_iota(I32, (blk, blk), 1)
    same = (rb // chunk) == (cb // chunk)
    causal = same & (cb <= rb)
    strict = same & (cb < rb)
    eye = jnp.where(rb == cb, 1.0, 0.0).astype(F32)
    n_dbl = max(int(math.log2(chunk)) - 1, 0)
    gn = gn_ref[...]
    z_all = z_ref[...]

    units = [(h, b) for h in range(N_HEADS) for b in range(n_blk)]
    qs, ks, gcols, p_mats, x_mats, qk_mats, vb_mats, kg_mats = {}, {}, {}, {}, {}, {}, {}, {}
    for h in range(N_HEADS):
        qh = y[:, COL_QA + h * HEAD_DIM:COL_QA + (h + 1) * HEAD_DIM]
        kh = y[:, COL_KA + h * HEAD_DIM:COL_KA + (h + 1) * HEAD_DIM]
        vh = y[:, COL_VA + h * HEAD_DIM:COL_VA + (h + 1) * HEAD_DIM]
        qh = qh * lax.rsqrt(jnp.sum(qh * qh, axis=-1, keepdims=True) + EPS) * (HEAD_DIM ** -0.5)
        kh = kh * lax.rsqrt(jnp.sum(kh * kh, axis=-1, keepdims=True) + EPS)
        for b in range(n_blk):
            rs = slice(b * blk, (b + 1) * blk)
            gcol = gcum[rs, LANE_ALPHA + h:LANE_ALPHA + h + 1]
            grow = gcum_t[LANE_ALPHA + h:LANE_ALPHA + h + 1, rs]
            bcol = beta_all[rs, LANE_BETA + h:LANE_BETA + h + 1]
            decay = jnp.where(causal, jnp.exp(jnp.where(causal, gcol - grow, 0.0)), 0.0)
            qc, kc, vc = qh[rs], kh[rs], vh[rs]
            kb = kc * bcol
            kc16 = kc.astype(BF16)
            a_mat = jnp.where(strict, _dot_nt(kb.astype(BF16), kc16) * decay, 0.0)
            qk_mats[h, b] = jnp.where(causal, _dot_nt(qc.astype(BF16), kc16) * decay, 0.0).astype(BF16)
            p_mats[h, b] = -a_mat
            x_mats[h, b] = eye - a_mat
            vb_mats[h, b] = (vc * bcol).astype(BF16)
            kg_mats[h, b] = (kb * jnp.exp(gcol)).astype(BF16)
            qs[h, b], ks[h, b], gcols[h, b] = qc, kc, gcol

    for _ in range(n_dbl):
        for u_ in units:
            p_mats[u_] = _dot3(p_mats[u_], p_mats[u_])
        for u_ in units:
            x_mats[u_] = x_mats[u_] + _dot3(x_mats[u_], p_mats[u_])

    u_mats, w_mats = {}, {}
    for u_ in units:
        x16 = x_mats[u_].astype(BF16)
        u_mats[u_] = _dot(x16, vb_mats[u_])
        w_mats[u_] = _dot(x16, kg_mats[u_]).astype(BF16)

    states = [s_ref[h] for h in range(N_HEADS)]
    for b in range(n_blk):
        o_state = {h: [] for h in range(N_HEADS)}
        v_new = {h: [] for h in range(N_HEADS)}
        for c in range(cpb):
            cs = slice(c * chunk, (c + 1) * chunk)
            for h in range(N_HEADS):
                s = states[h]
                s16 = s.astype(BF16)
                gcol = gcols[h, b][cs]
                glast = gcols[h, b][(c + 1) * chunk - 1:(c + 1) * chunk]
                vn = u_mats[h, b][cs] - _dot(w_mats[h, b][cs], s16)
                vn16 = vn.astype(BF16)
                kd_t = (ks[h, b][cs] * jnp.exp(glast - gcol)).T.astype(BF16)
                states[h] = s * jnp.exp(glast) + _dot(kd_t, vn16)
                o_state[h].append(_dot((qs[h, b][cs] * jnp.exp(gcol)).astype(BF16), s16))
                v_new[h].append(vn16)
        for h in range(N_HEADS):
            sl = slice(h * HEAD_DIM, (h + 1) * HEAD_DIM)
            rs = slice(b * blk, (b + 1) * blk)
            o = jnp.concatenate(o_state[h], axis=0) + _dot(qk_mats[h, b], jnp.concatenate(v_new[h], axis=0))
            on = o * lax.rsqrt(jnp.mean(o * o, axis=-1, keepdims=True) + EPS) * gn
            o_ref[rs, sl] = (on * _silu(z_all[rs, sl])).astype(o_ref.dtype)
    for h in range(N_HEADS):
        s_ref[h] = states[h]

    @pl.when(j == nj - 1)
    def _():
        st_ref[0] = s_ref[...]


def _gdn(proj, small, conv_w, conv_buf, s0, alog_row, dtb_row, gn_row, batch, seqlen):
    n = batch * seqlen
    t_blk = min(256, seqlen)
    chunk = min(CHUNK, seqlen)
    nj = seqlen // t_blk
    gw3 = 3 * GROUP_WIDTH
    kern = functools.partial(_gdn_kernel, t_blk=t_blk, chunk=chunk)
    return pl.pallas_call(
        kern,
        grid=(batch, nj),
        in_specs=[
            pl.BlockSpec((t_blk, gw3), lambda b, j: (b * nj + j, COL_QA // gw3)),
            pl.BlockSpec((t_blk, GROUP_WIDTH), lambda b, j: (b * nj + j, COL_ZA // GROUP_WIDTH)),
            pl.BlockSpec((t_blk, SMALL_COLS), lambda b, j: (b * nj + j, 0)),
            pl.BlockSpec((CONV_W, gw3), lambda b, j: (0, 0)),
            pl.BlockSpec((1, CONV_W - 1, gw3), lambda b, j: (b, 0, 0)),
            pl.BlockSpec((1, N_HEADS, HEAD_DIM, HEAD_DIM), lambda b, j: (b, 0, 0, 0)),
            pl.BlockSpec((1, SMALL_COLS), lambda b, j: (0, 0)),
            pl.BlockSpec((1, SMALL_COLS), lambda b, j: (0, 0)),
            pl.BlockSpec((1, HEAD_DIM), lambda b, j: (0, 0)),
        ],
        out_specs=[
            pl.BlockSpec((t_blk, GROUP_WIDTH), lambda b, j: (b * nj + j, 0)),
            pl.BlockSpec((1, CONV_W - 1, gw3), lambda b, j: (b, 0, 0)),
            pl.BlockSpec((1, N_HEADS, HEAD_DIM, HEAD_DIM), lambda b, j: (b, 0, 0, 0)),
        ],
        out_shape=[
            jax.ShapeDtypeStruct((n, GROUP_WIDTH), BF16),
            jax.ShapeDtypeStruct((batch, CONV_W - 1, gw3), F32),
            jax.ShapeDtypeStruct((batch, N_HEADS, HEAD_DIM, HEAD_DIM), F32),
        ],
        scratch_shapes=[pltpu.VMEM((t_blk + 8, gw3), F32),
                        pltpu.VMEM((N_HEADS, HEAD_DIM, HEAD_DIM), F32)],
        compiler_params=_cparams(("parallel", "arbitrary")),
        name="gdn",
    )(proj, proj, small, conv_w, conv_buf, s0, alog_row, dtb_row, gn_row)


def _ret_kernel(q_ref, k_ref, v_ref, z_ref, cos_ref, sin_ref, s0_ref, g_ref, b_ref,
                o_ref, st_ref, s_ref, *, t_blk):
    j = pl.program_id(1)
    nj = pl.num_programs(1)

    @pl.when(j == 0)
    def _():
        s_ref[...] = s0_ref[0]

    cos = cos_ref[...]
    sin = sin_ref[...]
    ri = lax.broadcasted_iota(I32, (t_blk, t_blk), 0)
    ci = lax.broadcasted_iota(I32, (t_blk, t_blk), 1)
    causal = ci <= ri
    rel = jnp.where(causal, ri - ci, 0).astype(F32)
    idx_col = lax.broadcasted_iota(I32, (t_blk, 1), 0).astype(F32)
    gamma_g = g_ref[...]
    gamma_b = b_ref[...]
    q_all, k_all, v_all, z_all = q_ref[...], k_ref[...], v_ref[...], z_ref[...]

    for h in range(N_HEADS):
        sl = slice(h * HEAD_DIM, (h + 1) * HEAD_DIM)
        lg = math.log(1.0 - 2.0 ** (-5.0 - h))
        qh, kh, vh = q_all[:, sl], k_all[:, sl], v_all[:, sl]
        qh = qh * cos + pltpu.roll(qh, HEAD_DIM // 2, 1) * sin
        kh = (kh * cos + pltpu.roll(kh, HEAD_DIM // 2, 1) * sin) * (HEAD_DIM ** -0.5)
        d_mat = jnp.where(causal, jnp.exp(lg * rel), 0.0)
        q16, k16, v16 = qh.astype(BF16), kh.astype(BF16), vh.astype(BF16)
        o_intra = _dot((_dot_nt(q16, k16) * d_mat).astype(BF16), v16)
        s = s_ref[h]
        o_cross = _dot(q16, s.astype(BF16)) * jnp.exp(lg * (idx_col + 1.0))
        kd_t = (kh * jnp.exp(lg * (t_blk - 1.0 - idx_col))).T.astype(BF16)
        s_ref[h] = s * math.exp(lg * t_blk) + _dot(kd_t, v16)
        o = o_intra + o_cross
        mu = jnp.mean(o, axis=-1, keepdims=True)
        oc = o - mu
        var = jnp.mean(oc * oc, axis=-1, keepdims=True)
        on = oc * lax.rsqrt(var + EPS) * gamma_g + gamma_b
        o_ref[:, sl] = (on * _silu(z_all[:, sl])).astype(o_ref.dtype)

    @pl.when(j == nj - 1)
    def _():
        st_ref[0] = s_ref[...]


def _retention(proj, cos_t, sin_t, s0, g_row, b_row, batch, seqlen):
    n = batch * seqlen
    t_blk = min(256, seqlen)
    nj = seqlen // t_blk
    gw = GROUP_WIDTH
    kern = functools.partial(_ret_kernel, t_blk=t_blk)

    def col(c):
        return pl.BlockSpec((t_blk, gw), lambda b, j: (b * nj + j, c // gw))

    return pl.pallas_call(
        kern,
        grid=(batch, nj),
        in_specs=[
            col(COL_QB), col(COL_KB), col(COL_VB), col(COL_ZB),
            pl.BlockSpec((t_blk, HEAD_DIM), lambda b, j: (j, 0)),
            pl.BlockSpec((t_blk, HEAD_DIM), lambda b, j: (j, 0)),
            pl.BlockSpec((1, N_HEADS, HEAD_DIM, HEAD_DIM), lambda b, j: (b, 0, 0, 0)),
            pl.BlockSpec((1, HEAD_DIM), lambda b, j: (0, 0)),
            pl.BlockSpec((1, HEAD_DIM), lambda b, j: (0, 0)),
        ],
        out_specs=[
            pl.BlockSpec((t_blk, gw), lambda b, j: (b * nj + j, 0)),
            pl.BlockSpec((1, N_HEADS, HEAD_DIM, HEAD_DIM), lambda b, j: (b, 0, 0, 0)),
        ],
        out_shape=[
            jax.ShapeDtypeStruct((n, gw), BF16),
            jax.ShapeDtypeStruct((batch, N_HEADS, HEAD_DIM, HEAD_DIM), F32),
        ],
        scratch_shapes=[pltpu.VMEM((N_HEADS, HEAD_DIM, HEAD_DIM), F32)],
        compiler_params=_cparams(("parallel", "arbitrary")),
        name="retention",
    )(proj, proj, proj, proj, cos_t, sin_t, s0, g_row, b_row)


def _dsa_prep_kernel(k_ref, v_ref, sm_ref, gk_ref, gi_ref, *rest, n_prev, with_vt):
    outs = rest[n_prev:]
    ko_ref, vo_ref, kio_ref, k16_ref, ki16_ref = outs[:5]
    k = k_ref[...]
    gk = gk_ref[...]
    for h in range(N_HEADS):
        sl = slice(h * HEAD_DIM, (h + 1) * HEAD_DIM)
        kh = k[:, sl]
        kn = kh * lax.rsqrt(jnp.mean(kh * kh, axis=-1, keepdims=True) + EPS) * gk
        ko_ref[:, h, :] = kn
        k16_ref[:, sl] = kn.astype(BF16)
    v = v_ref[...]
    for h in range(N_HEADS):
        vo_ref[:, h, :] = v[:, h * HEAD_DIM:(h + 1) * HEAD_DIM]
    if with_vt:
        outs[5][...] = v.T.astype(BF16)
    ki = sm_ref[...][:, LANE_KI:LANE_KI + IDX_DIM]
    kin = ki * lax.rsqrt(jnp.mean(ki * ki, axis=-1, keepdims=True) + EPS) * gi_ref[...]
    kio_ref[...] = kin
    ki16_ref[...] = kin.astype(BF16)


def _dsa_prep(proj, small, gk_row, gi_row, layer, depth, prev, with_vt):
    n = proj.shape[0]
    tm = min(KT_A, n)
    gw = GROUP_WIDTH
    n_prev = 0 if prev is None else 3
    kern = functools.partial(_dsa_prep_kernel, n_prev=n_prev, with_vt=with_vt)
    in_specs = [
        pl.BlockSpec((tm, gw), lambda i: (i, COL_KC // gw)),
        pl.BlockSpec((tm, gw), lambda i: (i, COL_VC // gw)),
        pl.BlockSpec((tm, SMALL_COLS), lambda i: (i, 0)),
        pl.BlockSpec((1, HEAD_DIM), lambda i: (0, 0)),
        pl.BlockSpec((1, IDX_DIM), lambda i: (0, 0)),
    ] + [pl.BlockSpec(memory_space=pl.ANY)] * n_prev
    out_specs = [pl.BlockSpec((None, tm, N_HEADS, HEAD_DIM), lambda i: (layer, i, 0, 0)),
                 pl.BlockSpec((None, tm, N_HEADS, HEAD_DIM), lambda i: (layer, i, 0, 0)),
                 pl.BlockSpec((None, tm, IDX_DIM), lambda i: (layer, i, 0)),
                 pl.BlockSpec((tm, gw), lambda i: (i, 0)),
                 pl.BlockSpec((tm, IDX_DIM), lambda i: (i, 0))]
    out_shape = [jax.ShapeDtypeStruct((depth, n, N_HEADS, HEAD_DIM), F32),
                 jax.ShapeDtypeStruct((depth, n, N_HEADS, HEAD_DIM), F32),
                 jax.ShapeDtypeStruct((depth, n, IDX_DIM), F32),
                 jax.ShapeDtypeStruct((n, gw), BF16),
                 jax.ShapeDtypeStruct((n, IDX_DIM), BF16)]
    if with_vt:
        out_specs.append(pl.BlockSpec((None, gw, tm), lambda i: (i, 0, 0)))
        out_shape.append(jax.ShapeDtypeStruct((n // tm, gw, tm), BF16))
    args = (proj, proj, small, gk_row, gi_row) + (() if prev is None else tuple(prev))
    return pl.pallas_call(
        kern,
        grid=(n // tm,),
        in_specs=in_specs,
        out_specs=out_specs,
        out_shape=out_shape,
        input_output_aliases={5 + t: t for t in range(n_prev)},
        compiler_params=_cparams(("parallel",)),
        name="dsa_prep",
    )(*args)


KT_I = 128
KT_A = 512


def _dsa_kernel(q_ref, qi_ref, z_ref, sm_ref, k_ref, vt_ref, ki_ref, gq_ref, o_ref,
                key_ref, qit_ref, wt_ref, jlim_ref, qt_ref, acc_ref, lg_ref, p_ref, key16_ref, low16_ref, *, tq, offset, s_valid, topk):
    i = pl.program_id(1)
    pos0 = offset + i * tq
    t_pos = pos0 + lax.broadcasted_iota(I32, (1, tq), 1)
    t_chunk = t_pos // CHUNK
    n_adm_row = jnp.minimum((t_chunk + 1) * CHUNK, s_valid)
    n_keys = jnp.minimum(((pos0 + tq - 1) // CHUNK + 1) * CHUNK, s_valid)
    n_at = (n_keys + KT_A - 1) // KT_A
    n_sub = KT_A // KT_I
    n_it_full = n_at * n_sub

    qit_ref[...] = qi_ref[...].T.astype(BF16)
    wt_ref[...] = sm_ref[...].T * (IDX_HEADS ** -0.5 * IDX_DIM ** -0.5)

    def index_step(kt, carry):
        r0 = pl.multiple_of(kt * KT_I, KT_I)
        ki_t = ki_ref[0, pl.ds(r0, KT_I), :]
        acc = jnp.zeros((KT_I, tq), F32)
        for h in range(IDX_HEADS):
            sc = _dot(ki_t, qit_ref[h * IDX_DIM:(h + 1) * IDX_DIM, :])
            acc = acc + wt_ref[LANE_WI + h:LANE_WI + h + 1, :] * jnp.maximum(sc, 0.0)
        bits = lax.bitcast_convert_type(acc, I32)
        key = bits ^ ((bits >> 31) & 0x7FFFFFFF)
        s_pos = r0 + lax.broadcasted_iota(I32, (KT_I, 1), 0)
        adm = (s_pos // CHUNK <= t_chunk) & (s_pos < s_valid)
        key = jnp.where(adm, key, INT_MIN)
        key_ref[pl.ds(r0, KT_I), :] = key
        key16_ref[pl.ds(r0, KT_I), :] = (key >> 16).astype(jnp.int16)
        return carry

    lax.fori_loop(0, n_it_full // 2, lambda t, c: index_step(2 * t + 1, index_step(2 * t, c)), 0)

    def count(pred_fn):
        def body(kt, accs):
            out = []
            for s in range(n_sub):
                r0 = pl.multiple_of(kt * KT_A + s * KT_I, KT_I)
                key = key_ref[pl.ds(r0, KT_I), :]
                s_pos = r0 + lax.broadcasted_iota(I32, (KT_I, 1), 0)
                m = jnp.where(pred_fn(key, s_pos), 1, 0).astype(I32)
                out.append(accs[s] + jnp.sum(m.reshape(KT_I // 8, 8, tq), axis=0))
            return tuple(out)
        accs = lax.fori_loop(0, n_at, body, tuple(jnp.zeros((8, tq), I32) for _ in range(n_sub)))
        return jnp.sum(sum(accs[1:], accs[0]), axis=0, keepdims=True)

    def count16(ref16, pred_fn):
        def body(kt, accs):
            out = []
            for s in range(n_sub):
                r0 = pl.multiple_of(kt * KT_A + s * KT_I, KT_I)
                m = jnp.where(pred_fn(ref16[pl.ds(r0, KT_I), :]), jnp.int16(1), jnp.int16(0))
                m = m.reshape(KT_I // 16, 16, tq)
                acc = accs[s]
                for q in range(KT_I // 16):
                    acc = acc + m[q]
                out.append(acc)
            return tuple(out)
        accs = lax.fori_loop(0, n_at, body, tuple(jnp.zeros((16, tq), jnp.int16) for _ in range(n_sub)))
        tot = sum((a.astype(I32) for a in accs[1:]), accs[0].astype(I32))
        return jnp.sum(tot, axis=0, keepdims=True)

    small = jnp.where(n_adm_row <= topk, 1, 0).astype(I32)

    def all_done(done):
        return jnp.min(done.astype(F32)) > 0.0

    def search(first_bit, last_bit, count_ge, carry, early_exit):
        def step(b, c):
            v, done, thr = c
            cand_u = v | jnp.left_shift(jnp.int32(1), 31 - b)
            cand_s = cand_u ^ INT_MIN
            cnt = count_ge(cand_s)
            v = jnp.where(cnt >= topk, cand_u, v)
            newly = (cnt == topk) & (done == 0)
            thr = jnp.where(newly, cand_s, thr)
            return v, jnp.where(newly, 1, done), thr

        if not early_exit:
            return lax.fori_loop(first_bit, last_bit, step, carry)

        def cond(c):
            return (c[0] < last_bit) & jnp.logical_not(all_done(c[1][1]))

        def body(c):
            b, inner = c
            return b + 4, lax.fori_loop(b, b + 4, step, inner)

        assert (last_bit - first_bit) % 4 == 0
        return lax.while_loop(cond, body, (jnp.int32(first_bit), carry))[1]

    v0 = jnp.zeros((1, tq), I32)
    thr0 = jnp.full((1, tq), INT_MIN + 1, I32)
    carry = search(0, 16, lambda cand_s: count16(key16_ref, lambda k16: k16 >= (cand_s >> 16).astype(jnp.int16)),
                   (v0, small, thr0), early_exit=False)
    hi16 = ((carry[0] ^ INT_MIN) >> 16).astype(jnp.int16)
    n_above = count16(key16_ref, lambda k16: k16 > hi16)

    @pl.when(jnp.logical_not(all_done(carry[1])))
    def _():
        hi32 = (carry[0] ^ INT_MIN) >> 16

        def low_step(kt, c):
            for s in range(n_sub):
                r0 = pl.multiple_of(kt * KT_A + s * KT_I, KT_I)
                key = key_ref[pl.ds(r0, KT_I), :]
                low = jnp.where((key >> 16) == hi32, (key & 0xFFFF) - 32768, -32768)
                low16_ref[pl.ds(r0, KT_I), :] = low.astype(jnp.int16)
            return c

        lax.fori_loop(0, n_at, low_step, 0)

    def count_low(cand_s):
        c16 = ((cand_s & 0xFFFF) - 32768).astype(jnp.int16)
        return n_above + count16(low16_ref, lambda l16: l16 >= c16)

    v_u, done_i, thr = search(16, 32, count_low, carry, early_exit=True)
    done = done_i != 0
    v_s = v_u ^ INT_MIN
    any_tie = jnp.logical_not(all_done(done_i))

    jlim_ref[...] = jnp.zeros(jlim_ref.shape, I32)

    @pl.when(any_tie)
    def _():
        n_gt = count(lambda key, s_pos: key > v_s)
        need = topk - n_gt
        pos_bits = max(int(math.ceil(math.log2(max(k_ref.shape[1], 2)))), 1) + 1

        def pos_step(b, jv):
            cand = jv | jnp.left_shift(jnp.int32(1), pos_bits - 1 - b)
            cnt = count(lambda key, s_pos: (key == v_s) & (s_pos < cand))
            return jnp.where(cnt <= need, cand, jv)

        jlim_ref[0:1, :] = lax.fori_loop(0, pos_bits, pos_step, jnp.zeros((1, tq), I32))

    j_lim = jlim_ref[0:1, :]

    v_eff = jnp.where(done, thr - 1, v_s)
    j_eff = jnp.where(done, 0, j_lim)

    def bias_loop(with_ties):
        def bias_step(kt, carry):
            for s in range(n_sub):
                r0 = pl.multiple_of(kt * KT_A + s * KT_I, KT_I)
                key = key_ref[pl.ds(r0, KT_I), :]
                sel = key > v_eff
                if with_ties:
                    s_pos = r0 + lax.broadcasted_iota(I32, (KT_I, 1), 0)
                    sel = sel | ((key == v_eff) & (s_pos < j_eff) & (key != INT_MIN))
                bias = jnp.where(sel, 0.0, NEG_BIG).astype(F32)
                key_ref[pl.ds(r0, KT_I), :] = lax.bitcast_convert_type(bias, I32)
            return carry

        lax.fori_loop(0, n_at, bias_step, 0)

    pl.when(any_tie)(lambda: bias_loop(True))
    pl.when(jnp.logical_not(any_tie))(lambda: bias_loop(False))

    gq = gq_ref[...]
    q_all = q_ref[...]
    for h in range(N_HEADS):
        sl = slice(h * HEAD_DIM, (h + 1) * HEAD_DIM)
        qh = q_all[:, sl]
        qh = qh * lax.rsqrt(jnp.mean(qh * qh, axis=-1, keepdims=True) + EPS) * gq * (HEAD_DIM ** -0.5 * LOG2_E)
        qt_ref[sl, :] = qh.T.astype(BF16)
    acc_ref[...] = jnp.zeros(acc_ref.shape, F32)

    heads = [slice(h * HEAD_DIM, (h + 1) * HEAD_DIM) for h in range(N_HEADS)]
    kt_h = KT_A // 2
    n_sub_h = kt_h // KT_I

    def issue_scores(kt, half):
        r0 = pl.multiple_of(kt * KT_A + half * kt_h, kt_h)
        for h, sl in enumerate(heads):
            lg_ref[half, h] = _dot(k_ref[0, pl.ds(r0, kt_h), sl], qt_ref[sl, :])

    def issue_values(kt, half):
        return [_dot(vt_ref[0, kt, sl, half * kt_h:(half + 1) * kt_h], p_ref[half, h])
                for h, sl in enumerate(heads)]

    def softmax_half(kt, half, ms, ls):
        r0 = pl.multiple_of(kt * KT_A + half * kt_h, kt_h)
        new_ms, new_ls, alphas = [], [], []
        for h in range(N_HEADS):
            mx = None
            for s in range(n_sub_h):
                rs = slice(s * KT_I, (s + 1) * KT_I)
                bias = lax.bitcast_convert_type(key_ref[pl.ds(r0 + s * KT_I, KT_I), :], F32)
                lg = lg_ref[half, h, rs, :] + bias
                lg_ref[half, h, rs, :] = lg
                part = jnp.max(lg.reshape(KT_I // 8, 8, tq), axis=0)
                mx = part if mx is None else jnp.maximum(mx, part)
            m_new = jnp.maximum(ms[h], jnp.max(mx, axis=0, keepdims=True))
            alphas.append(jnp.exp2(ms[h] - m_new))
            new_ms.append(m_new)
        for h in range(N_HEADS):
            lsum = None
            for s in range(n_sub_h):
                rs = slice(s * KT_I, (s + 1) * KT_I)
                p = jnp.exp2(lg_ref[half, h, rs, :] - new_ms[h])
                p_ref[half, h, rs, :] = p.astype(BF16)
                part = jnp.sum(p.reshape(KT_I // 8, 8, tq), axis=0)
                lsum = part if lsum is None else lsum + part
            new_ls.append(alphas[h] * ls[h] + jnp.sum(lsum, axis=0, keepdims=True))
        return new_ms, new_ls, alphas

    def accumulate(alphas, pv):
        for h in range(N_HEADS):
            acc_ref[h] = alphas[h] * acc_ref[h] + pv[h]

    def att_step(kt, carry):
        ms, ls = carry
        issue_scores(kt, 0)
        issue_scores(kt, 1)
        ms, ls, alphas_a = softmax_half(kt, 0, ms, ls)
        pv_a = issue_values(kt, 0)
        ms, ls, alphas_b = softmax_half(kt, 1, ms, ls)
        accumulate(alphas_a, pv_a)
        accumulate(alphas_b, issue_values(kt, 1))
        return tuple(ms), tuple(ls)

    m0 = tuple(jnp.full((1, tq), NEG_BIG, F32) for _ in range(N_HEADS))
    l0 = tuple(jnp.zeros((1, tq), F32) for _ in range(N_HEADS))
    _, ls = lax.fori_loop(0, n_at, att_step, (m0, l0))
    z_all = z_ref[...]
    for h in range(N_HEADS):
        sl = slice(h * HEAD_DIM, (h + 1) * HEAD_DIM)
        oh = (acc_ref[h] / ls[h]).T
        o_ref[:, sl] = (oh * _silu(z_all[:, sl])).astype(o_ref.dtype)


def _dsa(proj, small, k_all16, vt16, ki16, gq_row, batch, q_len, tq, offset, s_valid):
    n = batch * q_len
    nq = q_len // tq
    s_pad = k_all16.shape[1]
    topk = min(TOPK_MAX, s_valid // 4)
    gw = GROUP_WIDTH
    qi_w = IDX_HEADS * IDX_DIM
    kern = functools.partial(_dsa_kernel, tq=tq, offset=offset, s_valid=s_valid, topk=topk)
    once = pl.Buffered(1)
    return pl.pallas_call(
        kern,
        grid=(batch, nq),
        in_specs=[
            pl.BlockSpec((tq, gw), lambda b, i: (b * nq + i, COL_QC // gw)),
            pl.BlockSpec((tq, qi_w), lambda b, i: (b * nq + i, COL_QI // qi_w)),
            pl.BlockSpec((tq, gw), lambda b, i: (b * nq + i, COL_ZC // gw)),
            pl.BlockSpec((tq, SMALL_COLS), lambda b, i: (b * nq + i, 0)),
            pl.BlockSpec((1, s_pad, gw), lambda b, i: (b, 0, 0), pipeline_mode=once),
            pl.BlockSpec((1, s_pad // KT_A, gw, KT_A), lambda b, i: (b, 0, 0, 0), pipeline_mode=once),
            pl.BlockSpec((1, s_pad, IDX_DIM), lambda b, i: (b, 0, 0), pipeline_mode=once),
            pl.BlockSpec((1, HEAD_DIM), lambda b, i: (0, 0)),
        ],
        out_specs=pl.BlockSpec((tq, gw), lambda b, i: (b * nq + i, 0)),
        out_shape=jax.ShapeDtypeStruct((n, gw), BF16),
        scratch_shapes=[pltpu.VMEM((s_pad, tq), I32),
                        pltpu.VMEM((qi_w, tq), BF16),
                        pltpu.VMEM((SMALL_COLS, tq), F32),
                        pltpu.VMEM((8, tq), I32),
                        pltpu.VMEM((gw, tq), BF16),
                        pltpu.VMEM((N_HEADS, HEAD_DIM, tq), F32),
                        pltpu.VMEM((2, N_HEADS, KT_A // 2, tq), F32),
                        pltpu.VMEM((2, N_HEADS, KT_A // 2, tq), BF16),
                        pltpu.VMEM((s_pad, tq), jnp.int16),
                        pltpu.VMEM((s_pad, tq), jnp.int16)],
        compiler_params=_cparams(("parallel", "arbitrary")),
        name="dsa",
    )(proj, proj, proj, small, k_all16, vt16, ki16, gq_row)


def _mem_kernel(q_ref, z_ref, mk_ref, mv_ref, gq_ref, o_ref):
    gq = gq_ref[...]
    q_all, z_all = q_ref[...], z_ref[...]
    mk = mk_ref[0].astype(BF16)
    mv = mv_ref[0].astype(BF16)
    for h in range(N_HEADS):
        sl = slice(h * HEAD_DIM, (h + 1) * HEAD_DIM)
        qh = q_all[:, sl]
        qh = qh * lax.rsqrt(jnp.mean(qh * qh, axis=-1, keepdims=True) + EPS) * gq * (HEAD_DIM ** -0.5)
        logit = _dot_nt(qh.astype(BF16), mk[:, sl])
        m = jnp.max(logit, axis=-1, keepdims=True)
        p = jnp.exp(logit - m)
        l = jnp.sum(p, axis=-1, keepdims=True)
        oh = _dot(p.astype(BF16), mv[:, sl]) / l
        o_ref[:, sl] = (oh * _silu(z_all[:, sl])).astype(o_ref.dtype)


def _mem_attend(proj, mk, mv, gq_row, batch, seqlen):
    n = batch * seqlen
    tm = min(512, seqlen)
    nj = seqlen // tm
    gw = GROUP_WIDTH
    n_mem = mk.shape[1]
    return pl.pallas_call(
        _mem_kernel,
        grid=(batch, nj),
        in_specs=[
            pl.BlockSpec((tm, gw), lambda b, j: (b * nj + j, COL_QD // gw)),
            pl.BlockSpec((tm, gw), lambda b, j: (b * nj + j, COL_ZD // gw)),
            pl.BlockSpec((1, n_mem, gw), lambda b, j: (b, 0, 0)),
            pl.BlockSpec((1, n_mem, gw), lambda b, j: (b, 0, 0)),
            pl.BlockSpec((1, HEAD_DIM), lambda b, j: (0, 0)),
        ],
        out_specs=pl.BlockSpec((tm, gw), lambda b, j: (b * nj + j, 0)),
        out_shape=jax.ShapeDtypeStruct((n, gw), BF16),
        compiler_params=_cparams(("parallel", "arbitrary")),
        name="mem_attend",
    )(proj, proj, mk, mv, gq_row)


def _outproj_kernel(x_ref, a_ref, b_ref, c_ref, d_ref, w_ref, y_ref):
    gw = GROUP_WIDTH
    acc = x_ref[...] + _dot(a_ref[...], w_ref[0:gw, :])
    acc = acc + _dot(b_ref[...], w_ref[gw:2 * gw, :])
    acc = acc + _dot(c_ref[...], w_ref[2 * gw:3 * gw, :])
    acc = acc + _dot(d_ref[...], w_ref[3 * gw:4 * gw, :])
    y_ref[...] = acc


def _out_proj(x2d, oa, ob, oc, od, w16):
    n, d = x2d.shape
    tm = min(512, n)
    gw = GROUP_WIDTH
    grp = pl.BlockSpec((tm, gw), lambda i: (i, 0))
    return pl.pallas_call(
        _outproj_kernel,
        grid=(n // tm,),
        in_specs=[pl.BlockSpec((tm, d), lambda i: (i, 0)), grp, grp, grp, grp,
                  pl.BlockSpec((4 * gw, d), lambda i: (0, 0))],
        out_specs=pl.BlockSpec((tm, d), lambda i: (i, 0)),
        out_shape=jax.ShapeDtypeStruct((n, d), F32),
        compiler_params=_cparams(("parallel",)),
        name="out_proj",
    )(x2d, oa, ob, oc, od, w16)


def _reorder_w_in(w):
    gw = GROUP_WIDTH
    o = 0
    seg = {}
    for name, width in (("qa", gw), ("ka", gw), ("va", gw), ("za", gw), ("ba", N_HEADS), ("aa", N_HEADS),
                        ("qb", gw), ("kb", gw), ("vb", gw), ("zb", gw),
                        ("qc", gw), ("kc", gw), ("vc", gw), ("zc", gw),
                        ("qi", IDX_HEADS * IDX_DIM), ("ki", IDX_DIM), ("wi", IDX_HEADS),
                        ("qd", gw), ("zd", gw)):
        seg[name] = w[:, o:o + width]
        o += width
    a_end = 4 * gw
    b_start = a_end + 2 * N_HEADS
    b_end = b_start + 8 * gw + IDX_HEADS * IDX_DIM
    d_start = b_end + IDX_DIM + IDX_HEADS
    w16 = w.astype(BF16)
    main = jnp.concatenate([w16[:, :a_end], w16[:, b_start:b_end], w16[:, d_start:]], axis=1)
    pad = jnp.zeros((w.shape[0], SMALL_COLS - IDX_DIM - IDX_HEADS - 2 * N_HEADS), BF16)
    small = jnp.concatenate([seg["ki"].astype(BF16), seg["wi"].astype(BF16), seg["ba"].astype(BF16),
                             seg["aa"].astype(BF16), pad], axis=1)
    return main, small


def _lane_row(vals, lane0):
    row = jnp.zeros((1, SMALL_COLS), F32)
    return row.at[0, lane0:lane0 + vals.shape[0]].set(vals.astype(F32))


def _rope_tables(pos):
    half = HEAD_DIM // 2
    inv = ROPE_THETA ** (-jnp.arange(half, dtype=F32) / half)
    ang = pos.astype(F32)[:, None] * inv[None, :]
    cos, sin = jnp.cos(ang), jnp.sin(ang)
    return jnp.concatenate([cos, cos], axis=-1), jnp.concatenate([-sin, sin], axis=-1)


def _round_up(x, m):
    return (x + m - 1) // m * m


def _mixer_layer(x, conv_buf, s_gdn, s_ret, past_k, past_v, past_ki, mem_k, mem_v, wts, layer, depth, prev_cache):
    b, l, d = x.shape
    n = b * l
    offset = 0 if past_k is None else past_k.shape[1]
    x2d = x.reshape(n, d)
    proj, small = _in_proj(x2d, wts["norm_g"], wts["w_main"], wts["w_small"])

    o_a, conv_new, s_gdn_new = _gdn(proj, small, wts["conv_w"], conv_buf, s_gdn, wts["alog_row"],
                                    wts["dtb_row"], wts["gdn_norm_g"], b, l)

    cos_t, sin_t = _rope_tables(offset + jnp.arange(l, dtype=I32))
    o_b, s_ret_new = _retention(proj, cos_t, sin_t, s_ret, wts["ret_norm_g"], wts["ret_norm_b"], b, l)

    direct = past_k is None and l % KT_A == 0
    prep = _dsa_prep(proj, small, wts["dsa_k_norm_g"], wts["idx_k_norm_g"], layer, depth, prev_cache, direct)
    cache = tuple(prep[:3])
    kc16, kic16 = prep[3].reshape(b, l, GROUP_WIDTH), prep[4].reshape(b, l, IDX_DIM)
    if direct:
        s_valid = l
        k16, ki16 = kc16, kic16
        vt16 = prep[5].reshape(b, l // KT_A, GROUP_WIDTH, KT_A)
    else:
        vc16 = cache[1][layer].astype(BF16).reshape(b, l, GROUP_WIDTH)
        if past_k is None:
            k_all, v_all, ki_all = kc16, vc16, kic16
        else:
            k_all = jnp.concatenate([past_k.reshape(b, offset, GROUP_WIDTH).astype(BF16), kc16], axis=1)
            v_all = jnp.concatenate([past_v.reshape(b, offset, GROUP_WIDTH).astype(BF16), vc16], axis=1)
            ki_all = jnp.concatenate([past_ki.astype(BF16), kic16], axis=1)
        s_valid = k_all.shape[1]
        s_pad = _round_up(s_valid, KT_A)
        padw = ((0, 0), (0, s_pad - s_valid), (0, 0))
        k16 = jnp.pad(k_all, padw)
        ki16 = jnp.pad(ki_all, padw)
        vt16 = jnp.pad(v_all, padw).reshape(b, s_pad // KT_A, KT_A, GROUP_WIDTH).transpose(0, 1, 3, 2)

    tq = 256 if l % 256 == 0 else 128
    q_len = _round_up(l, tq)
    if q_len != l:
        proj_q = jnp.pad(proj.reshape(b, l, MAIN_COLS), ((0, 0), (0, q_len - l), (0, 0))).reshape(b * q_len, MAIN_COLS)
        small_q = jnp.pad(small.reshape(b, l, SMALL_COLS), ((0, 0), (0, q_len - l), (0, 0))).reshape(b * q_len, SMALL_COLS)
    else:
        proj_q, small_q = proj, small
    o_c = _dsa(proj_q, small_q, k16, vt16, ki16, wts["dsa_q_norm_g"], b, q_len, tq, offset, s_valid)
    if q_len != l:
        o_c = o_c.reshape(b, q_len, GROUP_WIDTH)[:, :l].reshape(n, GROUP_WIDTH)

    o_d = _mem_attend(proj, mem_k, mem_v, wts["mem_q_norm_g"], b, l)

    y = _out_proj(x2d, o_a, o_b, o_c, o_d, wts["w_out"]).reshape(b, l, d)
    return y, (conv_new, s_gdn_new, s_ret_new), cache


def kernel(x_prompt, x_sample, cache_gdn_conv, state_gdn, state_ret, cache_dsa_k, cache_dsa_v, cache_idx_k, cache_mem_k, cache_mem_v, mem_prompt, norm_g, w_in, gdn_conv_w, gdn_a_log, gdn_dt_bias, gdn_norm_g, ret_norm_g, ret_norm_b, dsa_q_norm_g, dsa_k_norm_g, idx_k_norm_g, mem_norm_g, w_mem_kv, mem_q_norm_g, mem_k_norm_g, w_out):
    depth = w_in.shape[0]
    b = x_prompt.shape[0]
    n_mem = mem_prompt.shape[1]
    d = x_prompt.shape[-1]
    y_p, y_s = x_prompt, x_sample
    st_p, st_s, mem_p = [], [], []
    cache_p = cache_s = None
    for li in range(depth):
        w_main, w_small = _reorder_w_in(w_in[li])
        wts = dict(
            norm_g=norm_g[li][None, :], w_main=w_main, w_small=w_small, conv_w=gdn_conv_w[li],
            alog_row=_lane_row(gdn_a_log[li], LANE_ALPHA), dtb_row=_lane_row(gdn_dt_bias[li], LANE_ALPHA),
            gdn_norm_g=gdn_norm_g[li][None, :], ret_norm_g=ret_norm_g[li][None, :],
            ret_norm_b=ret_norm_b[li][None, :], dsa_q_norm_g=dsa_q_norm_g[li][None, :],
            dsa_k_norm_g=dsa_k_norm_g[li][None, :], idx_k_norm_g=idx_k_norm_g[li][None, :],
            mem_q_norm_g=mem_q_norm_g[li][None, :], w_out=w_out[li].astype(BF16),
        )
        mk, mv = _memory_kv(mem_prompt.reshape(b * n_mem, d), mem_norm_g[li][None, :],
                            w_mem_kv[li].astype(BF16), mem_k_norm_g[li][None, :])
        mk = mk.reshape(b, n_mem, GROUP_WIDTH)
        mv = mv.reshape(b, n_mem, GROUP_WIDTH)
        conv0 = jnp.zeros((b, CONV_W - 1, 3 * GROUP_WIDTH), F32)
        s0 = jnp.zeros((b, N_HEADS, HEAD_DIM, HEAD_DIM), F32)
        y_p, sp, cache_p = _mixer_layer(y_p, conv0, s0, s0, None, None, None, mk, mv, wts, li, depth, cache_p)
        st_p.append(sp)
        mem_p.append((mk.reshape(b, n_mem, N_HEADS, HEAD_DIM), mv.reshape(b, n_mem, N_HEADS, HEAD_DIM)))
        bs = x_sample.shape[0]
        y_s, ss, cache_s = _mixer_layer(y_s, cache_gdn_conv[li], state_gdn[li], state_ret[li],
                                        cache_dsa_k[li], cache_dsa_v[li], cache_idx_k[li],
                                        cache_mem_k[li].reshape(bs, n_mem, GROUP_WIDTH),
                                        cache_mem_v[li].reshape(bs, n_mem, GROUP_WIDTH), wts, li, depth, cache_s)
        st_s.append(ss)

    def stack(lst, k):
        return jnp.stack([s[k] for s in lst])

    def caches(c, bb, ll):
        return (c[0].reshape(depth, bb, ll, N_HEADS, HEAD_DIM), c[1].reshape(depth, bb, ll, N_HEADS, HEAD_DIM),
                c[2].reshape(depth, bb, ll, IDX_DIM))

    return ((y_p, y_s, stack(st_p, 0), stack(st_p, 1), stack(st_p, 2))
            + caches(cache_p, b, x_prompt.shape[1])
            + (stack(mem_p, 0), stack(mem_p, 1), stack(st_s, 0), stack(st_s, 1), stack(st_s, 2))
            + caches(cache_s, x_sample.shape[0], x_sample.shape[1]))
```

```python
import functools
import math

import jax
import jax.numpy as jnp
from jax import lax
from jax.experimental import pallas as pl
from jax.experimental.pallas import tpu as pltpu

F32 = jnp.float32
BF16 = jnp.bfloat16
I32 = jnp.int32

HEAD_DIM = 128
N_HEADS = 4
GROUP_WIDTH = N_HEADS * HEAD_DIM
CHUNK = 64
CONV_W = 4
IDX_HEADS = 16
IDX_DIM = 64
TOPK_MAX = 256
ROPE_THETA = 10000.0
EPS = 1e-6

COL_QA, COL_KA, COL_VA, COL_ZA = 0, 512, 1024, 1536
COL_QB, COL_KB, COL_VB, COL_ZB = 2048, 2560, 3072, 3584
COL_QC, COL_KC, COL_VC, COL_ZC = 4096, 4608, 5120, 5632
COL_QI = 6144
COL_QD, COL_ZD = 7168, 7680
MAIN_COLS = 8192
LANE_KI, LANE_WI, LANE_BETA, LANE_ALPHA = 0, 64, 80, 84
SMALL_COLS = 128

INT_MIN = -2 ** 31
NEG_BIG = -1e30
LOG2_E = 1.4426950408889634
VMEM_LIMIT = 56 * 1024 * 1024
HI = lax.Precision.HIGHEST


def _cparams(sem):
    return pltpu.CompilerParams(dimension_semantics=sem, vmem_limit_bytes=VMEM_LIMIT)


def _dot(a, b):
    return jnp.dot(a, b, preferred_element_type=F32)


def _dot_nt(a, b):
    return lax.dot_general(a, b, (((1,), (1,)), ((), ())), preferred_element_type=F32)


def _dot3(a, b):
    a_hi = a.astype(BF16)
    b_hi = b.astype(BF16)
    a_lo = (a - a_hi.astype(F32)).astype(BF16)
    b_lo = (b - b_hi.astype(F32)).astype(BF16)
    return _dot(a_hi, b_hi) + (_dot(a_hi, b_lo) + _dot(a_lo, b_hi))


def _silu(x):
    return x * jax.nn.sigmoid(x)


def _inproj_kernel(x_ref, g_ref, w_ref, ws_ref, o_ref, os_ref, h_ref):
    @pl.when(pl.program_id(1) == 0)
    def _():
        x = x_ref[...]
        y = x * lax.rsqrt(jnp.mean(x * x, axis=-1, keepdims=True) + EPS) * g_ref[...]
        hb = y.astype(BF16)
        h_ref[...] = hb
        os_ref[...] = _dot(hb, ws_ref[...])

    o_ref[...] = _dot(h_ref[...], w_ref[...])


def _in_proj(x2d, g_row, w_main, w_small, layer):
    n, d = x2d.shape
    tm = min(1024, n)
    tn = 1024
    return pl.pallas_call(
        _inproj_kernel,
        grid=(n // tm, MAIN_COLS // tn),
        in_specs=[
            pl.BlockSpec((tm, d), lambda i, j: (i, 0)),
            pl.BlockSpec((1, d), lambda i, j: (0, 0)),
            pl.BlockSpec((None, d, tn), lambda i, j: (layer, 0, j)),
            pl.BlockSpec((None, d, SMALL_COLS), lambda i, j: (layer, 0, 0)),
        ],
        out_specs=[
            pl.BlockSpec((tm, tn), lambda i, j: (i, j)),
            pl.BlockSpec((tm, SMALL_COLS), lambda i, j: (i, 0)),
        ],
        out_shape=[jax.ShapeDtypeStruct((n, MAIN_COLS), F32),
                   jax.ShapeDtypeStruct((n, SMALL_COLS), F32)],
        scratch_shapes=[pltpu.VMEM((tm, d), BF16)],
        compiler_params=_cparams(("parallel", "arbitrary")),
        name="in_proj",
    )(x2d, g_row, w_main, w_small)


def _memkv_kernel(x_ref, g_ref, w_ref, gk_ref, mk_ref, mv_ref):
    x = x_ref[...]
    y = x * lax.rsqrt(jnp.mean(x * x, axis=-1, keepdims=True) + EPS) * g_ref[...]
    kv = _dot(y.astype(BF16), w_ref[...])
    gk = gk_ref[...]
    for h in range(N_HEADS):
        sl = slice(h * HEAD_DIM, (h + 1) * HEAD_DIM)
        kh = kv[:, sl]
        mk_ref[:, sl] = kh * lax.rsqrt(jnp.mean(kh * kh, axis=-1, keepdims=True) + EPS) * gk
    mv_ref[...] = kv[:, GROUP_WIDTH:]


def _memory_kv(mem2d, g_row, w_kv, gk_row):
    n, d = mem2d.shape
    tm = min(256, n)
    return pl.pallas_call(
        _memkv_kernel,
        grid=(n // tm,),
        in_specs=[
            pl.BlockSpec((tm, d), lambda i: (i, 0)),
            pl.BlockSpec((1, d), lambda i: (0, 0)),
            pl.BlockSpec((d, 2 * GROUP_WIDTH), lambda i: (0, 0)),
            pl.BlockSpec((1, HEAD_DIM), lambda i: (0, 0)),
        ],
        out_specs=[pl.BlockSpec((tm, GROUP_WIDTH), lambda i: (i, 0)),
                   pl.BlockSpec((tm, GROUP_WIDTH), lambda i: (i, 0))],
        out_shape=[jax.ShapeDtypeStruct((n, GROUP_WIDTH), F32),
                   jax.ShapeDtypeStruct((n, GROUP_WIDTH), F32)],
        compiler_params=_cparams(("parallel",)),
        name="memory_kv",
    )(mem2d, g_row, w_kv, gk_row)


def _gdn_kernel(qkv_ref, z_ref, sm_ref, cw_ref, cb_ref, s0_ref, alog_ref, dtb_ref, gn_ref,
                o_ref, conv_ref, st_ref, xbuf_ref, s_ref, *, t_blk, chunk):
    j = pl.program_id(1)
    nj = pl.num_programs(1)
    gw3 = 3 * GROUP_WIDTH

    @pl.when(j == 0)
    def _():
        xbuf_ref[0:8, :] = jnp.zeros((8, gw3), F32)
        xbuf_ref[5:8, :] = cb_ref[0]
        s_ref[...] = s0_ref[0]

    @pl.when(j > 0)
    def _():
        xbuf_ref[0:8, :] = xbuf_ref[t_blk:t_blk + 8, :]

    xbuf_ref[8:8 + t_blk, :] = qkv_ref[...]
    conv_ref[0] = xbuf_ref[t_blk + 5:t_blk + 8, :]

    cw = cw_ref[...]
    x_all = xbuf_ref[...]
    x_prev = pltpu.roll(x_all, 1, 0)
    u = x_all * cw[3:4, :] + x_prev * cw[2:3, :]
    v = x_all * cw[1:2, :] + x_prev * cw[0:1, :]
    y = _silu((u + pltpu.roll(v, 2, 0))[8:8 + t_blk, :])

    sm = sm_ref[...]
    lane = lax.broadcasted_iota(I32, sm.shape, 1)
    beta_all = jax.nn.sigmoid(sm)
    xs = sm + dtb_ref[...]
    softplus = jnp.maximum(xs, 0.0) + jnp.log1p(jnp.exp(-jnp.abs(xs)))
    g_all = -jnp.exp(alog_ref[...]) * softplus
    g_all = jnp.where((lane >= LANE_ALPHA) & (lane < LANE_ALPHA + N_HEADS), g_all, 0.0)
    ri = lax.broadcasted_iota(I32, (t_blk, t_blk), 0)
    ci = lax.broadcasted_iota(I32, (t_blk, t_blk), 1)
    tri = jnp.where((ri // chunk == ci // chunk) & (ci <= ri), 1.0, 0.0).astype(F32)
    gcum = jnp.dot(tri, g_all, preferred_element_type=F32, precision=HI)
    gcum_t = gcum.T

    blk = min(2 * chunk, t_blk)
    n_blk = t_blk // blk
    cpb = blk // chunk
    rb = lax.broadcasted_iota(I32, (blk, blk), 0)
    cb = lax.broadcasted_iota(I32, (blk, blk), 1)
    same = (rb // chunk) == (cb // chunk)
    causal = same & (cb <= rb)
    strict = same & (cb < rb)
    eye = jnp.where(rb == cb, 1.0, 0.0).astype(F32)
    n_dbl = max(int(math.log2(chunk)) - 1, 0)
    gn = gn_ref[...]
    z_all = z_ref[...]

    units = [(h, b) for h in range(N_HEADS) for b in range(n_blk)]
    qs, ks, gcols, p_mats, x_mats, qk_mats, vb_mats, kg_mats = {}, {}, {}, {}, {}, {}, {}, {}
    for h in range(N_HEADS):
        qh = y[:, COL_QA + h * HEAD_DIM:COL_QA + (h + 1) * HEAD_DIM]
        kh = y[:, COL_KA + h * HEAD_DIM:COL_KA + (h + 1) * HEAD_DIM]
        vh = y[:, COL_VA + h * HEAD_DIM:COL_VA + (h + 1) * HEAD_DIM]
        qh = qh * lax.rsqrt(jnp.sum(qh * qh, axis=-1, keepdims=True) + EPS) * (HEAD_DIM ** -0.5)
        kh = kh * lax.rsqrt(jnp.sum(kh * kh, axis=-1, keepdims=True) + EPS)
        for b in range(n_blk):
            rs = slice(b * blk, (b + 1) * blk)
            gcol = gcum[rs, LANE_ALPHA + h:LANE_ALPHA + h + 1]
            grow = gcum_t[LANE_ALPHA + h:LANE_ALPHA + h + 1, rs]
            bcol = beta_all[rs, LANE_BETA + h:LANE_BETA + h + 1]
            decay = jnp.where(causal, jnp.exp(jnp.where(causal, gcol - grow, 0.0)), 0.0)
            qc, kc, vc = qh[rs], kh[rs], vh[rs]
            kb = kc * bcol
            kc16 = kc.astype(BF16)
            a_mat = jnp.where(strict, _dot_nt(kb.astype(BF16), kc16) * decay, 0.0)
            qk_mats[h, b] = jnp.where(causal, _dot_nt(qc.astype(BF16), kc16) * decay, 0.0).astype(BF16)
            p_mats[h, b] = -a_mat
            x_mats[h, b] = eye - a_mat
            vb_mats[h, b] = (vc * bcol).astype(BF16)
            kg_mats[h, b] = (kb * jnp.exp(gcol)).astype(BF16)
            qs[h, b], ks[h, b], gcols[h, b] = qc, kc, gcol

    for _ in range(n_dbl):
        for u_ in units:
            p_mats[u_] = _dot3(p_mats[u_], p_mats[u_])
        for u_ in units:
            x_mats[u_] = x_mats[u_] + _dot3(x_mats[u_], p_mats[u_])

    u_mats, w_mats = {}, {}
    for u_ in units:
        x16 = x_mats[u_].astype(BF16)
        u_mats[u_] = _dot(x16, vb_mats[u_])
        w_mats[u_] = _dot(x16, kg_mats[u_]).astype(BF16)

    states = [s_ref[h] for h in range(N_HEADS)]
    for b in range(n_blk):
        o_state = {h: [] for h in range(N_HEADS)}
        v_new = {h: [] for h in range(N_HEADS)}
        for c in range(cpb):
            cs = slice(c * chunk, (c + 1) * chunk)
            for h in range(N_HEADS):
                s = states[h]
                s16 = s.astype(BF16)
                gcol = gcols[h, b][cs]
                glast = gcols[h, b][(c + 1) * chunk - 1:(c + 1) * chunk]
                vn = u_mats[h, b][cs] - _dot(w_mats[h, b][cs], s16)
                vn16 = vn.astype(BF16)
                kd_t = (ks[h, b][cs] * jnp.exp(glast - gcol)).T.astype(BF16)
                states[h] = s * jnp.exp(glast) + _dot(kd_t, vn16)
                o_state[h].append(_dot((qs[h, b][cs] * jnp.exp(gcol)).astype(BF16), s16))
                v_new[h].append(vn16)
        for h in range(N_HEADS):
            sl = slice(h * HEAD_DIM, (h + 1) * HEAD_DIM)
            rs = slice(b * blk, (b + 1) * blk)
            o = jnp.concatenate(o_state[h], axis=0) + _dot(qk_mats[h, b], jnp.concatenate(v_new[h], axis=0))
            on = o * lax.rsqrt(jnp.mean(o * o, axis=-1, keepdims=True) + EPS) * gn
            o_ref[rs, sl] = (on * _silu(z_all[rs, sl])).astype(o_ref.dtype)
    for h in range(N_HEADS):
        s_ref[h] = states[h]

    @pl.when(j == nj - 1)
    def _():
        st_ref[0] = s_ref[...]


def _gdn(proj, small, conv_w, conv_buf, s0, alog_row, dtb_row, gn_row, batch, seqlen):
    n = batch * seqlen
    t_blk = min(256, seqlen)
    chunk = min(CHUNK, seqlen)
    nj = seqlen // t_blk
    gw3 = 3 * GROUP_WIDTH
    kern = functools.partial(_gdn_kernel, t_blk=t_blk, chunk=chunk)
    return pl.pallas_call(
        kern,
        grid=(batch, nj),
        in_specs=[
            pl.BlockSpec((t_blk, gw3), lambda b, j: (b * nj + j, COL_QA // gw3)),
            pl.BlockSpec((t_blk, GROUP_WIDTH), lambda b, j: (b * nj + j, COL_ZA // GROUP_WIDTH)),
            pl.BlockSpec((t_blk, SMALL_COLS), lambda b, j: (b * nj + j, 0)),
            pl.BlockSpec((CONV_W, gw3), lambda b, j: (0, 0)),
            pl.BlockSpec((1, CONV_W - 1, gw3), lambda b, j: (b, 0, 0)),
            pl.BlockSpec((1, N_HEADS, HEAD_DIM, HEAD_DIM), lambda b, j: (b, 0, 0, 0)),
            pl.BlockSpec((1, SMALL_COLS), lambda b, j: (0, 0)),
            pl.BlockSpec((1, SMALL_COLS), lambda b, j: (0, 0)),
            pl.BlockSpec((1, HEAD_DIM), lambda b, j: (0, 0)),
        ],
        out_specs=[
            pl.BlockSpec((t_blk, GROUP_WIDTH), lambda b, j: (b * nj + j, 0)),
            pl.BlockSpec((1, CONV_W - 1, gw3), lambda b, j: (b, 0, 0)),
            pl.BlockSpec((1, N_HEADS, HEAD_DIM, HEAD_DIM), lambda b, j: (b, 0, 0, 0)),
        ],
        out_shape=[
            jax.ShapeDtypeStruct((n, GROUP_WIDTH), BF16),
            jax.ShapeDtypeStruct((batch, CONV_W - 1, gw3), F32),
            jax.ShapeDtypeStruct((batch, N_HEADS, HEAD_DIM, HEAD_DIM), F32),
        ],
        scratch_shapes=[pltpu.VMEM((t_blk + 8, gw3), F32),
                        pltpu.VMEM((N_HEADS, HEAD_DIM, HEAD_DIM), F32)],
        compiler_params=_cparams(("parallel", "arbitrary")),
        name="gdn",
    )(proj, proj, small, conv_w, conv_buf, s0, alog_row, dtb_row, gn_row)


def _ret_kernel(q_ref, k_ref, v_ref, z_ref, cos_ref, sin_ref, s0_ref, g_ref, b_ref,
                o_ref, st_ref, s_ref, *, t_blk):
    j = pl.program_id(1)
    nj = pl.num_programs(1)

    @pl.when(j == 0)
    def _():
        s_ref[...] = s0_ref[0]

    cos = cos_ref[...]
    sin = sin_ref[...]
    ri = lax.broadcasted_iota(I32, (t_blk, t_blk), 0)
    ci = lax.broadcasted_iota(I32, (t_blk, t_blk), 1)
    causal = ci <= ri
    rel = jnp.where(causal, ri - ci, 0).astype(F32)
    idx_col = lax.broadcasted_iota(I32, (t_blk, 1), 0).astype(F32)
    gamma_g = g_ref[...]
    gamma_b = b_ref[...]
    q_all, k_all, v_all, z_all = q_ref[...], k_ref[...], v_ref[...], z_ref[...]

    for h in range(N_HEADS):
        sl = slice(h * HEAD_DIM, (h + 1) * HEAD_DIM)
        lg = math.log(1.0 - 2.0 ** (-5.0 - h))
        qh, kh, vh = q_all[:, sl], k_all[:, sl], v_all[:, sl]
        qh = qh * cos + pltpu.roll(qh, HEAD_DIM // 2, 1) * sin
        kh = (kh * cos + pltpu.roll(kh, HEAD_DIM // 2, 1) * sin) * (HEAD_DIM ** -0.5)
        d_mat = jnp.where(causal, jnp.exp(lg * rel), 0.0)
        q16, k16, v16 = qh.astype(BF16), kh.astype(BF16), vh.astype(BF16)
        o_intra = _dot((_dot_nt(q16, k16) * d_mat).astype(BF16), v16)
        s = s_ref[h]
        o_cross = _dot(q16, s.astype(BF16)) * jnp.exp(lg * (idx_col + 1.0))
        kd_t = (kh * jnp.exp(lg * (t_blk - 1.0 - idx_col))).T.astype(BF16)
        s_ref[h] = s * math.exp(lg * t_blk) + _dot(kd_t, v16)
        o = o_intra + o_cross
        mu = jnp.mean(o, axis=-1, keepdims=True)
        oc = o - mu
        var = jnp.mean(oc * oc, axis=-1, keepdims=True)
        on = oc * lax.rsqrt(var + EPS) * gamma_g + gamma_b
        o_ref[:, sl] = (on * _silu(z_all[:, sl])).astype(o_ref.dtype)

    @pl.when(j == nj - 1)
    def _():
        st_ref[0] = s_ref[...]


def _retention(proj, cos_t, sin_t, s0, g_row, b_row, batch, seqlen):
    n = batch * seqlen
    t_blk = min(256, seqlen)
    nj = seqlen // t_blk
    gw = GROUP_WIDTH
    kern = functools.partial(_ret_kernel, t_blk=t_blk)

    def col(c):
        return pl.BlockSpec((t_blk, gw), lambda b, j: (b * nj + j, c // gw))

    return pl.pallas_call(
        kern,
        grid=(batch, nj),
        in_specs=[
            col(COL_QB), col(COL_KB), col(COL_VB), col(COL_ZB),
            pl.BlockSpec((t_blk, HEAD_DIM), lambda b, j: (j, 0)),
            pl.BlockSpec((t_blk, HEAD_DIM), lambda b, j: (j, 0)),
            pl.BlockSpec((1, N_HEADS, HEAD_DIM, HEAD_DIM), lambda b, j: (b, 0, 0, 0)),
            pl.BlockSpec((1, HEAD_DIM), lambda b, j: (0, 0)),
            pl.BlockSpec((1, HEAD_DIM), lambda b, j: (0, 0)),
        ],
        out_specs=[
            pl.BlockSpec((t_blk, gw), lambda b, j: (b * nj + j, 0)),
            pl.BlockSpec((1, N_HEADS, HEAD_DIM, HEAD_DIM), lambda b, j: (b, 0, 0, 0)),
        ],
        out_shape=[
            jax.ShapeDtypeStruct((n, gw), BF16),
            jax.ShapeDtypeStruct((batch, N_HEADS, HEAD_DIM, HEAD_DIM), F32),
        ],
        scratch_shapes=[pltpu.VMEM((N_HEADS, HEAD_DIM, HEAD_DIM), F32)],
        compiler_params=_cparams(("parallel", "arbitrary")),
        name="retention",
    )(proj, proj, proj, proj, cos_t, sin_t, s0, g_row, b_row)


def _dsa_prep_kernel(k_ref, v_ref, sm_ref, gk_ref, gi_ref, *rest, n_prev, with_vt):
    outs = rest[n_prev:]
    ko_ref, vo_ref, kio_ref, k16_ref, ki16_ref = outs[:5]
    k = k_ref[...]
    gk = gk_ref[...]
    for h in range(N_HEADS):
        sl = slice(h * HEAD_DIM, (h + 1) * HEAD_DIM)
        kh = k[:, sl]
        kn = kh * lax.rsqrt(jnp.mean(kh * kh, axis=-1, keepdims=True) + EPS) * gk
        ko_ref[:, h, :] = kn
        k16_ref[:, sl] = kn.astype(BF16)
    v = v_ref[...]
    for h in range(N_HEADS):
        vo_ref[:, h, :] = v[:, h * HEAD_DIM:(h + 1) * HEAD_DIM]
    if with_vt:
        outs[5][...] = v.T.astype(BF16)
    ki = sm_ref[...][:, LANE_KI:LANE_KI + IDX_DIM]
    kin = ki * lax.rsqrt(jnp.mean(ki * ki, axis=-1, keepdims=True) + EPS) * gi_ref[...]
    kio_ref[...] = kin
    ki16_ref[...] = kin.astype(BF16)


def _dsa_prep(proj, small, gk_row, gi_row, layer, depth, prev, with_vt):
    n = proj.shape[0]
    tm = min(KT_A, n)
    gw = GROUP_WIDTH
    n_prev = 0 if prev is None else 3
    kern = functools.partial(_dsa_prep_kernel, n_prev=n_prev, with_vt=with_vt)
    in_specs = [
        pl.BlockSpec((tm, gw), lambda i: (i, COL_KC // gw)),
        pl.BlockSpec((tm, gw), lambda i: (i, COL_VC // gw)),
        pl.BlockSpec((tm, SMALL_COLS), lambda i: (i, 0)),
        pl.BlockSpec((1, HEAD_DIM), lambda i: (0, 0)),
        pl.BlockSpec((1, IDX_DIM), lambda i: (0, 0)),
    ] + [pl.BlockSpec(memory_space=pl.ANY)] * n_prev
    out_specs = [pl.BlockSpec((None, tm, N_HEADS, HEAD_DIM), lambda i: (layer, i, 0, 0)),
                 pl.BlockSpec((None, tm, N_HEADS, HEAD_DIM), lambda i: (layer, i, 0, 0)),
                 pl.BlockSpec((None, tm, IDX_DIM), lambda i: (layer, i, 0)),
                 pl.BlockSpec((tm, gw), lambda i: (i, 0)),
                 pl.BlockSpec((tm, IDX_DIM), lambda i: (i, 0))]
    out_shape = [jax.ShapeDtypeStruct((depth, n, N_HEADS, HEAD_DIM), F32),
                 jax.ShapeDtypeStruct((depth, n, N_HEADS, HEAD_DIM), F32),
                 jax.ShapeDtypeStruct((depth, n, IDX_DIM), F32),
                 jax.ShapeDtypeStruct((n, gw), BF16),
                 jax.ShapeDtypeStruct((n, IDX_DIM), BF16)]
    if with_vt:
        out_specs.append(pl.BlockSpec((None, gw, tm), lambda i: (i, 0, 0)))
        out_shape.append(jax.ShapeDtypeStruct((n // tm, gw, tm), BF16))
    args = (proj, proj, small, gk_row, gi_row) + (() if prev is None else tuple(prev))
    return pl.pallas_call(
        kern,
        grid=(n // tm,),
        in_specs=in_specs,
        out_specs=out_specs,
        out_shape=out_shape,
        input_output_aliases={5 + t: t for t in range(n_prev)},
        compiler_params=_cparams(("parallel",)),
        name="dsa_prep",
    )(*args)


KT_I = 128
KT_A = 512


def _dsa_kernel(q_ref, qi_ref, z_ref, sm_ref, k_ref, vt_ref, ki_ref, gq_ref, o_ref,
                key_ref, qit_ref, wt_ref, jlim_ref, qt_ref, acc_ref, lg_ref, p_ref, key16_ref, low16_ref, *, tq, offset, s_valid, topk):
    i = pl.program_id(1)
    pos0 = offset + i * tq
    t_pos = pos0 + lax.broadcasted_iota(I32, (1, tq), 1)
    t_chunk = t_pos // CHUNK
    n_adm_row = jnp.minimum((t_chunk + 1) * CHUNK, s_valid)
    n_keys = jnp.minimum(((pos0 + tq - 1) // CHUNK + 1) * CHUNK, s_valid)
    n_at = (n_keys + KT_A - 1) // KT_A
    n_sub = KT_A // KT_I
    n_it_full = n_at * n_sub

    qit_ref[...] = qi_ref[...].T.astype(BF16)
    wt_ref[...] = sm_ref[...].T * (IDX_HEADS ** -0.5 * IDX_DIM ** -0.5)

    def index_step(kt, carry):
        r0 = pl.multiple_of(kt * KT_I, KT_I)
        ki_t = ki_ref[0, pl.ds(r0, KT_I), :]
        acc = jnp.zeros((KT_I, tq), F32)
        for h in range(IDX_HEADS):
            sc = _dot(ki_t, qit_ref[h * IDX_DIM:(h + 1) * IDX_DIM, :])
            acc = acc + wt_ref[LANE_WI + h:LANE_WI + h + 1, :] * jnp.maximum(sc, 0.0)
        bits = lax.bitcast_convert_type(acc, I32)
        key = bits ^ ((bits >> 31) & 0x7FFFFFFF)
        s_pos = r0 + lax.broadcasted_iota(I32, (KT_I, 1), 0)
        adm = (s_pos // CHUNK <= t_chunk) & (s_pos < s_valid)
        key = jnp.where(adm, key, INT_MIN)
        key_ref[pl.ds(r0, KT_I), :] = key
        key16_ref[pl.ds(r0, KT_I), :] = (key >> 16).astype(jnp.int16)
        return carry

    lax.fori_loop(0, n_it_full // 2, lambda t, c: index_step(2 * t + 1, index_step(2 * t, c)), 0)

    def count(pred_fn):
        def body(kt, accs):
            out = []
            for s in range(n_sub):
                r0 = pl.multiple_of(kt * KT_A + s * KT_I, KT_I)
                key = key_ref[pl.ds(r0, KT_I), :]
                s_pos = r0 + lax.broadcasted_iota(I32, (KT_I, 1), 0)
                m = jnp.where(pred_fn(key, s_pos), 1, 0).astype(I32)
                out.append(accs[s] + jnp.sum(m.reshape(KT_I // 8, 8, tq), axis=0))
            return tuple(out)
        accs = lax.fori_loop(0, n_at, body, tuple(jnp.zeros((8, tq), I32) for _ in range(n_sub)))
        return jnp.sum(sum(accs[1:], accs[0]), axis=0, keepdims=True)

    def count16(ref16, pred_fn):
        def body(kt, accs):
            out = []
            for s in range(n_sub):
                r0 = pl.multiple_of(kt * KT_A + s * KT_I, KT_I)
                m = jnp.where(pred_fn(ref16[pl.ds(r0, KT_I), :]), jnp.int16(1), jnp.int16(0))
                m = m.reshape(KT_I // 16, 16, tq)
                acc = accs[s]
                for q in range(KT_I // 16):
                    acc = acc + m[q]
                out.append(acc)
            return tuple(out)
        accs = lax.fori_loop(0, n_at, body, tuple(jnp.zeros((16, tq), jnp.int16) for _ in range(n_sub)))
        tot = sum((a.astype(I32) for a in accs[1:]), accs[0].astype(I32))
        return jnp.sum(tot, axis=0, keepdims=True)

    small = jnp.where(n_adm_row <= topk, 1, 0).astype(I32)

    def all_done(done):
        return jnp.min(done.astype(F32)) > 0.0

    def search(first_bit, last_bit, count_ge, carry, early_exit):
        def step(b, c):
            v, done, thr = c
            cand_u = v | jnp.left_shift(jnp.int32(1), 31 - b)
            cand_s = cand_u ^ INT_MIN
            cnt = count_ge(cand_s)
            v = jnp.where(cnt >= topk, cand_u, v)
            newly = (cnt == topk) & (done == 0)
            thr = jnp.where(newly, cand_s, thr)
            return v, jnp.where(newly, 1, done), thr

        if not early_exit:
            return lax.fori_loop(first_bit, last_bit, step, carry)

        def cond(c):
            return (c[0] < last_bit) & jnp.logical_not(all_done(c[1][1]))

        def body(c):
            b, inner = c
            return b + 4, lax.fori_loop(b, b + 4, step, inner)

        assert (last_bit - first_bit) % 4 == 0
        return lax.while_loop(cond, body, (jnp.int32(first_bit), carry))[1]

    v0 = jnp.zeros((1, tq), I32)
    thr0 = jnp.full((1, tq), INT_MIN + 1, I32)
    carry = search(0, 16, lambda cand_s: count16(key16_ref, lambda k16: k16 >= (cand_s >> 16).astype(jnp.int16)),
                   (v0, small, thr0), early_exit=False)
    hi16 = ((carry[0] ^ INT_MIN) >> 16).astype(jnp.int16)
    n_above = count16(key16_ref, lambda k16: k16 > hi16)

    @pl.when(jnp.logical_not(all_done(carry[1])))
    def _():
        hi32 = (carry[0] ^ INT_MIN) >> 16

        def low_step(kt, c):
            for s in range(n_sub):
                r0 = pl.multiple_of(kt * KT_A + s * KT_I, KT_I)
                key = key_ref[pl.ds(r0, KT_I), :]
                low = jnp.where((key >> 16) == hi32, (key & 0xFFFF) - 32768, -32768)
                low16_ref[pl.ds(r0, KT_I), :] = low.astype(jnp.int16)
            return c

        lax.fori_loop(0, n_at, low_step, 0)

    def count_low(cand_s):
        c16 = ((cand_s & 0xFFFF) - 32768).astype(jnp.int16)
        return n_above + count16(low16_ref, lambda l16: l16 >= c16)

    v_u, done_i, thr = search(16, 32, count_low, carry, early_exit=True)
    done = done_i != 0
    v_s = v_u ^ INT_MIN
    any_tie = jnp.logical_not(all_done(done_i))

    jlim_ref[...] = jnp.zeros(jlim_ref.shape, I32)

    @pl.when(any_tie)
    def _():
        n_gt = count(lambda key, s_pos: key > v_s)
        need = topk - n_gt
        pos_bits = max(int(math.ceil(math.log2(max(k_ref.shape[1], 2)))), 1) + 1

        def pos_step(b, jv):
            cand = jv | jnp.left_shift(jnp.int32(1), pos_bits - 1 - b)
            cnt = count(lambda key, s_pos: (key == v_s) & (s_pos < cand))
            return jnp.where(cnt <= need, cand, jv)

        jlim_ref[0:1, :] = lax.fori_loop(0, pos_bits, pos_step, jnp.zeros((1, tq), I32))

    j_lim = jlim_ref[0:1, :]

    v_eff = jnp.where(done, thr - 1, v_s)
    j_eff = jnp.where(done, 0, j_lim)

    def bias_loop(with_ties):
        def bias_step(kt, carry):
            for s in range(n_sub):
                r0 = pl.multiple_of(kt * KT_A + s * KT_I, KT_I)
                key = key_ref[pl.ds(r0, KT_I), :]
                sel = key > v_eff
                if with_ties:
                    s_pos = r0 + lax.broadcasted_iota(I32, (KT_I, 1), 0)
                    sel = sel | ((key == v_eff) & (s_pos < j_eff) & (key != INT_MIN))
                bias = jnp.where(sel, 0.0, NEG_BIG).astype(F32)
                key_ref[pl.ds(r0, KT_I), :] = lax.bitcast_convert_type(bias, I32)
            return carry

        lax.fori_loop(0, n_at, bias_step, 0)

    pl.when(any_tie)(lambda: bias_loop(True))
    pl.when(jnp.logical_not(any_tie))(lambda: bias_loop(False))

    gq = gq_ref[...]
    q_all = q_ref[...]
    for h in range(N_HEADS):
        sl = slice(h * HEAD_DIM, (h + 1) * HEAD_DIM)
        qh = q_all[:, sl]
        qh = qh * lax.rsqrt(jnp.mean(qh * qh, axis=-1, keepdims=True) + EPS) * gq * (HEAD_DIM ** -0.5 * LOG2_E)
        qt_ref[sl, :] = qh.T.astype(BF16)
    acc_ref[...] = jnp.zeros(acc_ref.shape, F32)

    heads = [slice(h * HEAD_DIM, (h + 1) * HEAD_DIM) for h in range(N_HEADS)]
    kt_h = KT_A // 2
    n_sub_h = kt_h // KT_I

    def issue_scores(kt, half):
        r0 = pl.multiple_of(kt * KT_A + half * kt_h, kt_h)
        for h, sl in enumerate(heads):
            lg_ref[half, h] = _dot(k_ref[0, pl.ds(r0, kt_h), sl], qt_ref[sl, :])

    def issue_values(kt, half):
        return [_dot(vt_ref[0, kt, sl, half * kt_h:(half + 1) * kt_h], p_ref[half, h])
                for h, sl in enumerate(heads)]

    def softmax_half(kt, half, ms, ls):
        r0 = pl.multiple_of(kt * KT_A + half * kt_h, kt_h)
        new_ms, new_ls, alphas = [], [], []
        for h in range(N_HEADS):
            mx = None
            for s in range(n_sub_h):
                rs = slice(s * KT_I, (s + 1) * KT_I)
                bias = lax.bitcast_convert_type(key_ref[pl.ds(r0 + s * KT_I, KT_I), :], F32)
                lg = lg_ref[half, h, rs, :] + bias
                lg_ref[half, h, rs, :] = lg
                part = jnp.max(lg.reshape(KT_I // 8, 8, tq), axis=0)
                mx = part if mx is None else jnp.maximum(mx, part)
            m_new = jnp.maximum(ms[h], jnp.max(mx, axis=0, keepdims=True))
            alphas.append(jnp.exp2(ms[h] - m_new))
            new_ms.append(m_new)
        for h in range(N_HEADS):
            lsum = None
            for s in range(n_sub_h):
                rs = slice(s * KT_I, (s + 1) * KT_I)
                p = jnp.exp2(lg_ref[half, h, rs, :] - new_ms[h])
                p_ref[half, h, rs, :] = p.astype(BF16)
                part = jnp.sum(p.reshape(KT_I // 8, 8, tq), axis=0)
                lsum = part if lsum is None else lsum + part
            new_ls.append(alphas[h] * ls[h] + jnp.sum(lsum, axis=0, keepdims=True))
        return new_ms, new_ls, alphas

    def accumulate(alphas, pv):
        for h in range(N_HEADS):
            acc_ref[h] = alphas[h] * acc_ref[h] + pv[h]

    def att_step(kt, carry):
        ms, ls = carry
        issue_scores(kt, 0)
        issue_scores(kt, 1)
        ms, ls, alphas_a = softmax_half(kt, 0, ms, ls)
        pv_a = issue_values(kt, 0)
        ms, ls, alphas_b = softmax_half(kt, 1, ms, ls)
        accumulate(alphas_a, pv_a)
        accumulate(alphas_b, issue_values(kt, 1))
        return tuple(ms), tuple(ls)

    m0 = tuple(jnp.full((1, tq), NEG_BIG, F32) for _ in range(N_HEADS))
    l0 = tuple(jnp.zeros((1, tq), F32) for _ in range(N_HEADS))
    _, ls = lax.fori_loop(0, n_at, att_step, (m0, l0))
    z_all = z_ref[...]
    for h in range(N_HEADS):
        sl = slice(h * HEAD_DIM, (h + 1) * HEAD_DIM)
        oh = (acc_ref[h] / ls[h]).T
        o_ref[:, sl] = (oh * _silu(z_all[:, sl])).astype(o_ref.dtype)


def _dsa(proj, small, k_all16, vt16, ki16, gq_row, batch, q_len, tq, offset, s_valid):
    n = batch * q_len
    nq = q_len // tq
    s_pad = k_all16.shape[1]
    topk = min(TOPK_MAX, s_valid // 4)
    gw = GROUP_WIDTH
    qi_w = IDX_HEADS * IDX_DIM
    kern = functools.partial(_dsa_kernel, tq=tq, offset=offset, s_valid=s_valid, topk=topk)
    once = pl.Buffered(1)
    return pl.pallas_call(
        kern,
        grid=(batch, nq),
        in_specs=[
            pl.BlockSpec((tq, gw), lambda b, i: (b * nq + i, COL_QC // gw)),
            pl.BlockSpec((tq, qi_w), lambda b, i: (b * nq + i, COL_QI // qi_w)),
            pl.BlockSpec((tq, gw), lambda b, i: (b * nq + i, COL_ZC // gw)),
            pl.BlockSpec((tq, SMALL_COLS), lambda b, i: (b * nq + i, 0)),
            pl.BlockSpec((1, s_pad, gw), lambda b, i: (b, 0, 0), pipeline_mode=once),
            pl.BlockSpec((1, s_pad // KT_A, gw, KT_A), lambda b, i: (b, 0, 0, 0), pipeline_mode=once),
            pl.BlockSpec((1, s_pad, IDX_DIM), lambda b, i: (b, 0, 0), pipeline_mode=once),
            pl.BlockSpec((1, HEAD_DIM), lambda b, i: (0, 0)),
        ],
        out_specs=pl.BlockSpec((tq, gw), lambda b, i: (b * nq + i, 0)),
        out_shape=jax.ShapeDtypeStruct((n, gw), BF16),
        scratch_shapes=[pltpu.VMEM((s_pad, tq), I32),
                        pltpu.VMEM((qi_w, tq), BF16),
                        pltpu.VMEM((SMALL_COLS, tq), F32),
                        pltpu.VMEM((8, tq), I32),
                        pltpu.VMEM((gw, tq), BF16),
                        pltpu.VMEM((N_HEADS, HEAD_DIM, tq), F32),
                        pltpu.VMEM((2, N_HEADS, KT_A // 2, tq), F32),
                        pltpu.VMEM((2, N_HEADS, KT_A // 2, tq), BF16),
                        pltpu.VMEM((s_pad, tq), jnp.int16),
                        pltpu.VMEM((s_pad, tq), jnp.int16)],
        compiler_params=_cparams(("parallel", "arbitrary")),
        name="dsa",
    )(proj, proj, proj, small, k_all16, vt16, ki16, gq_row)


def _mem_kernel(q_ref, z_ref, mk_ref, mv_ref, gq_ref, o_ref):
    gq = gq_ref[...]
    q_all, z_all = q_ref[...], z_ref[...]
    mk = mk_ref[0].astype(BF16)
    mv = mv_ref[0].astype(BF16)
    for h in range(N_HEADS):
        sl = slice(h * HEAD_DIM, (h + 1) * HEAD_DIM)
        qh = q_all[:, sl]
        qh = qh * lax.rsqrt(jnp.mean(qh * qh, axis=-1, keepdims=True) + EPS) * gq * (HEAD_DIM ** -0.5)
        logit = _dot_nt(qh.astype(BF16), mk[:, sl])
        m = jnp.max(logit, axis=-1, keepdims=True)
        p = jnp.exp(logit - m)
        l = jnp.sum(p, axis=-1, keepdims=True)
        oh = _dot(p.astype(BF16), mv[:, sl]) / l
        o_ref[:, sl] = (oh * _silu(z_all[:, sl])).astype(o_ref.dtype)


def _mem_attend(proj, mk, mv, gq_row, batch, seqlen):
    n = batch * seqlen
    tm = min(512, seqlen)
    nj = seqlen // tm
    gw = GROUP_WIDTH
    n_mem = mk.shape[1]
    return pl.pallas_call(
        _mem_kernel,
        grid=(batch, nj),
        in_specs=[
            pl.BlockSpec((tm, gw), lambda b, j: (b * nj + j, COL_QD // gw)),
            pl.BlockSpec((tm, gw), lambda b, j: (b * nj + j, COL_ZD // gw)),
            pl.BlockSpec((1, n_mem, gw), lambda b, j: (b, 0, 0)),
            pl.BlockSpec((1, n_mem, gw), lambda b, j: (b, 0, 0)),
            pl.BlockSpec((1, HEAD_DIM), lambda b, j: (0, 0)),
        ],
        out_specs=pl.BlockSpec((tm, gw), lambda b, j: (b * nj + j, 0)),
        out_shape=jax.ShapeDtypeStruct((n, gw), BF16),
        compiler_params=_cparams(("parallel", "arbitrary")),
        name="mem_attend",
    )(proj, proj, mk, mv, gq_row)


def _outproj_kernel(x_ref, a_ref, b_ref, c_ref, d_ref, w_ref, y_ref):
    gw = GROUP_WIDTH
    acc = x_ref[...] + _dot(a_ref[...], w_ref[0:gw, :])
    acc = acc + _dot(b_ref[...], w_ref[gw:2 * gw, :])
    acc = acc + _dot(c_ref[...], w_ref[2 * gw:3 * gw, :])
    acc = acc + _dot(d_ref[...], w_ref[3 * gw:4 * gw, :])
    y_ref[...] = acc


def _out_proj(x2d, oa, ob, oc, od, w16):
    n, d = x2d.shape
    tm = min(512, n)
    gw = GROUP_WIDTH
    grp = pl.BlockSpec((tm, gw), lambda i: (i, 0))
    return pl.pallas_call(
        _outproj_kernel,
        grid=(n // tm,),
        in_specs=[pl.BlockSpec((tm, d), lambda i: (i, 0)), grp, grp, grp, grp,
                  pl.BlockSpec((4 * gw, d), lambda i: (0, 0))],
        out_specs=pl.BlockSpec((tm, d), lambda i: (i, 0)),
        out_shape=jax.ShapeDtypeStruct((n, d), F32),
        compiler_params=_cparams(("parallel",)),
        name="out_proj",
    )(x2d, oa, ob, oc, od, w16)


_W_A_END = 4 * GROUP_WIDTH
_W_B_START = _W_A_END + 2 * N_HEADS
_W_B_END = _W_B_START + 8 * GROUP_WIDTH + IDX_HEADS * IDX_DIM
_W_D_START = _W_B_END + IDX_DIM + IDX_HEADS
_W_COLS = _W_D_START + 2 * GROUP_WIDTH


def _w_in_kernel(w_ref, m_ref, s_ref):
    m_ref[:, 0:_W_A_END] = w_ref[:, 0:_W_A_END].astype(BF16)
    m_ref[:, _W_A_END:_W_A_END + _W_B_END - _W_B_START] = w_ref[:, _W_B_START:_W_B_END].astype(BF16)
    m_ref[:, MAIN_COLS - 2 * GROUP_WIDTH:MAIN_COLS] = w_ref[:, _W_D_START:_W_COLS].astype(BF16)
    n_kw = IDX_DIM + IDX_HEADS
    s_ref[:, 0:n_kw] = w_ref[:, _W_B_END:_W_D_START].astype(BF16)
    s_ref[:, n_kw:n_kw + 2 * N_HEADS] = w_ref[:, _W_A_END:_W_B_START].astype(BF16)
    s_ref[:, n_kw + 2 * N_HEADS:] = jnp.zeros((w_ref.shape[0], SMALL_COLS - n_kw - 2 * N_HEADS), BF16)


def _prep_w_in(w_in):
    depth, d, cols = w_in.shape
    assert cols == _W_COLS and MAIN_COLS == _W_A_END + (_W_B_END - _W_B_START) + 2 * GROUP_WIDTH
    tm = 128
    return pl.pallas_call(
        _w_in_kernel,
        grid=(depth, d // tm),
        in_specs=[pl.BlockSpec((None, tm, cols), lambda l, i: (l, i, 0))],
        out_specs=[pl.BlockSpec((None, tm, MAIN_COLS), lambda l, i: (l, i, 0)),
                   pl.BlockSpec((None, tm, SMALL_COLS), lambda l, i: (l, i, 0))],
        out_shape=[jax.ShapeDtypeStruct((depth, d, MAIN_COLS), BF16),
                   jax.ShapeDtypeStruct((depth, d, SMALL_COLS), BF16)],
        compiler_params=_cparams(("parallel", "parallel")),
        name="w_in_prep",
    )(w_in)


def _lane_row(vals, lane0):
    row = jnp.zeros((1, SMALL_COLS), F32)
    return row.at[0, lane0:lane0 + vals.shape[0]].set(vals.astype(F32))


def _rope_tables(pos):
    half = HEAD_DIM // 2
    inv = ROPE_THETA ** (-jnp.arange(half, dtype=F32) / half)
    ang = pos.astype(F32)[:, None] * inv[None, :]
    cos, sin = jnp.cos(ang), jnp.sin(ang)
    return jnp.concatenate([cos, cos], axis=-1), jnp.concatenate([-sin, sin], axis=-1)


def _round_up(x, m):
    return (x + m - 1) // m * m


def _mixer_layer(x, conv_buf, s_gdn, s_ret, past_k, past_v, past_ki, mem_k, mem_v, wts, layer, depth, prev_cache):
    b, l, d = x.shape
    n = b * l
    offset = 0 if past_k is None else past_k.shape[1]
    x2d = x.reshape(n, d)
    proj, small = _in_proj(x2d, wts["norm_g"], wts["w_main"], wts["w_small"], layer)

    o_a, conv_new, s_gdn_new = _gdn(proj, small, wts["conv_w"], conv_buf, s_gdn, wts["alog_row"],
                                    wts["dtb_row"], wts["gdn_norm_g"], b, l)

    cos_t, sin_t = wts["rope_p"] if past_k is None else wts["rope_s"]
    o_b, s_ret_new = _retention(proj, cos_t, sin_t, s_ret, wts["ret_norm_g"], wts["ret_norm_b"], b, l)

    direct = past_k is None and l % KT_A == 0
    prep = _dsa_prep(proj, small, wts["dsa_k_norm_g"], wts["idx_k_norm_g"], layer, depth, prev_cache, direct)
    cache = tuple(prep[:3])
    kc16, kic16 = prep[3].reshape(b, l, GROUP_WIDTH), prep[4].reshape(b, l, IDX_DIM)
    if direct:
        s_valid = l
        k16, ki16 = kc16, kic16
        vt16 = prep[5].reshape(b, l // KT_A, GROUP_WIDTH, KT_A)
    else:
        vc16 = cache[1][layer].astype(BF16).reshape(b, l, GROUP_WIDTH)
        if past_k is None:
            k_all, v_all, ki_all = kc16, vc16, kic16
        else:
            k_all = jnp.concatenate([past_k.reshape(b, offset, GROUP_WIDTH).astype(BF16), kc16], axis=1)
            v_all = jnp.concatenate([past_v.reshape(b, offset, GROUP_WIDTH).astype(BF16), vc16], axis=1)
            ki_all = jnp.concatenate([past_ki.astype(BF16), kic16], axis=1)
        s_valid = k_all.shape[1]
        s_pad = _round_up(s_valid, KT_A)
        padw = ((0, 0), (0, s_pad - s_valid), (0, 0))
        k16 = jnp.pad(k_all, padw)
        ki16 = jnp.pad(ki_all, padw)
        vt16 = jnp.pad(v_all, padw).reshape(b, s_pad // KT_A, KT_A, GROUP_WIDTH).transpose(0, 1, 3, 2)

    tq = 256 if l % 256 == 0 else 128
    q_len = _round_up(l, tq)
    if q_len != l:
        proj_q = jnp.pad(proj.reshape(b, l, MAIN_COLS), ((0, 0), (0, q_len - l), (0, 0))).reshape(b * q_len, MAIN_COLS)
        small_q = jnp.pad(small.reshape(b, l, SMALL_COLS), ((0, 0), (0, q_len - l), (0, 0))).reshape(b * q_len, SMALL_COLS)
    else:
        proj_q, small_q = proj, small
    o_c = _dsa(proj_q, small_q, k16, vt16, ki16, wts["dsa_q_norm_g"], b, q_len, tq, offset, s_valid)
    if q_len != l:
        o_c = o_c.reshape(b, q_len, GROUP_WIDTH)[:, :l].reshape(n, GROUP_WIDTH)

    o_d = _mem_attend(proj, mem_k, mem_v, wts["mem_q_norm_g"], b, l)

    y = _out_proj(x2d, o_a, o_b, o_c, o_d, wts["w_out"]).reshape(b, l, d)
    return y, (conv_new, s_gdn_new, s_ret_new), cache


def kernel(x_prompt, x_sample, cache_gdn_conv, state_gdn, state_ret, cache_dsa_k, cache_dsa_v, cache_idx_k, cache_mem_k, cache_mem_v, mem_prompt, norm_g, w_in, gdn_conv_w, gdn_a_log, gdn_dt_bias, gdn_norm_g, ret_norm_g, ret_norm_b, dsa_q_norm_g, dsa_k_norm_g, idx_k_norm_g, mem_norm_g, w_mem_kv, mem_q_norm_g, mem_k_norm_g, w_out):
    depth = w_in.shape[0]
    b = x_prompt.shape[0]
    n_mem = mem_prompt.shape[1]
    d = x_prompt.shape[-1]
    y_p, y_s = x_prompt, x_sample
    st_p, st_s, mem_p = [], [], []
    cache_p = cache_s = None
    w_main, w_small = _prep_w_in(w_in)
    rope_p = _rope_tables(jnp.arange(x_prompt.shape[1], dtype=I32))
    rope_s = _rope_tables(cache_dsa_k.shape[2] + jnp.arange(x_sample.shape[1], dtype=I32))
    for li in range(depth):
        wts = dict(
            norm_g=norm_g[li][None, :], w_main=w_main, w_small=w_small, rope_p=rope_p, rope_s=rope_s,
            conv_w=gdn_conv_w[li],
            alog_row=_lane_row(gdn_a_log[li], LANE_ALPHA), dtb_row=_lane_row(gdn_dt_bias[li], LANE_ALPHA),
            gdn_norm_g=gdn_norm_g[li][None, :], ret_norm_g=ret_norm_g[li][None, :],
            ret_norm_b=ret_norm_b[li][None, :], dsa_q_norm_g=dsa_q_norm_g[li][None, :],
            dsa_k_norm_g=dsa_k_norm_g[li][None, :], idx_k_norm_g=idx_k_norm_g[li][None, :],
            mem_q_norm_g=mem_q_norm_g[li][None, :], w_out=w_out[li].astype(BF16),
        )
        mk, mv = _memory_kv(mem_prompt.reshape(b * n_mem, d), mem_norm_g[li][None, :],
                            w_mem_kv[li].astype(BF16), mem_k_norm_g[li][None, :])
        mk = mk.reshape(b, n_mem, GROUP_WIDTH)
        mv = mv.reshape(b, n_mem, GROUP_WIDTH)
        conv0 = jnp.zeros((b, CONV_W - 1, 3 * GROUP_WIDTH), F32)
        s0 = jnp.zeros((b, N_HEADS, HEAD_DIM, HEAD_DIM), F32)
        y_p, sp, cache_p = _mixer_layer(y_p, conv0, s0, s0, None, None, None, mk, mv, wts, li, depth, cache_p)
        st_p.append(sp)
        mem_p.append((mk.reshape(b, n_mem, N_HEADS, HEAD_DIM), mv.reshape(b, n_mem, N_HEADS, HEAD_DIM)))
        bs = x_sample.shape[0]
        y_s, ss, cache_s = _mixer_layer(y_s, cache_gdn_conv[li], state_gdn[li], state_ret[li],
                                        cache_dsa_k[li], cache_dsa_v[li], cache_idx_k[li],
                                        cache_mem_k[li].reshape(bs, n_mem, GROUP_WIDTH),
                                        cache_mem_v[li].reshape(bs, n_mem, GROUP_WIDTH), wts, li, depth, cache_s)
        st_s.append(ss)

    def stack(lst, k):
        return jnp.stack([s[k] for s in lst])

    def caches(c, bb, ll):
        return (c[0].reshape(depth, bb, ll, N_HEADS, HEAD_DIM), c[1].reshape(depth, bb, ll, N_HEADS, HEAD_DIM),
                c[2].reshape(depth, bb, ll, IDX_DIM))

    return ((y_p, y_s, stack(st_p, 0), stack(st_p, 1), stack(st_p, 2))
            + caches(cache_p, b, x_prompt.shape[1])
            + (stack(mem_p, 0), stack(mem_p, 1), stack(st_s, 0), stack(st_s, 1), stack(st_s, 2))
            + caches(cache_s, x_sample.shape[0], x_sample.shape[1]))
```

```python
import functools
import math

import jax
import jax.numpy as jnp
from jax import lax
from jax.experimental import pallas as pl
from jax.experimental.pallas import tpu as pltpu

F32 = jnp.float32
BF16 = jnp.bfloat16
I32 = jnp.int32

HEAD_DIM = 128
N_HEADS = 4
GROUP_WIDTH = N_HEADS * HEAD_DIM
CHUNK = 64
CONV_W = 4
IDX_HEADS = 16
IDX_DIM = 64
TOPK_MAX = 256
ROPE_THETA = 10000.0
EPS = 1e-6

COL_QA, COL_KA, COL_VA, COL_ZA = 0, 512, 1024, 1536
COL_QB, COL_KB, COL_VB, COL_ZB = 2048, 2560, 3072, 3584
COL_QC, COL_KC, COL_VC, COL_ZC = 4096, 4608, 5120, 5632
COL_QI = 6144
COL_QD, COL_ZD = 7168, 7680
MAIN_COLS = 8192
LANE_KI, LANE_WI, LANE_BETA, LANE_ALPHA = 0, 64, 80, 84
SMALL_COLS = 128

INT_MIN = -2 ** 31
NEG_BIG = -1e30
LOG2_E = 1.4426950408889634
VMEM_LIMIT = 56 * 1024 * 1024
HI = lax.Precision.HIGHEST


def _cparams(sem):
    return pltpu.CompilerParams(dimension_semantics=sem, vmem_limit_bytes=VMEM_LIMIT)


def _dot(a, b):
    return jnp.dot(a, b, preferred_element_type=F32)


def _dot_nt(a, b):
    return lax.dot_general(a, b, (((1,), (1,)), ((), ())), preferred_element_type=F32)


def _dot3(a, b):
    a_hi = a.astype(BF16)
    b_hi = b.astype(BF16)
    a_lo = (a - a_hi.astype(F32)).astype(BF16)
    b_lo = (b - b_hi.astype(F32)).astype(BF16)
    return _dot(a_hi, b_hi) + (_dot(a_hi, b_lo) + _dot(a_lo, b_hi))


def _silu(x):
    return x * jax.nn.sigmoid(x)


def _inproj_kernel(x_ref, g_ref, w_ref, ws_ref, o_ref, os_ref, h_ref):
    @pl.when(pl.program_id(1) == 0)
    def _():
        x = x_ref[...]
        y = x * lax.rsqrt(jnp.mean(x * x, axis=-1, keepdims=True) + EPS) * g_ref[...]
        hb = y.astype(BF16)
        h_ref[...] = hb
        os_ref[...] = _dot(hb, ws_ref[...])

    o_ref[...] = _dot(h_ref[...], w_ref[...])


def _in_proj(x2d, g_row, w_main, w_small, layer):
    n, d = x2d.shape
    tm = min(1024, n)
    tn = 1024
    return pl.pallas_call(
        _inproj_kernel,
        grid=(n // tm, MAIN_COLS // tn),
        in_specs=[
            pl.BlockSpec((tm, d), lambda i, j: (i, 0)),
            pl.BlockSpec((1, d), lambda i, j: (0, 0)),
            pl.BlockSpec((None, d, tn), lambda i, j: (layer, 0, j)),
            pl.BlockSpec((None, d, SMALL_COLS), lambda i, j: (layer, 0, 0)),
        ],
        out_specs=[
            pl.BlockSpec((tm, tn), lambda i, j: (i, j)),
            pl.BlockSpec((tm, SMALL_COLS), lambda i, j: (i, 0)),
        ],
        out_shape=[jax.ShapeDtypeStruct((n, MAIN_COLS), F32),
                   jax.ShapeDtypeStruct((n, SMALL_COLS), F32)],
        scratch_shapes=[pltpu.VMEM((tm, d), BF16)],
        compiler_params=_cparams(("parallel", "arbitrary")),
        name="in_proj",
    )(x2d, g_row, w_main, w_small)


def _memkv_kernel(x_ref, g_ref, w_ref, gk_ref, mk_ref, mv_ref):
    x = x_ref[...]
    y = x * lax.rsqrt(jnp.mean(x * x, axis=-1, keepdims=True) + EPS) * g_ref[...]
    kv = _dot(y.astype(BF16), w_ref[...])
    gk = gk_ref[...]
    for h in range(N_HEADS):
        sl = slice(h * HEAD_DIM, (h + 1) * HEAD_DIM)
        kh = kv[:, sl]
        mk_ref[:, sl] = kh * lax.rsqrt(jnp.mean(kh * kh, axis=-1, keepdims=True) + EPS) * gk
    mv_ref[...] = kv[:, GROUP_WIDTH:]


def _memory_kv(mem2d, g_row, w_kv, gk_row):
    n, d = mem2d.shape
    tm = min(256, n)
    return pl.pallas_call(
        _memkv_kernel,
        grid=(n // tm,),
        in_specs=[
            pl.BlockSpec((tm, d), lambda i: (i, 0)),
            pl.BlockSpec((1, d), lambda i: (0, 0)),
            pl.BlockSpec((d, 2 * GROUP_WIDTH), lambda i: (0, 0)),
            pl.BlockSpec((1, HEAD_DIM), lambda i: (0, 0)),
        ],
        out_specs=[pl.BlockSpec((tm, GROUP_WIDTH), lambda i: (i, 0)),
                   pl.BlockSpec((tm, GROUP_WIDTH), lambda i: (i, 0))],
        out_shape=[jax.ShapeDtypeStruct((n, GROUP_WIDTH), F32),
                   jax.ShapeDtypeStruct((n, GROUP_WIDTH), F32)],
        compiler_params=_cparams(("parallel",)),
        name="memory_kv",
    )(mem2d, g_row, w_kv, gk_row)


def _gdn_kernel(qkv_ref, z_ref, sm_ref, cw_ref, cb_ref, s0_ref, alog_ref, dtb_ref, gn_ref,
                o_ref, conv_ref, st_ref, xbuf_ref, s_ref, *, t_blk, chunk):
    j = pl.program_id(1)
    nj = pl.num_programs(1)
    gw3 = 3 * GROUP_WIDTH

    @pl.when(j == 0)
    def _():
        xbuf_ref[0:8, :] = jnp.zeros((8, gw3), F32)
        xbuf_ref[5:8, :] = cb_ref[0]
        s_ref[...] = s0_ref[0]

    @pl.when(j > 0)
    def _():
        xbuf_ref[0:8, :] = xbuf_ref[t_blk:t_blk + 8, :]

    xbuf_ref[8:8 + t_blk, :] = qkv_ref[...]
    conv_ref[0] = xbuf_ref[t_blk + 5:t_blk + 8, :]

    cw = cw_ref[...]
    x_all = xbuf_ref[...]
    x_prev = pltpu.roll(x_all, 1, 0)
    u = x_all * cw[3:4, :] + x_prev * cw[2:3, :]
    v = x_all * cw[1:2, :] + x_prev * cw[0:1, :]
    y = _silu((u + pltpu.roll(v, 2, 0))[8:8 + t_blk, :])

    sm = sm_ref[...]
    lane = lax.broadcasted_iota(I32, sm.shape, 1)
    beta_all = jax.nn.sigmoid(sm)
    xs = sm + dtb_ref[...]
    softplus = jnp.maximum(xs, 0.0) + jnp.log1p(jnp.exp(-jnp.abs(xs)))
    g_all = -jnp.exp(alog_ref[...]) * softplus
    g_all = jnp.where((lane >= LANE_ALPHA) & (lane < LANE_ALPHA + N_HEADS), g_all, 0.0)
    ri = lax.broadcasted_iota(I32, (t_blk, t_blk), 0)
    ci = lax.broadcasted_iota(I32, (t_blk, t_blk), 1)
    tri = jnp.where((ri // chunk == ci // chunk) & (ci <= ri), 1.0, 0.0).astype(F32)
    gcum = jnp.dot(tri, g_all, preferred_element_type=F32, precision=HI)
    gcum_t = gcum.T

    blk = min(2 * chunk, t_blk)
    n_blk = t_blk // blk
    cpb = blk // chunk
    rb = lax.broadcasted_iota(I32, (blk, blk), 0)
    cb = lax.broadcasted_iota(I32, (blk, blk), 1)
    same = (rb // chunk) == (cb // chunk)
    causal = same & (cb <= rb)
    strict = same & (cb < rb)
    eye = jnp.where(rb == cb, 1.0, 0.0).astype(F32)
    n_dbl = max(int(math.log2(chunk)) - 1, 0)
    gn = gn_ref[...]
    z_all = z_ref[...]

    units = [(h, b) for h in range(N_HEADS) for b in range(n_blk)]
    qs, ks, gcols, p_mats, x_mats, qk_mats, vb_mats, kg_mats = {}, {}, {}, {}, {}, {}, {}, {}
    for h in range(N_HEADS):
        qh = y[:, COL_QA + h * HEAD_DIM:COL_QA + (h + 1) * HEAD_DIM]
        kh = y[:, COL_KA + h * HEAD_DIM:COL_KA + (h + 1) * HEAD_DIM]
        vh = y[:, COL_VA + h * HEAD_DIM:COL_VA + (h + 1) * HEAD_DIM]
        qh = qh * lax.rsqrt(jnp.sum(qh * qh, axis=-1, keepdims=True) + EPS) * (HEAD_DIM ** -0.5)
        kh = kh * lax.rsqrt(jnp.sum(kh * kh, axis=-1, keepdims=True) + EPS)
        for b in range(n_blk):
            rs = slice(b * blk, (b + 1) * blk)
            gcol = gcum[rs, LANE_ALPHA + h:LANE_ALPHA + h + 1]
            grow = gcum_t[LANE_ALPHA + h:LANE_ALPHA + h + 1, rs]
            bcol = beta_all[rs, LANE_BETA + h:LANE_BETA + h + 1]
            decay = jnp.where(causal, jnp.exp(jnp.where(causal, gcol - grow, 0.0)), 0.0)
            qc, kc, vc = qh[rs], kh[rs], vh[rs]
            kb = kc * bcol
            kc16 = kc.astype(BF16)
            a_mat = jnp.where(strict, _dot_nt(kb.astype(BF16), kc16) * decay, 0.0)
            qk_mats[h, b] = jnp.where(causal, _dot_nt(qc.astype(BF16), kc16) * decay, 0.0).astype(BF16)
            p_mats[h, b] = -a_mat
            x_mats[h, b] = eye - a_mat
            vb_mats[h, b] = (vc * bcol).astype(BF16)
            kg_mats[h, b] = (kb * jnp.exp(gcol)).astype(BF16)
            qs[h, b], ks[h, b], gcols[h, b] = qc, kc, gcol

    for _ in range(n_dbl):
        for u_ in units:
            p_mats[u_] = _dot3(p_mats[u_], p_mats[u_])
        for u_ in units:
            x_mats[u_] = x_mats[u_] + _dot3(x_mats[u_], p_mats[u_])

    u_mats, w_mats = {}, {}
    for u_ in units:
        x16 = x_mats[u_].astype(BF16)
        u_mats[u_] = _dot(x16, vb_mats[u_])
        w_mats[u_] = _dot(x16, kg_mats[u_]).astype(BF16)

    states = [s_ref[h] for h in range(N_HEADS)]
    for b in range(n_blk):
        o_state = {h: [] for h in range(N_HEADS)}
        v_new = {h: [] for h in range(N_HEADS)}
        for c in range(cpb):
            cs = slice(c * chunk, (c + 1) * chunk)
            for h in range(N_HEADS):
                s = states[h]
                s16 = s.astype(BF16)
                gcol = gcols[h, b][cs]
                glast = gcols[h, b][(c + 1) * chunk - 1:(c + 1) * chunk]
                vn = u_mats[h, b][cs] - _dot(w_mats[h, b][cs], s16)
                vn16 = vn.astype(BF16)
                kd_t = (ks[h, b][cs] * jnp.exp(glast - gcol)).T.astype(BF16)
                states[h] = s * jnp.exp(glast) + _dot(kd_t, vn16)
                o_state[h].append(_dot((qs[h, b][cs] * jnp.exp(gcol)).astype(BF16), s16))
                v_new[h].append(vn16)
        for h in range(N_HEADS):
            sl = slice(h * HEAD_DIM, (h + 1) * HEAD_DIM)
            rs = slice(b * blk, (b + 1) * blk)
            o = jnp.concatenate(o_state[h], axis=0) + _dot(qk_mats[h, b], jnp.concatenate(v_new[h], axis=0))
            on = o * lax.rsqrt(jnp.mean(o * o, axis=-1, keepdims=True) + EPS) * gn
            o_ref[rs, sl] = (on * _silu(z_all[rs, sl])).astype(o_ref.dtype)
    for h in range(N_HEADS):
        s_ref[h] = states[h]

    @pl.when(j == nj - 1)
    def _():
        st_ref[0] = s_ref[...]


def _gdn(proj, small, conv_w, conv_buf, s0, alog_row, dtb_row, gn_row, batch, seqlen):
    n = batch * seqlen
    t_blk = min(256, seqlen)
    chunk = min(CHUNK, seqlen)
    nj = seqlen // t_blk
    gw3 = 3 * GROUP_WIDTH
    kern = functools.partial(_gdn_kernel, t_blk=t_blk, chunk=chunk)
    return pl.pallas_call(
        kern,
        grid=(batch, nj),
        in_specs=[
            pl.BlockSpec((t_blk, gw3), lambda b, j: (b * nj + j, COL_QA // gw3)),
            pl.BlockSpec((t_blk, GROUP_WIDTH), lambda b, j: (b * nj + j, COL_ZA // GROUP_WIDTH)),
            pl.BlockSpec((t_blk, SMALL_COLS), lambda b, j: (b * nj + j, 0)),
            pl.BlockSpec((CONV_W, gw3), lambda b, j: (0, 0)),
            pl.BlockSpec((1, CONV_W - 1, gw3), lambda b, j: (b, 0, 0)),
            pl.BlockSpec((1, N_HEADS, HEAD_DIM, HEAD_DIM), lambda b, j: (b, 0, 0, 0)),
            pl.BlockSpec((1, SMALL_COLS), lambda b, j: (0, 0)),
            pl.BlockSpec((1, SMALL_COLS), lambda b, j: (0, 0)),
            pl.BlockSpec((1, HEAD_DIM), lambda b, j: (0, 0)),
        ],
        out_specs=[
            pl.BlockSpec((t_blk, GROUP_WIDTH), lambda b, j: (b * nj + j, 0)),
            pl.BlockSpec((1, CONV_W - 1, gw3), lambda b, j: (b, 0, 0)),
            pl.BlockSpec((1, N_HEADS, HEAD_DIM, HEAD_DIM), lambda b, j: (b, 0, 0, 0)),
        ],
        out_shape=[
            jax.ShapeDtypeStruct((n, GROUP_WIDTH), BF16),
            jax.ShapeDtypeStruct((batch, CONV_W - 1, gw3), F32),
            jax.ShapeDtypeStruct((batch, N_HEADS, HEAD_DIM, HEAD_DIM), F32),
        ],
        scratch_shapes=[pltpu.VMEM((t_blk + 8, gw3), F32),
                        pltpu.VMEM((N_HEADS, HEAD_DIM, HEAD_DIM), F32)],
        compiler_params=_cparams(("parallel", "arbitrary")),
        name="gdn",
    )(proj, proj, small, conv_w, conv_buf, s0, alog_row, dtb_row, gn_row)


def _ret_kernel(q_ref, k_ref, v_ref, z_ref, cos_ref, sin_ref, s0_ref, g_ref, b_ref,
                o_ref, st_ref, s_ref, *, t_blk):
    j = pl.program_id(1)
    nj = pl.num_programs(1)

    @pl.when(j == 0)
    def _():
        s_ref[...] = s0_ref[0]

    cos = cos_ref[...]
    sin = sin_ref[...]
    ri = lax.broadcasted_iota(I32, (t_blk, t_blk), 0)
    ci = lax.broadcasted_iota(I32, (t_blk, t_blk), 1)
    causal = ci <= ri
    rel = jnp.where(causal, ri - ci, 0).astype(F32)
    idx_col = lax.broadcasted_iota(I32, (t_blk, 1), 0).astype(F32)
    gamma_g = g_ref[...]
    gamma_b = b_ref[...]
    q_all, k_all, v_all, z_all = q_ref[...], k_ref[...], v_ref[...], z_ref[...]

    for h in range(N_HEADS):
        sl = slice(h * HEAD_DIM, (h + 1) * HEAD_DIM)
        lg = math.log(1.0 - 2.0 ** (-5.0 - h))
        qh, kh, vh = q_all[:, sl], k_all[:, sl], v_all[:, sl]
        qh = qh * cos + pltpu.roll(qh, HEAD_DIM // 2, 1) * sin
        kh = (kh * cos + pltpu.roll(kh, HEAD_DIM // 2, 1) * sin) * (HEAD_DIM ** -0.5)
        d_mat = jnp.where(causal, jnp.exp(lg * rel), 0.0)
        q16, k16, v16 = qh.astype(BF16), kh.astype(BF16), vh.astype(BF16)
        o_intra = _dot((_dot_nt(q16, k16) * d_mat).astype(BF16), v16)
        s = s_ref[h]
        o_cross = _dot(q16, s.astype(BF16)) * jnp.exp(lg * (idx_col + 1.0))
        kd_t = (kh * jnp.exp(lg * (t_blk - 1.0 - idx_col))).T.astype(BF16)
        s_ref[h] = s * math.exp(lg * t_blk) + _dot(kd_t, v16)
        o = o_intra + o_cross
        mu = jnp.mean(o, axis=-1, keepdims=True)
        oc = o - mu
        var = jnp.mean(oc * oc, axis=-1, keepdims=True)
        on = oc * lax.rsqrt(var + EPS) * gamma_g + gamma_b
        o_ref[:, sl] = (on * _silu(z_all[:, sl])).astype(o_ref.dtype)

    @pl.when(j == nj - 1)
    def _():
        st_ref[0] = s_ref[...]


def _retention(proj, cos_t, sin_t, s0, g_row, b_row, batch, seqlen):
    n = batch * seqlen
    t_blk = min(256, seqlen)
    nj = seqlen // t_blk
    gw = GROUP_WIDTH
    kern = functools.partial(_ret_kernel, t_blk=t_blk)

    def col(c):
        return pl.BlockSpec((t_blk, gw), lambda b, j: (b * nj + j, c // gw))

    return pl.pallas_call(
        kern,
        grid=(batch, nj),
        in_specs=[
            col(COL_QB), col(COL_KB), col(COL_VB), col(COL_ZB),
            pl.BlockSpec((t_blk, HEAD_DIM), lambda b, j: (j, 0)),
            pl.BlockSpec((t_blk, HEAD_DIM), lambda b, j: (j, 0)),
            pl.BlockSpec((1, N_HEADS, HEAD_DIM, HEAD_DIM), lambda b, j: (b, 0, 0, 0)),
            pl.BlockSpec((1, HEAD_DIM), lambda b, j: (0, 0)),
            pl.BlockSpec((1, HEAD_DIM), lambda b, j: (0, 0)),
        ],
        out_specs=[
            pl.BlockSpec((t_blk, gw), lambda b, j: (b * nj + j, 0)),
            pl.BlockSpec((1, N_HEADS, HEAD_DIM, HEAD_DIM), lambda b, j: (b, 0, 0, 0)),
        ],
        out_shape=[
            jax.ShapeDtypeStruct((n, gw), BF16),
            jax.ShapeDtypeStruct((batch, N_HEADS, HEAD_DIM, HEAD_DIM), F32),
        ],
        scratch_shapes=[pltpu.VMEM((N_HEADS, HEAD_DIM, HEAD_DIM), F32)],
        compiler_params=_cparams(("parallel", "arbitrary")),
        name="retention",
    )(proj, proj, proj, proj, cos_t, sin_t, s0, g_row, b_row)


def _dsa_prep_kernel(k_ref, v_ref, sm_ref, gk_ref, gi_ref, *rest, n_prev, with_vt):
    outs = rest[n_prev:]
    ko_ref, vo_ref, kio_ref, k16_ref, ki16_ref = outs[:5]
    k = k_ref[...]
    gk = gk_ref[...]
    for h in range(N_HEADS):
        sl = slice(h * HEAD_DIM, (h + 1) * HEAD_DIM)
        kh = k[:, sl]
        kn = kh * lax.rsqrt(jnp.mean(kh * kh, axis=-1, keepdims=True) + EPS) * gk
        ko_ref[:, h, :] = kn
        k16_ref[:, sl] = kn.astype(BF16)
    v = v_ref[...]
    for h in range(N_HEADS):
        vo_ref[:, h, :] = v[:, h * HEAD_DIM:(h + 1) * HEAD_DIM]
    if with_vt:
        outs[5][...] = v.T.astype(BF16)
    ki = sm_ref[...][:, LANE_KI:LANE_KI + IDX_DIM]
    kin = ki * lax.rsqrt(jnp.mean(ki * ki, axis=-1, keepdims=True) + EPS) * gi_ref[...]
    kio_ref[...] = kin
    ki16_ref[...] = kin.astype(BF16)


def _dsa_prep(proj, small, gk_row, gi_row, layer, depth, prev, with_vt):
    n = proj.shape[0]
    tm = min(KT_A, n)
    gw = GROUP_WIDTH
    n_prev = 0 if prev is None else 3
    kern = functools.partial(_dsa_prep_kernel, n_prev=n_prev, with_vt=with_vt)
    in_specs = [
        pl.BlockSpec((tm, gw), lambda i: (i, COL_KC // gw)),
        pl.BlockSpec((tm, gw), lambda i: (i, COL_VC // gw)),
        pl.BlockSpec((tm, SMALL_COLS), lambda i: (i, 0)),
        pl.BlockSpec((1, HEAD_DIM), lambda i: (0, 0)),
        pl.BlockSpec((1, IDX_DIM), lambda i: (0, 0)),
    ] + [pl.BlockSpec(memory_space=pl.ANY)] * n_prev
    out_specs = [pl.BlockSpec((None, tm, N_HEADS, HEAD_DIM), lambda i: (layer, i, 0, 0)),
                 pl.BlockSpec((None, tm, N_HEADS, HEAD_DIM), lambda i: (layer, i, 0, 0)),
                 pl.BlockSpec((None, tm, IDX_DIM), lambda i: (layer, i, 0)),
                 pl.BlockSpec((tm, gw), lambda i: (i, 0)),
                 pl.BlockSpec((tm, IDX_DIM), lambda i: (i, 0))]
    out_shape = [jax.ShapeDtypeStruct((depth, n, N_HEADS, HEAD_DIM), F32),
                 jax.ShapeDtypeStruct((depth, n, N_HEADS, HEAD_DIM), F32),
                 jax.ShapeDtypeStruct((depth, n, IDX_DIM), F32),
                 jax.ShapeDtypeStruct((n, gw), BF16),
                 jax.ShapeDtypeStruct((n, IDX_DIM), BF16)]
    if with_vt:
        out_specs.append(pl.BlockSpec((None, gw, tm), lambda i: (i, 0, 0)))
        out_shape.append(jax.ShapeDtypeStruct((n // tm, gw, tm), BF16))
    args = (proj, proj, small, gk_row, gi_row) + (() if prev is None else tuple(prev))
    return pl.pallas_call(
        kern,
        grid=(n // tm,),
        in_specs=in_specs,
        out_specs=out_specs,
        out_shape=out_shape,
        input_output_aliases={5 + t: t for t in range(n_prev)},
        compiler_params=_cparams(("parallel",)),
        name="dsa_prep",
    )(*args)


KT_I = 128
KT_A = 512


def _dsa_kernel(q_ref, qi_ref, z_ref, sm_ref, k_ref, vt_ref, ki_ref, gq_ref, o_ref,
                key_ref, qit_ref, wt_ref, jlim_ref, qt_ref, acc_ref, lg_ref, p_ref, key16_ref, low16_ref, *, tq, offset, s_valid, topk):
    i = pl.program_id(1)
    pos0 = offset + i * tq
    t_pos = pos0 + lax.broadcasted_iota(I32, (1, tq), 1)
    t_chunk = t_pos // CHUNK
    n_adm_row = jnp.minimum((t_chunk + 1) * CHUNK, s_valid)
    n_keys = jnp.minimum(((pos0 + tq - 1) // CHUNK + 1) * CHUNK, s_valid)
    n_at = (n_keys + KT_A - 1) // KT_A
    n_sub = KT_A // KT_I
    n_it_full = n_at * n_sub

    qit_ref[...] = qi_ref[...].T.astype(BF16)
    wt_ref[...] = sm_ref[...].T * (IDX_HEADS ** -0.5 * IDX_DIM ** -0.5)

    def index_step(kt, carry):
        r0 = pl.multiple_of(kt * KT_I, KT_I)
        ki_t = ki_ref[0, pl.ds(r0, KT_I), :]
        acc = jnp.zeros((KT_I, tq), F32)
        for h in range(IDX_HEADS):
            sc = _dot(ki_t, qit_ref[h * IDX_DIM:(h + 1) * IDX_DIM, :])
            acc = acc + wt_ref[LANE_WI + h:LANE_WI + h + 1, :] * jnp.maximum(sc, 0.0)
        bits = lax.bitcast_convert_type(acc, I32)
        key = bits ^ ((bits >> 31) & 0x7FFFFFFF)
        s_pos = r0 + lax.broadcasted_iota(I32, (KT_I, 1), 0)
        adm = (s_pos // CHUNK <= t_chunk) & (s_pos < s_valid)
        key = jnp.where(adm, key, INT_MIN)
        key_ref[pl.ds(r0, KT_I), :] = key
        key16_ref[pl.ds(r0, KT_I), :] = (key >> 16).astype(jnp.int16)
        return carry

    lax.fori_loop(0, n_it_full // 2, lambda t, c: index_step(2 * t + 1, index_step(2 * t, c)), 0)

    def count(pred_fn):
        def body(kt, accs):
            out = []
            for s in range(n_sub):
                r0 = pl.multiple_of(kt * KT_A + s * KT_I, KT_I)
                key = key_ref[pl.ds(r0, KT_I), :]
                s_pos = r0 + lax.broadcasted_iota(I32, (KT_I, 1), 0)
                m = jnp.where(pred_fn(key, s_pos), 1, 0).astype(I32)
                out.append(accs[s] + jnp.sum(m.reshape(KT_I // 8, 8, tq), axis=0))
            return tuple(out)
        accs = lax.fori_loop(0, n_at, body, tuple(jnp.zeros((8, tq), I32) for _ in range(n_sub)))
        return jnp.sum(sum(accs[1:], accs[0]), axis=0, keepdims=True)

    def count16(ref16, pred_fn):
        def body(kt, accs):
            out = []
            for s in range(n_sub):
                r0 = pl.multiple_of(kt * KT_A + s * KT_I, KT_I)
                m = jnp.where(pred_fn(ref16[pl.ds(r0, KT_I), :]), jnp.int16(1), jnp.int16(0))
                m = m.reshape(KT_I // 16, 16, tq)
                acc = accs[s]
                for q in range(KT_I // 16):
                    acc = acc + m[q]
                out.append(acc)
            return tuple(out)
        accs = lax.fori_loop(0, n_at, body, tuple(jnp.zeros((16, tq), jnp.int16) for _ in range(n_sub)))
        tot = sum((a.astype(I32) for a in accs[1:]), accs[0].astype(I32))
        return jnp.sum(tot, axis=0, keepdims=True)

    small = jnp.where(n_adm_row <= topk, 1, 0).astype(I32)

    def all_done(done):
        return jnp.min(done.astype(F32)) > 0.0

    def search(first_bit, last_bit, count_ge, carry, early_exit):
        def step(b, c):
            v, done, thr = c
            cand_u = v | jnp.left_shift(jnp.int32(1), 31 - b)
            cand_s = cand_u ^ INT_MIN
            cnt = count_ge(cand_s)
            v = jnp.where(cnt >= topk, cand_u, v)
            newly = (cnt == topk) & (done == 0)
            thr = jnp.where(newly, cand_s, thr)
            return v, jnp.where(newly, 1, done), thr

        if not early_exit:
            return lax.fori_loop(first_bit, last_bit, step, carry)

        def cond(c):
            return (c[0] < last_bit) & jnp.logical_not(all_done(c[1][1]))

        def body(c):
            b, inner = c
            return b + 4, lax.fori_loop(b, b + 4, step, inner)

        assert (last_bit - first_bit) % 4 == 0
        return lax.while_loop(cond, body, (jnp.int32(first_bit), carry))[1]

    v0 = jnp.zeros((1, tq), I32)
    thr0 = jnp.full((1, tq), INT_MIN + 1, I32)
    carry = search(0, 16, lambda cand_s: count16(key16_ref, lambda k16: k16 >= (cand_s >> 16).astype(jnp.int16)),
                   (v0, small, thr0), early_exit=False)
    hi16 = ((carry[0] ^ INT_MIN) >> 16).astype(jnp.int16)
    n_above = count16(key16_ref, lambda k16: k16 > hi16)

    @pl.when(jnp.logical_not(all_done(carry[1])))
    def _():
        hi32 = (carry[0] ^ INT_MIN) >> 16

        def low_step(kt, c):
            for s in range(n_sub):
                r0 = pl.multiple_of(kt * KT_A + s * KT_I, KT_I)
                key = key_ref[pl.ds(r0, KT_I), :]
                low = jnp.where((key >> 16) == hi32, (key & 0xFFFF) - 32768, -32768)
                low16_ref[pl.ds(r0, KT_I), :] = low.astype(jnp.int16)
            return c

        lax.fori_loop(0, n_at, low_step, 0)

    def count_low(cand_s):
        c16 = ((cand_s & 0xFFFF) - 32768).astype(jnp.int16)
        return n_above + count16(low16_ref, lambda l16: l16 >= c16)

    v_u, done_i, thr = search(16, 32, count_low, carry, early_exit=True)
    done = done_i != 0
    v_s = v_u ^ INT_MIN
    any_tie = jnp.logical_not(all_done(done_i))

    jlim_ref[...] = jnp.zeros(jlim_ref.shape, I32)

    @pl.when(any_tie)
    def _():
        n_gt = count(lambda key, s_pos: key > v_s)
        need = topk - n_gt
        pos_bits = max(int(math.ceil(math.log2(max(k_ref.shape[1], 2)))), 1) + 1

        def pos_step(b, jv):
            cand = jv | jnp.left_shift(jnp.int32(1), pos_bits - 1 - b)
            cnt = count(lambda key, s_pos: (key == v_s) & (s_pos < cand))
            return jnp.where(cnt <= need, cand, jv)

        jlim_ref[0:1, :] = lax.fori_loop(0, pos_bits, pos_step, jnp.zeros((1, tq), I32))

    j_lim = jlim_ref[0:1, :]

    v_eff = jnp.where(done, thr - 1, v_s)
    j_eff = jnp.where(done, 0, j_lim)

    def bias_loop(with_ties):
        def bias_step(kt, carry):
            for s in range(n_sub):
                r0 = pl.multiple_of(kt * KT_A + s * KT_I, KT_I)
                key = key_ref[pl.ds(r0, KT_I), :]
                sel = key > v_eff
                if with_ties:
                    s_pos = r0 + lax.broadcasted_iota(I32, (KT_I, 1), 0)
                    sel = sel | ((key == v_eff) & (s_pos < j_eff) & (key != INT_MIN))
                bias = jnp.where(sel, 0.0, NEG_BIG).astype(F32)
                key_ref[pl.ds(r0, KT_I), :] = lax.bitcast_convert_type(bias, I32)
            return carry

        lax.fori_loop(0, n_at, bias_step, 0)

    pl.when(any_tie)(lambda: bias_loop(True))
    pl.when(jnp.logical_not(any_tie))(lambda: bias_loop(False))

    gq = gq_ref[...]
    q_all = q_ref[...]
    for h in range(N_HEADS):
        sl = slice(h * HEAD_DIM, (h + 1) * HEAD_DIM)
        qh = q_all[:, sl]
        qh = qh * lax.rsqrt(jnp.mean(qh * qh, axis=-1, keepdims=True) + EPS) * gq * (HEAD_DIM ** -0.5 * LOG2_E)
        qt_ref[sl, :] = qh.T.astype(BF16)
    acc_ref[...] = jnp.zeros(acc_ref.shape, F32)

    heads = [slice(h * HEAD_DIM, (h + 1) * HEAD_DIM) for h in range(N_HEADS)]
    kt_h = KT_A // 2
    n_sub_h = kt_h // KT_I

    def issue_scores(kt, half):
        r0 = pl.multiple_of(kt * KT_A + half * kt_h, kt_h)
        for h, sl in enumerate(heads):
            lg_ref[half, h] = _dot(k_ref[0, pl.ds(r0, kt_h), sl], qt_ref[sl, :])

    def issue_values(kt, half):
        return [_dot(vt_ref[0, kt, sl, half * kt_h:(half + 1) * kt_h], p_ref[half, h])
                for h, sl in enumerate(heads)]

    def softmax_half(kt, half, ms, ls):
        r0 = pl.multiple_of(kt * KT_A + half * kt_h, kt_h)
        new_ms, new_ls, alphas = [], [], []
        for h in range(N_HEADS):
            mx = None
            for s in range(n_sub_h):
                rs = slice(s * KT_I, (s + 1) * KT_I)
                bias = lax.bitcast_convert_type(key_ref[pl.ds(r0 + s * KT_I, KT_I), :], F32)
                lg = lg_ref[half, h, rs, :] + bias
                lg_ref[half, h, rs, :] = lg
                part = jnp.max(lg.reshape(KT_I // 8, 8, tq), axis=0)
                mx = part if mx is None else jnp.maximum(mx, part)
            m_new = jnp.maximum(ms[h], jnp.max(mx, axis=0, keepdims=True))
            alphas.append(jnp.exp2(ms[h] - m_new))
            new_ms.append(m_new)
        for h in range(N_HEADS):
            lsum = None
            for s in range(n_sub_h):
                rs = slice(s * KT_I, (s + 1) * KT_I)
                p = jnp.exp2(lg_ref[half, h, rs, :] - new_ms[h])
                p_ref[half, h, rs, :] = p.astype(BF16)
                part = jnp.sum(p.reshape(KT_I // 8, 8, tq), axis=0)
                lsum = part if lsum is None else lsum + part
            new_ls.append(alphas[h] * ls[h] + jnp.sum(lsum, axis=0, keepdims=True))
        return new_ms, new_ls, alphas

    def accumulate(alphas, pv):
        for h in range(N_HEADS):
            acc_ref[h] = alphas[h] * acc_ref[h] + pv[h]

    def att_step(kt, carry):
        ms, ls = carry
        issue_scores(kt, 0)
        issue_scores(kt, 1)
        ms, ls, alphas_a = softmax_half(kt, 0, ms, ls)
        pv_a = issue_values(kt, 0)
        ms, ls, alphas_b = softmax_half(kt, 1, ms, ls)
        accumulate(alphas_a, pv_a)
        accumulate(alphas_b, issue_values(kt, 1))
        return tuple(ms), tuple(ls)

    m0 = tuple(jnp.full((1, tq), NEG_BIG, F32) for _ in range(N_HEADS))
    l0 = tuple(jnp.zeros((1, tq), F32) for _ in range(N_HEADS))
    _, ls = lax.fori_loop(0, n_at, att_step, (m0, l0))
    z_all = z_ref[...]
    for h in range(N_HEADS):
        sl = slice(h * HEAD_DIM, (h + 1) * HEAD_DIM)
        oh = (acc_ref[h] / ls[h]).T
        o_ref[:, sl] = (oh * _silu(z_all[:, sl])).astype(o_ref.dtype)


def _dsa(proj, small, k_all16, vt16, ki16, gq_row, batch, q_len, tq, offset, s_valid):
    n = batch * q_len
    nq = q_len // tq
    s_pad = k_all16.shape[1]
    topk = min(TOPK_MAX, s_valid // 4)
    gw = GROUP_WIDTH
    qi_w = IDX_HEADS * IDX_DIM
    kern = functools.partial(_dsa_kernel, tq=tq, offset=offset, s_valid=s_valid, topk=topk)
    once = pl.Buffered(1)
    return pl.pallas_call(
        kern,
        grid=(batch, nq),
        in_specs=[
            pl.BlockSpec((tq, gw), lambda b, i: (b * nq + i, COL_QC // gw)),
            pl.BlockSpec((tq, qi_w), lambda b, i: (b * nq + i, COL_QI // qi_w)),
            pl.BlockSpec((tq, gw), lambda b, i: (b * nq + i, COL_ZC // gw)),
            pl.BlockSpec((tq, SMALL_COLS), lambda b, i: (b * nq + i, 0)),
            pl.BlockSpec((1, s_pad, gw), lambda b, i: (b, 0, 0), pipeline_mode=once),
            pl.BlockSpec((1, s_pad // KT_A, gw, KT_A), lambda b, i: (b, 0, 0, 0), pipeline_mode=once),
            pl.BlockSpec((1, s_pad, IDX_DIM), lambda b, i: (b, 0, 0), pipeline_mode=once),
            pl.BlockSpec((1, HEAD_DIM), lambda b, i: (0, 0)),
        ],
        out_specs=pl.BlockSpec((tq, gw), lambda b, i: (b * nq + i, 0)),
        out_shape=jax.ShapeDtypeStruct((n, gw), BF16),
        scratch_shapes=[pltpu.VMEM((s_pad, tq), I32),
                        pltpu.VMEM((qi_w, tq), BF16),
                        pltpu.VMEM((SMALL_COLS, tq), F32),
                        pltpu.VMEM((8, tq), I32),
                        pltpu.VMEM((gw, tq), BF16),
                        pltpu.VMEM((N_HEADS, HEAD_DIM, tq), F32),
                        pltpu.VMEM((2, N_HEADS, KT_A // 2, tq), F32),
                        pltpu.VMEM((2, N_HEADS, KT_A // 2, tq), BF16),
                        pltpu.VMEM((s_pad, tq), jnp.int16),
                        pltpu.VMEM((s_pad, tq), jnp.int16)],
        compiler_params=_cparams(("parallel", "arbitrary")),
        name="dsa",
    )(proj, proj, proj, small, k_all16, vt16, ki16, gq_row)


def _dsa_select_kernel(qi_ref, sm_ref, pki_ref, nki_ref, bias_ref, ki_ref, key_ref, jl_ref, *,
                       lq, offset, s_valid, topk):
    s_pad = ki_ref.shape[0]
    n_t = s_pad // KT_A
    ki_ref[0:offset, :] = pki_ref[0].astype(BF16)
    ki_ref[offset:s_pad, :] = jnp.zeros((s_pad - offset, IDX_DIM), BF16)
    ki_ref[offset:offset + lq, :] = nki_ref[...]
    qi = qi_ref[...]
    q2 = jnp.concatenate([qi[:, h * IDX_DIM:(h + 1) * IDX_DIM] for h in range(IDX_HEADS)], axis=0).astype(BF16)
    w = sm_ref[...] * (IDX_HEADS ** -0.5 * IDX_DIM ** -0.5)
    w_cols = [w[:, LANE_WI + h:LANE_WI + h + 1] for h in range(IDX_HEADS)]
    t_chunk = (offset + lax.broadcasted_iota(I32, (lq, 1), 0)) // CHUNK

    def index_step(j, carry):
        c0 = pl.multiple_of(j * KT_A, KT_A)
        sc = _dot_nt(q2, ki_ref[pl.ds(c0, KT_A), :])
        acc = jnp.zeros((lq, KT_A), F32)
        for h in range(IDX_HEADS):
            acc = acc + w_cols[h] * jnp.maximum(sc[h * lq:(h + 1) * lq, :], 0.0)
        bits = lax.bitcast_convert_type(acc, I32)
        key = bits ^ ((bits >> 31) & 0x7FFFFFFF)
        s_pos = c0 + lax.broadcasted_iota(I32, (1, KT_A), 1)
        adm = (s_pos // CHUNK <= t_chunk) & (s_pos < s_valid)
        key_ref[:, pl.ds(c0, KT_A)] = jnp.where(adm, key, INT_MIN)
        return carry

    lax.fori_loop(0, n_t, index_step, 0)

    def count(pred_fn):
        def body(j, acc):
            c0 = pl.multiple_of(j * KT_A, KT_A)
            s_pos = c0 + lax.broadcasted_iota(I32, (1, KT_A), 1)
            m = jnp.where(pred_fn(key_ref[:, pl.ds(c0, KT_A)], s_pos), 1, 0).astype(I32)
            for c in range(KT_A // 128):
                acc = acc + m[:, c * 128:(c + 1) * 128]
            return acc
        acc = lax.fori_loop(0, n_t, body, jnp.zeros((lq, 128), I32))
        return jnp.sum(acc, axis=1, keepdims=True)

    def bit_step(b, v):
        cand_u = v | jnp.left_shift(jnp.int32(1), 31 - b)
        cnt = count(lambda key, s_pos: key >= (cand_u ^ INT_MIN))
        return jnp.where(cnt >= topk, cand_u, v)

    v_s = lax.fori_loop(0, 32, bit_step, jnp.zeros((lq, 1), I32)) ^ INT_MIN
    n_gt = count(lambda key, s_pos: key > v_s)
    n_ge = count(lambda key, s_pos: key >= v_s)
    need = topk - n_gt
    tied = (n_ge != topk) & (v_s != INT_MIN)
    jl_ref[...] = jnp.full(jl_ref.shape, s_pad, I32)

    @pl.when(jnp.max(jnp.where(tied, 1.0, 0.0)) > 0.0)
    def _():
        pos_bits = max(int(math.ceil(math.log2(max(s_pad, 2)))), 1) + 1

        def pos_step(b, jv):
            cand = jv | jnp.left_shift(jnp.int32(1), pos_bits - 1 - b)
            cnt = count(lambda key, s_pos: (key == v_s) & (s_pos < cand))
            return jnp.where(cnt <= need, cand, jv)

        jl_ref[:, 0:1] = lax.fori_loop(0, pos_bits, pos_step, jnp.zeros((lq, 1), I32))

    j_lim = jnp.where(tied, jl_ref[:, 0:1], s_pad)

    def bias_step(j, carry):
        c0 = pl.multiple_of(j * KT_A, KT_A)
        key = key_ref[:, pl.ds(c0, KT_A)]
        s_pos = c0 + lax.broadcasted_iota(I32, (1, KT_A), 1)
        sel = (key > v_s) | ((key == v_s) & (s_pos < j_lim) & (key != INT_MIN))
        bias_ref[0, :, pl.ds(c0, KT_A)] = jnp.where(sel, 0.0, NEG_BIG).astype(F32)
        return carry

    lax.fori_loop(0, n_t, bias_step, 0)


def _dsa_select(proj, small, past_ki, ki16, batch, lq, offset):
    s_valid = offset + lq
    s_pad = _round_up(s_valid, KT_A)
    topk = min(TOPK_MAX, s_valid // 4)
    qi_w = IDX_HEADS * IDX_DIM
    kern = functools.partial(_dsa_select_kernel, lq=lq, offset=offset, s_valid=s_valid, topk=topk)
    return pl.pallas_call(
        kern,
        grid=(batch,),
        in_specs=[
            pl.BlockSpec((lq, qi_w), lambda b: (b, COL_QI // qi_w)),
            pl.BlockSpec((lq, SMALL_COLS), lambda b: (b, 0)),
            pl.BlockSpec((1, offset, IDX_DIM), lambda b: (b, 0, 0)),
            pl.BlockSpec((lq, IDX_DIM), lambda b: (b, 0)),
        ],
        out_specs=pl.BlockSpec((1, lq, s_pad), lambda b: (b, 0, 0)),
        out_shape=jax.ShapeDtypeStruct((batch, lq, s_pad), F32),
        scratch_shapes=[pltpu.VMEM((s_pad, IDX_DIM), BF16), pltpu.VMEM((lq, s_pad), I32),
                        pltpu.VMEM((lq, 128), I32)],
        compiler_params=_cparams(("parallel",)),
        name="dsa_select",
    )(proj, small, past_ki, ki16)


def _dsa_decode_kernel(q_ref, z_ref, bias_ref, pk_ref, pv_ref, nk_ref, nv_ref, gq_ref, o_ref,
                       qn_ref, m_ref, l_ref, acc_ref, *, lq, n_cache_tiles):
    j = pl.program_id(1)

    @pl.when(j == 0)
    def _():
        q_all = q_ref[...]
        gq = gq_ref[...]
        for h in range(N_HEADS):
            qh = q_all[:, h * HEAD_DIM:(h + 1) * HEAD_DIM]
            qh = qh * lax.rsqrt(jnp.mean(qh * qh, axis=-1, keepdims=True) + EPS) * gq * (HEAD_DIM ** -0.5 * LOG2_E)
            qn_ref[h] = qh.astype(BF16)
        m_ref[...] = jnp.full(m_ref.shape, NEG_BIG, F32)
        l_ref[...] = jnp.zeros(l_ref.shape, F32)
        acc_ref[...] = jnp.zeros(acc_ref.shape, F32)

    def attend(k_heads, v_heads, bias):
        for h in range(N_HEADS):
            logit = _dot_nt(qn_ref[h], k_heads[h]) + bias
            m_old = m_ref[h]
            m_new = jnp.maximum(m_old, jnp.max(logit, axis=1, keepdims=True))
            alpha = jnp.exp2(m_old - m_new)
            p = jnp.exp2(logit - m_new[:, 0:1])
            l_ref[h] = alpha * l_ref[h] + jnp.sum(p, axis=1, keepdims=True)
            acc_ref[h] = alpha * acc_ref[h] + _dot(p.astype(BF16), v_heads[h])
            m_ref[h] = m_new

    @pl.when(j < n_cache_tiles)
    def _():
        attend([pk_ref[0, :, h, :].astype(BF16) for h in range(N_HEADS)],
               [pv_ref[0, :, h, :].astype(BF16) for h in range(N_HEADS)], bias_ref[0])

    @pl.when(j == n_cache_tiles)
    def _():
        pad = jnp.zeros((128 - lq, HEAD_DIM), BF16)
        nk = nk_ref[...]
        attend([jnp.concatenate([nk[:, h * HEAD_DIM:(h + 1) * HEAD_DIM], pad], axis=0) for h in range(N_HEADS)],
               [jnp.concatenate([nv_ref[:, h, :].astype(BF16), pad], axis=0) for h in range(N_HEADS)],
               bias_ref[0][:, 0:128])
        z_all = z_ref[...]
        for h in range(N_HEADS):
            sl = slice(h * HEAD_DIM, (h + 1) * HEAD_DIM)
            o_ref[:, sl] = (acc_ref[h] / l_ref[h] * _silu(z_all[:, sl])).astype(o_ref.dtype)


def _dsa_decode(proj, bias, past_k, past_v, k16, v_new, gq_row, batch, lq, offset, layer):
    gw = GROUP_WIDTH
    n_cache_tiles = offset // KT_A
    kern = functools.partial(_dsa_decode_kernel, lq=lq, n_cache_tiles=n_cache_tiles)
    last = n_cache_tiles - 1
    return pl.pallas_call(
        kern,
        grid=(batch, n_cache_tiles + 1),
        in_specs=[
            pl.BlockSpec((lq, gw), lambda b, j: (b, COL_QC // gw)),
            pl.BlockSpec((lq, gw), lambda b, j: (b, COL_ZC // gw)),
            pl.BlockSpec((1, lq, KT_A), lambda b, j: (b, 0, j)),
            pl.BlockSpec((1, KT_A, N_HEADS, HEAD_DIM), lambda b, j: (b, jnp.minimum(j, last), 0, 0)),
            pl.BlockSpec((1, KT_A, N_HEADS, HEAD_DIM), lambda b, j: (b, jnp.minimum(j, last), 0, 0)),
            pl.BlockSpec((lq, gw), lambda b, j: (b, 0)),
            pl.BlockSpec((None, lq, N_HEADS, HEAD_DIM), lambda b, j: (layer, b, 0, 0)),
            pl.BlockSpec((1, HEAD_DIM), lambda b, j: (0, 0)),
        ],
        out_specs=pl.BlockSpec((lq, gw), lambda b, j: (b, 0)),
        out_shape=jax.ShapeDtypeStruct((batch * lq, gw), BF16),
        scratch_shapes=[pltpu.VMEM((N_HEADS, lq, HEAD_DIM), BF16),
                        pltpu.VMEM((N_HEADS, lq, HEAD_DIM), F32),
                        pltpu.VMEM((N_HEADS, lq, HEAD_DIM), F32),
                        pltpu.VMEM((N_HEADS, lq, HEAD_DIM), F32)],
        compiler_params=_cparams(("parallel", "arbitrary")),
        name="dsa_decode",
    )(proj, proj, bias, past_k, past_v, k16, v_new, gq_row)


def _mem_kernel(q_ref, z_ref, mk_ref, mv_ref, gq_ref, o_ref):
    gq = gq_ref[...]
    q_all, z_all = q_ref[...], z_ref[...]
    mk = mk_ref[0].astype(BF16)
    mv = mv_ref[0].astype(BF16)
    for h in range(N_HEADS):
        sl = slice(h * HEAD_DIM, (h + 1) * HEAD_DIM)
        qh = q_all[:, sl]
        qh = qh * lax.rsqrt(jnp.mean(qh * qh, axis=-1, keepdims=True) + EPS) * gq * (HEAD_DIM ** -0.5)
        logit = _dot_nt(qh.astype(BF16), mk[:, sl])
        m = jnp.max(logit, axis=-1, keepdims=True)
        p = jnp.exp(logit - m)
        l = jnp.sum(p, axis=-1, keepdims=True)
        oh = _dot(p.astype(BF16), mv[:, sl]) / l
        o_ref[:, sl] = (oh * _silu(z_all[:, sl])).astype(o_ref.dtype)


def _mem_attend(proj, mk, mv, gq_row, batch, seqlen):
    n = batch * seqlen
    tm = min(512, seqlen)
    nj = seqlen // tm
    gw = GROUP_WIDTH
    n_mem = mk.shape[1]
    return pl.pallas_call(
        _mem_kernel,
        grid=(batch, nj),
        in_specs=[
            pl.BlockSpec((tm, gw), lambda b, j: (b * nj + j, COL_QD // gw)),
            pl.BlockSpec((tm, gw), lambda b, j: (b * nj + j, COL_ZD // gw)),
            pl.BlockSpec((1, n_mem, gw), lambda b, j: (b, 0, 0)),
            pl.BlockSpec((1, n_mem, gw), lambda b, j: (b, 0, 0)),
            pl.BlockSpec((1, HEAD_DIM), lambda b, j: (0, 0)),
        ],
        out_specs=pl.BlockSpec((tm, gw), lambda b, j: (b * nj + j, 0)),
        out_shape=jax.ShapeDtypeStruct((n, gw), BF16),
        compiler_params=_cparams(("parallel", "arbitrary")),
        name="mem_attend",
    )(proj, proj, mk, mv, gq_row)


def _outproj_kernel(x_ref, a_ref, b_ref, c_ref, d_ref, w_ref, y_ref):
    gw = GROUP_WIDTH
    acc = x_ref[...] + _dot(a_ref[...], w_ref[0:gw, :])
    acc = acc + _dot(b_ref[...], w_ref[gw:2 * gw, :])
    acc = acc + _dot(c_ref[...], w_ref[2 * gw:3 * gw, :])
    acc = acc + _dot(d_ref[...], w_ref[3 * gw:4 * gw, :])
    y_ref[...] = acc


def _out_proj(x2d, oa, ob, oc, od, w16):
    n, d = x2d.shape
    tm = min(512, n)
    gw = GROUP_WIDTH
    grp = pl.BlockSpec((tm, gw), lambda i: (i, 0))
    return pl.pallas_call(
        _outproj_kernel,
        grid=(n // tm,),
        in_specs=[pl.BlockSpec((tm, d), lambda i: (i, 0)), grp, grp, grp, grp,
                  pl.BlockSpec((4 * gw, d), lambda i: (0, 0))],
        out_specs=pl.BlockSpec((tm, d), lambda i: (i, 0)),
        out_shape=jax.ShapeDtypeStruct((n, d), F32),
        compiler_params=_cparams(("parallel",)),
        name="out_proj",
    )(x2d, oa, ob, oc, od, w16)


_W_A_END = 4 * GROUP_WIDTH
_W_B_START = _W_A_END + 2 * N_HEADS
_W_B_END = _W_B_START + 8 * GROUP_WIDTH + IDX_HEADS * IDX_DIM
_W_D_START = _W_B_END + IDX_DIM + IDX_HEADS
_W_COLS = _W_D_START + 2 * GROUP_WIDTH


def _w_in_kernel(w_ref, m_ref, s_ref):
    m_ref[:, 0:_W_A_END] = w_ref[:, 0:_W_A_END].astype(BF16)
    m_ref[:, _W_A_END:_W_A_END + _W_B_END - _W_B_START] = w_ref[:, _W_B_START:_W_B_END].astype(BF16)
    m_ref[:, MAIN_COLS - 2 * GROUP_WIDTH:MAIN_COLS] = w_ref[:, _W_D_START:_W_COLS].astype(BF16)
    n_kw = IDX_DIM + IDX_HEADS
    s_ref[:, 0:n_kw] = w_ref[:, _W_B_END:_W_D_START].astype(BF16)
    s_ref[:, n_kw:n_kw + 2 * N_HEADS] = w_ref[:, _W_A_END:_W_B_START].astype(BF16)
    s_ref[:, n_kw + 2 * N_HEADS:] = jnp.zeros((w_ref.shape[0], SMALL_COLS - n_kw - 2 * N_HEADS), BF16)


def _prep_w_in(w_in):
    depth, d, cols = w_in.shape
    assert cols == _W_COLS and MAIN_COLS == _W_A_END + (_W_B_END - _W_B_START) + 2 * GROUP_WIDTH
    tm = 128
    return pl.pallas_call(
        _w_in_kernel,
        grid=(depth, d // tm),
        in_specs=[pl.BlockSpec((None, tm, cols), lambda l, i: (l, i, 0))],
        out_specs=[pl.BlockSpec((None, tm, MAIN_COLS), lambda l, i: (l, i, 0)),
                   pl.BlockSpec((None, tm, SMALL_COLS), lambda l, i: (l, i, 0))],
        out_shape=[jax.ShapeDtypeStruct((depth, d, MAIN_COLS), BF16),
                   jax.ShapeDtypeStruct((depth, d, SMALL_COLS), BF16)],
        compiler_params=_cparams(("parallel", "parallel")),
        name="w_in_prep",
    )(w_in)


def _lane_row(vals, lane0):
    row = jnp.zeros((1, SMALL_COLS), F32)
    return row.at[0, lane0:lane0 + vals.shape[0]].set(vals.astype(F32))


def _rope_tables(pos):
    half = HEAD_DIM // 2
    inv = ROPE_THETA ** (-jnp.arange(half, dtype=F32) / half)
    ang = pos.astype(F32)[:, None] * inv[None, :]
    cos, sin = jnp.cos(ang), jnp.sin(ang)
    return jnp.concatenate([cos, cos], axis=-1), jnp.concatenate([-sin, sin], axis=-1)


def _round_up(x, m):
    return (x + m - 1) // m * m


def _mixer_layer(x, conv_buf, s_gdn, s_ret, past_k, past_v, past_ki, mem_k, mem_v, wts, layer, depth, prev_cache):
    b, l, d = x.shape
    n = b * l
    offset = 0 if past_k is None else past_k.shape[1]
    x2d = x.reshape(n, d)
    proj, small = _in_proj(x2d, wts["norm_g"], wts["w_main"], wts["w_small"], layer)

    o_a, conv_new, s_gdn_new = _gdn(proj, small, wts["conv_w"], conv_buf, s_gdn, wts["alog_row"],
                                    wts["dtb_row"], wts["gdn_norm_g"], b, l)

    cos_t, sin_t = wts["rope_p"] if past_k is None else wts["rope_s"]
    o_b, s_ret_new = _retention(proj, cos_t, sin_t, s_ret, wts["ret_norm_g"], wts["ret_norm_b"], b, l)

    prefill = past_k is None
    prep = _dsa_prep(proj, small, wts["dsa_k_norm_g"], wts["idx_k_norm_g"], layer, depth, prev_cache, prefill)
    cache = tuple(prep[:3])
    if prefill:
        assert l % KT_A == 0, "prefill length must be a multiple of the key tile"
        o_c = _dsa(proj, small, prep[3].reshape(b, l, GROUP_WIDTH), prep[5].reshape(b, l // KT_A, GROUP_WIDTH, KT_A),
                   prep[4].reshape(b, l, IDX_DIM), wts["dsa_q_norm_g"], b, l, 256, 0, l)
    else:
        assert offset % KT_A == 0 and l % 16 == 0 and l <= 128, "decode step shape not supported"
        bias = _dsa_select(proj, small, past_ki, prep[4], b, l, offset)
        o_c = _dsa_decode(proj, bias, past_k, past_v, prep[3], cache[1], wts["dsa_q_norm_g"], b, l, offset, layer)

    o_d = _mem_attend(proj, mem_k, mem_v, wts["mem_q_norm_g"], b, l)

    y = _out_proj(x2d, o_a, o_b, o_c, o_d, wts["w_out"]).reshape(b, l, d)
    return y, (conv_new, s_gdn_new, s_ret_new), cache


def kernel(x_prompt, x_sample, cache_gdn_conv, state_gdn, state_ret, cache_dsa_k, cache_dsa_v, cache_idx_k, cache_mem_k, cache_mem_v, mem_prompt, norm_g, w_in, gdn_conv_w, gdn_a_log, gdn_dt_bias, gdn_norm_g, ret_norm_g, ret_norm_b, dsa_q_norm_g, dsa_k_norm_g, idx_k_norm_g, mem_norm_g, w_mem_kv, mem_q_norm_g, mem_k_norm_g, w_out):
    depth = w_in.shape[0]
    b = x_prompt.shape[0]
    n_mem = mem_prompt.shape[1]
    d = x_prompt.shape[-1]
    y_p, y_s = x_prompt, x_sample
    st_p, st_s, mem_p = [], [], []
    cache_p = cache_s = None
    w_main, w_small = _prep_w_in(w_in)
    rope_p = _rope_tables(jnp.arange(x_prompt.shape[1], dtype=I32))
    rope_s = _rope_tables(cache_dsa_k.shape[2] + jnp.arange(x_sample.shape[1], dtype=I32))
    for li in range(depth):
        wts = dict(
            norm_g=norm_g[li][None, :], w_main=w_main, w_small=w_small, rope_p=rope_p, rope_s=rope_s,
            conv_w=gdn_conv_w[li],
            alog_row=_lane_row(gdn_a_log[li], LANE_ALPHA), dtb_row=_lane_row(gdn_dt_bias[li], LANE_ALPHA),
            gdn_norm_g=gdn_norm_g[li][None, :], ret_norm_g=ret_norm_g[li][None, :],
            ret_norm_b=ret_norm_b[li][None, :], dsa_q_norm_g=dsa_q_norm_g[li][None, :],
            dsa_k_norm_g=dsa_k_norm_g[li][None, :], idx_k_norm_g=idx_k_norm_g[li][None, :],
            mem_q_norm_g=mem_q_norm_g[li][None, :], w_out=w_out[li].astype(BF16),
        )
        mk, mv = _memory_kv(mem_prompt.reshape(b * n_mem, d), mem_norm_g[li][None, :],
                            w_mem_kv[li].astype(BF16), mem_k_norm_g[li][None, :])
        mk = mk.reshape(b, n_mem, GROUP_WIDTH)
        mv = mv.reshape(b, n_mem, GROUP_WIDTH)
        conv0 = jnp.zeros((b, CONV_W - 1, 3 * GROUP_WIDTH), F32)
        s0 = jnp.zeros((b, N_HEADS, HEAD_DIM, HEAD_DIM), F32)
        y_p, sp, cache_p = _mixer_layer(y_p, conv0, s0, s0, None, None, None, mk, mv, wts, li, depth, cache_p)
        st_p.append(sp)
        mem_p.append((mk.reshape(b, n_mem, N_HEADS, HEAD_DIM), mv.reshape(b, n_mem, N_HEADS, HEAD_DIM)))
        bs = x_sample.shape[0]
        y_s, ss, cache_s = _mixer_layer(y_s, cache_gdn_conv[li], state_gdn[li], state_ret[li],
                                        cache_dsa_k[li], cache_dsa_v[li], cache_idx_k[li],
                                        cache_mem_k[li].reshape(bs, n_mem, GROUP_WIDTH),
                                        cache_mem_v[li].reshape(bs, n_mem, GROUP_WIDTH), wts, li, depth, cache_s)
        st_s.append(ss)

    def stack(lst, k):
        return jnp.stack([s[k] for s in lst])

    def caches(c, bb, ll):
        return (c[0].reshape(depth, bb, ll, N_HEADS, HEAD_DIM), c[1].reshape(depth, bb, ll, N_HEADS, HEAD_DIM),
                c[2].reshape(depth, bb, ll, IDX_DIM))

    return ((y_p, y_s, stack(st_p, 0), stack(st_p, 1), stack(st_p, 2))
            + caches(cache_p, b, x_prompt.shape[1])
            + (stack(mem_p, 0), stack(mem_p, 1), stack(st_s, 0), stack(st_s, 1), stack(st_s, 2))
            + caches(cache_s, x_sample.shape[0], x_sample.shape[1]))
```

```python
import functools
import math

import jax
import jax.numpy as jnp
from jax import lax
from jax.experimental import pallas as pl
from jax.experimental.pallas import tpu as pltpu

F32 = jnp.float32
BF16 = jnp.bfloat16
I32 = jnp.int32

HEAD_DIM = 128
N_HEADS = 4
GROUP_WIDTH = N_HEADS * HEAD_DIM
CHUNK = 64
CONV_W = 4
IDX_HEADS = 16
IDX_DIM = 64
TOPK_MAX = 256
ROPE_THETA = 10000.0
EPS = 1e-6

COL_QA, COL_KA, COL_VA, COL_ZA = 0, 512, 1024, 1536
COL_QB, COL_KB, COL_VB, COL_ZB = 2048, 2560, 3072, 3584
COL_QC, COL_KC, COL_VC, COL_ZC = 4096, 4608, 5120, 5632
COL_QI = 6144
COL_QD, COL_ZD = 7168, 7680
MAIN_COLS = 8192
LANE_KI, LANE_WI, LANE_BETA, LANE_ALPHA = 0, 64, 80, 84
SMALL_COLS = 128

INT_MIN = -2 ** 31
NEG_BIG = -1e30
LOG2_E = 1.4426950408889634
VMEM_LIMIT = 56 * 1024 * 1024
HI = lax.Precision.HIGHEST


def _cparams(sem):
    return pltpu.CompilerParams(dimension_semantics=sem, vmem_limit_bytes=VMEM_LIMIT)


def _dot(a, b):
    return jnp.dot(a, b, preferred_element_type=F32)


def _dot_nt(a, b):
    return lax.dot_general(a, b, (((1,), (1,)), ((), ())), preferred_element_type=F32)


def _dot3(a, b):
    a_hi = a.astype(BF16)
    b_hi = b.astype(BF16)
    a_lo = (a - a_hi.astype(F32)).astype(BF16)
    b_lo = (b - b_hi.astype(F32)).astype(BF16)
    return _dot(a_hi, b_hi) + (_dot(a_hi, b_lo) + _dot(a_lo, b_hi))


def _silu(x):
    return x * jax.nn.sigmoid(x)


def _inproj_kernel(x_ref, g_ref, w_ref, ws_ref, o_ref, os_ref, h_ref):
    @pl.when(pl.program_id(1) == 0)
    def _():
        x = x_ref[...]
        y = x * lax.rsqrt(jnp.mean(x * x, axis=-1, keepdims=True) + EPS) * g_ref[...]
        hb = y.astype(BF16)
        h_ref[...] = hb
        os_ref[...] = _dot(hb, ws_ref[...])

    o_ref[...] = _dot(h_ref[...], w_ref[...])


def _in_proj(x2d, g_row, w_main, w_small, layer):
    n, d = x2d.shape
    tm = min(1024, n)
    tn = 1024
    return pl.pallas_call(
        _inproj_kernel,
        grid=(n // tm, MAIN_COLS // tn),
        in_specs=[
            pl.BlockSpec((tm, d), lambda i, j: (i, 0)),
            pl.BlockSpec((1, d), lambda i, j: (0, 0)),
            pl.BlockSpec((None, d, tn), lambda i, j: (layer, 0, j)),
            pl.BlockSpec((None, d, SMALL_COLS), lambda i, j: (layer, 0, 0)),
        ],
        out_specs=[
            pl.BlockSpec((tm, tn), lambda i, j: (i, j)),
            pl.BlockSpec((tm, SMALL_COLS), lambda i, j: (i, 0)),
        ],
        out_shape=[jax.ShapeDtypeStruct((n, MAIN_COLS), F32),
                   jax.ShapeDtypeStruct((n, SMALL_COLS), F32)],
        scratch_shapes=[pltpu.VMEM((tm, d), BF16)],
        compiler_params=_cparams(("parallel", "arbitrary")),
        name="in_proj",
    )(x2d, g_row, w_main, w_small)


def _memkv_kernel(x_ref, g_ref, w_ref, gk_ref, mk_ref, mv_ref):
    x = x_ref[...]
    y = x * lax.rsqrt(jnp.mean(x * x, axis=-1, keepdims=True) + EPS) * g_ref[...]
    kv = _dot(y.astype(BF16), w_ref[...])
    gk = gk_ref[...]
    for h in range(N_HEADS):
        sl = slice(h * HEAD_DIM, (h + 1) * HEAD_DIM)
        kh = kv[:, sl]
        mk_ref[:, sl] = kh * lax.rsqrt(jnp.mean(kh * kh, axis=-1, keepdims=True) + EPS) * gk
    mv_ref[...] = kv[:, GROUP_WIDTH:]


def _memory_kv(mem2d, g_row, w_kv, gk_row):
    n, d = mem2d.shape
    tm = min(256, n)
    return pl.pallas_call(
        _memkv_kernel,
        grid=(n // tm,),
        in_specs=[
            pl.BlockSpec((tm, d), lambda i: (i, 0)),
            pl.BlockSpec((1, d), lambda i: (0, 0)),
            pl.BlockSpec((d, 2 * GROUP_WIDTH), lambda i: (0, 0)),
            pl.BlockSpec((1, HEAD_DIM), lambda i: (0, 0)),
        ],
        out_specs=[pl.BlockSpec((tm, GROUP_WIDTH), lambda i: (i, 0)),
                   pl.BlockSpec((tm, GROUP_WIDTH), lambda i: (i, 0))],
        out_shape=[jax.ShapeDtypeStruct((n, GROUP_WIDTH), F32),
                   jax.ShapeDtypeStruct((n, GROUP_WIDTH), F32)],
        compiler_params=_cparams(("parallel",)),
        name="memory_kv",
    )(mem2d, g_row, w_kv, gk_row)


def _gdn_kernel(qkv_ref, z_ref, sm_ref, cw_ref, cb_ref, s0_ref, alog_ref, dtb_ref, gn_ref,
                o_ref, conv_ref, st_ref, xbuf_ref, s_ref, *, t_blk, chunk):
    j = pl.program_id(1)
    nj = pl.num_programs(1)
    gw3 = 3 * GROUP_WIDTH

    @pl.when(j == 0)
    def _():
        xbuf_ref[0:8, :] = jnp.zeros((8, gw3), F32)
        xbuf_ref[5:8, :] = cb_ref[0]
        s_ref[...] = s0_ref[0]

    @pl.when(j > 0)
    def _():
        xbuf_ref[0:8, :] = xbuf_ref[t_blk:t_blk + 8, :]

    xbuf_ref[8:8 + t_blk, :] = qkv_ref[...]
    conv_ref[0] = xbuf_ref[t_blk + 5:t_blk + 8, :]

    cw = cw_ref[...]
    x_all = xbuf_ref[...]
    x_prev = pltpu.roll(x_all, 1, 0)
    u = x_all * cw[3:4, :] + x_prev * cw[2:3, :]
    v = x_all * cw[1:2, :] + x_prev * cw[0:1, :]
    y = _silu((u + pltpu.roll(v, 2, 0))[8:8 + t_blk, :])

    sm = sm_ref[...]
    lane = lax.broadcasted_iota(I32, sm.shape, 1)
    beta_all = jax.nn.sigmoid(sm)
    xs = sm + dtb_ref[...]
    softplus = jnp.maximum(xs, 0.0) + jnp.log1p(jnp.exp(-jnp.abs(xs)))
    g_all = -jnp.exp(alog_ref[...]) * softplus
    g_all = jnp.where((lane >= LANE_ALPHA) & (lane < LANE_ALPHA + N_HEADS), g_all, 0.0)
    ri = lax.broadcasted_iota(I32, (t_blk, t_blk), 0)
    ci = lax.broadcasted_iota(I32, (t_blk, t_blk), 1)
    tri = jnp.where((ri // chunk == ci // chunk) & (ci <= ri), 1.0, 0.0).astype(F32)
    gcum = jnp.dot(tri, g_all, preferred_element_type=F32, precision=HI)
    gcum_t = gcum.T

    blk = min(2 * chunk, t_blk)
    n_blk = t_blk // blk
    cpb = blk // chunk
    rb = lax.broadcasted_iota(I32, (blk, blk), 0)
    cb = lax.broadcasted_iota(I32, (blk, blk), 1)
    same = (rb // chunk) == (cb // chunk)
    causal = same & (cb <= rb)
    strict = same & (cb < rb)
    eye = jnp.where(rb == cb, 1.0, 0.0).astype(F32)
    n_dbl = max(int(math.log2(chunk)) - 1, 0)
    gn = gn_ref[...]
    z_all = z_ref[...]

    units = [(h, b) for h in range(N_HEADS) for b in range(n_blk)]
    qs, ks, gcols, p_mats, x_mats, qk_mats, vb_mats, kg_mats = {}, {}, {}, {}, {}, {}, {}, {}
    for h in range(N_HEADS):
        qh = y[:, COL_QA + h * HEAD_DIM:COL_QA + (h + 1) * HEAD_DIM]
        kh = y[:, COL_KA + h * HEAD_DIM:COL_KA + (h + 1) * HEAD_DIM]
        vh = y[:, COL_VA + h * HEAD_DIM:COL_VA + (h + 1) * HEAD_DIM]
        qh = qh * lax.rsqrt(jnp.sum(qh * qh, axis=-1, keepdims=True) + EPS) * (HEAD_DIM ** -0.5)
        kh = kh * lax.rsqrt(jnp.sum(kh * kh, axis=-1, keepdims=True) + EPS)
        for b in range(n_blk):
            rs = slice(b * blk, (b + 1) * blk)
            gcol = gcum[rs, LANE_ALPHA + h:LANE_ALPHA + h + 1]
            grow = gcum_t[LANE_ALPHA + h:LANE_ALPHA + h + 1, rs]
            bcol = beta_all[rs, LANE_BETA + h:LANE_BETA + h + 1]
            decay = jnp.where(causal, jnp.exp(jnp.where(causal, gcol - grow, 0.0)), 0.0)
            qc, kc, vc = qh[rs], kh[rs], vh[rs]
            kb = kc * bcol
            kc16 = kc.astype(BF16)
            a_mat = jnp.where(strict, _dot_nt(kb.astype(BF16), kc16) * decay, 0.0)
            qk_mats[h, b] = jnp.where(causal, _dot_nt(qc.astype(BF16), kc16) * decay, 0.0).astype(BF16)
            p_mats[h, b] = -a_mat
            x_mats[h, b] = eye - a_mat
            vb_mats[h, b] = (vc * bcol).astype(BF16)
            kg_mats[h, b] = (kb * jnp.exp(gcol)).astype(BF16)
            qs[h, b], ks[h, b], gcols[h, b] = qc, kc, gcol

    for _ in range(n_dbl):
        for u_ in units:
            p_mats[u_] = _dot3(p_mats[u_], p_mats[u_])
        for u_ in units:
            x_mats[u_] = x_mats[u_] + _dot3(x_mats[u_], p_mats[u_])

    u_mats, w_mats = {}, {}
    for u_ in units:
        x16 = x_mats[u_].astype(BF16)
        u_mats[u_] = _dot(x16, vb_mats[u_])
        w_mats[u_] = _dot(x16, kg_mats[u_]).astype(BF16)

    states = [s_ref[h] for h in range(N_HEADS)]
    for b in range(n_blk):
        o_state = {h: [] for h in range(N_HEADS)}
        v_new = {h: [] for h in range(N_HEADS)}
        for c in range(cpb):
            cs = slice(c * chunk, (c + 1) * chunk)
            for h in range(N_HEADS):
                s = states[h]
                s16 = s.astype(BF16)
                gcol = gcols[h, b][cs]
                glast = gcols[h, b][(c + 1) * chunk - 1:(c + 1) * chunk]
                vn = u_mats[h, b][cs] - _dot(w_mats[h, b][cs], s16)
                vn16 = vn.astype(BF16)
                kd_t = (ks[h, b][cs] * jnp.exp(glast - gcol)).T.astype(BF16)
                states[h] = s * jnp.exp(glast) + _dot(kd_t, vn16)
                o_state[h].append(_dot((qs[h, b][cs] * jnp.exp(gcol)).astype(BF16), s16))
                v_new[h].append(vn16)
        for h in range(N_HEADS):
            sl = slice(h * HEAD_DIM, (h + 1) * HEAD_DIM)
            rs = slice(b * blk, (b + 1) * blk)
            o = jnp.concatenate(o_state[h], axis=0) + _dot(qk_mats[h, b], jnp.concatenate(v_new[h], axis=0))
            on = o * lax.rsqrt(jnp.mean(o * o, axis=-1, keepdims=True) + EPS) * gn
            o_ref[rs, sl] = (on * _silu(z_all[rs, sl])).astype(o_ref.dtype)
    for h in range(N_HEADS):
        s_ref[h] = states[h]

    @pl.when(j == nj - 1)
    def _():
        st_ref[0] = s_ref[...]


def _gdn(proj, small, conv_w, conv_buf, s0, alog_row, dtb_row, gn_row, batch, seqlen):
    n = batch * seqlen
    t_blk = min(256, seqlen)
    chunk = min(CHUNK, seqlen)
    nj = seqlen // t_blk
    gw3 = 3 * GROUP_WIDTH
    kern = functools.partial(_gdn_kernel, t_blk=t_blk, chunk=chunk)
    return pl.pallas_call(
        kern,
        grid=(batch, nj),
        in_specs=[
            pl.BlockSpec((t_blk, gw3), lambda b, j: (b * nj + j, COL_QA // gw3)),
            pl.BlockSpec((t_blk, GROUP_WIDTH), lambda b, j: (b * nj + j, COL_ZA // GROUP_WIDTH)),
            pl.BlockSpec((t_blk, SMALL_COLS), lambda b, j: (b * nj + j, 0)),
            pl.BlockSpec((CONV_W, gw3), lambda b, j: (0, 0)),
            pl.BlockSpec((1, CONV_W - 1, gw3), lambda b, j: (b, 0, 0)),
            pl.BlockSpec((1, N_HEADS, HEAD_DIM, HEAD_DIM), lambda b, j: (b, 0, 0, 0)),
            pl.BlockSpec((1, SMALL_COLS), lambda b, j: (0, 0)),
            pl.BlockSpec((1, SMALL_COLS), lambda b, j: (0, 0)),
            pl.BlockSpec((1, HEAD_DIM), lambda b, j: (0, 0)),
        ],
        out_specs=[
            pl.BlockSpec((t_blk, GROUP_WIDTH), lambda b, j: (b * nj + j, 0)),
            pl.BlockSpec((1, CONV_W - 1, gw3), lambda b, j: (b, 0, 0)),
            pl.BlockSpec((1, N_HEADS, HEAD_DIM, HEAD_DIM), lambda b, j: (b, 0, 0, 0)),
        ],
        out_shape=[
            jax.ShapeDtypeStruct((n, GROUP_WIDTH), BF16),
            jax.ShapeDtypeStruct((batch, CONV_W - 1, gw3), F32),
            jax.ShapeDtypeStruct((batch, N_HEADS, HEAD_DIM, HEAD_DIM), F32),
        ],
        scratch_shapes=[pltpu.VMEM((t_blk + 8, gw3), F32),
                        pltpu.VMEM((N_HEADS, HEAD_DIM, HEAD_DIM), F32)],
        compiler_params=_cparams(("parallel", "arbitrary")),
        name="gdn",
    )(proj, proj, small, conv_w, conv_buf, s0, alog_row, dtb_row, gn_row)


def _ret_kernel(q_ref, k_ref, v_ref, z_ref, cos_ref, sin_ref, s0_ref, g_ref, b_ref,
                o_ref, st_ref, s_ref, *, t_blk):
    j = pl.program_id(1)
    nj = pl.num_programs(1)

    @pl.when(j == 0)
    def _():
        s_ref[...] = s0_ref[0]

    cos = cos_ref[...]
    sin = sin_ref[...]
    ri = lax.broadcasted_iota(I32, (t_blk, t_blk), 0)
    ci = lax.broadcasted_iota(I32, (t_blk, t_blk), 1)
    causal = ci <= ri
    rel = jnp.where(causal, ri - ci, 0).astype(F32)
    idx_col = lax.broadcasted_iota(I32, (t_blk, 1), 0).astype(F32)
    gamma_g = g_ref[...]
    gamma_b = b_ref[...]
    q_all, k_all, v_all, z_all = q_ref[...], k_ref[...], v_ref[...], z_ref[...]

    for h in range(N_HEADS):
        sl = slice(h * HEAD_DIM, (h + 1) * HEAD_DIM)
        lg = math.log(1.0 - 2.0 ** (-5.0 - h))
        qh, kh, vh = q_all[:, sl], k_all[:, sl], v_all[:, sl]
        qh = qh * cos + pltpu.roll(qh, HEAD_DIM // 2, 1) * sin
        kh = (kh * cos + pltpu.roll(kh, HEAD_DIM // 2, 1) * sin) * (HEAD_DIM ** -0.5)
        d_mat = jnp.where(causal, jnp.exp(lg * rel), 0.0)
        q16, k16, v16 = qh.astype(BF16), kh.astype(BF16), vh.astype(BF16)
        o_intra = _dot((_dot_nt(q16, k16) * d_mat).astype(BF16), v16)
        s = s_ref[h]
        o_cross = _dot(q16, s.astype(BF16)) * jnp.exp(lg * (idx_col + 1.0))
        kd_t = (kh * jnp.exp(lg * (t_blk - 1.0 - idx_col))).T.astype(BF16)
        s_ref[h] = s * math.exp(lg * t_blk) + _dot(kd_t, v16)
        o = o_intra + o_cross
        mu = jnp.mean(o, axis=-1, keepdims=True)
        oc = o - mu
        var = jnp.mean(oc * oc, axis=-1, keepdims=True)
        on = oc * lax.rsqrt(var + EPS) * gamma_g + gamma_b
        o_ref[:, sl] = (on * _silu(z_all[:, sl])).astype(o_ref.dtype)

    @pl.when(j == nj - 1)
    def _():
        st_ref[0] = s_ref[...]


def _retention(proj, cos_t, sin_t, s0, g_row, b_row, batch, seqlen):
    n = batch * seqlen
    t_blk = min(256, seqlen)
    nj = seqlen // t_blk
    gw = GROUP_WIDTH
    kern = functools.partial(_ret_kernel, t_blk=t_blk)

    def col(c):
        return pl.BlockSpec((t_blk, gw), lambda b, j: (b * nj + j, c // gw))

    return pl.pallas_call(
        kern,
        grid=(batch, nj),
        in_specs=[
            col(COL_QB), col(COL_KB), col(COL_VB), col(COL_ZB),
            pl.BlockSpec((t_blk, HEAD_DIM), lambda b, j: (j, 0)),
            pl.BlockSpec((t_blk, HEAD_DIM), lambda b, j: (j, 0)),
            pl.BlockSpec((1, N_HEADS, HEAD_DIM, HEAD_DIM), lambda b, j: (b, 0, 0, 0)),
            pl.BlockSpec((1, HEAD_DIM), lambda b, j: (0, 0)),
            pl.BlockSpec((1, HEAD_DIM), lambda b, j: (0, 0)),
        ],
        out_specs=[
            pl.BlockSpec((t_blk, gw), lambda b, j: (b * nj + j, 0)),
            pl.BlockSpec((1, N_HEADS, HEAD_DIM, HEAD_DIM), lambda b, j: (b, 0, 0, 0)),
        ],
        out_shape=[
            jax.ShapeDtypeStruct((n, gw), BF16),
            jax.ShapeDtypeStruct((batch, N_HEADS, HEAD_DIM, HEAD_DIM), F32),
        ],
        scratch_shapes=[pltpu.VMEM((N_HEADS, HEAD_DIM, HEAD_DIM), F32)],
        compiler_params=_cparams(("parallel", "arbitrary")),
        name="retention",
    )(proj, proj, proj, proj, cos_t, sin_t, s0, g_row, b_row)


def _dsa_prep_kernel(k_ref, v_ref, sm_ref, gk_ref, gi_ref, *rest, n_prev, with_vt):
    outs = rest[n_prev:]
    ko_ref, vo_ref, kio_ref, k16_ref, ki16_ref = outs[:5]
    k = k_ref[...]
    tm = k.shape[0]
    gk = gk_ref[...]
    for h in range(N_HEADS):
        sl = slice(h * HEAD_DIM, (h + 1) * HEAD_DIM)
        kh = k[:, sl]
        kn = kh * lax.rsqrt(jnp.mean(kh * kh, axis=-1, keepdims=True) + EPS) * gk
        ko_ref[pl.ds(h, tm, stride=N_HEADS), :] = kn
        k16_ref[:, sl] = kn.astype(BF16)
    v = v_ref[...]
    for h in range(N_HEADS):
        vo_ref[pl.ds(h, tm, stride=N_HEADS), :] = v[:, h * HEAD_DIM:(h + 1) * HEAD_DIM]
    if with_vt:
        outs[5][...] = v.T.astype(BF16)
    ki = sm_ref[...][:, LANE_KI:LANE_KI + IDX_DIM]
    kin = ki * lax.rsqrt(jnp.mean(ki * ki, axis=-1, keepdims=True) + EPS) * gi_ref[...]
    kio_ref[...] = kin
    ki16_ref[...] = kin.astype(BF16)


def _dsa_prep(proj, small, gk_row, gi_row, layer, depth, prev, with_vt):
    n = proj.shape[0]
    tm = min(KT_A, n)
    gw = GROUP_WIDTH
    n_prev = 0 if prev is None else 3
    kern = functools.partial(_dsa_prep_kernel, n_prev=n_prev, with_vt=with_vt)
    in_specs = [
        pl.BlockSpec((tm, gw), lambda i: (i, COL_KC // gw)),
        pl.BlockSpec((tm, gw), lambda i: (i, COL_VC // gw)),
        pl.BlockSpec((tm, SMALL_COLS), lambda i: (i, 0)),
        pl.BlockSpec((1, HEAD_DIM), lambda i: (0, 0)),
        pl.BlockSpec((1, IDX_DIM), lambda i: (0, 0)),
    ] + [pl.BlockSpec(memory_space=pl.ANY)] * n_prev
    out_specs = [pl.BlockSpec((None, tm * N_HEADS, HEAD_DIM), lambda i: (layer, i, 0)),
                 pl.BlockSpec((None, tm * N_HEADS, HEAD_DIM), lambda i: (layer, i, 0)),
                 pl.BlockSpec((None, tm, IDX_DIM), lambda i: (layer, i, 0)),
                 pl.BlockSpec((tm, gw), lambda i: (i, 0)),
                 pl.BlockSpec((tm, IDX_DIM), lambda i: (i, 0))]
    out_shape = [jax.ShapeDtypeStruct((depth, n * N_HEADS, HEAD_DIM), F32),
                 jax.ShapeDtypeStruct((depth, n * N_HEADS, HEAD_DIM), F32),
                 jax.ShapeDtypeStruct((depth, n, IDX_DIM), F32),
                 jax.ShapeDtypeStruct((n, gw), BF16),
                 jax.ShapeDtypeStruct((n, IDX_DIM), BF16)]
    if with_vt:
        out_specs.append(pl.BlockSpec((None, gw, tm), lambda i: (i, 0, 0)))
        out_shape.append(jax.ShapeDtypeStruct((n // tm, gw, tm), BF16))
    args = (proj, proj, small, gk_row, gi_row) + (() if prev is None else tuple(prev))
    return pl.pallas_call(
        kern,
        grid=(n // tm,),
        in_specs=in_specs,
        out_specs=out_specs,
        out_shape=out_shape,
        input_output_aliases={5 + t: t for t in range(n_prev)},
        compiler_params=_cparams(("parallel",)),
        name="dsa_prep",
    )(*args)


KT_I = 128
KT_A = 512


def _dsa_kernel(q_ref, qi_ref, z_ref, sm_ref, k_ref, vt_ref, ki_ref, gq_ref, o_ref,
                key_ref, qit_ref, wt_ref, jlim_ref, qt_ref, acc_ref, lg_ref, p_ref, key16_ref, low16_ref, *, tq, offset, s_valid, topk):
    i = pl.program_id(1)
    pos0 = offset + i * tq
    t_pos = pos0 + lax.broadcasted_iota(I32, (1, tq), 1)
    t_chunk = t_pos // CHUNK
    n_adm_row = jnp.minimum((t_chunk + 1) * CHUNK, s_valid)
    n_keys = jnp.minimum(((pos0 + tq - 1) // CHUNK + 1) * CHUNK, s_valid)
    n_at = (n_keys + KT_A - 1) // KT_A
    n_sub = KT_A // KT_I
    n_it_full = n_at * n_sub

    qit_ref[...] = qi_ref[...].T.astype(BF16)
    wt_ref[...] = sm_ref[...].T * (IDX_HEADS ** -0.5 * IDX_DIM ** -0.5)

    def index_step(kt, carry):
        r0 = pl.multiple_of(kt * KT_I, KT_I)
        ki_t = ki_ref[0, pl.ds(r0, KT_I), :]
        acc = jnp.zeros((KT_I, tq), F32)
        for h in range(IDX_HEADS):
            sc = _dot(ki_t, qit_ref[h * IDX_DIM:(h + 1) * IDX_DIM, :])
            acc = acc + wt_ref[LANE_WI + h:LANE_WI + h + 1, :] * jnp.maximum(sc, 0.0)
        bits = lax.bitcast_convert_type(acc, I32)
        key = bits ^ ((bits >> 31) & 0x7FFFFFFF)
        s_pos = r0 + lax.broadcasted_iota(I32, (KT_I, 1), 0)
        adm = (s_pos // CHUNK <= t_chunk) & (s_pos < s_valid)
        key = jnp.where(adm, key, INT_MIN)
        key_ref[pl.ds(r0, KT_I), :] = key
        key16_ref[pl.ds(r0, KT_I), :] = (key >> 16).astype(jnp.int16)
        return carry

    lax.fori_loop(0, n_it_full // 2, lambda t, c: index_step(2 * t + 1, index_step(2 * t, c)), 0)

    def count(pred_fn):
        def body(kt, accs):
            out = []
            for s in range(n_sub):
                r0 = pl.multiple_of(kt * KT_A + s * KT_I, KT_I)
                key = key_ref[pl.ds(r0, KT_I), :]
                s_pos = r0 + lax.broadcasted_iota(I32, (KT_I, 1), 0)
                m = jnp.where(pred_fn(key, s_pos), 1, 0).astype(I32)
                out.append(accs[s] + jnp.sum(m.reshape(KT_I // 8, 8, tq), axis=0))
            return tuple(out)
        accs = lax.fori_loop(0, n_at, body, tuple(jnp.zeros((8, tq), I32) for _ in range(n_sub)))
        return jnp.sum(sum(accs[1:], accs[0]), axis=0, keepdims=True)

    def count16(ref16, pred_fn):
        def body(kt, accs):
            out = []
            for s in range(n_sub):
                r0 = pl.multiple_of(kt * KT_A + s * KT_I, KT_I)
                m = jnp.where(pred_fn(ref16[pl.ds(r0, KT_I), :]), jnp.int16(1), jnp.int16(0))
                m = m.reshape(KT_I // 16, 16, tq)
                acc = accs[s]
                for q in range(KT_I // 16):
                    acc = acc + m[q]
                out.append(acc)
            return tuple(out)
        accs = lax.fori_loop(0, n_at, body, tuple(jnp.zeros((16, tq), jnp.int16) for _ in range(n_sub)))
        tot = sum((a.astype(I32) for a in accs[1:]), accs[0].astype(I32))
        return jnp.sum(tot, axis=0, keepdims=True)

    small = jnp.where(n_adm_row <= topk, 1, 0).astype(I32)

    def all_done(done):
        return jnp.min(done.astype(F32)) > 0.0

    def search(first_bit, last_bit, count_ge, carry, early_exit):
        def step(b, c):
            v, done, thr = c
            cand_u = v | jnp.left_shift(jnp.int32(1), 31 - b)
            cand_s = cand_u ^ INT_MIN
            cnt = count_ge(cand_s)
            v = jnp.where(cnt >= topk, cand_u, v)
            newly = (cnt == topk) & (done == 0)
            thr = jnp.where(newly, cand_s, thr)
            return v, jnp.where(newly, 1, done), thr

        if not early_exit:
            return lax.fori_loop(first_bit, last_bit, step, carry)

        def cond(c):
            return (c[0] < last_bit) & jnp.logical_not(all_done(c[1][1]))

        def body(c):
            b, inner = c
            return b + 4, lax.fori_loop(b, b + 4, step, inner)

        assert (last_bit - first_bit) % 4 == 0
        return lax.while_loop(cond, body, (jnp.int32(first_bit), carry))[1]

    v0 = jnp.zeros((1, tq), I32)
    thr0 = jnp.full((1, tq), INT_MIN + 1, I32)
    carry = search(0, 16, lambda cand_s: count16(key16_ref, lambda k16: k16 >= (cand_s >> 16).astype(jnp.int16)),
                   (v0, small, thr0), early_exit=False)
    hi16 = ((carry[0] ^ INT_MIN) >> 16).astype(jnp.int16)
    n_above = count16(key16_ref, lambda k16: k16 > hi16)

    @pl.when(jnp.logical_not(all_done(carry[1])))
    def _():
        hi32 = (carry[0] ^ INT_MIN) >> 16

        def low_step(kt, c):
            for s in range(n_sub):
                r0 = pl.multiple_of(kt * KT_A + s * KT_I, KT_I)
                key = key_ref[pl.ds(r0, KT_I), :]
                low = jnp.where((key >> 16) == hi32, (key & 0xFFFF) - 32768, -32768)
                low16_ref[pl.ds(r0, KT_I), :] = low.astype(jnp.int16)
            return c

        lax.fori_loop(0, n_at, low_step, 0)

    def count_low(cand_s):
        c16 = ((cand_s & 0xFFFF) - 32768).astype(jnp.int16)
        return n_above + count16(low16_ref, lambda l16: l16 >= c16)

    v_u, done_i, thr = search(16, 32, count_low, carry, early_exit=True)
    done = done_i != 0
    v_s = v_u ^ INT_MIN
    any_tie = jnp.logical_not(all_done(done_i))

    jlim_ref[...] = jnp.zeros(jlim_ref.shape, I32)

    @pl.when(any_tie)
    def _():
        n_gt = count(lambda key, s_pos: key > v_s)
        need = topk - n_gt
        pos_bits = max(int(math.ceil(math.log2(max(k_ref.shape[1], 2)))), 1) + 1

        def pos_step(b, jv):
            cand = jv | jnp.left_shift(jnp.int32(1), pos_bits - 1 - b)
            cnt = count(lambda key, s_pos: (key == v_s) & (s_pos < cand))
            return jnp.where(cnt <= need, cand, jv)

        jlim_ref[0:1, :] = lax.fori_loop(0, pos_bits, pos_step, jnp.zeros((1, tq), I32))

    j_lim = jlim_ref[0:1, :]

    v_eff = jnp.where(done, thr - 1, v_s)
    j_eff = jnp.where(done, 0, j_lim)

    def bias_loop(with_ties):
        def bias_step(kt, carry):
            for s in range(n_sub):
                r0 = pl.multiple_of(kt * KT_A + s * KT_I, KT_I)
                key = key_ref[pl.ds(r0, KT_I), :]
                sel = key > v_eff
                if with_ties:
                    s_pos = r0 + lax.broadcasted_iota(I32, (KT_I, 1), 0)
                    sel = sel | ((key == v_eff) & (s_pos < j_eff) & (key != INT_MIN))
                bias = jnp.where(sel, 0.0, NEG_BIG).astype(F32)
                key_ref[pl.ds(r0, KT_I), :] = lax.bitcast_convert_type(bias, I32)
            return carry

        lax.fori_loop(0, n_at, bias_step, 0)

    pl.when(any_tie)(lambda: bias_loop(True))
    pl.when(jnp.logical_not(any_tie))(lambda: bias_loop(False))

    gq = gq_ref[...]
    q_all = q_ref[...]
    for h in range(N_HEADS):
        sl = slice(h * HEAD_DIM, (h + 1) * HEAD_DIM)
        qh = q_all[:, sl]
        qh = qh * lax.rsqrt(jnp.mean(qh * qh, axis=-1, keepdims=True) + EPS) * gq * (HEAD_DIM ** -0.5 * LOG2_E)
        qt_ref[sl, :] = qh.T.astype(BF16)
    acc_ref[...] = jnp.zeros(acc_ref.shape, F32)

    heads = [slice(h * HEAD_DIM, (h + 1) * HEAD_DIM) for h in range(N_HEADS)]
    kt_h = KT_A // 2
    n_sub_h = kt_h // KT_I

    def issue_scores(kt, half):
        r0 = pl.multiple_of(kt * KT_A + half * kt_h, kt_h)
        for h, sl in enumerate(heads):
            lg_ref[half, h] = _dot(k_ref[0, pl.ds(r0, kt_h), sl], qt_ref[sl, :])

    def issue_values(kt, half):
        return [_dot(vt_ref[0, kt, sl, half * kt_h:(half + 1) * kt_h], p_ref[half, h])
                for h, sl in enumerate(heads)]

    def softmax_half(kt, half, ms, ls):
        r0 = pl.multiple_of(kt * KT_A + half * kt_h, kt_h)
        new_ms, new_ls, alphas = [], [], []
        for h in range(N_HEADS):
            mx = None
            for s in range(n_sub_h):
                rs = slice(s * KT_I, (s + 1) * KT_I)
                bias = lax.bitcast_convert_type(key_ref[pl.ds(r0 + s * KT_I, KT_I), :], F32)
                lg = lg_ref[half, h, rs, :] + bias
                lg_ref[half, h, rs, :] = lg
                part = jnp.max(lg.reshape(KT_I // 8, 8, tq), axis=0)
                mx = part if mx is None else jnp.maximum(mx, part)
            m_new = jnp.maximum(ms[h], jnp.max(mx, axis=0, keepdims=True))
            alphas.append(jnp.exp2(ms[h] - m_new))
            new_ms.append(m_new)
        for h in range(N_HEADS):
            lsum = None
            for s in range(n_sub_h):
                rs = slice(s * KT_I, (s + 1) * KT_I)
                p = jnp.exp2(lg_ref[half, h, rs, :] - new_ms[h])
                p_ref[half, h, rs, :] = p.astype(BF16)
                part = jnp.sum(p.reshape(KT_I // 8, 8, tq), axis=0)
                lsum = part if lsum is None else lsum + part
            new_ls.append(alphas[h] * ls[h] + jnp.sum(lsum, axis=0, keepdims=True))
        return new_ms, new_ls, alphas

    def accumulate(alphas, pv):
        for h in range(N_HEADS):
            acc_ref[h] = alphas[h] * acc_ref[h] + pv[h]

    def att_step(kt, carry):
        ms, ls = carry
        issue_scores(kt, 0)
        issue_scores(kt, 1)
        ms, ls, alphas_a = softmax_half(kt, 0, ms, ls)
        pv_a = issue_values(kt, 0)
        ms, ls, alphas_b = softmax_half(kt, 1, ms, ls)
        accumulate(alphas_a, pv_a)
        accumulate(alphas_b, issue_values(kt, 1))
        return tuple(ms), tuple(ls)

    m0 = tuple(jnp.full((1, tq), NEG_BIG, F32) for _ in range(N_HEADS))
    l0 = tuple(jnp.zeros((1, tq), F32) for _ in range(N_HEADS))
    _, ls = lax.fori_loop(0, n_at, att_step, (m0, l0))
    z_all = z_ref[...]
    for h in range(N_HEADS):
        sl = slice(h * HEAD_DIM, (h + 1) * HEAD_DIM)
        oh = (acc_ref[h] / ls[h]).T
        o_ref[:, sl] = (oh * _silu(z_all[:, sl])).astype(o_ref.dtype)


def _dsa(proj, small, k_all16, vt16, ki16, gq_row, batch, q_len, tq, offset, s_valid):
    n = batch * q_len
    nq = q_len // tq
    s_pad = k_all16.shape[1]
    topk = min(TOPK_MAX, s_valid // 4)
    gw = GROUP_WIDTH
    qi_w = IDX_HEADS * IDX_DIM
    kern = functools.partial(_dsa_kernel, tq=tq, offset=offset, s_valid=s_valid, topk=topk)
    once = pl.Buffered(1)
    return pl.pallas_call(
        kern,
        grid=(batch, nq),
        in_specs=[
            pl.BlockSpec((tq, gw), lambda b, i: (b * nq + i, COL_QC // gw)),
            pl.BlockSpec((tq, qi_w), lambda b, i: (b * nq + i, COL_QI // qi_w)),
            pl.BlockSpec((tq, gw), lambda b, i: (b * nq + i, COL_ZC // gw)),
            pl.BlockSpec((tq, SMALL_COLS), lambda b, i: (b * nq + i, 0)),
            pl.BlockSpec((1, s_pad, gw), lambda b, i: (b, 0, 0), pipeline_mode=once),
            pl.BlockSpec((1, s_pad // KT_A, gw, KT_A), lambda b, i: (b, 0, 0, 0), pipeline_mode=once),
            pl.BlockSpec((1, s_pad, IDX_DIM), lambda b, i: (b, 0, 0), pipeline_mode=once),
            pl.BlockSpec((1, HEAD_DIM), lambda b, i: (0, 0)),
        ],
        out_specs=pl.BlockSpec((tq, gw), lambda b, i: (b * nq + i, 0)),
        out_shape=jax.ShapeDtypeStruct((n, gw), BF16),
        scratch_shapes=[pltpu.VMEM((s_pad, tq), I32),
                        pltpu.VMEM((qi_w, tq), BF16),
                        pltpu.VMEM((SMALL_COLS, tq), F32),
                        pltpu.VMEM((8, tq), I32),
                        pltpu.VMEM((gw, tq), BF16),
                        pltpu.VMEM((N_HEADS, HEAD_DIM, tq), F32),
                        pltpu.VMEM((2, N_HEADS, KT_A // 2, tq), F32),
                        pltpu.VMEM((2, N_HEADS, KT_A // 2, tq), BF16),
                        pltpu.VMEM((s_pad, tq), jnp.int16),
                        pltpu.VMEM((s_pad, tq), jnp.int16)],
        compiler_params=_cparams(("parallel", "arbitrary")),
        name="dsa",
    )(proj, proj, proj, small, k_all16, vt16, ki16, gq_row)


def _dsa_select_kernel(qi_ref, sm_ref, pki_ref, nki_ref, bias_ref, ki_ref, key_ref, jl_ref, *,
                       lq, offset, s_valid, topk):
    s_pad = ki_ref.shape[0]
    n_t = s_pad // KT_A
    ki_ref[0:offset, :] = pki_ref[0].astype(BF16)
    ki_ref[offset:s_pad, :] = jnp.zeros((s_pad - offset, IDX_DIM), BF16)
    ki_ref[offset:offset + lq, :] = nki_ref[...]
    qi = qi_ref[...]
    q2 = jnp.concatenate([qi[:, h * IDX_DIM:(h + 1) * IDX_DIM] for h in range(IDX_HEADS)], axis=0).astype(BF16)
    w = sm_ref[...] * (IDX_HEADS ** -0.5 * IDX_DIM ** -0.5)
    w_cols = [w[:, LANE_WI + h:LANE_WI + h + 1] for h in range(IDX_HEADS)]
    t_chunk = (offset + lax.broadcasted_iota(I32, (lq, 1), 0)) // CHUNK

    def index_step(j, carry):
        c0 = pl.multiple_of(j * KT_A, KT_A)
        sc = _dot_nt(q2, ki_ref[pl.ds(c0, KT_A), :])
        acc = jnp.zeros((lq, KT_A), F32)
        for h in range(IDX_HEADS):
            acc = acc + w_cols[h] * jnp.maximum(sc[h * lq:(h + 1) * lq, :], 0.0)
        bits = lax.bitcast_convert_type(acc, I32)
        key = bits ^ ((bits >> 31) & 0x7FFFFFFF)
        s_pos = c0 + lax.broadcasted_iota(I32, (1, KT_A), 1)
        adm = (s_pos // CHUNK <= t_chunk) & (s_pos < s_valid)
        key_ref[:, pl.ds(c0, KT_A)] = jnp.where(adm, key, INT_MIN)
        return carry

    lax.fori_loop(0, n_t, index_step, 0)

    def count(pred_fn):
        def body(j, acc):
            c0 = pl.multiple_of(j * KT_A, KT_A)
            s_pos = c0 + lax.broadcasted_iota(I32, (1, KT_A), 1)
            m = jnp.where(pred_fn(key_ref[:, pl.ds(c0, KT_A)], s_pos), 1, 0).astype(I32)
            for c in range(KT_A // 128):
                acc = acc + m[:, c * 128:(c + 1) * 128]
            return acc
        acc = lax.fori_loop(0, n_t, body, jnp.zeros((lq, 128), I32))
        return jnp.sum(acc, axis=1, keepdims=True)

    def bit_step(b, v):
        cand_u = v | jnp.left_shift(jnp.int32(1), 31 - b)
        cnt = count(lambda key, s_pos: key >= (cand_u ^ INT_MIN))
        return jnp.where(cnt >= topk, cand_u, v)

    v_s = lax.fori_loop(0, 32, bit_step, jnp.zeros((lq, 1), I32)) ^ INT_MIN
    n_gt = count(lambda key, s_pos: key > v_s)
    n_ge = count(lambda key, s_pos: key >= v_s)
    need = topk - n_gt
    tied = (n_ge != topk) & (v_s != INT_MIN)
    jl_ref[...] = jnp.full(jl_ref.shape, s_pad, I32)

    @pl.when(jnp.max(jnp.where(tied, 1.0, 0.0)) > 0.0)
    def _():
        pos_bits = max(int(math.ceil(math.log2(max(s_pad, 2)))), 1) + 1

        def pos_step(b, jv):
            cand = jv | jnp.left_shift(jnp.int32(1), pos_bits - 1 - b)
            cnt = count(lambda key, s_pos: (key == v_s) & (s_pos < cand))
            return jnp.where(cnt <= need, cand, jv)

        jl_ref[:, 0:1] = lax.fori_loop(0, pos_bits, pos_step, jnp.zeros((lq, 1), I32))

    j_lim = jnp.where(tied, jl_ref[:, 0:1], s_pad)

    def bias_step(j, carry):
        c0 = pl.multiple_of(j * KT_A, KT_A)
        key = key_ref[:, pl.ds(c0, KT_A)]
        s_pos = c0 + lax.broadcasted_iota(I32, (1, KT_A), 1)
        sel = (key > v_s) | ((key == v_s) & (s_pos < j_lim) & (key != INT_MIN))
        bias_ref[0, :, pl.ds(c0, KT_A)] = jnp.where(sel, 0.0, NEG_BIG).astype(F32)
        return carry

    lax.fori_loop(0, n_t, bias_step, 0)


def _dsa_select(proj, small, past_ki, ki16, batch, lq, offset, layer):
    s_valid = offset + lq
    s_pad = _round_up(s_valid, KT_A)
    topk = min(TOPK_MAX, s_valid // 4)
    qi_w = IDX_HEADS * IDX_DIM
    kern = functools.partial(_dsa_select_kernel, lq=lq, offset=offset, s_valid=s_valid, topk=topk)
    return pl.pallas_call(
        kern,
        grid=(batch,),
        in_specs=[
            pl.BlockSpec((lq, qi_w), lambda b: (b, COL_QI // qi_w)),
            pl.BlockSpec((lq, SMALL_COLS), lambda b: (b, 0)),
            pl.BlockSpec((None, 1, offset, IDX_DIM), lambda b: (layer, b, 0, 0)),
            pl.BlockSpec((lq, IDX_DIM), lambda b: (b, 0)),
        ],
        out_specs=pl.BlockSpec((1, lq, s_pad), lambda b: (b, 0, 0)),
        out_shape=jax.ShapeDtypeStruct((batch, lq, s_pad), F32),
        scratch_shapes=[pltpu.VMEM((s_pad, IDX_DIM), BF16), pltpu.VMEM((lq, s_pad), I32),
                        pltpu.VMEM((lq, 128), I32)],
        compiler_params=_cparams(("parallel",)),
        name="dsa_select",
    )(proj, small, past_ki, ki16)


def _dsa_decode_kernel(q_ref, z_ref, bias_ref, pk_ref, pv_ref, nk_ref, nv_ref, gq_ref, o_ref,
                       qn_ref, m_ref, l_ref, acc_ref, *, lq, n_cache_tiles):
    j = pl.program_id(1)

    @pl.when(j == 0)
    def _():
        q_all = q_ref[...]
        gq = gq_ref[...]
        for h in range(N_HEADS):
            qh = q_all[:, h * HEAD_DIM:(h + 1) * HEAD_DIM]
            qh = qh * lax.rsqrt(jnp.mean(qh * qh, axis=-1, keepdims=True) + EPS) * gq * (HEAD_DIM ** -0.5 * LOG2_E)
            qn_ref[h] = qh.astype(BF16)
        m_ref[...] = jnp.full(m_ref.shape, NEG_BIG, F32)
        l_ref[...] = jnp.zeros(l_ref.shape, F32)
        acc_ref[...] = jnp.zeros(acc_ref.shape, F32)

    def attend(k_heads, v_heads, bias):
        for h in range(N_HEADS):
            logit = _dot_nt(qn_ref[h], k_heads[h]) + bias
            m_old = m_ref[h]
            m_new = jnp.maximum(m_old, jnp.max(logit, axis=1, keepdims=True))
            alpha = jnp.exp2(m_old - m_new)
            p = jnp.exp2(logit - m_new[:, 0:1])
            l_ref[h] = alpha * l_ref[h] + jnp.sum(p, axis=1, keepdims=True)
            acc_ref[h] = alpha * acc_ref[h] + _dot(p.astype(BF16), v_heads[h])
            m_ref[h] = m_new

    @pl.when(j < n_cache_tiles)
    def _():
        attend([pk_ref[0, pl.ds(h, KT_A, stride=N_HEADS), :].astype(BF16) for h in range(N_HEADS)],
               [pv_ref[0, pl.ds(h, KT_A, stride=N_HEADS), :].astype(BF16) for h in range(N_HEADS)], bias_ref[0])

    @pl.when(j == n_cache_tiles)
    def _():
        pad = jnp.zeros((128 - lq, HEAD_DIM), BF16)
        nk = nk_ref[...]
        attend([jnp.concatenate([nk[:, h * HEAD_DIM:(h + 1) * HEAD_DIM], pad], axis=0) for h in range(N_HEADS)],
               [jnp.concatenate([nv_ref[pl.ds(h, lq, stride=N_HEADS), :].astype(BF16), pad], axis=0)
                for h in range(N_HEADS)],
               bias_ref[0][:, 0:128])
        z_all = z_ref[...]
        for h in range(N_HEADS):
            sl = slice(h * HEAD_DIM, (h + 1) * HEAD_DIM)
            o_ref[:, sl] = (acc_ref[h] / l_ref[h] * _silu(z_all[:, sl])).astype(o_ref.dtype)


def _dsa_decode(proj, bias, past_k, past_v, k16, v_new, gq_row, batch, lq, offset, layer):
    gw = GROUP_WIDTH
    n_cache_tiles = offset // KT_A
    kern = functools.partial(_dsa_decode_kernel, lq=lq, n_cache_tiles=n_cache_tiles)
    last = n_cache_tiles - 1

    def rows(c):
        return c.reshape(c.shape[0], c.shape[1], offset * N_HEADS, HEAD_DIM)
    return pl.pallas_call(
        kern,
        grid=(batch, n_cache_tiles + 1),
        in_specs=[
            pl.BlockSpec((lq, gw), lambda b, j: (b, COL_QC // gw)),
            pl.BlockSpec((lq, gw), lambda b, j: (b, COL_ZC // gw)),
            pl.BlockSpec((1, lq, KT_A), lambda b, j: (b, 0, j)),
            pl.BlockSpec((None, 1, KT_A * N_HEADS, HEAD_DIM), lambda b, j: (layer, b, jnp.minimum(j, last), 0)),
            pl.BlockSpec((None, 1, KT_A * N_HEADS, HEAD_DIM), lambda b, j: (layer, b, jnp.minimum(j, last), 0)),
            pl.BlockSpec((lq, gw), lambda b, j: (b, 0)),
            pl.BlockSpec((None, lq * N_HEADS, HEAD_DIM), lambda b, j: (layer, b, 0)),
            pl.BlockSpec((1, HEAD_DIM), lambda b, j: (0, 0)),
        ],
        out_specs=pl.BlockSpec((lq, gw), lambda b, j: (b, 0)),
        out_shape=jax.ShapeDtypeStruct((batch * lq, gw), BF16),
        scratch_shapes=[pltpu.VMEM((N_HEADS, lq, HEAD_DIM), BF16),
                        pltpu.VMEM((N_HEADS, lq, HEAD_DIM), F32),
                        pltpu.VMEM((N_HEADS, lq, HEAD_DIM), F32),
                        pltpu.VMEM((N_HEADS, lq, HEAD_DIM), F32)],
        compiler_params=_cparams(("parallel", "arbitrary")),
        name="dsa_decode",
    )(proj, proj, bias, rows(past_k), rows(past_v), k16, v_new, gq_row)


def _mem_kernel(q_ref, z_ref, mk_ref, mv_ref, gq_ref, o_ref):
    gq = gq_ref[...]
    q_all, z_all = q_ref[...], z_ref[...]
    mk = mk_ref[0].astype(BF16)
    mv = mv_ref[0].astype(BF16)
    for h in range(N_HEADS):
        sl = slice(h * HEAD_DIM, (h + 1) * HEAD_DIM)
        qh = q_all[:, sl]
        qh = qh * lax.rsqrt(jnp.mean(qh * qh, axis=-1, keepdims=True) + EPS) * gq * (HEAD_DIM ** -0.5)
        logit = _dot_nt(qh.astype(BF16), mk[:, sl])
        m = jnp.max(logit, axis=-1, keepdims=True)
        p = jnp.exp(logit - m)
        l = jnp.sum(p, axis=-1, keepdims=True)
        oh = _dot(p.astype(BF16), mv[:, sl]) / l
        o_ref[:, sl] = (oh * _silu(z_all[:, sl])).astype(o_ref.dtype)


def _mem_attend(proj, mk, mv, gq_row, batch, seqlen):
    n = batch * seqlen
    tm = min(512, seqlen)
    nj = seqlen // tm
    gw = GROUP_WIDTH
    n_mem = mk.shape[1]
    return pl.pallas_call(
        _mem_kernel,
        grid=(batch, nj),
        in_specs=[
            pl.BlockSpec((tm, gw), lambda b, j: (b * nj + j, COL_QD // gw)),
            pl.BlockSpec((tm, gw), lambda b, j: (b * nj + j, COL_ZD // gw)),
            pl.BlockSpec((1, n_mem, gw), lambda b, j: (b, 0, 0)),
            pl.BlockSpec((1, n_mem, gw), lambda b, j: (b, 0, 0)),
            pl.BlockSpec((1, HEAD_DIM), lambda b, j: (0, 0)),
        ],
        out_specs=pl.BlockSpec((tm, gw), lambda b, j: (b * nj + j, 0)),
        out_shape=jax.ShapeDtypeStruct((n, gw), BF16),
        compiler_params=_cparams(("parallel", "arbitrary")),
        name="mem_attend",
    )(proj, proj, mk, mv, gq_row)


def _outproj_kernel(x_ref, a_ref, b_ref, c_ref, d_ref, w_ref, y_ref):
    gw = GROUP_WIDTH
    acc = x_ref[...] + _dot(a_ref[...], w_ref[0:gw, :])
    acc = acc + _dot(b_ref[...], w_ref[gw:2 * gw, :])
    acc = acc + _dot(c_ref[...], w_ref[2 * gw:3 * gw, :])
    acc = acc + _dot(d_ref[...], w_ref[3 * gw:4 * gw, :])
    y_ref[...] = acc


def _out_proj(x2d, oa, ob, oc, od, w16):
    n, d = x2d.shape
    tm = min(512, n)
    gw = GROUP_WIDTH
    grp = pl.BlockSpec((tm, gw), lambda i: (i, 0))
    return pl.pallas_call(
        _outproj_kernel,
        grid=(n // tm,),
        in_specs=[pl.BlockSpec((tm, d), lambda i: (i, 0)), grp, grp, grp, grp,
                  pl.BlockSpec((4 * gw, d), lambda i: (0, 0))],
        out_specs=pl.BlockSpec((tm, d), lambda i: (i, 0)),
        out_shape=jax.ShapeDtypeStruct((n, d), F32),
        compiler_params=_cparams(("parallel",)),
        name="out_proj",
    )(x2d, oa, ob, oc, od, w16)


_W_A_END = 4 * GROUP_WIDTH
_W_B_START = _W_A_END + 2 * N_HEADS
_W_B_END = _W_B_START + 8 * GROUP_WIDTH + IDX_HEADS * IDX_DIM
_W_D_START = _W_B_END + IDX_DIM + IDX_HEADS
_W_COLS = _W_D_START + 2 * GROUP_WIDTH


def _w_in_kernel(w_ref, m_ref, s_ref):
    m_ref[:, 0:_W_A_END] = w_ref[:, 0:_W_A_END].astype(BF16)
    m_ref[:, _W_A_END:_W_A_END + _W_B_END - _W_B_START] = w_ref[:, _W_B_START:_W_B_END].astype(BF16)
    m_ref[:, MAIN_COLS - 2 * GROUP_WIDTH:MAIN_COLS] = w_ref[:, _W_D_START:_W_COLS].astype(BF16)
    n_kw = IDX_DIM + IDX_HEADS
    s_ref[:, 0:n_kw] = w_ref[:, _W_B_END:_W_D_START].astype(BF16)
    s_ref[:, n_kw:n_kw + 2 * N_HEADS] = w_ref[:, _W_A_END:_W_B_START].astype(BF16)
    s_ref[:, n_kw + 2 * N_HEADS:] = jnp.zeros((w_ref.shape[0], SMALL_COLS - n_kw - 2 * N_HEADS), BF16)


def _prep_w_in(w_in):
    depth, d, cols = w_in.shape
    assert cols == _W_COLS and MAIN_COLS == _W_A_END + (_W_B_END - _W_B_START) + 2 * GROUP_WIDTH
    tm = 128
    return pl.pallas_call(
        _w_in_kernel,
        grid=(depth, d // tm),
        in_specs=[pl.BlockSpec((None, tm, cols), lambda l, i: (l, i, 0))],
        out_specs=[pl.BlockSpec((None, tm, MAIN_COLS), lambda l, i: (l, i, 0)),
                   pl.BlockSpec((None, tm, SMALL_COLS), lambda l, i: (l, i, 0))],
        out_shape=[jax.ShapeDtypeStruct((depth, d, MAIN_COLS), BF16),
                   jax.ShapeDtypeStruct((depth, d, SMALL_COLS), BF16)],
        compiler_params=_cparams(("parallel", "parallel")),
        name="w_in_prep",
    )(w_in)


def _lane_row(vals, lane0):
    row = jnp.zeros((1, SMALL_COLS), F32)
    return row.at[0, lane0:lane0 + vals.shape[0]].set(vals.astype(F32))


def _rope_tables(pos):
    half = HEAD_DIM // 2
    inv = ROPE_THETA ** (-jnp.arange(half, dtype=F32) / half)
    ang = pos.astype(F32)[:, None] * inv[None, :]
    cos, sin = jnp.cos(ang), jnp.sin(ang)
    return jnp.concatenate([cos, cos], axis=-1), jnp.concatenate([-sin, sin], axis=-1)


def _round_up(x, m):
    return (x + m - 1) // m * m


def _mixer_layer(x, conv_buf, s_gdn, s_ret, past_k, past_v, past_ki, mem_k, mem_v, wts, layer, depth, prev_cache):
    b, l, d = x.shape
    n = b * l
    offset = 0 if past_k is None else past_k.shape[2]
    x2d = x.reshape(n, d)
    proj, small = _in_proj(x2d, wts["norm_g"], wts["w_main"], wts["w_small"], layer)

    o_a, conv_new, s_gdn_new = _gdn(proj, small, wts["conv_w"], conv_buf, s_gdn, wts["alog_row"],
                                    wts["dtb_row"], wts["gdn_norm_g"], b, l)

    cos_t, sin_t = wts["rope_p"] if past_k is None else wts["rope_s"]
    o_b, s_ret_new = _retention(proj, cos_t, sin_t, s_ret, wts["ret_norm_g"], wts["ret_norm_b"], b, l)

    prefill = past_k is None
    prep = _dsa_prep(proj, small, wts["dsa_k_norm_g"], wts["idx_k_norm_g"], layer, depth, prev_cache, prefill)
    cache = tuple(prep[:3])
    if prefill:
        assert l % KT_A == 0, "prefill length must be a multiple of the key tile"
        o_c = _dsa(proj, small, prep[3].reshape(b, l, GROUP_WIDTH), prep[5].reshape(b, l // KT_A, GROUP_WIDTH, KT_A),
                   prep[4].reshape(b, l, IDX_DIM), wts["dsa_q_norm_g"], b, l, 256, 0, l)
    else:
        assert offset % KT_A == 0 and l % 16 == 0 and l <= 128, "decode step shape not supported"
        bias = _dsa_select(proj, small, past_ki, prep[4], b, l, offset, layer)
        o_c = _dsa_decode(proj, bias, past_k, past_v, prep[3], cache[1], wts["dsa_q_norm_g"], b, l, offset, layer)

    o_d = _mem_attend(proj, mem_k, mem_v, wts["mem_q_norm_g"], b, l)

    y = _out_proj(x2d, o_a, o_b, o_c, o_d, wts["w_out"]).reshape(b, l, d)
    return y, (conv_new, s_gdn_new, s_ret_new), cache


def kernel(x_prompt, x_sample, cache_gdn_conv, state_gdn, state_ret, cache_dsa_k, cache_dsa_v, cache_idx_k, cache_mem_k, cache_mem_v, mem_prompt, norm_g, w_in, gdn_conv_w, gdn_a_log, gdn_dt_bias, gdn_norm_g, ret_norm_g, ret_norm_b, dsa_q_norm_g, dsa_k_norm_g, idx_k_norm_g, mem_norm_g, w_mem_kv, mem_q_norm_g, mem_k_norm_g, w_out):
    depth = w_in.shape[0]
    b = x_prompt.shape[0]
    n_mem = mem_prompt.shape[1]
    d = x_prompt.shape[-1]
    y_p, y_s = x_prompt, x_sample
    st_p, st_s, mem_p = [], [], []
    cache_p = cache_s = None
    w_main, w_small = _prep_w_in(w_in)
    rope_p = _rope_tables(jnp.arange(x_prompt.shape[1], dtype=I32))
    rope_s = _rope_tables(cache_dsa_k.shape[2] + jnp.arange(x_sample.shape[1], dtype=I32))
    for li in range(depth):
        wts = dict(
            norm_g=norm_g[li][None, :], w_main=w_main, w_small=w_small, rope_p=rope_p, rope_s=rope_s,
            conv_w=gdn_conv_w[li],
            alog_row=_lane_row(gdn_a_log[li], LANE_ALPHA), dtb_row=_lane_row(gdn_dt_bias[li], LANE_ALPHA),
            gdn_norm_g=gdn_norm_g[li][None, :], ret_norm_g=ret_norm_g[li][None, :],
            ret_norm_b=ret_norm_b[li][None, :], dsa_q_norm_g=dsa_q_norm_g[li][None, :],
            dsa_k_norm_g=dsa_k_norm_g[li][None, :], idx_k_norm_g=idx_k_norm_g[li][None, :],
            mem_q_norm_g=mem_q_norm_g[li][None, :], w_out=w_out[li].astype(BF16),
        )
        mk, mv = _memory_kv(mem_prompt.reshape(b * n_mem, d), mem_norm_g[li][None, :],
                            w_mem_kv[li].astype(BF16), mem_k_norm_g[li][None, :])
        mk = mk.reshape(b, n_mem, GROUP_WIDTH)
        mv = mv.reshape(b, n_mem, GROUP_WIDTH)
        conv0 = jnp.zeros((b, CONV_W - 1, 3 * GROUP_WIDTH), F32)
        s0 = jnp.zeros((b, N_HEADS, HEAD_DIM, HEAD_DIM), F32)
        y_p, sp, cache_p = _mixer_layer(y_p, conv0, s0, s0, None, None, None, mk, mv, wts, li, depth, cache_p)
        st_p.append(sp)
        mem_p.append((mk.reshape(b, n_mem, N_HEADS, HEAD_DIM), mv.reshape(b, n_mem, N_HEADS, HEAD_DIM)))
        bs = x_sample.shape[0]
        y_s, ss, cache_s = _mixer_layer(y_s, cache_gdn_conv[li], state_gdn[li], state_ret[li],
                                        cache_dsa_k, cache_dsa_v, cache_idx_k,
                                        cache_mem_k[li].reshape(bs, n_mem, GROUP_WIDTH),
                                        cache_mem_v[li].reshape(bs, n_mem, GROUP_WIDTH), wts, li, depth, cache_s)
        st_s.append(ss)

    def stack(lst, k):
        return jnp.stack([s[k] for s in lst])

    def caches(c, bb, ll):
        return (c[0].reshape(depth, bb, ll, N_HEADS, HEAD_DIM), c[1].reshape(depth, bb, ll, N_HEADS, HEAD_DIM),
                c[2].reshape(depth, bb, ll, IDX_DIM))

    return ((y_p, y_s, stack(st_p, 0), stack(st_p, 1), stack(st_p, 2))
            + caches(cache_p, b, x_prompt.shape[1])
            + (stack(mem_p, 0), stack(mem_p, 1), stack(st_s, 0), stack(st_s, 1), stack(st_s, 2))
            + caches(cache_s, x_sample.shape[0], x_sample.shape[1]))
```

```python
import functools
import math

import jax
import jax.numpy as jnp
from jax import lax
from jax.experimental import pallas as pl
from jax.experimental.pallas import tpu as pltpu

F32 = jnp.float32
BF16 = jnp.bfloat16
I32 = jnp.int32

HEAD_DIM = 128
N_HEADS = 4
GROUP_WIDTH = N_HEADS * HEAD_DIM
CHUNK = 64
CONV_W = 4
IDX_HEADS = 16
IDX_DIM = 64
TOPK_MAX = 256
ROPE_THETA = 10000.0
EPS = 1e-6

COL_QA, COL_KA, COL_VA, COL_ZA = 0, 512, 1024, 1536
COL_QB, COL_KB, COL_VB, COL_ZB = 2048, 2560, 3072, 3584
COL_QC, COL_KC, COL_VC, COL_ZC = 4096, 4608, 5120, 5632
COL_QI = 6144
COL_QD, COL_ZD = 7168, 7680
MAIN_COLS = 8192
LANE_KI, LANE_WI, LANE_BETA, LANE_ALPHA = 0, 64, 80, 84
SMALL_COLS = 128

INT_MIN = -2 ** 31
NEG_BIG = -1e30
LOG2_E = 1.4426950408889634
VMEM_LIMIT = 56 * 1024 * 1024
HI = lax.Precision.HIGHEST


def _cparams(sem):
    return pltpu.CompilerParams(dimension_semantics=sem, vmem_limit_bytes=VMEM_LIMIT)


def _dot(a, b):
    return jnp.dot(a, b, preferred_element_type=F32)


def _dot_nt(a, b):
    return lax.dot_general(a, b, (((1,), (1,)), ((), ())), preferred_element_type=F32)


def _dot3(a, b):
    a_hi = a.astype(BF16)
    b_hi = b.astype(BF16)
    a_lo = (a - a_hi.astype(F32)).astype(BF16)
    b_lo = (b - b_hi.astype(F32)).astype(BF16)
    return _dot(a_hi, b_hi) + (_dot(a_hi, b_lo) + _dot(a_lo, b_hi))


def _silu(x):
    return x * jax.nn.sigmoid(x)


def _inproj_kernel(x_ref, g_ref, w_ref, ws_ref, o_ref, os_ref, h_ref):
    @pl.when(pl.program_id(1) == 0)
    def _():
        x = x_ref[...]
        y = x * lax.rsqrt(jnp.mean(x * x, axis=-1, keepdims=True) + EPS) * g_ref[...]
        hb = y.astype(BF16)
        h_ref[...] = hb
        os_ref[...] = _dot(hb, ws_ref[...])

    o_ref[...] = _dot(h_ref[...], w_ref[...])


def _in_proj(x2d, g_row, w_main, w_small, layer):
    n, d = x2d.shape
    tm = min(1024, n)
    tn = 1024
    return pl.pallas_call(
        _inproj_kernel,
        grid=(n // tm, MAIN_COLS // tn),
        in_specs=[
            pl.BlockSpec((tm, d), lambda i, j: (i, 0)),
            pl.BlockSpec((1, d), lambda i, j: (0, 0)),
            pl.BlockSpec((None, d, tn), lambda i, j: (layer, 0, j)),
            pl.BlockSpec((None, d, SMALL_COLS), lambda i, j: (layer, 0, 0)),
        ],
        out_specs=[
            pl.BlockSpec((tm, tn), lambda i, j: (i, j)),
            pl.BlockSpec((tm, SMALL_COLS), lambda i, j: (i, 0)),
        ],
        out_shape=[jax.ShapeDtypeStruct((n, MAIN_COLS), F32),
                   jax.ShapeDtypeStruct((n, SMALL_COLS), F32)],
        scratch_shapes=[pltpu.VMEM((tm, d), BF16)],
        compiler_params=_cparams(("parallel", "arbitrary")),
        name="in_proj",
    )(x2d, g_row, w_main, w_small)


def _memkv_kernel(x_ref, g_ref, w_ref, gk_ref, mk_ref, mv_ref):
    x = x_ref[...]
    y = x * lax.rsqrt(jnp.mean(x * x, axis=-1, keepdims=True) + EPS) * g_ref[...]
    kv = _dot(y.astype(BF16), w_ref[...])
    gk = gk_ref[...]
    for h in range(N_HEADS):
        sl = slice(h * HEAD_DIM, (h + 1) * HEAD_DIM)
        kh = kv[:, sl]
        mk_ref[:, sl] = kh * lax.rsqrt(jnp.mean(kh * kh, axis=-1, keepdims=True) + EPS) * gk
    mv_ref[...] = kv[:, GROUP_WIDTH:]


def _memory_kv(mem2d, g_row, w_kv, gk_row):
    n, d = mem2d.shape
    tm = min(256, n)
    return pl.pallas_call(
        _memkv_kernel,
        grid=(n // tm,),
        in_specs=[
            pl.BlockSpec((tm, d), lambda i: (i, 0)),
            pl.BlockSpec((1, d), lambda i: (0, 0)),
            pl.BlockSpec((d, 2 * GROUP_WIDTH), lambda i: (0, 0)),
            pl.BlockSpec((1, HEAD_DIM), lambda i: (0, 0)),
        ],
        out_specs=[pl.BlockSpec((tm, GROUP_WIDTH), lambda i: (i, 0)),
                   pl.BlockSpec((tm, GROUP_WIDTH), lambda i: (i, 0))],
        out_shape=[jax.ShapeDtypeStruct((n, GROUP_WIDTH), F32),
                   jax.ShapeDtypeStruct((n, GROUP_WIDTH), F32)],
        compiler_params=_cparams(("parallel",)),
        name="memory_kv",
    )(mem2d, g_row, w_kv, gk_row)


def _gdn_kernel(qkv_ref, z_ref, sm_ref, cw_ref, cb_ref, s0_ref, alog_ref, dtb_ref, gn_ref,
                o_ref, conv_ref, st_ref, xbuf_ref, s_ref, *, t_blk, chunk):
    j = pl.program_id(1)
    nj = pl.num_programs(1)
    gw3 = 3 * GROUP_WIDTH

    @pl.when(j == 0)
    def _():
        xbuf_ref[0:8, :] = jnp.zeros((8, gw3), F32)
        xbuf_ref[5:8, :] = cb_ref[0]
        s_ref[...] = s0_ref[0]

    @pl.when(j > 0)
    def _():
        xbuf_ref[0:8, :] = xbuf_ref[t_blk:t_blk + 8, :]

    xbuf_ref[8:8 + t_blk, :] = qkv_ref[...]
    conv_ref[0] = xbuf_ref[t_blk + 5:t_blk + 8, :]

    cw = cw_ref[...]
    x_all = xbuf_ref[...]
    x_prev = pltpu.roll(x_all, 1, 0)
    u = x_all * cw[3:4, :] + x_prev * cw[2:3, :]
    v = x_all * cw[1:2, :] + x_prev * cw[0:1, :]
    y = _silu((u + pltpu.roll(v, 2, 0))[8:8 + t_blk, :])

    sm = sm_ref[...]
    lane = lax.broadcasted_iota(I32, sm.shape, 1)
    beta_all = jax.nn.sigmoid(sm)
    xs = sm + dtb_ref[...]
    softplus = jnp.maximum(xs, 0.0) + jnp.log1p(jnp.exp(-jnp.abs(xs)))
    g_all = -jnp.exp(alog_ref[...]) * softplus
    g_all = jnp.where((lane >= LANE_ALPHA) & (lane < LANE_ALPHA + N_HEADS), g_all, 0.0)
    ri = lax.broadcasted_iota(I32, (t_blk, t_blk), 0)
    ci = lax.broadcasted_iota(I32, (t_blk, t_blk), 1)
    tri = jnp.where((ri // chunk == ci // chunk) & (ci <= ri), 1.0, 0.0).astype(F32)
    gcum = jnp.dot(tri, g_all, preferred_element_type=F32, precision=HI)
    gcum_t = gcum.T

    blk = min(2 * chunk, t_blk)
    n_blk = t_blk // blk
    cpb = blk // chunk
    rb = lax.broadcasted_iota(I32, (blk, blk), 0)
    cb = lax.broadcasted_iota(I32, (blk, blk), 1)
    same = (rb // chunk) == (cb // chunk)
    causal = same & (cb <= rb)
    strict = same & (cb < rb)
    eye = jnp.where(rb == cb, 1.0, 0.0).astype(F32)
    n_dbl = max(int(math.log2(chunk)) - 1, 0)
    gn = gn_ref[...]
    z_all = z_ref[...]

    qn, kn, vv = [], [], []
    for h in range(N_HEADS):
        qh = y[:, COL_QA + h * HEAD_DIM:COL_QA + (h + 1) * HEAD_DIM]
        kh = y[:, COL_KA + h * HEAD_DIM:COL_KA + (h + 1) * HEAD_DIM]
        qn.append(qh * lax.rsqrt(jnp.sum(qh * qh, axis=-1, keepdims=True) + EPS) * (HEAD_DIM ** -0.5))
        kn.append(kh * lax.rsqrt(jnp.sum(kh * kh, axis=-1, keepdims=True) + EPS))
        vv.append(y[:, COL_VA + h * HEAD_DIM:COL_VA + (h + 1) * HEAD_DIM])

    qs, ks, gcols, p_mats, x_mats, qk_mats, vb_mats, kg_mats, u_mats, w_mats = ({} for _ in range(10))

    def wy_operands(b):
        rs = slice(b * blk, (b + 1) * blk)
        for h in range(N_HEADS):
            gcol = gcum[rs, LANE_ALPHA + h:LANE_ALPHA + h + 1]
            grow = gcum_t[LANE_ALPHA + h:LANE_ALPHA + h + 1, rs]
            bcol = beta_all[rs, LANE_BETA + h:LANE_BETA + h + 1]
            decay = jnp.where(causal, jnp.exp(jnp.where(causal, gcol - grow, 0.0)), 0.0)
            qc, kc, vc = qn[h][rs], kn[h][rs], vv[h][rs]
            kb = kc * bcol
            kc16 = kc.astype(BF16)
            a_mat = jnp.where(strict, _dot_nt(kb.astype(BF16), kc16) * decay, 0.0)
            qk_mats[h, b] = jnp.where(causal, _dot_nt(qc.astype(BF16), kc16) * decay, 0.0).astype(BF16)
            p_mats[h, b] = -a_mat
            x_mats[h, b] = eye - a_mat
            vb_mats[h, b] = (vc * bcol).astype(BF16)
            kg_mats[h, b] = (kb * jnp.exp(gcol)).astype(BF16)
            qs[h, b], ks[h, b], gcols[h, b] = qc, kc, gcol

    def square_p(b):
        for h in range(N_HEADS):
            p_mats[h, b] = _dot3(p_mats[h, b], p_mats[h, b])

    def extend_x(b):
        for h in range(N_HEADS):
            x_mats[h, b] = x_mats[h, b] + _dot3(x_mats[h, b], p_mats[h, b])

    def wy_finish(b):
        for h in range(N_HEADS):
            x16 = x_mats[h, b].astype(BF16)
            u_mats[h, b] = _dot(x16, vb_mats[h, b])
            w_mats[h, b] = _dot(x16, kg_mats[h, b]).astype(BF16)

    states = [s_ref[h] for h in range(N_HEADS)]

    def recurrence_tasks(b):
        o_state = {h: [] for h in range(N_HEADS)}
        v_new = {h: [] for h in range(N_HEADS)}

        def chunk_step(c):
            cs = slice(c * chunk, (c + 1) * chunk)
            for h in range(N_HEADS):
                s = states[h]
                s16 = s.astype(BF16)
                gcol = gcols[h, b][cs]
                glast = gcols[h, b][(c + 1) * chunk - 1:(c + 1) * chunk]
                vn = u_mats[h, b][cs] - _dot(w_mats[h, b][cs], s16)
                vn16 = vn.astype(BF16)
                kd_t = (ks[h, b][cs] * jnp.exp(glast - gcol)).T.astype(BF16)
                states[h] = s * jnp.exp(glast) + _dot(kd_t, vn16)
                o_state[h].append(_dot((qs[h, b][cs] * jnp.exp(gcol)).astype(BF16), s16))
                v_new[h].append(vn16)

        def emit():
            rs = slice(b * blk, (b + 1) * blk)
            for h in range(N_HEADS):
                sl = slice(h * HEAD_DIM, (h + 1) * HEAD_DIM)
                o = jnp.concatenate(o_state[h], axis=0) + _dot(qk_mats[h, b], jnp.concatenate(v_new[h], axis=0))
                on = o * lax.rsqrt(jnp.mean(o * o, axis=-1, keepdims=True) + EPS) * gn
                o_ref[rs, sl] = (on * _silu(z_all[rs, sl])).astype(o_ref.dtype)

        return [functools.partial(chunk_step, c) for c in range(cpb)] + [emit]

    blocks = range(n_blk)
    for b in blocks:
        wy_operands(b)
    for _ in range(n_dbl):
        for b in blocks:
            square_p(b)
        for b in blocks:
            extend_x(b)
    for b in blocks:
        wy_finish(b)
    for b in blocks:
        for task in recurrence_tasks(b):
            task()
    for h in range(N_HEADS):
        s_ref[h] = states[h]

    @pl.when(j == nj - 1)
    def _():
        st_ref[0] = s_ref[...]


def _gdn(proj, small, conv_w, conv_buf, s0, alog_row, dtb_row, gn_row, batch, seqlen):
    n = batch * seqlen
    t_blk = min(256, seqlen)
    chunk = min(CHUNK, seqlen)
    nj = seqlen // t_blk
    gw3 = 3 * GROUP_WIDTH
    kern = functools.partial(_gdn_kernel, t_blk=t_blk, chunk=chunk)
    return pl.pallas_call(
        kern,
        grid=(batch, nj),
        in_specs=[
            pl.BlockSpec((t_blk, gw3), lambda b, j: (b * nj + j, COL_QA // gw3)),
            pl.BlockSpec((t_blk, GROUP_WIDTH), lambda b, j: (b * nj + j, COL_ZA // GROUP_WIDTH)),
            pl.BlockSpec((t_blk, SMALL_COLS), lambda b, j: (b * nj + j, 0)),
            pl.BlockSpec((CONV_W, gw3), lambda b, j: (0, 0)),
            pl.BlockSpec((1, CONV_W - 1, gw3), lambda b, j: (b, 0, 0)),
            pl.BlockSpec((1, N_HEADS, HEAD_DIM, HEAD_DIM), lambda b, j: (b, 0, 0, 0)),
            pl.BlockSpec((1, SMALL_COLS), lambda b, j: (0, 0)),
            pl.BlockSpec((1, SMALL_COLS), lambda b, j: (0, 0)),
            pl.BlockSpec((1, HEAD_DIM), lambda b, j: (0, 0)),
        ],
        out_specs=[
            pl.BlockSpec((t_blk, GROUP_WIDTH), lambda b, j: (b * nj + j, 0)),
            pl.BlockSpec((1, CONV_W - 1, gw3), lambda b, j: (b, 0, 0)),
            pl.BlockSpec((1, N_HEADS, HEAD_DIM, HEAD_DIM), lambda b, j: (b, 0, 0, 0)),
        ],
        out_shape=[
            jax.ShapeDtypeStruct((n, GROUP_WIDTH), BF16),
            jax.ShapeDtypeStruct((batch, CONV_W - 1, gw3), F32),
            jax.ShapeDtypeStruct((batch, N_HEADS, HEAD_DIM, HEAD_DIM), F32),
        ],
        scratch_shapes=[pltpu.VMEM((t_blk + 8, gw3), F32),
                        pltpu.VMEM((N_HEADS, HEAD_DIM, HEAD_DIM), F32)],
        compiler_params=_cparams(("parallel", "arbitrary")),
        name="gdn",
    )(proj, proj, small, conv_w, conv_buf, s0, alog_row, dtb_row, gn_row)


def _ret_kernel(q_ref, k_ref, v_ref, z_ref, cos_ref, sin_ref, s0_ref, g_ref, b_ref,
                o_ref, st_ref, s_ref, *, t_blk):
    j = pl.program_id(1)
    nj = pl.num_programs(1)

    @pl.when(j == 0)
    def _():
        s_ref[...] = s0_ref[0]

    cos = cos_ref[...]
    sin = sin_ref[...]
    ri = lax.broadcasted_iota(I32, (t_blk, t_blk), 0)
    ci = lax.broadcasted_iota(I32, (t_blk, t_blk), 1)
    causal = ci <= ri
    rel = jnp.where(causal, ri - ci, 0).astype(F32)
    idx_col = lax.broadcasted_iota(I32, (t_blk, 1), 0).astype(F32)
    gamma_g = g_ref[...]
    gamma_b = b_ref[...]
    q_all, k_all, v_all, z_all = q_ref[...], k_ref[...], v_ref[...], z_ref[...]

    for h in range(N_HEADS):
        sl = slice(h * HEAD_DIM, (h + 1) * HEAD_DIM)
        lg = math.log(1.0 - 2.0 ** (-5.0 - h))
        qh, kh, vh = q_all[:, sl], k_all[:, sl], v_all[:, sl]
        qh = qh * cos + pltpu.roll(qh, HEAD_DIM // 2, 1) * sin
        kh = (kh * cos + pltpu.roll(kh, HEAD_DIM // 2, 1) * sin) * (HEAD_DIM ** -0.5)
        d_mat = jnp.where(causal, jnp.exp(lg * rel), 0.0)
        q16, k16, v16 = qh.astype(BF16), kh.astype(BF16), vh.astype(BF16)
        o_intra = _dot((_dot_nt(q16, k16) * d_mat).astype(BF16), v16)
        s = s_ref[h]
        o_cross = _dot(q16, s.astype(BF16)) * jnp.exp(lg * (idx_col + 1.0))
        kd_t = (kh * jnp.exp(lg * (t_blk - 1.0 - idx_col))).T.astype(BF16)
        s_ref[h] = s * math.exp(lg * t_blk) + _dot(kd_t, v16)
        o = o_intra + o_cross
        mu = jnp.mean(o, axis=-1, keepdims=True)
        oc = o - mu
        var = jnp.mean(oc * oc, axis=-1, keepdims=True)
        on = oc * lax.rsqrt(var + EPS) * gamma_g + gamma_b
        o_ref[:, sl] = (on * _silu(z_all[:, sl])).astype(o_ref.dtype)

    @pl.when(j == nj - 1)
    def _():
        st_ref[0] = s_ref[...]


def _retention(proj, cos_t, sin_t, s0, g_row, b_row, batch, seqlen):
    n = batch * seqlen
    t_blk = min(256, seqlen)
    nj = seqlen // t_blk
    gw = GROUP_WIDTH
    kern = functools.partial(_ret_kernel, t_blk=t_blk)

    def col(c):
        return pl.BlockSpec((t_blk, gw), lambda b, j: (b * nj + j, c // gw))

    return pl.pallas_call(
        kern,
        grid=(batch, nj),
        in_specs=[
            col(COL_QB), col(COL_KB), col(COL_VB), col(COL_ZB),
            pl.BlockSpec((t_blk, HEAD_DIM), lambda b, j: (j, 0)),
            pl.BlockSpec((t_blk, HEAD_DIM), lambda b, j: (j, 0)),
            pl.BlockSpec((1, N_HEADS, HEAD_DIM, HEAD_DIM), lambda b, j: (b, 0, 0, 0)),
            pl.BlockSpec((1, HEAD_DIM), lambda b, j: (0, 0)),
            pl.BlockSpec((1, HEAD_DIM), lambda b, j: (0, 0)),
        ],
        out_specs=[
            pl.BlockSpec((t_blk, gw), lambda b, j: (b * nj + j, 0)),
            pl.BlockSpec((1, N_HEADS, HEAD_DIM, HEAD_DIM), lambda b, j: (b, 0, 0, 0)),
        ],
        out_shape=[
            jax.ShapeDtypeStruct((n, gw), BF16),
            jax.ShapeDtypeStruct((batch, N_HEADS, HEAD_DIM, HEAD_DIM), F32),
        ],
        scratch_shapes=[pltpu.VMEM((N_HEADS, HEAD_DIM, HEAD_DIM), F32)],
        compiler_params=_cparams(("parallel", "arbitrary")),
        name="retention",
    )(proj, proj, proj, proj, cos_t, sin_t, s0, g_row, b_row)


def _dsa_prep_kernel(k_ref, v_ref, sm_ref, gk_ref, gi_ref, *rest, n_prev, with_vt):
    outs = rest[n_prev:]
    ko_ref, vo_ref, kio_ref, k16_ref, ki16_ref = outs[:5]
    k = k_ref[...]
    tm = k.shape[0]
    gk = gk_ref[...]
    for h in range(N_HEADS):
        sl = slice(h * HEAD_DIM, (h + 1) * HEAD_DIM)
        kh = k[:, sl]
        kn = kh * lax.rsqrt(jnp.mean(kh * kh, axis=-1, keepdims=True) + EPS) * gk
        ko_ref[pl.ds(h, tm, stride=N_HEADS), :] = kn
        k16_ref[:, sl] = kn.astype(BF16)
    v = v_ref[...]
    for h in range(N_HEADS):
        vo_ref[pl.ds(h, tm, stride=N_HEADS), :] = v[:, h * HEAD_DIM:(h + 1) * HEAD_DIM]
    if with_vt:
        outs[5][...] = v.T.astype(BF16)
    ki = sm_ref[...][:, LANE_KI:LANE_KI + IDX_DIM]
    kin = ki * lax.rsqrt(jnp.mean(ki * ki, axis=-1, keepdims=True) + EPS) * gi_ref[...]
    kio_ref[...] = kin
    ki16_ref[...] = kin.astype(BF16)


def _dsa_prep(proj, small, gk_row, gi_row, layer, depth, prev, with_vt):
    n = proj.shape[0]
    tm = min(KT_A, n)
    gw = GROUP_WIDTH
    n_prev = 0 if prev is None else 3
    kern = functools.partial(_dsa_prep_kernel, n_prev=n_prev, with_vt=with_vt)
    in_specs = [
        pl.BlockSpec((tm, gw), lambda i: (i, COL_KC // gw)),
        pl.BlockSpec((tm, gw), lambda i: (i, COL_VC // gw)),
        pl.BlockSpec((tm, SMALL_COLS), lambda i: (i, 0)),
        pl.BlockSpec((1, HEAD_DIM), lambda i: (0, 0)),
        pl.BlockSpec((1, IDX_DIM), lambda i: (0, 0)),
    ] + [pl.BlockSpec(memory_space=pl.ANY)] * n_prev
    out_specs = [pl.BlockSpec((None, tm * N_HEADS, HEAD_DIM), lambda i: (layer, i, 0)),
                 pl.BlockSpec((None, tm * N_HEADS, HEAD_DIM), lambda i: (layer, i, 0)),
                 pl.BlockSpec((None, tm, IDX_DIM), lambda i: (layer, i, 0)),
                 pl.BlockSpec((tm, gw), lambda i: (i, 0)),
                 pl.BlockSpec((tm, IDX_DIM), lambda i: (i, 0))]
    out_shape = [jax.ShapeDtypeStruct((depth, n * N_HEADS, HEAD_DIM), F32),
                 jax.ShapeDtypeStruct((depth, n * N_HEADS, HEAD_DIM), F32),
                 jax.ShapeDtypeStruct((depth, n, IDX_DIM), F32),
                 jax.ShapeDtypeStruct((n, gw), BF16),
                 jax.ShapeDtypeStruct((n, IDX_DIM), BF16)]
    if with_vt:
        out_specs.append(pl.BlockSpec((None, gw, tm), lambda i: (i, 0, 0)))
        out_shape.append(jax.ShapeDtypeStruct((n // tm, gw, tm), BF16))
    args = (proj, proj, small, gk_row, gi_row) + (() if prev is None else tuple(prev))
    return pl.pallas_call(
        kern,
        grid=(n // tm,),
        in_specs=in_specs,
        out_specs=out_specs,
        out_shape=out_shape,
        input_output_aliases={5 + t: t for t in range(n_prev)},
        compiler_params=_cparams(("parallel",)),
        name="dsa_prep",
    )(*args)


KT_I = 128
KT_A = 512


def _dsa_kernel(q_ref, qi_ref, z_ref, sm_ref, k_ref, vt_ref, ki_ref, gq_ref, o_ref,
                key_ref, qit_ref, wt_ref, jlim_ref, qt_ref, acc_ref, lg_ref, p_ref, key16_ref, low16_ref, *, tq, offset, s_valid, topk):
    i = pl.program_id(1)
    pos0 = offset + i * tq
    t_pos = pos0 + lax.broadcasted_iota(I32, (1, tq), 1)
    t_chunk = t_pos // CHUNK
    n_adm_row = jnp.minimum((t_chunk + 1) * CHUNK, s_valid)
    n_keys = jnp.minimum(((pos0 + tq - 1) // CHUNK + 1) * CHUNK, s_valid)
    n_at = (n_keys + KT_A - 1) // KT_A
    n_sub = KT_A // KT_I

    qit_ref[...] = qi_ref[...].T.astype(BF16)
    w_t = sm_ref[...].T * (IDX_HEADS ** -0.5 * IDX_DIM ** -0.5)
    for h in range(IDX_HEADS):
        wt_ref[h] = jnp.broadcast_to(w_t[LANE_WI + h:LANE_WI + h + 1, :], (8, tq))

    def index_step(kt, carry):
        r0 = pl.multiple_of(kt * KT_I, KT_I)
        ki_t = ki_ref[0, pl.ds(r0, KT_I), :]
        acc = jnp.zeros((KT_I // 8, 8, tq), F32)
        for h in range(IDX_HEADS):
            sc = _dot(ki_t, qit_ref[h * IDX_DIM:(h + 1) * IDX_DIM, :])
            acc = acc + wt_ref[h][None] * jnp.maximum(sc, 0.0).reshape(KT_I // 8, 8, tq)
        bits = lax.bitcast_convert_type(acc.reshape(KT_I, tq), I32)
        key = bits ^ ((bits >> 31) & 0x7FFFFFFF)
        s_pos = r0 + lax.broadcasted_iota(I32, (KT_I, 1), 0)
        adm = (s_pos // CHUNK <= t_chunk) & (s_pos < s_valid)
        key = jnp.where(adm, key, INT_MIN)
        key_ref[pl.ds(r0, KT_I), :] = key
        key16_ref[pl.ds(r0, KT_I), :] = (key >> 16).astype(jnp.int16)
        return carry

    def index_trip(t, c):
        for s in range(n_sub):
            c = index_step(n_sub * t + s, c)
        return c

    lax.fori_loop(0, n_at, index_trip, 0)

    def count(pred_fn):
        def body(kt, accs):
            out = []
            for s in range(n_sub):
                r0 = pl.multiple_of(kt * KT_A + s * KT_I, KT_I)
                key = key_ref[pl.ds(r0, KT_I), :]
                s_pos = r0 + lax.broadcasted_iota(I32, (KT_I, 1), 0)
                m = jnp.where(pred_fn(key, s_pos), 1, 0).astype(I32)
                out.append(accs[s] + jnp.sum(m.reshape(KT_I // 8, 8, tq), axis=0))
            return tuple(out)
        accs = lax.fori_loop(0, n_at, body, tuple(jnp.zeros((8, tq), I32) for _ in range(n_sub)))
        return jnp.sum(sum(accs[1:], accs[0]), axis=0, keepdims=True)

    def count16(ref16, pred_fn):
        def body(kt, accs):
            out = []
            for s in range(n_sub):
                r0 = pl.multiple_of(kt * KT_A + s * KT_I, KT_I)
                m = jnp.where(pred_fn(ref16[pl.ds(r0, KT_I), :]), jnp.int16(1), jnp.int16(0))
                m = m.reshape(KT_I // 16, 16, tq)
                acc = accs[s]
                for q in range(KT_I // 16):
                    acc = acc + m[q]
                out.append(acc)
            return tuple(out)
        accs = lax.fori_loop(0, n_at, body, tuple(jnp.zeros((16, tq), jnp.int16) for _ in range(n_sub)))
        tot = sum((a.astype(I32) for a in accs[1:]), accs[0].astype(I32))
        return jnp.sum(tot, axis=0, keepdims=True)

    small = jnp.where(n_adm_row <= topk, 1, 0).astype(I32)

    def all_done(done):
        return jnp.min(done.astype(F32)) > 0.0

    def search(first_bit, last_bit, count_ge, carry, early_exit):
        def step(b, c):
            v, done, thr = c
            cand_u = v | jnp.left_shift(jnp.int32(1), 31 - b)
            cand_s = cand_u ^ INT_MIN
            cnt = count_ge(cand_s)
            v = jnp.where(cnt >= topk, cand_u, v)
            newly = (cnt == topk) & (done == 0)
            thr = jnp.where(newly, cand_s, thr)
            return v, jnp.where(newly, 1, done), thr

        if not early_exit:
            return lax.fori_loop(first_bit, last_bit, step, carry)

        def cond(c):
            return (c[0] < last_bit) & jnp.logical_not(all_done(c[1][1]))

        def body(c):
            b, inner = c
            return b + 4, lax.fori_loop(b, b + 4, step, inner)

        assert (last_bit - first_bit) % 4 == 0
        return lax.while_loop(cond, body, (jnp.int32(first_bit), carry))[1]

    v0 = jnp.zeros((1, tq), I32)
    thr0 = jnp.full((1, tq), INT_MIN + 1, I32)
    carry = search(0, 16, lambda cand_s: count16(key16_ref, lambda k16: k16 >= (cand_s >> 16).astype(jnp.int16)),
                   (v0, small, thr0), early_exit=False)
    hi16 = ((carry[0] ^ INT_MIN) >> 16).astype(jnp.int16)
    n_above = count16(key16_ref, lambda k16: k16 > hi16)

    @pl.when(jnp.logical_not(all_done(carry[1])))
    def _():
        hi32 = (carry[0] ^ INT_MIN) >> 16

        def low_step(kt, c):
            for s in range(n_sub):
                r0 = pl.multiple_of(kt * KT_A + s * KT_I, KT_I)
                key = key_ref[pl.ds(r0, KT_I), :]
                low = jnp.where((key >> 16) == hi32, (key & 0xFFFF) - 32768, -32768)
                low16_ref[pl.ds(r0, KT_I), :] = low.astype(jnp.int16)
            return c

        lax.fori_loop(0, n_at, low_step, 0)

    def count_low(cand_s):
        c16 = ((cand_s & 0xFFFF) - 32768).astype(jnp.int16)
        return n_above + count16(low16_ref, lambda l16: l16 >= c16)

    v_u, done_i, thr = search(16, 32, count_low, carry, early_exit=True)
    done = done_i != 0
    v_s = v_u ^ INT_MIN
    any_tie = jnp.logical_not(all_done(done_i))

    jlim_ref[...] = jnp.zeros(jlim_ref.shape, I32)

    @pl.when(any_tie)
    def _():
        n_gt = count(lambda key, s_pos: key > v_s)
        need = topk - n_gt
        pos_bits = max(int(math.ceil(math.log2(max(k_ref.shape[1], 2)))), 1) + 1

        def pos_step(b, jv):
            cand = jv | jnp.left_shift(jnp.int32(1), pos_bits - 1 - b)
            cnt = count(lambda key, s_pos: (key == v_s) & (s_pos < cand))
            return jnp.where(cnt <= need, cand, jv)

        jlim_ref[0:1, :] = lax.fori_loop(0, pos_bits, pos_step, jnp.zeros((1, tq), I32))

    j_lim = jlim_ref[0:1, :]

    v_eff = jnp.where(done, thr - 1, v_s)
    j_eff = jnp.where(done, 0, j_lim)

    def bias_loop(with_ties):
        def bias_step(kt, carry):
            for s in range(n_sub):
                r0 = pl.multiple_of(kt * KT_A + s * KT_I, KT_I)
                key = key_ref[pl.ds(r0, KT_I), :]
                sel = key > v_eff
                if with_ties:
                    s_pos = r0 + lax.broadcasted_iota(I32, (KT_I, 1), 0)
                    sel = sel | ((key == v_eff) & (s_pos < j_eff) & (key != INT_MIN))
                bias = jnp.where(sel, 0.0, NEG_BIG).astype(F32)
                key_ref[pl.ds(r0, KT_I), :] = lax.bitcast_convert_type(bias, I32)
            return carry

        lax.fori_loop(0, n_at, bias_step, 0)

    pl.when(any_tie)(lambda: bias_loop(True))
    pl.when(jnp.logical_not(any_tie))(lambda: bias_loop(False))

    gq = gq_ref[...]
    q_all = q_ref[...]
    for h in range(N_HEADS):
        sl = slice(h * HEAD_DIM, (h + 1) * HEAD_DIM)
        qh = q_all[:, sl]
        qh = qh * lax.rsqrt(jnp.mean(qh * qh, axis=-1, keepdims=True) + EPS) * gq * (HEAD_DIM ** -0.5 * LOG2_E)
        qt_ref[sl, :] = qh.T.astype(BF16)
    acc_ref[...] = jnp.zeros(acc_ref.shape, F32)

    heads = [slice(h * HEAD_DIM, (h + 1) * HEAD_DIM) for h in range(N_HEADS)]
    kt_h = KT_A // 2
    n_sub_h = kt_h // KT_I

    def issue_scores(kt, half):
        r0 = pl.multiple_of(kt * KT_A + half * kt_h, kt_h)
        for h, sl in enumerate(heads):
            lg_ref[half, h] = _dot(k_ref[0, pl.ds(r0, kt_h), sl], qt_ref[sl, :])

    def issue_values(kt, half):
        return [_dot(vt_ref[0, kt, sl, half * kt_h:(half + 1) * kt_h], p_ref[half, h])
                for h, sl in enumerate(heads)]

    def softmax_half(kt, half, ms, ls):
        r0 = pl.multiple_of(kt * KT_A + half * kt_h, kt_h)
        new_ms, new_ls, alphas = [], [], []
        for h in range(N_HEADS):
            mx = None
            for s in range(n_sub_h):
                rs = slice(s * KT_I, (s + 1) * KT_I)
                bias = lax.bitcast_convert_type(key_ref[pl.ds(r0 + s * KT_I, KT_I), :], F32)
                lg = lg_ref[half, h, rs, :] + bias
                lg_ref[half, h, rs, :] = lg
                part = jnp.max(lg.reshape(KT_I // 8, 8, tq), axis=0)
                mx = part if mx is None else jnp.maximum(mx, part)
            m_new = jnp.maximum(ms[h], jnp.max(mx, axis=0, keepdims=True))
            alphas.append(jnp.exp2(ms[h] - m_new))
            new_ms.append(m_new)
        for h in range(N_HEADS):
            lsum = None
            for s in range(n_sub_h):
                rs = slice(s * KT_I, (s + 1) * KT_I)
                p = jnp.exp2(lg_ref[half, h, rs, :] - new_ms[h])
                p_ref[half, h, rs, :] = p.astype(BF16)
                part = jnp.sum(p.reshape(KT_I // 8, 8, tq), axis=0)
                lsum = part if lsum is None else lsum + part
            new_ls.append(alphas[h] * ls[h] + jnp.sum(lsum, axis=0, keepdims=True))
        return new_ms, new_ls, alphas

    def accumulate(alphas, pv):
        for h in range(N_HEADS):
            acc_ref[h] = alphas[h] * acc_ref[h] + pv[h]

    def att_step(kt, carry):
        ms, ls = carry
        issue_scores(kt, 0)
        issue_scores(kt, 1)
        ms, ls, alphas_a = softmax_half(kt, 0, ms, ls)
        pv_a = issue_values(kt, 0)
        ms, ls, alphas_b = softmax_half(kt, 1, ms, ls)
        accumulate(alphas_a, pv_a)
        accumulate(alphas_b, issue_values(kt, 1))
        return tuple(ms), tuple(ls)

    m0 = tuple(jnp.full((1, tq), NEG_BIG, F32) for _ in range(N_HEADS))
    l0 = tuple(jnp.zeros((1, tq), F32) for _ in range(N_HEADS))
    _, ls = lax.fori_loop(0, n_at, att_step, (m0, l0))
    z_all = z_ref[...]
    for h in range(N_HEADS):
        sl = slice(h * HEAD_DIM, (h + 1) * HEAD_DIM)
        oh = (acc_ref[h] / ls[h]).T
        o_ref[:, sl] = (oh * _silu(z_all[:, sl])).astype(o_ref.dtype)


def _dsa(proj, small, k_all16, vt16, ki16, gq_row, batch, q_len, tq, offset, s_valid):
    n = batch * q_len
    nq = q_len // tq
    s_pad = k_all16.shape[1]
    topk = min(TOPK_MAX, s_valid // 4)
    gw = GROUP_WIDTH
    qi_w = IDX_HEADS * IDX_DIM
    kern = functools.partial(_dsa_kernel, tq=tq, offset=offset, s_valid=s_valid, topk=topk)
    once = pl.Buffered(1)
    return pl.pallas_call(
        kern,
        grid=(batch, nq),
        in_specs=[
            pl.BlockSpec((tq, gw), lambda b, i: (b * nq + i, COL_QC // gw)),
            pl.BlockSpec((tq, qi_w), lambda b, i: (b * nq + i, COL_QI // qi_w)),
            pl.BlockSpec((tq, gw), lambda b, i: (b * nq + i, COL_ZC // gw)),
            pl.BlockSpec((tq, SMALL_COLS), lambda b, i: (b * nq + i, 0)),
            pl.BlockSpec((1, s_pad, gw), lambda b, i: (b, 0, 0), pipeline_mode=once),
            pl.BlockSpec((1, s_pad // KT_A, gw, KT_A), lambda b, i: (b, 0, 0, 0), pipeline_mode=once),
            pl.BlockSpec((1, s_pad, IDX_DIM), lambda b, i: (b, 0, 0), pipeline_mode=once),
            pl.BlockSpec((1, HEAD_DIM), lambda b, i: (0, 0)),
        ],
        out_specs=pl.BlockSpec((tq, gw), lambda b, i: (b * nq + i, 0)),
        out_shape=jax.ShapeDtypeStruct((n, gw), BF16),
        scratch_shapes=[pltpu.VMEM((s_pad, tq), I32),
                        pltpu.VMEM((qi_w, tq), BF16),
                        pltpu.VMEM((IDX_HEADS, 8, tq), F32),
                        pltpu.VMEM((8, tq), I32),
                        pltpu.VMEM((gw, tq), BF16),
                        pltpu.VMEM((N_HEADS, HEAD_DIM, tq), F32),
                        pltpu.VMEM((2, N_HEADS, KT_A // 2, tq), F32),
                        pltpu.VMEM((2, N_HEADS, KT_A // 2, tq), BF16),
                        pltpu.VMEM((s_pad, tq), jnp.int16),
                        pltpu.VMEM((s_pad, tq), jnp.int16)],
        compiler_params=_cparams(("parallel", "arbitrary")),
        name="dsa",
    )(proj, proj, proj, small, k_all16, vt16, ki16, gq_row)


def _dsa_select_kernel(qi_ref, sm_ref, pki_ref, nki_ref, bias_ref, ki_ref, key_ref, jl_ref, *,
                       lq, offset, s_valid, topk):
    s_pad = ki_ref.shape[0]
    n_t = s_pad // KT_A
    ki_ref[0:offset, :] = pki_ref[0].astype(BF16)
    ki_ref[offset:s_pad, :] = jnp.zeros((s_pad - offset, IDX_DIM), BF16)
    ki_ref[offset:offset + lq, :] = nki_ref[...]
    qi = qi_ref[...]
    q2 = jnp.concatenate([qi[:, h * IDX_DIM:(h + 1) * IDX_DIM] for h in range(IDX_HEADS)], axis=0).astype(BF16)
    w = sm_ref[...] * (IDX_HEADS ** -0.5 * IDX_DIM ** -0.5)
    w_cols = [w[:, LANE_WI + h:LANE_WI + h + 1] for h in range(IDX_HEADS)]
    t_chunk = (offset + lax.broadcasted_iota(I32, (lq, 1), 0)) // CHUNK

    def index_step(j, carry):
        c0 = pl.multiple_of(j * KT_A, KT_A)
        sc = _dot_nt(q2, ki_ref[pl.ds(c0, KT_A), :])
        acc = jnp.zeros((lq, KT_A), F32)
        for h in range(IDX_HEADS):
            acc = acc + w_cols[h] * jnp.maximum(sc[h * lq:(h + 1) * lq, :], 0.0)
        bits = lax.bitcast_convert_type(acc, I32)
        key = bits ^ ((bits >> 31) & 0x7FFFFFFF)
        s_pos = c0 + lax.broadcasted_iota(I32, (1, KT_A), 1)
        adm = (s_pos // CHUNK <= t_chunk) & (s_pos < s_valid)
        key_ref[:, pl.ds(c0, KT_A)] = jnp.where(adm, key, INT_MIN)
        return carry

    lax.fori_loop(0, n_t, index_step, 0)

    def count(pred_fn):
        def body(j, acc):
            c0 = pl.multiple_of(j * KT_A, KT_A)
            s_pos = c0 + lax.broadcasted_iota(I32, (1, KT_A), 1)
            m = jnp.where(pred_fn(key_ref[:, pl.ds(c0, KT_A)], s_pos), 1, 0).astype(I32)
            for c in range(KT_A // 128):
                acc = acc + m[:, c * 128:(c + 1) * 128]
            return acc
        acc = lax.fori_loop(0, n_t, body, jnp.zeros((lq, 128), I32))
        return jnp.sum(acc, axis=1, keepdims=True)

    def bit_step(b, v):
        cand_u = v | jnp.left_shift(jnp.int32(1), 31 - b)
        cnt = count(lambda key, s_pos: key >= (cand_u ^ INT_MIN))
        return jnp.where(cnt >= topk, cand_u, v)

    v_s = lax.fori_loop(0, 32, bit_step, jnp.zeros((lq, 1), I32)) ^ INT_MIN
    n_gt = count(lambda key, s_pos: key > v_s)
    n_ge = count(lambda key, s_pos: key >= v_s)
    need = topk - n_gt
    tied = (n_ge != topk) & (v_s != INT_MIN)
    jl_ref[...] = jnp.full(jl_ref.shape, s_pad, I32)

    @pl.when(jnp.max(jnp.where(tied, 1.0, 0.0)) > 0.0)
    def _():
        pos_bits = max(int(math.ceil(math.log2(max(s_pad, 2)))), 1) + 1

        def pos_step(b, jv):
            cand = jv | jnp.left_shift(jnp.int32(1), pos_bits - 1 - b)
            cnt = count(lambda key, s_pos: (key == v_s) & (s_pos < cand))
            return jnp.where(cnt <= need, cand, jv)

        jl_ref[:, 0:1] = lax.fori_loop(0, pos_bits, pos_step, jnp.zeros((lq, 1), I32))

    j_lim = jnp.where(tied, jl_ref[:, 0:1], s_pad)

    def bias_step(j, carry):
        c0 = pl.multiple_of(j * KT_A, KT_A)
        key = key_ref[:, pl.ds(c0, KT_A)]
        s_pos = c0 + lax.broadcasted_iota(I32, (1, KT_A), 1)
        sel = (key > v_s) | ((key == v_s) & (s_pos < j_lim) & (key != INT_MIN))
        bias_ref[0, :, pl.ds(c0, KT_A)] = jnp.where(sel, 0.0, NEG_BIG).astype(F32)
        return carry

    lax.fori_loop(0, n_t, bias_step, 0)


def _dsa_select(proj, small, past_ki, ki16, batch, lq, offset, layer):
    s_valid = offset + lq
    s_pad = _round_up(s_valid, KT_A)
    topk = min(TOPK_MAX, s_valid // 4)
    qi_w = IDX_HEADS * IDX_DIM
    kern = functools.partial(_dsa_select_kernel, lq=lq, offset=offset, s_valid=s_valid, topk=topk)
    return pl.pallas_call(
        kern,
        grid=(batch,),
        in_specs=[
            pl.BlockSpec((lq, qi_w), lambda b: (b, COL_QI // qi_w)),
            pl.BlockSpec((lq, SMALL_COLS), lambda b: (b, 0)),
            pl.BlockSpec((None, 1, offset, IDX_DIM), lambda b: (layer, b, 0, 0)),
            pl.BlockSpec((lq, IDX_DIM), lambda b: (b, 0)),
        ],
        out_specs=pl.BlockSpec((1, lq, s_pad), lambda b: (b, 0, 0)),
        out_shape=jax.ShapeDtypeStruct((batch, lq, s_pad), F32),
        scratch_shapes=[pltpu.VMEM((s_pad, IDX_DIM), BF16), pltpu.VMEM((lq, s_pad), I32),
                        pltpu.VMEM((lq, 128), I32)],
        compiler_params=_cparams(("parallel",)),
        name="dsa_select",
    )(proj, small, past_ki, ki16)


def _dsa_decode_kernel(q_ref, z_ref, bias_ref, pk_ref, pv_ref, nk_ref, nv_ref, gq_ref, o_ref,
                       qn_ref, m_ref, l_ref, acc_ref, *, lq, n_cache_tiles):
    j = pl.program_id(1)

    @pl.when(j == 0)
    def _():
        q_all = q_ref[...]
        gq = gq_ref[...]
        for h in range(N_HEADS):
            qh = q_all[:, h * HEAD_DIM:(h + 1) * HEAD_DIM]
            qh = qh * lax.rsqrt(jnp.mean(qh * qh, axis=-1, keepdims=True) + EPS) * gq * (HEAD_DIM ** -0.5 * LOG2_E)
            qn_ref[h] = qh.astype(BF16)
        m_ref[...] = jnp.full(m_ref.shape, NEG_BIG, F32)
        l_ref[...] = jnp.zeros(l_ref.shape, F32)
        acc_ref[...] = jnp.zeros(acc_ref.shape, F32)

    def attend(k_heads, v_heads, bias):
        for h in range(N_HEADS):
            logit = _dot_nt(qn_ref[h], k_heads[h]) + bias
            m_old = m_ref[h]
            m_new = jnp.maximum(m_old, jnp.max(logit, axis=1, keepdims=True))
            alpha = jnp.exp2(m_old - m_new)
            p = jnp.exp2(logit - m_new[:, 0:1])
            l_ref[h] = alpha * l_ref[h] + jnp.sum(p, axis=1, keepdims=True)
            acc_ref[h] = alpha * acc_ref[h] + _dot(p.astype(BF16), v_heads[h])
            m_ref[h] = m_new

    @pl.when(j < n_cache_tiles)
    def _():
        attend([pk_ref[0, pl.ds(h, KT_A, stride=N_HEADS), :].astype(BF16) for h in range(N_HEADS)],
               [pv_ref[0, pl.ds(h, KT_A, stride=N_HEADS), :].astype(BF16) for h in range(N_HEADS)], bias_ref[0])

    @pl.when(j == n_cache_tiles)
    def _():
        pad = jnp.zeros((128 - lq, HEAD_DIM), BF16)
        nk = nk_ref[...]
        attend([jnp.concatenate([nk[:, h * HEAD_DIM:(h + 1) * HEAD_DIM], pad], axis=0) for h in range(N_HEADS)],
               [jnp.concatenate([nv_ref[pl.ds(h, lq, stride=N_HEADS), :].astype(BF16), pad], axis=0)
                for h in range(N_HEADS)],
               bias_ref[0][:, 0:128])
        z_all = z_ref[...]
        for h in range(N_HEADS):
            sl = slice(h * HEAD_DIM, (h + 1) * HEAD_DIM)
            o_ref[:, sl] = (acc_ref[h] / l_ref[h] * _silu(z_all[:, sl])).astype(o_ref.dtype)


def _dsa_decode(proj, bias, past_k, past_v, k16, v_new, gq_row, batch, lq, offset, layer):
    gw = GROUP_WIDTH
    n_cache_tiles = offset // KT_A
    kern = functools.partial(_dsa_decode_kernel, lq=lq, n_cache_tiles=n_cache_tiles)
    last = n_cache_tiles - 1

    def rows(c):
        return c.reshape(c.shape[0], c.shape[1], offset * N_HEADS, HEAD_DIM)
    return pl.pallas_call(
        kern,
        grid=(batch, n_cache_tiles + 1),
        in_specs=[
            pl.BlockSpec((lq, gw), lambda b, j: (b, COL_QC // gw)),
            pl.BlockSpec((lq, gw), lambda b, j: (b, COL_ZC // gw)),
            pl.BlockSpec((1, lq, KT_A), lambda b, j: (b, 0, j)),
            pl.BlockSpec((None, 1, KT_A * N_HEADS, HEAD_DIM), lambda b, j: (layer, b, jnp.minimum(j, last), 0)),
            pl.BlockSpec((None, 1, KT_A * N_HEADS, HEAD_DIM), lambda b, j: (layer, b, jnp.minimum(j, last), 0)),
            pl.BlockSpec((lq, gw), lambda b, j: (b, 0)),
            pl.BlockSpec((None, lq * N_HEADS, HEAD_DIM), lambda b, j: (layer, b, 0)),
            pl.BlockSpec((1, HEAD_DIM), lambda b, j: (0, 0)),
        ],
        out_specs=pl.BlockSpec((lq, gw), lambda b, j: (b, 0)),
        out_shape=jax.ShapeDtypeStruct((batch * lq, gw), BF16),
        scratch_shapes=[pltpu.VMEM((N_HEADS, lq, HEAD_DIM), BF16),
                        pltpu.VMEM((N_HEADS, lq, HEAD_DIM), F32),
                        pltpu.VMEM((N_HEADS, lq, HEAD_DIM), F32),
                        pltpu.VMEM((N_HEADS, lq, HEAD_DIM), F32)],
        compiler_params=_cparams(("parallel", "arbitrary")),
        name="dsa_decode",
    )(proj, proj, bias, rows(past_k), rows(past_v), k16, v_new, gq_row)


def _mem_kernel(q_ref, z_ref, mk_ref, mv_ref, gq_ref, o_ref):
    gq = gq_ref[...]
    q_all, z_all = q_ref[...], z_ref[...]
    mk = mk_ref[0].astype(BF16)
    mv = mv_ref[0].astype(BF16)
    for h in range(N_HEADS):
        sl = slice(h * HEAD_DIM, (h + 1) * HEAD_DIM)
        qh = q_all[:, sl]
        qh = qh * lax.rsqrt(jnp.mean(qh * qh, axis=-1, keepdims=True) + EPS) * gq * (HEAD_DIM ** -0.5)
        logit = _dot_nt(qh.astype(BF16), mk[:, sl])
        m = jnp.max(logit, axis=-1, keepdims=True)
        p = jnp.exp(logit - m)
        l = jnp.sum(p, axis=-1, keepdims=True)
        oh = _dot(p.astype(BF16), mv[:, sl]) / l
        o_ref[:, sl] = (oh * _silu(z_all[:, sl])).astype(o_ref.dtype)


def _mem_attend(proj, mk, mv, gq_row, batch, seqlen):
    n = batch * seqlen
    tm = min(512, seqlen)
    nj = seqlen // tm
    gw = GROUP_WIDTH
    n_mem = mk.shape[1]
    return pl.pallas_call(
        _mem_kernel,
        grid=(batch, nj),
        in_specs=[
            pl.BlockSpec((tm, gw), lambda b, j: (b * nj + j, COL_QD // gw)),
            pl.BlockSpec((tm, gw), lambda b, j: (b * nj + j, COL_ZD // gw)),
            pl.BlockSpec((1, n_mem, gw), lambda b, j: (b, 0, 0)),
            pl.BlockSpec((1, n_mem, gw), lambda b, j: (b, 0, 0)),
            pl.BlockSpec((1, HEAD_DIM), lambda b, j: (0, 0)),
        ],
        out_specs=pl.BlockSpec((tm, gw), lambda b, j: (b * nj + j, 0)),
        out_shape=jax.ShapeDtypeStruct((n, gw), BF16),
        compiler_params=_cparams(("parallel", "arbitrary")),
        name="mem_attend",
    )(proj, proj, mk, mv, gq_row)


def _outproj_kernel(x_ref, a_ref, b_ref, c_ref, d_ref, w_ref, y_ref):
    gw = GROUP_WIDTH
    acc = x_ref[...] + _dot(a_ref[...], w_ref[0:gw, :])
    acc = acc + _dot(b_ref[...], w_ref[gw:2 * gw, :])
    acc = acc + _dot(c_ref[...], w_ref[2 * gw:3 * gw, :])
    acc = acc + _dot(d_ref[...], w_ref[3 * gw:4 * gw, :])
    y_ref[...] = acc


def _out_proj(x2d, oa, ob, oc, od, w16):
    n, d = x2d.shape
    tm = min(512, n)
    gw = GROUP_WIDTH
    grp = pl.BlockSpec((tm, gw), lambda i: (i, 0))
    return pl.pallas_call(
        _outproj_kernel,
        grid=(n // tm,),
        in_specs=[pl.BlockSpec((tm, d), lambda i: (i, 0)), grp, grp, grp, grp,
                  pl.BlockSpec((4 * gw, d), lambda i: (0, 0))],
        out_specs=pl.BlockSpec((tm, d), lambda i: (i, 0)),
        out_shape=jax.ShapeDtypeStruct((n, d), F32),
        compiler_params=_cparams(("parallel",)),
        name="out_proj",
    )(x2d, oa, ob, oc, od, w16)


_W_A_END = 4 * GROUP_WIDTH
_W_B_START = _W_A_END + 2 * N_HEADS
_W_B_END = _W_B_START + 8 * GROUP_WIDTH + IDX_HEADS * IDX_DIM
_W_D_START = _W_B_END + IDX_DIM + IDX_HEADS
_W_COLS = _W_D_START + 2 * GROUP_WIDTH


def _w_in_kernel(w_ref, m_ref, s_ref):
    m_ref[:, 0:_W_A_END] = w_ref[:, 0:_W_A_END].astype(BF16)
    m_ref[:, _W_A_END:_W_A_END + _W_B_END - _W_B_START] = w_ref[:, _W_B_START:_W_B_END].astype(BF16)
    m_ref[:, MAIN_COLS - 2 * GROUP_WIDTH:MAIN_COLS] = w_ref[:, _W_D_START:_W_COLS].astype(BF16)
    n_kw = IDX_DIM + IDX_HEADS
    s_ref[:, 0:n_kw] = w_ref[:, _W_B_END:_W_D_START].astype(BF16)
    s_ref[:, n_kw:n_kw + 2 * N_HEADS] = w_ref[:, _W_A_END:_W_B_START].astype(BF16)
    s_ref[:, n_kw + 2 * N_HEADS:] = jnp.zeros((w_ref.shape[0], SMALL_COLS - n_kw - 2 * N_HEADS), BF16)


def _prep_w_in(w_in):
    depth, d, cols = w_in.shape
    assert cols == _W_COLS and MAIN_COLS == _W_A_END + (_W_B_END - _W_B_START) + 2 * GROUP_WIDTH
    tm = 128
    return pl.pallas_call(
        _w_in_kernel,
        grid=(depth, d // tm),
        in_specs=[pl.BlockSpec((None, tm, cols), lambda l, i: (l, i, 0))],
        out_specs=[pl.BlockSpec((None, tm, MAIN_COLS), lambda l, i: (l, i, 0)),
                   pl.BlockSpec((None, tm, SMALL_COLS), lambda l, i: (l, i, 0))],
        out_shape=[jax.ShapeDtypeStruct((depth, d, MAIN_COLS), BF16),
                   jax.ShapeDtypeStruct((depth, d, SMALL_COLS), BF16)],
        compiler_params=_cparams(("parallel", "parallel")),
        name="w_in_prep",
    )(w_in)


def _lane_row(vals, lane0):
    row = jnp.zeros((1, SMALL_COLS), F32)
    return row.at[0, lane0:lane0 + vals.shape[0]].set(vals.astype(F32))


def _rope_tables(pos):
    half = HEAD_DIM // 2
    inv = ROPE_THETA ** (-jnp.arange(half, dtype=F32) / half)
    ang = pos.astype(F32)[:, None] * inv[None, :]
    cos, sin = jnp.cos(ang), jnp.sin(ang)
    return jnp.concatenate([cos, cos], axis=-1), jnp.concatenate([-sin, sin], axis=-1)


def _round_up(x, m):
    return (x + m - 1) // m * m


def _mixer_layer(x, conv_buf, s_gdn, s_ret, past_k, past_v, past_ki, mem_k, mem_v, wts, layer, depth, prev_cache):
    b, l, d = x.shape
    n = b * l
    offset = 0 if past_k is None else past_k.shape[2]
    x2d = x.reshape(n, d)
    proj, small = _in_proj(x2d, wts["norm_g"], wts["w_main"], wts["w_small"], layer)

    o_a, conv_new, s_gdn_new = _gdn(proj, small, wts["conv_w"], conv_buf, s_gdn, wts["alog_row"],
                                    wts["dtb_row"], wts["gdn_norm_g"], b, l)

    cos_t, sin_t = wts["rope_p"] if past_k is None else wts["rope_s"]
    o_b, s_ret_new = _retention(proj, cos_t, sin_t, s_ret, wts["ret_norm_g"], wts["ret_norm_b"], b, l)

    prefill = past_k is None
    prep = _dsa_prep(proj, small, wts["dsa_k_norm_g"], wts["idx_k_norm_g"], layer, depth, prev_cache, prefill)
    cache = tuple(prep[:3])
    if prefill:
        assert l % KT_A == 0, "prefill length must be a multiple of the key tile"
        o_c = _dsa(proj, small, prep[3].reshape(b, l, GROUP_WIDTH), prep[5].reshape(b, l // KT_A, GROUP_WIDTH, KT_A),
                   prep[4].reshape(b, l, IDX_DIM), wts["dsa_q_norm_g"], b, l, 256, 0, l)
    else:
        assert offset % KT_A == 0 and l % 16 == 0 and l <= 128, "decode step shape not supported"
        bias = _dsa_select(proj, small, past_ki, prep[4], b, l, offset, layer)
        o_c = _dsa_decode(proj, bias, past_k, past_v, prep[3], cache[1], wts["dsa_q_norm_g"], b, l, offset, layer)

    o_d = _mem_attend(proj, mem_k, mem_v, wts["mem_q_norm_g"], b, l)

    y = _out_proj(x2d, o_a, o_b, o_c, o_d, wts["w_out"]).reshape(b, l, d)
    return y, (conv_new, s_gdn_new, s_ret_new), cache


def kernel(x_prompt, x_sample, cache_gdn_conv, state_gdn, state_ret, cache_dsa_k, cache_dsa_v, cache_idx_k, cache_mem_k, cache_mem_v, mem_prompt, norm_g, w_in, gdn_conv_w, gdn_a_log, gdn_dt_bias, gdn_norm_g, ret_norm_g, ret_norm_b, dsa_q_norm_g, dsa_k_norm_g, idx_k_norm_g, mem_norm_g, w_mem_kv, mem_q_norm_g, mem_k_norm_g, w_out):
    depth = w_in.shape[0]
    b = x_prompt.shape[0]
    n_mem = mem_prompt.shape[1]
    d = x_prompt.shape[-1]
    y_p, y_s = x_prompt, x_sample
    st_p, st_s, mem_p = [], [], []
    cache_p = cache_s = None
    w_main, w_small = _prep_w_in(w_in)
    rope_p = _rope_tables(jnp.arange(x_prompt.shape[1], dtype=I32))
    rope_s = _rope_tables(cache_dsa_k.shape[2] + jnp.arange(x_sample.shape[1], dtype=I32))
    for li in range(depth):
        wts = dict(
            norm_g=norm_g[li][None, :], w_main=w_main, w_small=w_small, rope_p=rope_p, rope_s=rope_s,
            conv_w=gdn_conv_w[li],
            alog_row=_lane_row(gdn_a_log[li], LANE_ALPHA), dtb_row=_lane_row(gdn_dt_bias[li], LANE_ALPHA),
            gdn_norm_g=gdn_norm_g[li][None, :], ret_norm_g=ret_norm_g[li][None, :],
            ret_norm_b=ret_norm_b[li][None, :], dsa_q_norm_g=dsa_q_norm_g[li][None, :],
            dsa_k_norm_g=dsa_k_norm_g[li][None, :], idx_k_norm_g=idx_k_norm_g[li][None, :],
            mem_q_norm_g=mem_q_norm_g[li][None, :], w_out=w_out[li].astype(BF16),
        )
        mk, mv = _memory_kv(mem_prompt.reshape(b * n_mem, d), mem_norm_g[li][None, :],
                            w_mem_kv[li].astype(BF16), mem_k_norm_g[li][None, :])
        mk = mk.reshape(b, n_mem, GROUP_WIDTH)
        mv = mv.reshape(b, n_mem, GROUP_WIDTH)
        conv0 = jnp.zeros((b, CONV_W - 1, 3 * GROUP_WIDTH), F32)
        s0 = jnp.zeros((b, N_HEADS, HEAD_DIM, HEAD_DIM), F32)
        y_p, sp, cache_p = _mixer_layer(y_p, conv0, s0, s0, None, None, None, mk, mv, wts, li, depth, cache_p)
        st_p.append(sp)
        mem_p.append((mk.reshape(b, n_mem, N_HEADS, HEAD_DIM), mv.reshape(b, n_mem, N_HEADS, HEAD_DIM)))
        bs = x_sample.shape[0]
        y_s, ss, cache_s = _mixer_layer(y_s, cache_gdn_conv[li], state_gdn[li], state_ret[li],
                                        cache_dsa_k, cache_dsa_v, cache_idx_k,
                                        cache_mem_k[li].reshape(bs, n_mem, GROUP_WIDTH),
                                        cache_mem_v[li].reshape(bs, n_mem, GROUP_WIDTH), wts, li, depth, cache_s)
        st_s.append(ss)

    def stack(lst, k):
        return jnp.stack([s[k] for s in lst])

    def caches(c, bb, ll):
        return (c[0].reshape(depth, bb, ll, N_HEADS, HEAD_DIM), c[1].reshape(depth, bb, ll, N_HEADS, HEAD_DIM),
                c[2].reshape(depth, bb, ll, IDX_DIM))

    return ((y_p, y_s, stack(st_p, 0), stack(st_p, 1), stack(st_p, 2))
            + caches(cache_p, b, x_prompt.shape[1])
            + (stack(mem_p, 0), stack(mem_p, 1), stack(st_s, 0), stack(st_s, 1), stack(st_s, 2))
            + caches(cache_s, x_sample.shape[0], x_sample.shape[1]))
```

```python
import functools
import math

import jax
import jax.numpy as jnp
from jax import lax
from jax.experimental import pallas as pl
from jax.experimental.pallas import tpu as pltpu

F32 = jnp.float32
BF16 = jnp.bfloat16
I32 = jnp.int32

HEAD_DIM = 128
N_HEADS = 4
GROUP_WIDTH = N_HEADS * HEAD_DIM
CHUNK = 64
CONV_W = 4
IDX_HEADS = 16
IDX_DIM = 64
TOPK_MAX = 256
ROPE_THETA = 10000.0
EPS = 1e-6

COL_QA, COL_KA, COL_VA, COL_ZA = 0, 512, 1024, 1536
COL_QB, COL_KB, COL_VB, COL_ZB = 2048, 2560, 3072, 3584
COL_QC, COL_KC, COL_VC, COL_ZC = 4096, 4608, 5120, 5632
COL_QI = 6144
COL_QD, COL_ZD = 7168, 7680
MAIN_COLS = 8192
LANE_KI, LANE_WI, LANE_BETA, LANE_ALPHA = 0, 64, 80, 84
SMALL_COLS = 128

INT_MIN = -2 ** 31
NEG_BIG = -1e30
LOG2_E = 1.4426950408889634
VMEM_LIMIT = 56 * 1024 * 1024
HI = lax.Precision.HIGHEST


def _cparams(sem):
    return pltpu.CompilerParams(dimension_semantics=sem, vmem_limit_bytes=VMEM_LIMIT)


def _dot(a, b):
    return jnp.dot(a, b, preferred_element_type=F32)


def _dot_nt(a, b):
    return lax.dot_general(a, b, (((1,), (1,)), ((), ())), preferred_element_type=F32)


def _dot3(a, b):
    a_hi = a.astype(BF16)
    b_hi = b.astype(BF16)
    a_lo = (a - a_hi.astype(F32)).astype(BF16)
    b_lo = (b - b_hi.astype(F32)).astype(BF16)
    return _dot(a_hi, b_hi) + (_dot(a_hi, b_lo) + _dot(a_lo, b_hi))


def _silu(x):
    return x * jax.nn.sigmoid(x)


def _inproj_kernel(x_ref, g_ref, w_ref, ws_ref, o_ref, os_ref, h_ref):
    @pl.when(pl.program_id(1) == 0)
    def _():
        x = x_ref[...]
        y = x * lax.rsqrt(jnp.mean(x * x, axis=-1, keepdims=True) + EPS) * g_ref[...]
        hb = y.astype(BF16)
        h_ref[...] = hb
        os_ref[...] = _dot(hb, ws_ref[...])

    o_ref[...] = _dot(h_ref[...], w_ref[...])


def _in_proj(x2d, g_row, w_main, w_small, layer):
    n, d = x2d.shape
    tm = min(1024, n)
    tn = 1024
    return pl.pallas_call(
        _inproj_kernel,
        grid=(n // tm, MAIN_COLS // tn),
        in_specs=[
            pl.BlockSpec((tm, d), lambda i, j: (i, 0)),
            pl.BlockSpec((1, d), lambda i, j: (0, 0)),
            pl.BlockSpec((None, d, tn), lambda i, j: (layer, 0, j)),
            pl.BlockSpec((None, d, SMALL_COLS), lambda i, j: (layer, 0, 0)),
        ],
        out_specs=[
            pl.BlockSpec((tm, tn), lambda i, j: (i, j)),
            pl.BlockSpec((tm, SMALL_COLS), lambda i, j: (i, 0)),
        ],
        out_shape=[jax.ShapeDtypeStruct((n, MAIN_COLS), F32),
                   jax.ShapeDtypeStruct((n, SMALL_COLS), F32)],
        scratch_shapes=[pltpu.VMEM((tm, d), BF16)],
        compiler_params=_cparams(("parallel", "arbitrary")),
        name="in_proj",
    )(x2d, g_row, w_main, w_small)


def _memkv_kernel(x_ref, g_ref, w_ref, gk_ref, mk_ref, mv_ref):
    x = x_ref[...]
    y = x * lax.rsqrt(jnp.mean(x * x, axis=-1, keepdims=True) + EPS) * g_ref[...]
    kv = _dot(y.astype(BF16), w_ref[...])
    gk = gk_ref[...]
    for h in range(N_HEADS):
        sl = slice(h * HEAD_DIM, (h + 1) * HEAD_DIM)
        kh = kv[:, sl]
        mk_ref[:, sl] = kh * lax.rsqrt(jnp.mean(kh * kh, axis=-1, keepdims=True) + EPS) * gk
    mv_ref[...] = kv[:, GROUP_WIDTH:]


def _memory_kv(mem2d, g_row, w_kv, gk_row):
    n, d = mem2d.shape
    tm = min(256, n)
    return pl.pallas_call(
        _memkv_kernel,
        grid=(n // tm,),
        in_specs=[
            pl.BlockSpec((tm, d), lambda i: (i, 0)),
            pl.BlockSpec((1, d), lambda i: (0, 0)),
            pl.BlockSpec((d, 2 * GROUP_WIDTH), lambda i: (0, 0)),
            pl.BlockSpec((1, HEAD_DIM), lambda i: (0, 0)),
        ],
        out_specs=[pl.BlockSpec((tm, GROUP_WIDTH), lambda i: (i, 0)),
                   pl.BlockSpec((tm, GROUP_WIDTH), lambda i: (i, 0))],
        out_shape=[jax.ShapeDtypeStruct((n, GROUP_WIDTH), F32),
                   jax.ShapeDtypeStruct((n, GROUP_WIDTH), F32)],
        compiler_params=_cparams(("parallel",)),
        name="memory_kv",
    )(mem2d, g_row, w_kv, gk_row)


def _gdn_kernel(qkv_ref, z_ref, sm_ref, cw_ref, cb_ref, s0_ref, alog_ref, dtb_ref, gn_ref,
                o_ref, conv_ref, st_ref, xbuf_ref, s_ref, *, t_blk, chunk):
    j = pl.program_id(1)
    nj = pl.num_programs(1)
    gw3 = 3 * GROUP_WIDTH

    @pl.when(j == 0)
    def _():
        xbuf_ref[0:8, :] = jnp.zeros((8, gw3), F32)
        xbuf_ref[5:8, :] = cb_ref[0]
        s_ref[...] = s0_ref[0]

    @pl.when(j > 0)
    def _():
        xbuf_ref[0:8, :] = xbuf_ref[t_blk:t_blk + 8, :]

    xbuf_ref[8:8 + t_blk, :] = qkv_ref[...]
    conv_ref[0] = xbuf_ref[t_blk + 5:t_blk + 8, :]

    cw = cw_ref[...]
    x_all = xbuf_ref[...]
    x_prev = pltpu.roll(x_all, 1, 0)
    u = x_all * cw[3:4, :] + x_prev * cw[2:3, :]
    v = x_all * cw[1:2, :] + x_prev * cw[0:1, :]
    y = _silu((u + pltpu.roll(v, 2, 0))[8:8 + t_blk, :])

    sm = sm_ref[...]
    lane = lax.broadcasted_iota(I32, sm.shape, 1)
    beta_all = jax.nn.sigmoid(sm)
    xs = sm + dtb_ref[...]
    softplus = jnp.maximum(xs, 0.0) + jnp.log1p(jnp.exp(-jnp.abs(xs)))
    g_all = -jnp.exp(alog_ref[...]) * softplus
    g_all = jnp.where((lane >= LANE_ALPHA) & (lane < LANE_ALPHA + N_HEADS), g_all, 0.0)
    ri = lax.broadcasted_iota(I32, (t_blk, t_blk), 0)
    ci = lax.broadcasted_iota(I32, (t_blk, t_blk), 1)
    tri = jnp.where((ri // chunk == ci // chunk) & (ci <= ri), 1.0, 0.0).astype(F32)
    gcum = jnp.dot(tri, g_all, preferred_element_type=F32, precision=HI)
    gcum_t = gcum.T

    blk = min(2 * chunk, t_blk)
    n_blk = t_blk // blk
    cpb = blk // chunk
    rb = lax.broadcasted_iota(I32, (blk, blk), 0)
    cb = lax.broadcasted_iota(I32, (blk, blk), 1)
    same = (rb // chunk) == (cb // chunk)
    causal = same & (cb <= rb)
    strict = same & (cb < rb)
    eye = jnp.where(rb == cb, 1.0, 0.0).astype(F32)
    n_dbl = max(int(math.log2(chunk)) - 1, 0)
    gn = gn_ref[...]
    z_all = z_ref[...]

    qn, kn, vv = [], [], []
    for h in range(N_HEADS):
        qh = y[:, COL_QA + h * HEAD_DIM:COL_QA + (h + 1) * HEAD_DIM]
        kh = y[:, COL_KA + h * HEAD_DIM:COL_KA + (h + 1) * HEAD_DIM]
        qn.append(qh * lax.rsqrt(jnp.sum(qh * qh, axis=-1, keepdims=True) + EPS) * (HEAD_DIM ** -0.5))
        kn.append(kh * lax.rsqrt(jnp.sum(kh * kh, axis=-1, keepdims=True) + EPS))
        vv.append(y[:, COL_VA + h * HEAD_DIM:COL_VA + (h + 1) * HEAD_DIM])

    qs, ks, gcols, p_mats, x_mats, qk_mats, vb_mats, kg_mats, u_mats, w_mats = ({} for _ in range(10))

    def wy_operands(b):
        rs = slice(b * blk, (b + 1) * blk)
        for h in range(N_HEADS):
            gcol = gcum[rs, LANE_ALPHA + h:LANE_ALPHA + h + 1]
            grow = gcum_t[LANE_ALPHA + h:LANE_ALPHA + h + 1, rs]
            bcol = beta_all[rs, LANE_BETA + h:LANE_BETA + h + 1]
            decay = jnp.where(causal, jnp.exp(jnp.where(causal, gcol - grow, 0.0)), 0.0)
            qc, kc, vc = qn[h][rs], kn[h][rs], vv[h][rs]
            kb = kc * bcol
            kc16 = kc.astype(BF16)
            a_mat = jnp.where(strict, _dot_nt(kb.astype(BF16), kc16) * decay, 0.0)
            qk_mats[h, b] = jnp.where(causal, _dot_nt(qc.astype(BF16), kc16) * decay, 0.0).astype(BF16)
            p_mats[h, b] = -a_mat
            x_mats[h, b] = eye - a_mat
            vb_mats[h, b] = (vc * bcol).astype(BF16)
            kg_mats[h, b] = (kb * jnp.exp(gcol)).astype(BF16)
            qs[h, b], ks[h, b], gcols[h, b] = qc, kc, gcol

    def square_p(b):
        for h in range(N_HEADS):
            p_mats[h, b] = _dot3(p_mats[h, b], p_mats[h, b])

    def extend_x(b):
        for h in range(N_HEADS):
            x_mats[h, b] = x_mats[h, b] + _dot3(x_mats[h, b], p_mats[h, b])

    def wy_finish(b):
        for h in range(N_HEADS):
            x16 = x_mats[h, b].astype(BF16)
            u_mats[h, b] = _dot(x16, vb_mats[h, b])
            w_mats[h, b] = _dot(x16, kg_mats[h, b]).astype(BF16)

    states = [s_ref[h] for h in range(N_HEADS)]

    def recurrence_tasks(b):
        o_state = {h: [] for h in range(N_HEADS)}
        v_new = {h: [] for h in range(N_HEADS)}

        def chunk_step(c):
            cs = slice(c * chunk, (c + 1) * chunk)
            for h in range(N_HEADS):
                s = states[h]
                s16 = s.astype(BF16)
                gcol = gcols[h, b][cs]
                glast = gcols[h, b][(c + 1) * chunk - 1:(c + 1) * chunk]
                vn = u_mats[h, b][cs] - _dot(w_mats[h, b][cs], s16)
                vn16 = vn.astype(BF16)
                kd_t = (ks[h, b][cs] * jnp.exp(glast - gcol)).T.astype(BF16)
                states[h] = s * jnp.exp(glast) + _dot(kd_t, vn16)
                o_state[h].append(_dot((qs[h, b][cs] * jnp.exp(gcol)).astype(BF16), s16))
                v_new[h].append(vn16)

        def emit():
            rs = slice(b * blk, (b + 1) * blk)
            for h in range(N_HEADS):
                sl = slice(h * HEAD_DIM, (h + 1) * HEAD_DIM)
                o = jnp.concatenate(o_state[h], axis=0) + _dot(qk_mats[h, b], jnp.concatenate(v_new[h], axis=0))
                on = o * lax.rsqrt(jnp.mean(o * o, axis=-1, keepdims=True) + EPS) * gn
                o_ref[rs, sl] = (on * _silu(z_all[rs, sl])).astype(o_ref.dtype)

        return [functools.partial(chunk_step, c) for c in range(cpb)] + [emit]

    blocks = range(n_blk)
    for b in blocks:
        wy_operands(b)
    for _ in range(n_dbl):
        for b in blocks:
            square_p(b)
        for b in blocks:
            extend_x(b)
    for b in blocks:
        wy_finish(b)
    for b in blocks:
        for task in recurrence_tasks(b):
            task()
    for h in range(N_HEADS):
        s_ref[h] = states[h]

    @pl.when(j == nj - 1)
    def _():
        st_ref[0] = s_ref[...]


def _gdn(proj, small, conv_w, conv_buf, s0, alog_row, dtb_row, gn_row, batch, seqlen):
    n = batch * seqlen
    t_blk = min(256, seqlen)
    chunk = min(CHUNK, seqlen)
    nj = seqlen // t_blk
    gw3 = 3 * GROUP_WIDTH
    kern = functools.partial(_gdn_kernel, t_blk=t_blk, chunk=chunk)
    return pl.pallas_call(
        kern,
        grid=(batch, nj),
        in_specs=[
            pl.BlockSpec((t_blk, gw3), lambda b, j: (b * nj + j, COL_QA // gw3)),
            pl.BlockSpec((t_blk, GROUP_WIDTH), lambda b, j: (b * nj + j, COL_ZA // GROUP_WIDTH)),
            pl.BlockSpec((t_blk, SMALL_COLS), lambda b, j: (b * nj + j, 0)),
            pl.BlockSpec((CONV_W, gw3), lambda b, j: (0, 0)),
            pl.BlockSpec((1, CONV_W - 1, gw3), lambda b, j: (b, 0, 0)),
            pl.BlockSpec((1, N_HEADS, HEAD_DIM, HEAD_DIM), lambda b, j: (b, 0, 0, 0)),
            pl.BlockSpec((1, SMALL_COLS), lambda b, j: (0, 0)),
            pl.BlockSpec((1, SMALL_COLS), lambda b, j: (0, 0)),
            pl.BlockSpec((1, HEAD_DIM), lambda b, j: (0, 0)),
        ],
        out_specs=[
            pl.BlockSpec((t_blk, GROUP_WIDTH), lambda b, j: (b * nj + j, 0)),
            pl.BlockSpec((1, CONV_W - 1, gw3), lambda b, j: (b, 0, 0)),
            pl.BlockSpec((1, N_HEADS, HEAD_DIM, HEAD_DIM), lambda b, j: (b, 0, 0, 0)),
        ],
        out_shape=[
            jax.ShapeDtypeStruct((n, GROUP_WIDTH), BF16),
            jax.ShapeDtypeStruct((batch, CONV_W - 1, gw3), F32),
            jax.ShapeDtypeStruct((batch, N_HEADS, HEAD_DIM, HEAD_DIM), F32),
        ],
        scratch_shapes=[pltpu.VMEM((t_blk + 8, gw3), F32),
                        pltpu.VMEM((N_HEADS, HEAD_DIM, HEAD_DIM), F32)],
        compiler_params=_cparams(("parallel", "arbitrary")),
        name="gdn",
    )(proj, proj, small, conv_w, conv_buf, s0, alog_row, dtb_row, gn_row)


def _ret_kernel(q_ref, k_ref, v_ref, z_ref, cos_ref, sin_ref, s0_ref, g_ref, b_ref,
                o_ref, st_ref, s_ref, *, t_blk):
    j = pl.program_id(1)
    nj = pl.num_programs(1)

    @pl.when(j == 0)
    def _():
        s_ref[...] = s0_ref[0]

    cos = cos_ref[...]
    sin = sin_ref[...]
    ri = lax.broadcasted_iota(I32, (t_blk, t_blk), 0)
    ci = lax.broadcasted_iota(I32, (t_blk, t_blk), 1)
    causal = ci <= ri
    rel = jnp.where(causal, ri - ci, 0).astype(F32)
    idx_col = lax.broadcasted_iota(I32, (t_blk, 1), 0).astype(F32)
    gamma_g = g_ref[...]
    gamma_b = b_ref[...]
    q_all, k_all, v_all, z_all = q_ref[...], k_ref[...], v_ref[...], z_ref[...]

    for h in range(N_HEADS):
        sl = slice(h * HEAD_DIM, (h + 1) * HEAD_DIM)
        lg = math.log(1.0 - 2.0 ** (-5.0 - h))
        qh, kh, vh = q_all[:, sl], k_all[:, sl], v_all[:, sl]
        qh = qh * cos + pltpu.roll(qh, HEAD_DIM // 2, 1) * sin
        kh = (kh * cos + pltpu.roll(kh, HEAD_DIM // 2, 1) * sin) * (HEAD_DIM ** -0.5)
        d_mat = jnp.where(causal, jnp.exp(lg * rel), 0.0)
        q16, k16, v16 = qh.astype(BF16), kh.astype(BF16), vh.astype(BF16)
        o_intra = _dot((_dot_nt(q16, k16) * d_mat).astype(BF16), v16)
        s = s_ref[h]
        o_cross = _dot(q16, s.astype(BF16)) * jnp.exp(lg * (idx_col + 1.0))
        kd_t = (kh * jnp.exp(lg * (t_blk - 1.0 - idx_col))).T.astype(BF16)
        s_ref[h] = s * math.exp(lg * t_blk) + _dot(kd_t, v16)
        o = o_intra + o_cross
        mu = jnp.mean(o, axis=-1, keepdims=True)
        oc = o - mu
        var = jnp.mean(oc * oc, axis=-1, keepdims=True)
        on = oc * lax.rsqrt(var + EPS) * gamma_g + gamma_b
        o_ref[:, sl] = (on * _silu(z_all[:, sl])).astype(o_ref.dtype)

    @pl.when(j == nj - 1)
    def _():
        st_ref[0] = s_ref[...]


def _retention(proj, cos_t, sin_t, s0, g_row, b_row, batch, seqlen):
    n = batch * seqlen
    t_blk = min(256, seqlen)
    nj = seqlen // t_blk
    gw = GROUP_WIDTH
    kern = functools.partial(_ret_kernel, t_blk=t_blk)

    def col(c):
        return pl.BlockSpec((t_blk, gw), lambda b, j: (b * nj + j, c // gw))

    return pl.pallas_call(
        kern,
        grid=(batch, nj),
        in_specs=[
            col(COL_QB), col(COL_KB), col(COL_VB), col(COL_ZB),
            pl.BlockSpec((t_blk, HEAD_DIM), lambda b, j: (j, 0)),
            pl.BlockSpec((t_blk, HEAD_DIM), lambda b, j: (j, 0)),
            pl.BlockSpec((1, N_HEADS, HEAD_DIM, HEAD_DIM), lambda b, j: (b, 0, 0, 0)),
            pl.BlockSpec((1, HEAD_DIM), lambda b, j: (0, 0)),
            pl.BlockSpec((1, HEAD_DIM), lambda b, j: (0, 0)),
        ],
        out_specs=[
            pl.BlockSpec((t_blk, gw), lambda b, j: (b * nj + j, 0)),
            pl.BlockSpec((1, N_HEADS, HEAD_DIM, HEAD_DIM), lambda b, j: (b, 0, 0, 0)),
        ],
        out_shape=[
            jax.ShapeDtypeStruct((n, gw), BF16),
            jax.ShapeDtypeStruct((batch, N_HEADS, HEAD_DIM, HEAD_DIM), F32),
        ],
        scratch_shapes=[pltpu.VMEM((N_HEADS, HEAD_DIM, HEAD_DIM), F32)],
        compiler_params=_cparams(("parallel", "arbitrary")),
        name="retention",
    )(proj, proj, proj, proj, cos_t, sin_t, s0, g_row, b_row)


def _dsa_prep_kernel(k_ref, v_ref, sm_ref, gk_ref, gi_ref, *rest, n_prev, with_vt):
    outs = rest[n_prev:]
    ko_ref, vo_ref, kio_ref, k16_ref, ki16_ref = outs[:5]
    k = k_ref[...]
    tm = k.shape[0]
    gk = gk_ref[...]
    for h in range(N_HEADS):
        sl = slice(h * HEAD_DIM, (h + 1) * HEAD_DIM)
        kh = k[:, sl]
        kn = kh * lax.rsqrt(jnp.mean(kh * kh, axis=-1, keepdims=True) + EPS) * gk
        ko_ref[pl.ds(h, tm, stride=N_HEADS), :] = kn
        k16_ref[:, sl] = kn.astype(BF16)
    v = v_ref[...]
    for h in range(N_HEADS):
        vo_ref[pl.ds(h, tm, stride=N_HEADS), :] = v[:, h * HEAD_DIM:(h + 1) * HEAD_DIM]
    if with_vt:
        outs[5][...] = v.T.astype(BF16)
    ki = sm_ref[...][:, LANE_KI:LANE_KI + IDX_DIM]
    kin = ki * lax.rsqrt(jnp.mean(ki * ki, axis=-1, keepdims=True) + EPS) * gi_ref[...]
    kio_ref[...] = kin
    ki16_ref[...] = kin.astype(BF16)


def _dsa_prep(proj, small, gk_row, gi_row, layer, depth, prev, with_vt):
    n = proj.shape[0]
    tm = min(KT_A, n)
    gw = GROUP_WIDTH
    n_prev = 0 if prev is None else 3
    kern = functools.partial(_dsa_prep_kernel, n_prev=n_prev, with_vt=with_vt)
    in_specs = [
        pl.BlockSpec((tm, gw), lambda i: (i, COL_KC // gw)),
        pl.BlockSpec((tm, gw), lambda i: (i, COL_VC // gw)),
        pl.BlockSpec((tm, SMALL_COLS), lambda i: (i, 0)),
        pl.BlockSpec((1, HEAD_DIM), lambda i: (0, 0)),
        pl.BlockSpec((1, IDX_DIM), lambda i: (0, 0)),
    ] + [pl.BlockSpec(memory_space=pl.ANY)] * n_prev
    out_specs = [pl.BlockSpec((None, tm * N_HEADS, HEAD_DIM), lambda i: (layer, i, 0)),
                 pl.BlockSpec((None, tm * N_HEADS, HEAD_DIM), lambda i: (layer, i, 0)),
                 pl.BlockSpec((None, tm, IDX_DIM), lambda i: (layer, i, 0)),
                 pl.BlockSpec((tm, gw), lambda i: (i, 0)),
                 pl.BlockSpec((tm, IDX_DIM), lambda i: (i, 0))]
    out_shape = [jax.ShapeDtypeStruct((depth, n * N_HEADS, HEAD_DIM), F32),
                 jax.ShapeDtypeStruct((depth, n * N_HEADS, HEAD_DIM), F32),
                 jax.ShapeDtypeStruct((depth, n, IDX_DIM), F32),
                 jax.ShapeDtypeStruct((n, gw), BF16),
                 jax.ShapeDtypeStruct((n, IDX_DIM), BF16)]
    if with_vt:
        out_specs.append(pl.BlockSpec((None, gw, tm), lambda i: (i, 0, 0)))
        out_shape.append(jax.ShapeDtypeStruct((n // tm, gw, tm), BF16))
    args = (proj, proj, small, gk_row, gi_row) + (() if prev is None else tuple(prev))
    return pl.pallas_call(
        kern,
        grid=(n // tm,),
        in_specs=in_specs,
        out_specs=out_specs,
        out_shape=out_shape,
        input_output_aliases={5 + t: t for t in range(n_prev)},
        compiler_params=_cparams(("parallel",)),
        name="dsa_prep",
    )(*args)


KT_I = 128
KT_A = 512


def _dsa_kernel(q_ref, qi_ref, z_ref, sm_ref, k_ref, vt_ref, ki_ref, gq_ref, o_ref,
                key_ref, qit_ref, wt_ref, jlim_ref, qt_ref, acc_ref, lg_ref, p_ref, key16_ref, low16_ref, *, tq, offset, s_valid, topk):
    i = pl.program_id(1)
    pos0 = offset + i * tq
    t_pos = pos0 + lax.broadcasted_iota(I32, (1, tq), 1)
    t_chunk = t_pos // CHUNK
    n_adm_row = jnp.minimum((t_chunk + 1) * CHUNK, s_valid)
    n_keys = jnp.minimum(((pos0 + tq - 1) // CHUNK + 1) * CHUNK, s_valid)
    n_at = (n_keys + KT_A - 1) // KT_A
    n_sub = KT_A // KT_I

    qit_ref[...] = qi_ref[...].T.astype(BF16)
    w_t = sm_ref[...].T * (IDX_HEADS ** -0.5 * IDX_DIM ** -0.5)
    for h in range(IDX_HEADS):
        wt_ref[h] = jnp.broadcast_to(w_t[LANE_WI + h:LANE_WI + h + 1, :], (8, tq))

    def index_step(kt, carry):
        r0 = pl.multiple_of(kt * KT_I, KT_I)
        ki_t = ki_ref[0, pl.ds(r0, KT_I), :]
        acc = jnp.zeros((KT_I // 8, 8, tq), F32)
        for h in range(IDX_HEADS):
            sc = _dot(ki_t, qit_ref[h * IDX_DIM:(h + 1) * IDX_DIM, :])
            acc = acc + wt_ref[h][None] * jnp.maximum(sc, 0.0).reshape(KT_I // 8, 8, tq)
        bits = lax.bitcast_convert_type(acc.reshape(KT_I, tq), I32)
        key = bits ^ ((bits >> 31) & 0x7FFFFFFF)
        s_pos = r0 + lax.broadcasted_iota(I32, (KT_I, 1), 0)
        adm = (s_pos // CHUNK <= t_chunk) & (s_pos < s_valid)
        key = jnp.where(adm, key, INT_MIN)
        key_ref[pl.ds(r0, KT_I), :] = key
        key16_ref[pl.ds(r0, KT_I), :] = (key >> 16).astype(jnp.int16)
        return carry

    def index_trip(t, c):
        for s in range(n_sub):
            c = index_step(n_sub * t + s, c)
        return c

    lax.fori_loop(0, n_at, index_trip, 0)

    def count(pred_fn):
        def body(kt, accs):
            out = []
            for s in range(n_sub):
                r0 = pl.multiple_of(kt * KT_A + s * KT_I, KT_I)
                key = key_ref[pl.ds(r0, KT_I), :]
                s_pos = r0 + lax.broadcasted_iota(I32, (KT_I, 1), 0)
                m = jnp.where(pred_fn(key, s_pos), 1, 0).astype(I32)
                out.append(accs[s] + jnp.sum(m.reshape(KT_I // 8, 8, tq), axis=0))
            return tuple(out)
        accs = lax.fori_loop(0, n_at, body, tuple(jnp.zeros((8, tq), I32) for _ in range(n_sub)))
        return jnp.sum(sum(accs[1:], accs[0]), axis=0, keepdims=True)

    def count16(ref16, pred_fn):
        def body(kt, accs):
            out = []
            for s in range(n_sub):
                r0 = pl.multiple_of(kt * KT_A + s * KT_I, KT_I)
                m = jnp.where(pred_fn(ref16[pl.ds(r0, KT_I), :]), jnp.int16(1), jnp.int16(0))
                m = m.reshape(KT_I // 16, 16, tq)
                acc = accs[s]
                for q in range(KT_I // 16):
                    acc = acc + m[q]
                out.append(acc)
            return tuple(out)
        accs = lax.fori_loop(0, n_at, body, tuple(jnp.zeros((16, tq), jnp.int16) for _ in range(n_sub)))
        tot = sum((a.astype(I32) for a in accs[1:]), accs[0].astype(I32))
        return jnp.sum(tot, axis=0, keepdims=True)

    small = jnp.where(n_adm_row <= topk, 1, 0).astype(I32)

    def all_done(done):
        return jnp.min(done.astype(F32)) > 0.0

    def search(first_bit, last_bit, count_ge, carry, early_exit):
        def step(b, c):
            v, done, thr, c_rej = c
            cand_u = v | jnp.left_shift(jnp.int32(1), 31 - b)
            cand_s = cand_u ^ INT_MIN
            cnt = count_ge(cand_s)
            accept = cnt >= topk
            v = jnp.where(accept, cand_u, v)
            newly = (cnt == topk) & (done == 0)
            thr = jnp.where(newly, cand_s, thr)
            return v, jnp.where(newly, 1, done), thr, jnp.where(accept, c_rej, cnt)

        if not early_exit:
            return lax.fori_loop(first_bit, last_bit, step, carry)

        def cond(c):
            return (c[0] < last_bit) & jnp.logical_not(all_done(c[1][1]))

        def body(c):
            b, inner = c
            return b + 4, lax.fori_loop(b, b + 4, step, inner)

        assert (last_bit - first_bit) % 4 == 0
        return lax.while_loop(cond, body, (jnp.int32(first_bit), carry))[1]

    v0 = jnp.zeros((1, tq), I32)
    thr0 = jnp.full((1, tq), INT_MIN + 1, I32)
    carry = search(0, 16, lambda cand_s: count16(key16_ref, lambda k16: k16 >= (cand_s >> 16).astype(jnp.int16)),
                   (v0, small, thr0, jnp.zeros((1, tq), I32)), early_exit=False)
    n_above = carry[3]

    @pl.when(jnp.logical_not(all_done(carry[1])))
    def _():
        hi32 = (carry[0] ^ INT_MIN) >> 16

        def low_step(kt, c):
            for s in range(n_sub):
                r0 = pl.multiple_of(kt * KT_A + s * KT_I, KT_I)
                key = key_ref[pl.ds(r0, KT_I), :]
                low = jnp.where((key >> 16) == hi32, (key & 0xFFFF) - 32768, -32768)
                low16_ref[pl.ds(r0, KT_I), :] = low.astype(jnp.int16)
            return c

        lax.fori_loop(0, n_at, low_step, 0)

    def count_low(cand_s):
        c16 = ((cand_s & 0xFFFF) - 32768).astype(jnp.int16)
        return n_above + count16(low16_ref, lambda l16: l16 >= c16)

    v_u, done_i, thr, _ = search(16, 32, count_low, carry, early_exit=True)
    done = done_i != 0
    v_s = v_u ^ INT_MIN
    any_tie = jnp.logical_not(all_done(done_i))

    jlim_ref[...] = jnp.zeros(jlim_ref.shape, I32)

    @pl.when(any_tie)
    def _():
        n_gt = count(lambda key, s_pos: key > v_s)
        need = topk - n_gt
        pos_bits = max(int(math.ceil(math.log2(max(k_ref.shape[1], 2)))), 1) + 1

        def pos_step(b, jv):
            cand = jv | jnp.left_shift(jnp.int32(1), pos_bits - 1 - b)
            cnt = count(lambda key, s_pos: (key == v_s) & (s_pos < cand))
            return jnp.where(cnt <= need, cand, jv)

        jlim_ref[0:1, :] = lax.fori_loop(0, pos_bits, pos_step, jnp.zeros((1, tq), I32))

    j_lim = jlim_ref[0:1, :]

    v_eff = jnp.where(done, thr - 1, v_s)
    j_eff = jnp.where(done, 0, j_lim)

    def bias_loop(with_ties):
        def bias_step(kt, carry):
            for s in range(n_sub):
                r0 = pl.multiple_of(kt * KT_A + s * KT_I, KT_I)
                key = key_ref[pl.ds(r0, KT_I), :]
                sel = key > v_eff
                if with_ties:
                    s_pos = r0 + lax.broadcasted_iota(I32, (KT_I, 1), 0)
                    sel = sel | ((key == v_eff) & (s_pos < j_eff) & (key != INT_MIN))
                bias = jnp.where(sel, 0.0, NEG_BIG).astype(F32)
                key_ref[pl.ds(r0, KT_I), :] = lax.bitcast_convert_type(bias, I32)
            return carry

        lax.fori_loop(0, n_at, bias_step, 0)

    pl.when(any_tie)(lambda: bias_loop(True))
    pl.when(jnp.logical_not(any_tie))(lambda: bias_loop(False))

    gq = gq_ref[...]
    q_all = q_ref[...]
    for h in range(N_HEADS):
        sl = slice(h * HEAD_DIM, (h + 1) * HEAD_DIM)
        qh = q_all[:, sl]
        qh = qh * lax.rsqrt(jnp.mean(qh * qh, axis=-1, keepdims=True) + EPS) * gq * (HEAD_DIM ** -0.5 * LOG2_E)
        qt_ref[sl, :] = qh.T.astype(BF16)
    acc_ref[...] = jnp.zeros(acc_ref.shape, F32)

    heads = [slice(h * HEAD_DIM, (h + 1) * HEAD_DIM) for h in range(N_HEADS)]
    kt_h = KT_A // 2
    n_sub_h = kt_h // KT_I

    def issue_scores(kt, half, slot):
        r0 = pl.multiple_of(kt * KT_A + half * kt_h, kt_h)
        for h, sl in enumerate(heads):
            lg_ref[slot, h] = _dot(k_ref[0, pl.ds(r0, kt_h), sl], qt_ref[sl, :])

    def issue_values(kt, half, slot):
        return [_dot(vt_ref[0, kt, sl, half * kt_h:(half + 1) * kt_h], p_ref[slot, h])
                for h, sl in enumerate(heads)]

    def softmax_half(kt, half, slot, ms, ls):
        r0 = pl.multiple_of(kt * KT_A + half * kt_h, kt_h)
        new_ms, new_ls, alphas = [], [], []
        for h in range(N_HEADS):
            mx = None
            for s in range(n_sub_h):
                rs = slice(s * KT_I, (s + 1) * KT_I)
                bias = lax.bitcast_convert_type(key_ref[pl.ds(r0 + s * KT_I, KT_I), :], F32)
                lg = lg_ref[slot, h, rs, :] + bias
                lg_ref[slot, h, rs, :] = lg
                part = jnp.max(lg.reshape(KT_I // 8, 8, tq), axis=0)
                mx = part if mx is None else jnp.maximum(mx, part)
            m_new = jnp.maximum(ms[h], jnp.max(mx, axis=0, keepdims=True))
            alphas.append(jnp.exp2(ms[h] - m_new))
            new_ms.append(m_new)
        for h in range(N_HEADS):
            lsum = None
            for s in range(n_sub_h):
                rs = slice(s * KT_I, (s + 1) * KT_I)
                p = jnp.exp2(lg_ref[slot, h, rs, :] - new_ms[h])
                p_ref[slot, h, rs, :] = p.astype(BF16)
                part = jnp.sum(p.reshape(KT_I // 8, 8, tq), axis=0)
                lsum = part if lsum is None else lsum + part
            new_ls.append(alphas[h] * ls[h] + jnp.sum(lsum, axis=0, keepdims=True))
        return new_ms, new_ls, alphas

    def accumulate(alphas, pv):
        for h in range(N_HEADS):
            acc_ref[h] = alphas[h] * acc_ref[h] + pv[h]

    def att_step(kt, carry):
        ms, ls = carry
        issue_scores(kt, 0, 0)
        issue_scores(kt, 1, 1)
        ms, ls, alphas_a = softmax_half(kt, 0, 0, ms, ls)
        pv_a = issue_values(kt, 0, 0)
        ms, ls, alphas_b = softmax_half(kt, 1, 1, ms, ls)
        accumulate(alphas_a, pv_a)
        accumulate(alphas_b, issue_values(kt, 1, 1))
        return tuple(ms), tuple(ls)

    m0 = tuple(jnp.full((1, tq), NEG_BIG, F32) for _ in range(N_HEADS))
    l0 = tuple(jnp.zeros((1, tq), F32) for _ in range(N_HEADS))
    _, ls = lax.fori_loop(0, n_at, att_step, (m0, l0))
    z_all = z_ref[...]
    for h in range(N_HEADS):
        sl = slice(h * HEAD_DIM, (h + 1) * HEAD_DIM)
        oh = (acc_ref[h] / ls[h]).T
        o_ref[:, sl] = (oh * _silu(z_all[:, sl])).astype(o_ref.dtype)


def _dsa(proj, small, k_all16, vt16, ki16, gq_row, batch, q_len, tq, offset, s_valid):
    n = batch * q_len
    nq = q_len // tq
    s_pad = k_all16.shape[1]
    topk = min(TOPK_MAX, s_valid // 4)
    gw = GROUP_WIDTH
    qi_w = IDX_HEADS * IDX_DIM
    kern = functools.partial(_dsa_kernel, tq=tq, offset=offset, s_valid=s_valid, topk=topk)
    once = pl.Buffered(1)
    return pl.pallas_call(
        kern,
        grid=(batch, nq),
        in_specs=[
            pl.BlockSpec((tq, gw), lambda b, i: (b * nq + i, COL_QC // gw)),
            pl.BlockSpec((tq, qi_w), lambda b, i: (b * nq + i, COL_QI // qi_w)),
            pl.BlockSpec((tq, gw), lambda b, i: (b * nq + i, COL_ZC // gw)),
            pl.BlockSpec((tq, SMALL_COLS), lambda b, i: (b * nq + i, 0)),
            pl.BlockSpec((1, s_pad, gw), lambda b, i: (b, 0, 0), pipeline_mode=once),
            pl.BlockSpec((1, s_pad // KT_A, gw, KT_A), lambda b, i: (b, 0, 0, 0), pipeline_mode=once),
            pl.BlockSpec((1, s_pad, IDX_DIM), lambda b, i: (b, 0, 0), pipeline_mode=once),
            pl.BlockSpec((1, HEAD_DIM), lambda b, i: (0, 0)),
        ],
        out_specs=pl.BlockSpec((tq, gw), lambda b, i: (b * nq + i, 0)),
        out_shape=jax.ShapeDtypeStruct((n, gw), BF16),
        scratch_shapes=[pltpu.VMEM((s_pad, tq), I32),
                        pltpu.VMEM((qi_w, tq), BF16),
                        pltpu.VMEM((IDX_HEADS, 8, tq), F32),
                        pltpu.VMEM((8, tq), I32),
                        pltpu.VMEM((gw, tq), BF16),
                        pltpu.VMEM((N_HEADS, HEAD_DIM, tq), F32),
                        pltpu.VMEM((2, N_HEADS, KT_A // 2, tq), F32),
                        pltpu.VMEM((2, N_HEADS, KT_A // 2, tq), BF16),
                        pltpu.VMEM((s_pad, tq), jnp.int16),
                        pltpu.VMEM((s_pad, tq), jnp.int16)],
        compiler_params=_cparams(("parallel", "arbitrary")),
        name="dsa",
    )(proj, proj, proj, small, k_all16, vt16, ki16, gq_row)


def _dsa_select_kernel(qi_ref, sm_ref, pki_ref, nki_ref, bias_ref, ki_ref, key_ref, jl_ref, *,
                       lq, offset, s_valid, topk):
    s_pad = ki_ref.shape[0]
    n_t = s_pad // KT_A
    ki_ref[0:offset, :] = pki_ref[0].astype(BF16)
    ki_ref[offset:s_pad, :] = jnp.zeros((s_pad - offset, IDX_DIM), BF16)
    ki_ref[offset:offset + lq, :] = nki_ref[...]
    qi = qi_ref[...]
    q2 = jnp.concatenate([qi[:, h * IDX_DIM:(h + 1) * IDX_DIM] for h in range(IDX_HEADS)], axis=0).astype(BF16)
    w = sm_ref[...] * (IDX_HEADS ** -0.5 * IDX_DIM ** -0.5)
    w_cols = [w[:, LANE_WI + h:LANE_WI + h + 1] for h in range(IDX_HEADS)]
    t_chunk = (offset + lax.broadcasted_iota(I32, (lq, 1), 0)) // CHUNK

    def index_step(j, carry):
        c0 = pl.multiple_of(j * KT_A, KT_A)
        sc = _dot_nt(q2, ki_ref[pl.ds(c0, KT_A), :])
        acc = jnp.zeros((lq, KT_A), F32)
        for h in range(IDX_HEADS):
            acc = acc + w_cols[h] * jnp.maximum(sc[h * lq:(h + 1) * lq, :], 0.0)
        bits = lax.bitcast_convert_type(acc, I32)
        key = bits ^ ((bits >> 31) & 0x7FFFFFFF)
        s_pos = c0 + lax.broadcasted_iota(I32, (1, KT_A), 1)
        adm = (s_pos // CHUNK <= t_chunk) & (s_pos < s_valid)
        key_ref[:, pl.ds(c0, KT_A)] = jnp.where(adm, key, INT_MIN)
        return carry

    lax.fori_loop(0, n_t, index_step, 0)

    def count(pred_fn):
        def body(j, acc):
            c0 = pl.multiple_of(j * KT_A, KT_A)
            s_pos = c0 + lax.broadcasted_iota(I32, (1, KT_A), 1)
            m = jnp.where(pred_fn(key_ref[:, pl.ds(c0, KT_A)], s_pos), 1, 0).astype(I32)
            for c in range(KT_A // 128):
                acc = acc + m[:, c * 128:(c + 1) * 128]
            return acc
        acc = lax.fori_loop(0, n_t, body, jnp.zeros((lq, 128), I32))
        return jnp.sum(acc, axis=1, keepdims=True)

    def bit_step(b, v):
        cand_u = v | jnp.left_shift(jnp.int32(1), 31 - b)
        cnt = count(lambda key, s_pos: key >= (cand_u ^ INT_MIN))
        return jnp.where(cnt >= topk, cand_u, v)

    v_s = lax.fori_loop(0, 32, bit_step, jnp.zeros((lq, 1), I32)) ^ INT_MIN
    n_gt = count(lambda key, s_pos: key > v_s)
    n_ge = count(lambda key, s_pos: key >= v_s)
    need = topk - n_gt
    tied = (n_ge != topk) & (v_s != INT_MIN)
    jl_ref[...] = jnp.full(jl_ref.shape, s_pad, I32)

    @pl.when(jnp.max(jnp.where(tied, 1.0, 0.0)) > 0.0)
    def _():
        pos_bits = max(int(math.ceil(math.log2(max(s_pad, 2)))), 1) + 1

        def pos_step(b, jv):
            cand = jv | jnp.left_shift(jnp.int32(1), pos_bits - 1 - b)
            cnt = count(lambda key, s_pos: (key == v_s) & (s_pos < cand))
            return jnp.where(cnt <= need, cand, jv)

        jl_ref[:, 0:1] = lax.fori_loop(0, pos_bits, pos_step, jnp.zeros((lq, 1), I32))

    j_lim = jnp.where(tied, jl_ref[:, 0:1], s_pad)

    def bias_step(j, carry):
        c0 = pl.multiple_of(j * KT_A, KT_A)
        key = key_ref[:, pl.ds(c0, KT_A)]
        s_pos = c0 + lax.broadcasted_iota(I32, (1, KT_A), 1)
        sel = (key > v_s) | ((key == v_s) & (s_pos < j_lim) & (key != INT_MIN))
        bias_ref[0, :, pl.ds(c0, KT_A)] = jnp.where(sel, 0.0, NEG_BIG).astype(F32)
        return carry

    lax.fori_loop(0, n_t, bias_step, 0)


def _dsa_select(proj, small, past_ki, ki16, batch, lq, offset, layer):
    s_valid = offset + lq
    s_pad = _round_up(s_valid, KT_A)
    topk = min(TOPK_MAX, s_valid // 4)
    qi_w = IDX_HEADS * IDX_DIM
    kern = functools.partial(_dsa_select_kernel, lq=lq, offset=offset, s_valid=s_valid, topk=topk)
    return pl.pallas_call(
        kern,
        grid=(batch,),
        in_specs=[
            pl.BlockSpec((lq, qi_w), lambda b: (b, COL_QI // qi_w)),
            pl.BlockSpec((lq, SMALL_COLS), lambda b: (b, 0)),
            pl.BlockSpec((None, 1, offset, IDX_DIM), lambda b: (layer, b, 0, 0)),
            pl.BlockSpec((lq, IDX_DIM), lambda b: (b, 0)),
        ],
        out_specs=pl.BlockSpec((1, lq, s_pad), lambda b: (b, 0, 0)),
        out_shape=jax.ShapeDtypeStruct((batch, lq, s_pad), F32),
        scratch_shapes=[pltpu.VMEM((s_pad, IDX_DIM), BF16), pltpu.VMEM((lq, s_pad), I32),
                        pltpu.VMEM((lq, 128), I32)],
        compiler_params=_cparams(("parallel",)),
        name="dsa_select",
    )(proj, small, past_ki, ki16)


def _dsa_decode_kernel(q_ref, z_ref, bias_ref, pk_ref, pv_ref, nk_ref, nv_ref, gq_ref, o_ref,
                       qn_ref, m_ref, l_ref, acc_ref, *, lq, n_cache_tiles):
    j = pl.program_id(1)

    @pl.when(j == 0)
    def _():
        q_all = q_ref[...]
        gq = gq_ref[...]
        for h in range(N_HEADS):
            qh = q_all[:, h * HEAD_DIM:(h + 1) * HEAD_DIM]
            qh = qh * lax.rsqrt(jnp.mean(qh * qh, axis=-1, keepdims=True) + EPS) * gq * (HEAD_DIM ** -0.5 * LOG2_E)
            qn_ref[h] = qh.astype(BF16)
        m_ref[...] = jnp.full(m_ref.shape, NEG_BIG, F32)
        l_ref[...] = jnp.zeros(l_ref.shape, F32)
        acc_ref[...] = jnp.zeros(acc_ref.shape, F32)

    def attend(k_heads, v_heads, bias):
        for h in range(N_HEADS):
            logit = _dot_nt(qn_ref[h], k_heads[h]) + bias
            m_old = m_ref[h]
            m_new = jnp.maximum(m_old, jnp.max(logit, axis=1, keepdims=True))
            alpha = jnp.exp2(m_old - m_new)
            p = jnp.exp2(logit - m_new[:, 0:1])
            l_ref[h] = alpha * l_ref[h] + jnp.sum(p, axis=1, keepdims=True)
            acc_ref[h] = alpha * acc_ref[h] + _dot(p.astype(BF16), v_heads[h])
            m_ref[h] = m_new

    @pl.when(j < n_cache_tiles)
    def _():
        attend([pk_ref[0, pl.ds(h, KT_A, stride=N_HEADS), :].astype(BF16) for h in range(N_HEADS)],
               [pv_ref[0, pl.ds(h, KT_A, stride=N_HEADS), :].astype(BF16) for h in range(N_HEADS)], bias_ref[0])

    @pl.when(j == n_cache_tiles)
    def _():
        pad = jnp.zeros((128 - lq, HEAD_DIM), BF16)
        nk = nk_ref[...]
        attend([jnp.concatenate([nk[:, h * HEAD_DIM:(h + 1) * HEAD_DIM], pad], axis=0) for h in range(N_HEADS)],
               [jnp.concatenate([nv_ref[pl.ds(h, lq, stride=N_HEADS), :].astype(BF16), pad], axis=0)
                for h in range(N_HEADS)],
               bias_ref[0][:, 0:128])
        z_all = z_ref[...]
        for h in range(N_HEADS):
            sl = slice(h * HEAD_DIM, (h + 1) * HEAD_DIM)
            o_ref[:, sl] = (acc_ref[h] / l_ref[h] * _silu(z_all[:, sl])).astype(o_ref.dtype)


def _dsa_decode(proj, bias, past_k, past_v, k16, v_new, gq_row, batch, lq, offset, layer):
    gw = GROUP_WIDTH
    n_cache_tiles = offset // KT_A
    kern = functools.partial(_dsa_decode_kernel, lq=lq, n_cache_tiles=n_cache_tiles)
    last = n_cache_tiles - 1

    def rows(c):
        return c.reshape(c.shape[0], c.shape[1], offset * N_HEADS, HEAD_DIM)
    return pl.pallas_call(
        kern,
        grid=(batch, n_cache_tiles + 1),
        in_specs=[
            pl.BlockSpec((lq, gw), lambda b, j: (b, COL_QC // gw)),
            pl.BlockSpec((lq, gw), lambda b, j: (b, COL_ZC // gw)),
            pl.BlockSpec((1, lq, KT_A), lambda b, j: (b, 0, j)),
            pl.BlockSpec((None, 1, KT_A * N_HEADS, HEAD_DIM), lambda b, j: (layer, b, jnp.minimum(j, last), 0)),
            pl.BlockSpec((None, 1, KT_A * N_HEADS, HEAD_DIM), lambda b, j: (layer, b, jnp.minimum(j, last), 0)),
            pl.BlockSpec((lq, gw), lambda b, j: (b, 0)),
            pl.BlockSpec((None, lq * N_HEADS, HEAD_DIM), lambda b, j: (layer, b, 0)),
            pl.BlockSpec((1, HEAD_DIM), lambda b, j: (0, 0)),
        ],
        out_specs=pl.BlockSpec((lq, gw), lambda b, j: (b, 0)),
        out_shape=jax.ShapeDtypeStruct((batch * lq, gw), BF16),
        scratch_shapes=[pltpu.VMEM((N_HEADS, lq, HEAD_DIM), BF16),
                        pltpu.VMEM((N_HEADS, lq, HEAD_DIM), F32),
                        pltpu.VMEM((N_HEADS, lq, HEAD_DIM), F32),
                        pltpu.VMEM((N_HEADS, lq, HEAD_DIM), F32)],
        compiler_params=_cparams(("parallel", "arbitrary")),
        name="dsa_decode",
    )(proj, proj, bias, rows(past_k), rows(past_v), k16, v_new, gq_row)


def _mem_kernel(q_ref, z_ref, mk_ref, mv_ref, gq_ref, o_ref):
    gq = gq_ref[...]
    q_all, z_all = q_ref[...], z_ref[...]
    mk = mk_ref[0].astype(BF16)
    mv = mv_ref[0].astype(BF16)
    for h in range(N_HEADS):
        sl = slice(h * HEAD_DIM, (h + 1) * HEAD_DIM)
        qh = q_all[:, sl]
        qh = qh * lax.rsqrt(jnp.mean(qh * qh, axis=-1, keepdims=True) + EPS) * gq * (HEAD_DIM ** -0.5)
        logit = _dot_nt(qh.astype(BF16), mk[:, sl])
        m = jnp.max(logit, axis=-1, keepdims=True)
        p = jnp.exp(logit - m)
        l = jnp.sum(p, axis=-1, keepdims=True)
        oh = _dot(p.astype(BF16), mv[:, sl]) / l
        o_ref[:, sl] = (oh * _silu(z_all[:, sl])).astype(o_ref.dtype)


def _mem_attend(proj, mk, mv, gq_row, batch, seqlen):
    n = batch * seqlen
    tm = min(512, seqlen)
    nj = seqlen // tm
    gw = GROUP_WIDTH
    n_mem = mk.shape[1]
    return pl.pallas_call(
        _mem_kernel,
        grid=(batch, nj),
        in_specs=[
            pl.BlockSpec((tm, gw), lambda b, j: (b * nj + j, COL_QD // gw)),
            pl.BlockSpec((tm, gw), lambda b, j: (b * nj + j, COL_ZD // gw)),
            pl.BlockSpec((1, n_mem, gw), lambda b, j: (b, 0, 0)),
            pl.BlockSpec((1, n_mem, gw), lambda b, j: (b, 0, 0)),
            pl.BlockSpec((1, HEAD_DIM), lambda b, j: (0, 0)),
        ],
        out_specs=pl.BlockSpec((tm, gw), lambda b, j: (b * nj + j, 0)),
        out_shape=jax.ShapeDtypeStruct((n, gw), BF16),
        compiler_params=_cparams(("parallel", "arbitrary")),
        name="mem_attend",
    )(proj, proj, mk, mv, gq_row)


def _outproj_kernel(x_ref, a_ref, b_ref, c_ref, d_ref, w_ref, y_ref):
    gw = GROUP_WIDTH
    acc = x_ref[...] + _dot(a_ref[...], w_ref[0:gw, :])
    acc = acc + _dot(b_ref[...], w_ref[gw:2 * gw, :])
    acc = acc + _dot(c_ref[...], w_ref[2 * gw:3 * gw, :])
    acc = acc + _dot(d_ref[...], w_ref[3 * gw:4 * gw, :])
    y_ref[...] = acc


def _out_proj(x2d, oa, ob, oc, od, w16):
    n, d = x2d.shape
    tm = min(512, n)
    gw = GROUP_WIDTH
    grp = pl.BlockSpec((tm, gw), lambda i: (i, 0))
    return pl.pallas_call(
        _outproj_kernel,
        grid=(n // tm,),
        in_specs=[pl.BlockSpec((tm, d), lambda i: (i, 0)), grp, grp, grp, grp,
                  pl.BlockSpec((4 * gw, d), lambda i: (0, 0))],
        out_specs=pl.BlockSpec((tm, d), lambda i: (i, 0)),
        out_shape=jax.ShapeDtypeStruct((n, d), F32),
        compiler_params=_cparams(("parallel",)),
        name="out_proj",
    )(x2d, oa, ob, oc, od, w16)


_W_A_END = 4 * GROUP_WIDTH
_W_B_START = _W_A_END + 2 * N_HEADS
_W_B_END = _W_B_START + 8 * GROUP_WIDTH + IDX_HEADS * IDX_DIM
_W_D_START = _W_B_END + IDX_DIM + IDX_HEADS
_W_COLS = _W_D_START + 2 * GROUP_WIDTH


def _w_in_kernel(w_ref, m_ref, s_ref):
    m_ref[:, 0:_W_A_END] = w_ref[:, 0:_W_A_END].astype(BF16)
    m_ref[:, _W_A_END:_W_A_END + _W_B_END - _W_B_START] = w_ref[:, _W_B_START:_W_B_END].astype(BF16)
    m_ref[:, MAIN_COLS - 2 * GROUP_WIDTH:MAIN_COLS] = w_ref[:, _W_D_START:_W_COLS].astype(BF16)
    n_kw = IDX_DIM + IDX_HEADS
    s_ref[:, 0:n_kw] = w_ref[:, _W_B_END:_W_D_START].astype(BF16)
    s_ref[:, n_kw:n_kw + 2 * N_HEADS] = w_ref[:, _W_A_END:_W_B_START].astype(BF16)
    s_ref[:, n_kw + 2 * N_HEADS:] = jnp.zeros((w_ref.shape[0], SMALL_COLS - n_kw - 2 * N_HEADS), BF16)


def _prep_w_in(w_in):
    depth, d, cols = w_in.shape
    assert cols == _W_COLS and MAIN_COLS == _W_A_END + (_W_B_END - _W_B_START) + 2 * GROUP_WIDTH
    tm = 128
    return pl.pallas_call(
        _w_in_kernel,
        grid=(depth, d // tm),
        in_specs=[pl.BlockSpec((None, tm, cols), lambda l, i: (l, i, 0))],
        out_specs=[pl.BlockSpec((None, tm, MAIN_COLS), lambda l, i: (l, i, 0)),
                   pl.BlockSpec((None, tm, SMALL_COLS), lambda l, i: (l, i, 0))],
        out_shape=[jax.ShapeDtypeStruct((depth, d, MAIN_COLS), BF16),
                   jax.ShapeDtypeStruct((depth, d, SMALL_COLS), BF16)],
        compiler_params=_cparams(("parallel", "parallel")),
        name="w_in_prep",
    )(w_in)


def _lane_row(vals, lane0):
    row = jnp.zeros((1, SMALL_COLS), F32)
    return row.at[0, lane0:lane0 + vals.shape[0]].set(vals.astype(F32))


def _rope_tables(pos):
    half = HEAD_DIM // 2
    inv = ROPE_THETA ** (-jnp.arange(half, dtype=F32) / half)
    ang = pos.astype(F32)[:, None] * inv[None, :]
    cos, sin = jnp.cos(ang), jnp.sin(ang)
    return jnp.concatenate([cos, cos], axis=-1), jnp.concatenate([-sin, sin], axis=-1)


def _round_up(x, m):
    return (x + m - 1) // m * m


def _mixer_layer(x, conv_buf, s_gdn, s_ret, past_k, past_v, past_ki, mem_k, mem_v, wts, layer, depth, prev_cache):
    b, l, d = x.shape
    n = b * l
    offset = 0 if past_k is None else past_k.shape[2]
    x2d = x.reshape(n, d)
    proj, small = _in_proj(x2d, wts["norm_g"], wts["w_main"], wts["w_small"], layer)

    o_a, conv_new, s_gdn_new = _gdn(proj, small, wts["conv_w"], conv_buf, s_gdn, wts["alog_row"],
                                    wts["dtb_row"], wts["gdn_norm_g"], b, l)

    cos_t, sin_t = wts["rope_p"] if past_k is None else wts["rope_s"]
    o_b, s_ret_new = _retention(proj, cos_t, sin_t, s_ret, wts["ret_norm_g"], wts["ret_norm_b"], b, l)

    prefill = past_k is None
    prep = _dsa_prep(proj, small, wts["dsa_k_norm_g"], wts["idx_k_norm_g"], layer, depth, prev_cache, prefill)
    cache = tuple(prep[:3])
    if prefill:
        assert l % KT_A == 0, "prefill length must be a multiple of the key tile"
        o_c = _dsa(proj, small, prep[3].reshape(b, l, GROUP_WIDTH), prep[5].reshape(b, l // KT_A, GROUP_WIDTH, KT_A),
                   prep[4].reshape(b, l, IDX_DIM), wts["dsa_q_norm_g"], b, l, 256, 0, l)
    else:
        assert offset % KT_A == 0 and l % 16 == 0 and l <= 128, "decode step shape not supported"
        bias = _dsa_select(proj, small, past_ki, prep[4], b, l, offset, layer)
        o_c = _dsa_decode(proj, bias, past_k, past_v, prep[3], cache[1], wts["dsa_q_norm_g"], b, l, offset, layer)

    o_d = _mem_attend(proj, mem_k, mem_v, wts["mem_q_norm_g"], b, l)

    y = _out_proj(x2d, o_a, o_b, o_c, o_d, wts["w_out"]).reshape(b, l, d)
    return y, (conv_new, s_gdn_new, s_ret_new), cache


def kernel(x_prompt, x_sample, cache_gdn_conv, state_gdn, state_ret, cache_dsa_k, cache_dsa_v, cache_idx_k, cache_mem_k, cache_mem_v, mem_prompt, norm_g, w_in, gdn_conv_w, gdn_a_log, gdn_dt_bias, gdn_norm_g, ret_norm_g, ret_norm_b, dsa_q_norm_g, dsa_k_norm_g, idx_k_norm_g, mem_norm_g, w_mem_kv, mem_q_norm_g, mem_k_norm_g, w_out):
    depth = w_in.shape[0]
    b = x_prompt.shape[0]
    n_mem = mem_prompt.shape[1]
    d = x_prompt.shape[-1]
    y_p, y_s = x_prompt, x_sample
    st_p, st_s, mem_p = [], [], []
    cache_p = cache_s = None
    w_main, w_small = _prep_w_in(w_in)
    rope_p = _rope_tables(jnp.arange(x_prompt.shape[1], dtype=I32))
    rope_s = _rope_tables(cache_dsa_k.shape[2] + jnp.arange(x_sample.shape[1], dtype=I32))
    for li in range(depth):
        wts = dict(
            norm_g=norm_g[li][None, :], w_main=w_main, w_small=w_small, rope_p=rope_p, rope_s=rope_s,
            conv_w=gdn_conv_w[li],
            alog_row=_lane_row(gdn_a_log[li], LANE_ALPHA), dtb_row=_lane_row(gdn_dt_bias[li], LANE_ALPHA),
            gdn_norm_g=gdn_norm_g[li][None, :], ret_norm_g=ret_norm_g[li][None, :],
            ret_norm_b=ret_norm_b[li][None, :], dsa_q_norm_g=dsa_q_norm_g[li][None, :],
            dsa_k_norm_g=dsa_k_norm_g[li][None, :], idx_k_norm_g=idx_k_norm_g[li][None, :],
            mem_q_norm_g=mem_q_norm_g[li][None, :], w_out=w_out[li].astype(BF16),
        )
        mk, mv = _memory_kv(mem_prompt.reshape(b * n_mem, d), mem_norm_g[li][None, :],
                            w_mem_kv[li].astype(BF16), mem_k_norm_g[li][None, :])
        mk = mk.reshape(b, n_mem, GROUP_WIDTH)
        mv = mv.reshape(b, n_mem, GROUP_WIDTH)
        conv0 = jnp.zeros((b, CONV_W - 1, 3 * GROUP_WIDTH), F32)
        s0 = jnp.zeros((b, N_HEADS, HEAD_DIM, HEAD_DIM), F32)
        y_p, sp, cache_p = _mixer_layer(y_p, conv0, s0, s0, None, None, None, mk, mv, wts, li, depth, cache_p)
        st_p.append(sp)
        mem_p.append((mk.reshape(b, n_mem, N_HEADS, HEAD_DIM), mv.reshape(b, n_mem, N_HEADS, HEAD_DIM)))
        bs = x_sample.shape[0]
        y_s, ss, cache_s = _mixer_layer(y_s, cache_gdn_conv[li], state_gdn[li], state_ret[li],
                                        cache_dsa_k, cache_dsa_v, cache_idx_k,
                                        cache_mem_k[li].reshape(bs, n_mem, GROUP_WIDTH),
                                        cache_mem_v[li].reshape(bs, n_mem, GROUP_WIDTH), wts, li, depth, cache_s)
        st_s.append(ss)

    def stack(lst, k):
        return jnp.stack([s[k] for s in lst])

    def caches(c, bb, ll):
        return (c[0].reshape(depth, bb, ll, N_HEADS, HEAD_DIM), c[1].reshape(depth, bb, ll, N_HEADS, HEAD_DIM),
                c[2].reshape(depth, bb, ll, IDX_DIM))

    return ((y_p, y_s, stack(st_p, 0), stack(st_p, 1), stack(st_p, 2))
            + caches(cache_p, b, x_prompt.shape[1])
            + (stack(mem_p, 0), stack(mem_p, 1), stack(st_s, 0), stack(st_s, 1), stack(st_s, 2))
            + caches(cache_s, x_sample.shape[0], x_sample.shape[1]))
```

```python
import functools
import math

import jax
import jax.numpy as jnp
from jax import lax
from jax.experimental import pallas as pl
from jax.experimental.pallas import tpu as pltpu

F32 = jnp.float32
BF16 = jnp.bfloat16
I32 = jnp.int32

HEAD_DIM = 128
N_HEADS = 4
GROUP_WIDTH = N_HEADS * HEAD_DIM
CHUNK = 64
CONV_W = 4
IDX_HEADS = 16
IDX_DIM = 64
TOPK_MAX = 256
ROPE_THETA = 10000.0
EPS = 1e-6

COL_QA, COL_KA, COL_VA, COL_ZA = 0, 512, 1024, 1536
COL_QB, COL_KB, COL_VB, COL_ZB = 2048, 2560, 3072, 3584
COL_QC, COL_KC, COL_VC, COL_ZC = 4096, 4608, 5120, 5632
COL_QI = 6144
COL_QD, COL_ZD = 7168, 7680
MAIN_COLS = 8192
LANE_KI, LANE_WI, LANE_BETA, LANE_ALPHA = 0, 64, 80, 84
SMALL_COLS = 128

INT_MIN = -2 ** 31
NEG_BIG = -1e30
LOG2_E = 1.4426950408889634
SUBLANES = 8
LANES = 128
SUBLANES_16BIT = 16
VMEM_LIMIT = 56 * 1024 * 1024
HI = lax.Precision.HIGHEST


def _cparams(sem):
    return pltpu.CompilerParams(dimension_semantics=sem, vmem_limit_bytes=VMEM_LIMIT)


def _dot(a, b):
    return jnp.dot(a, b, preferred_element_type=F32)


def _dot_nt(a, b):
    return lax.dot_general(a, b, (((1,), (1,)), ((), ())), preferred_element_type=F32)


def _dot3(a, b):
    a_hi = a.astype(BF16)
    b_hi = b.astype(BF16)
    a_lo = (a - a_hi.astype(F32)).astype(BF16)
    b_lo = (b - b_hi.astype(F32)).astype(BF16)
    return _dot(a_hi, b_hi) + (_dot(a_hi, b_lo) + _dot(a_lo, b_hi))


def _silu(x):
    return x * jax.nn.sigmoid(x)


def _inproj_kernel(x_ref, g_ref, w_ref, ws_ref, o_ref, os_ref, h_ref):
    @pl.when(pl.program_id(1) == 0)
    def _():
        x = x_ref[...]
        y = x * lax.rsqrt(jnp.mean(x * x, axis=-1, keepdims=True) + EPS) * g_ref[...]
        hb = y.astype(BF16)
        h_ref[...] = hb
        os_ref[...] = _dot(hb, ws_ref[...])

    o_ref[...] = _dot(h_ref[...], w_ref[...])


def _in_proj(x2d, g_row, w_main, w_small, layer):
    n, d = x2d.shape
    tm = min(1024, n)
    tn = 1024
    return pl.pallas_call(
        _inproj_kernel,
        grid=(n // tm, MAIN_COLS // tn),
        in_specs=[
            pl.BlockSpec((tm, d), lambda i, j: (i, 0)),
            pl.BlockSpec((1, d), lambda i, j: (0, 0)),
            pl.BlockSpec((None, d, tn), lambda i, j: (layer, 0, j)),
            pl.BlockSpec((None, d, SMALL_COLS), lambda i, j: (layer, 0, 0)),
        ],
        out_specs=[
            pl.BlockSpec((tm, tn), lambda i, j: (i, j)),
            pl.BlockSpec((tm, SMALL_COLS), lambda i, j: (i, 0)),
        ],
        out_shape=[jax.ShapeDtypeStruct((n, MAIN_COLS), F32),
                   jax.ShapeDtypeStruct((n, SMALL_COLS), F32)],
        scratch_shapes=[pltpu.VMEM((tm, d), BF16)],
        compiler_params=_cparams(("parallel", "arbitrary")),
        name="in_proj",
    )(x2d, g_row, w_main, w_small)


def _memkv_kernel(x_ref, g_ref, w_ref, gk_ref, mk_ref, mv_ref):
    x = x_ref[...]
    y = x * lax.rsqrt(jnp.mean(x * x, axis=-1, keepdims=True) + EPS) * g_ref[...]
    kv = _dot(y.astype(BF16), w_ref[...])
    gk = gk_ref[...]
    for h in range(N_HEADS):
        sl = slice(h * HEAD_DIM, (h + 1) * HEAD_DIM)
        kh = kv[:, sl]
        mk_ref[:, sl] = kh * lax.rsqrt(jnp.mean(kh * kh, axis=-1, keepdims=True) + EPS) * gk
    mv_ref[...] = kv[:, GROUP_WIDTH:]


def _memory_kv(mem2d, g_row, w_kv, gk_row):
    n, d = mem2d.shape
    tm = min(256, n)
    return pl.pallas_call(
        _memkv_kernel,
        grid=(n // tm,),
        in_specs=[
            pl.BlockSpec((tm, d), lambda i: (i, 0)),
            pl.BlockSpec((1, d), lambda i: (0, 0)),
            pl.BlockSpec((d, 2 * GROUP_WIDTH), lambda i: (0, 0)),
            pl.BlockSpec((1, HEAD_DIM), lambda i: (0, 0)),
        ],
        out_specs=[pl.BlockSpec((tm, GROUP_WIDTH), lambda i: (i, 0)),
                   pl.BlockSpec((tm, GROUP_WIDTH), lambda i: (i, 0))],
        out_shape=[jax.ShapeDtypeStruct((n, GROUP_WIDTH), F32),
                   jax.ShapeDtypeStruct((n, GROUP_WIDTH), F32)],
        compiler_params=_cparams(("parallel",)),
        name="memory_kv",
    )(mem2d, g_row, w_kv, gk_row)


def _gdn_kernel(qkv_ref, z_ref, sm_ref, cw_ref, cb_ref, s0_ref, alog_ref, dtb_ref, gn_ref,
                o_ref, conv_ref, st_ref, xbuf_ref, s_ref, *, t_blk, chunk):
    j = pl.program_id(1)
    nj = pl.num_programs(1)
    gw3 = 3 * GROUP_WIDTH

    hdr = SUBLANES
    tail = hdr - (CONV_W - 1)

    @pl.when(j == 0)
    def _():
        xbuf_ref[0:hdr, :] = jnp.zeros((hdr, gw3), F32)
        xbuf_ref[tail:hdr, :] = cb_ref[0]
        s_ref[...] = s0_ref[0]

    @pl.when(j > 0)
    def _():
        xbuf_ref[0:hdr, :] = xbuf_ref[t_blk:t_blk + hdr, :]

    xbuf_ref[hdr:hdr + t_blk, :] = qkv_ref[...]
    conv_ref[0] = xbuf_ref[t_blk + tail:t_blk + hdr, :]

    cw = cw_ref[...]
    x_all = xbuf_ref[...]
    x_prev = pltpu.roll(x_all, 1, 0)
    u = x_all * cw[3:4, :] + x_prev * cw[2:3, :]
    v = x_all * cw[1:2, :] + x_prev * cw[0:1, :]
    y = _silu((u + pltpu.roll(v, 2, 0))[hdr:hdr + t_blk, :])

    sm = sm_ref[...]
    lane = lax.broadcasted_iota(I32, sm.shape, 1)
    beta_all = jax.nn.sigmoid(sm)
    xs = sm + dtb_ref[...]
    softplus = jnp.maximum(xs, 0.0) + jnp.log1p(jnp.exp(-jnp.abs(xs)))
    g_all = -jnp.exp(alog_ref[...]) * softplus
    g_all = jnp.where((lane >= LANE_ALPHA) & (lane < LANE_ALPHA + N_HEADS), g_all, 0.0)
    ri = lax.broadcasted_iota(I32, (t_blk, t_blk), 0)
    ci = lax.broadcasted_iota(I32, (t_blk, t_blk), 1)
    tri = jnp.where((ri // chunk == ci // chunk) & (ci <= ri), 1.0, 0.0).astype(F32)
    gcum = jnp.dot(tri, g_all, preferred_element_type=F32, precision=HI)
    gcum_t = gcum.T

    blk = min(2 * chunk, t_blk)
    n_blk = t_blk // blk
    cpb = blk // chunk
    rb = lax.broadcasted_iota(I32, (blk, blk), 0)
    cb = lax.broadcasted_iota(I32, (blk, blk), 1)
    same = (rb // chunk) == (cb // chunk)
    causal = same & (cb <= rb)
    strict = same & (cb < rb)
    eye = jnp.where(rb == cb, 1.0, 0.0).astype(F32)
    n_dbl = max(int(math.log2(chunk)) - 1, 0)
    gn = gn_ref[...]
    z_all = z_ref[...]

    qn, kn, vv = [], [], []
    for h in range(N_HEADS):
        qh = y[:, COL_QA + h * HEAD_DIM:COL_QA + (h + 1) * HEAD_DIM]
        kh = y[:, COL_KA + h * HEAD_DIM:COL_KA + (h + 1) * HEAD_DIM]
        qn.append(qh * lax.rsqrt(jnp.sum(qh * qh, axis=-1, keepdims=True) + EPS) * (HEAD_DIM ** -0.5))
        kn.append(kh * lax.rsqrt(jnp.sum(kh * kh, axis=-1, keepdims=True) + EPS))
        vv.append(y[:, COL_VA + h * HEAD_DIM:COL_VA + (h + 1) * HEAD_DIM])

    qs, ks, gcols, p_mats, x_mats, qk_mats, vb_mats, kg_mats, u_mats, w_mats = ({} for _ in range(10))

    def wy_operands(b):
        rs = slice(b * blk, (b + 1) * blk)
        for h in range(N_HEADS):
            gcol = gcum[rs, LANE_ALPHA + h:LANE_ALPHA + h + 1]
            grow = gcum_t[LANE_ALPHA + h:LANE_ALPHA + h + 1, rs]
            bcol = beta_all[rs, LANE_BETA + h:LANE_BETA + h + 1]
            decay = jnp.where(causal, jnp.exp(jnp.where(causal, gcol - grow, 0.0)), 0.0)
            qc, kc, vc = qn[h][rs], kn[h][rs], vv[h][rs]
            kb = kc * bcol
            kc16 = kc.astype(BF16)
            a_mat = jnp.where(strict, _dot_nt(kb.astype(BF16), kc16) * decay, 0.0)
            qk_mats[h, b] = jnp.where(causal, _dot_nt(qc.astype(BF16), kc16) * decay, 0.0).astype(BF16)
            p_mats[h, b] = -a_mat
            x_mats[h, b] = eye - a_mat
            vb_mats[h, b] = (vc * bcol).astype(BF16)
            kg_mats[h, b] = (kb * jnp.exp(gcol)).astype(BF16)
            qs[h, b], ks[h, b], gcols[h, b] = qc, kc, gcol

    def square_p(b):
        for h in range(N_HEADS):
            p_mats[h, b] = _dot3(p_mats[h, b], p_mats[h, b])

    def extend_x(b):
        for h in range(N_HEADS):
            x_mats[h, b] = x_mats[h, b] + _dot3(x_mats[h, b], p_mats[h, b])

    def wy_finish(b):
        for h in range(N_HEADS):
            x16 = x_mats[h, b].astype(BF16)
            u_mats[h, b] = _dot(x16, vb_mats[h, b])
            w_mats[h, b] = _dot(x16, kg_mats[h, b]).astype(BF16)

    states = [s_ref[h] for h in range(N_HEADS)]

    def recurrence_tasks(b):
        o_state = {h: [] for h in range(N_HEADS)}
        v_new = {h: [] for h in range(N_HEADS)}

        def chunk_step(c):
            cs = slice(c * chunk, (c + 1) * chunk)
            for h in range(N_HEADS):
                s = states[h]
                s16 = s.astype(BF16)
                gcol = gcols[h, b][cs]
                glast = gcols[h, b][(c + 1) * chunk - 1:(c + 1) * chunk]
                vn = u_mats[h, b][cs] - _dot(w_mats[h, b][cs], s16)
                vn16 = vn.astype(BF16)
                kd_t = (ks[h, b][cs] * jnp.exp(glast - gcol)).T.astype(BF16)
                states[h] = s * jnp.exp(glast) + _dot(kd_t, vn16)
                o_state[h].append(_dot((qs[h, b][cs] * jnp.exp(gcol)).astype(BF16), s16))
                v_new[h].append(vn16)

        def emit():
            rs = slice(b * blk, (b + 1) * blk)
            for h in range(N_HEADS):
                sl = slice(h * HEAD_DIM, (h + 1) * HEAD_DIM)
                o = jnp.concatenate(o_state[h], axis=0) + _dot(qk_mats[h, b], jnp.concatenate(v_new[h], axis=0))
                on = o * lax.rsqrt(jnp.mean(o * o, axis=-1, keepdims=True) + EPS) * gn
                o_ref[rs, sl] = (on * _silu(z_all[rs, sl])).astype(o_ref.dtype)

        return [functools.partial(chunk_step, c) for c in range(cpb)] + [emit]

    blocks = range(n_blk)
    for b in blocks:
        wy_operands(b)
    for _ in range(n_dbl):
        for b in blocks:
            square_p(b)
        for b in blocks:
            extend_x(b)
    for b in blocks:
        wy_finish(b)
    for b in blocks:
        for task in recurrence_tasks(b):
            task()
    for h in range(N_HEADS):
        s_ref[h] = states[h]

    @pl.when(j == nj - 1)
    def _():
        st_ref[0] = s_ref[...]


def _gdn(proj, small, conv_w, conv_buf, s0, alog_row, dtb_row, gn_row, batch, seqlen):
    n = batch * seqlen
    t_blk = min(256, seqlen)
    chunk = min(CHUNK, seqlen)
    nj = seqlen // t_blk
    gw3 = 3 * GROUP_WIDTH
    kern = functools.partial(_gdn_kernel, t_blk=t_blk, chunk=chunk)
    return pl.pallas_call(
        kern,
        grid=(batch, nj),
        in_specs=[
            pl.BlockSpec((t_blk, gw3), lambda b, j: (b * nj + j, COL_QA // gw3)),
            pl.BlockSpec((t_blk, GROUP_WIDTH), lambda b, j: (b * nj + j, COL_ZA // GROUP_WIDTH)),
            pl.BlockSpec((t_blk, SMALL_COLS), lambda b, j: (b * nj + j, 0)),
            pl.BlockSpec((CONV_W, gw3), lambda b, j: (0, 0)),
            pl.BlockSpec((1, CONV_W - 1, gw3), lambda b, j: (b, 0, 0)),
            pl.BlockSpec((1, N_HEADS, HEAD_DIM, HEAD_DIM), lambda b, j: (b, 0, 0, 0)),
            pl.BlockSpec((1, SMALL_COLS), lambda b, j: (0, 0)),
            pl.BlockSpec((1, SMALL_COLS), lambda b, j: (0, 0)),
            pl.BlockSpec((1, HEAD_DIM), lambda b, j: (0, 0)),
        ],
        out_specs=[
            pl.BlockSpec((t_blk, GROUP_WIDTH), lambda b, j: (b * nj + j, 0)),
            pl.BlockSpec((1, CONV_W - 1, gw3), lambda b, j: (b, 0, 0)),
            pl.BlockSpec((1, N_HEADS, HEAD_DIM, HEAD_DIM), lambda b, j: (b, 0, 0, 0)),
        ],
        out_shape=[
            jax.ShapeDtypeStruct((n, GROUP_WIDTH), BF16),
            jax.ShapeDtypeStruct((batch, CONV_W - 1, gw3), F32),
            jax.ShapeDtypeStruct((batch, N_HEADS, HEAD_DIM, HEAD_DIM), F32),
        ],
        scratch_shapes=[pltpu.VMEM((t_blk + SUBLANES, gw3), F32),
                        pltpu.VMEM((N_HEADS, HEAD_DIM, HEAD_DIM), F32)],
        compiler_params=_cparams(("parallel", "arbitrary")),
        name="gdn",
    )(proj, proj, small, conv_w, conv_buf, s0, alog_row, dtb_row, gn_row)


def _ret_kernel(q_ref, k_ref, v_ref, z_ref, cos_ref, sin_ref, s0_ref, g_ref, b_ref,
                o_ref, st_ref, s_ref, dmat_ref, *, t_blk):
    j = pl.program_id(1)
    nj = pl.num_programs(1)
    log_gamma = [math.log(1.0 - 2.0 ** (-5.0 - h)) for h in range(N_HEADS)]

    @pl.when(j == 0)
    def _():
        s_ref[...] = s0_ref[0]
        ri = lax.broadcasted_iota(I32, (t_blk, t_blk), 0)
        ci = lax.broadcasted_iota(I32, (t_blk, t_blk), 1)
        causal = ci <= ri
        rel = jnp.where(causal, ri - ci, 0).astype(F32)
        for h in range(N_HEADS):
            dmat_ref[h] = jnp.where(causal, jnp.exp(log_gamma[h] * rel), 0.0)

    cos = cos_ref[...]
    sin = sin_ref[...]
    idx_col = lax.broadcasted_iota(I32, (t_blk, 1), 0).astype(F32)
    gamma_g = g_ref[...]
    gamma_b = b_ref[...]
    q_all, k_all, v_all, z_all = q_ref[...], k_ref[...], v_ref[...], z_ref[...]

    for h in range(N_HEADS):
        sl = slice(h * HEAD_DIM, (h + 1) * HEAD_DIM)
        lg = log_gamma[h]
        qh, kh, vh = q_all[:, sl], k_all[:, sl], v_all[:, sl]
        qh = qh * cos + pltpu.roll(qh, HEAD_DIM // 2, 1) * sin
        kh = (kh * cos + pltpu.roll(kh, HEAD_DIM // 2, 1) * sin) * (HEAD_DIM ** -0.5)
        q16, k16, v16 = qh.astype(BF16), kh.astype(BF16), vh.astype(BF16)
        o_intra = _dot((_dot_nt(q16, k16) * dmat_ref[h]).astype(BF16), v16)
        s = s_ref[h]
        o_cross = _dot(q16, s.astype(BF16)) * jnp.exp(lg * (idx_col + 1.0))
        kd_t = (kh * jnp.exp(lg * (t_blk - 1.0 - idx_col))).T.astype(BF16)
        s_ref[h] = s * math.exp(lg * t_blk) + _dot(kd_t, v16)
        o = o_intra + o_cross
        mu = jnp.mean(o, axis=-1, keepdims=True)
        oc = o - mu
        var = jnp.mean(oc * oc, axis=-1, keepdims=True)
        on = oc * lax.rsqrt(var + EPS) * gamma_g + gamma_b
        o_ref[:, sl] = (on * _silu(z_all[:, sl])).astype(o_ref.dtype)

    @pl.when(j == nj - 1)
    def _():
        st_ref[0] = s_ref[...]


def _retention(proj, cos_t, sin_t, s0, g_row, b_row, batch, seqlen):
    n = batch * seqlen
    t_blk = min(256, seqlen)
    nj = seqlen // t_blk
    gw = GROUP_WIDTH
    kern = functools.partial(_ret_kernel, t_blk=t_blk)

    def col(c):
        return pl.BlockSpec((t_blk, gw), lambda b, j: (b * nj + j, c // gw))

    return pl.pallas_call(
        kern,
        grid=(batch, nj),
        in_specs=[
            col(COL_QB), col(COL_KB), col(COL_VB), col(COL_ZB),
            pl.BlockSpec((t_blk, HEAD_DIM), lambda b, j: (j, 0)),
            pl.BlockSpec((t_blk, HEAD_DIM), lambda b, j: (j, 0)),
            pl.BlockSpec((1, N_HEADS, HEAD_DIM, HEAD_DIM), lambda b, j: (b, 0, 0, 0)),
            pl.BlockSpec((1, HEAD_DIM), lambda b, j: (0, 0)),
            pl.BlockSpec((1, HEAD_DIM), lambda b, j: (0, 0)),
        ],
        out_specs=[
            pl.BlockSpec((t_blk, gw), lambda b, j: (b * nj + j, 0)),
            pl.BlockSpec((1, N_HEADS, HEAD_DIM, HEAD_DIM), lambda b, j: (b, 0, 0, 0)),
        ],
        out_shape=[
            jax.ShapeDtypeStruct((n, gw), BF16),
            jax.ShapeDtypeStruct((batch, N_HEADS, HEAD_DIM, HEAD_DIM), F32),
        ],
        scratch_shapes=[pltpu.VMEM((N_HEADS, HEAD_DIM, HEAD_DIM), F32),
                        pltpu.VMEM((N_HEADS, t_blk, t_blk), F32)],
        compiler_params=_cparams(("parallel", "arbitrary")),
        name="retention",
    )(proj, proj, proj, proj, cos_t, sin_t, s0, g_row, b_row)


def _dsa_prep_kernel(k_ref, v_ref, sm_ref, gk_ref, gi_ref, *rest, n_prev, with_vt):
    outs = rest[n_prev:]
    ko_ref, vo_ref, kio_ref, k16_ref, ki16_ref = outs[:5]
    k = k_ref[...]
    tm = k.shape[0]
    gk = gk_ref[...]
    for h in range(N_HEADS):
        sl = slice(h * HEAD_DIM, (h + 1) * HEAD_DIM)
        kh = k[:, sl]
        kn = kh * lax.rsqrt(jnp.mean(kh * kh, axis=-1, keepdims=True) + EPS) * gk
        ko_ref[pl.ds(h, tm, stride=N_HEADS), :] = kn
        k16_ref[:, sl] = kn.astype(BF16)
    v = v_ref[...]
    for h in range(N_HEADS):
        vo_ref[pl.ds(h, tm, stride=N_HEADS), :] = v[:, h * HEAD_DIM:(h + 1) * HEAD_DIM]
    if with_vt:
        outs[5][...] = v.T.astype(BF16)
    ki = sm_ref[...][:, LANE_KI:LANE_KI + IDX_DIM]
    kin = ki * lax.rsqrt(jnp.mean(ki * ki, axis=-1, keepdims=True) + EPS) * gi_ref[...]
    kio_ref[...] = kin
    ki16_ref[...] = kin.astype(BF16)


def _dsa_prep(proj, small, gk_row, gi_row, layer, depth, prev, with_vt):
    n = proj.shape[0]
    tm = min(KT_A, n)
    gw = GROUP_WIDTH
    n_prev = 0 if prev is None else 3
    kern = functools.partial(_dsa_prep_kernel, n_prev=n_prev, with_vt=with_vt)
    in_specs = [
        pl.BlockSpec((tm, gw), lambda i: (i, COL_KC // gw)),
        pl.BlockSpec((tm, gw), lambda i: (i, COL_VC // gw)),
        pl.BlockSpec((tm, SMALL_COLS), lambda i: (i, 0)),
        pl.BlockSpec((1, HEAD_DIM), lambda i: (0, 0)),
        pl.BlockSpec((1, IDX_DIM), lambda i: (0, 0)),
    ] + [pl.BlockSpec(memory_space=pl.ANY)] * n_prev
    out_specs = [pl.BlockSpec((None, tm * N_HEADS, HEAD_DIM), lambda i: (layer, i, 0)),
                 pl.BlockSpec((None, tm * N_HEADS, HEAD_DIM), lambda i: (layer, i, 0)),
                 pl.BlockSpec((None, tm, IDX_DIM), lambda i: (layer, i, 0)),
                 pl.BlockSpec((tm, gw), lambda i: (i, 0)),
                 pl.BlockSpec((tm, IDX_DIM), lambda i: (i, 0))]
    out_shape = [jax.ShapeDtypeStruct((depth, n * N_HEADS, HEAD_DIM), F32),
                 jax.ShapeDtypeStruct((depth, n * N_HEADS, HEAD_DIM), F32),
                 jax.ShapeDtypeStruct((depth, n, IDX_DIM), F32),
                 jax.ShapeDtypeStruct((n, gw), BF16),
                 jax.ShapeDtypeStruct((n, IDX_DIM), BF16)]
    if with_vt:
        out_specs.append(pl.BlockSpec((None, gw, tm), lambda i: (i, 0, 0)))
        out_shape.append(jax.ShapeDtypeStruct((n // tm, gw, tm), BF16))
    args = (proj, proj, small, gk_row, gi_row) + (() if prev is None else tuple(prev))
    return pl.pallas_call(
        kern,
        grid=(n // tm,),
        in_specs=in_specs,
        out_specs=out_specs,
        out_shape=out_shape,
        input_output_aliases={5 + t: t for t in range(n_prev)},
        compiler_params=_cparams(("parallel",)),
        name="dsa_prep",
    )(*args)


KT_I = 128
KT_A = 512


def _dsa_kernel(q_ref, qi_ref, z_ref, sm_ref, k_ref, vt_ref, ki_ref, gq_ref, o_ref,
                key_ref, qit_ref, wt_ref, jlim_ref, qt_ref, acc_ref, lg_ref, p_ref, key16_ref, low16_ref, *, tq, offset, s_valid, topk):
    i = pl.program_id(1)
    pos0 = offset + i * tq
    t_pos = pos0 + lax.broadcasted_iota(I32, (1, tq), 1)
    t_chunk = t_pos // CHUNK
    n_adm_row = jnp.minimum((t_chunk + 1) * CHUNK, s_valid)
    n_keys = jnp.minimum(((pos0 + tq - 1) // CHUNK + 1) * CHUNK, s_valid)
    n_at = (n_keys + KT_A - 1) // KT_A
    n_sub = KT_A // KT_I

    qit_ref[...] = qi_ref[...].T.astype(BF16)
    w_t = sm_ref[...].T * (IDX_HEADS ** -0.5 * IDX_DIM ** -0.5)
    for h in range(IDX_HEADS):
        wt_ref[h] = jnp.broadcast_to(w_t[LANE_WI + h:LANE_WI + h + 1, :], (SUBLANES, tq))

    def index_step(kt, carry):
        r0 = pl.multiple_of(kt * KT_I, KT_I)
        ki_t = ki_ref[0, pl.ds(r0, KT_I), :]
        acc = jnp.zeros((KT_I // SUBLANES, SUBLANES, tq), F32)
        for h in range(IDX_HEADS):
            sc = _dot(ki_t, qit_ref[h * IDX_DIM:(h + 1) * IDX_DIM, :])
            acc = acc + wt_ref[h][None] * jnp.maximum(sc, 0.0).reshape(KT_I // SUBLANES, SUBLANES, tq)
        bits = lax.bitcast_convert_type(acc.reshape(KT_I, tq), I32)
        key = bits ^ ((bits >> 31) & 0x7FFFFFFF)
        s_pos = r0 + lax.broadcasted_iota(I32, (KT_I, 1), 0)
        adm = (s_pos // CHUNK <= t_chunk) & (s_pos < s_valid)
        key = jnp.where(adm, key, INT_MIN)
        key_ref[pl.ds(r0, KT_I), :] = key
        key16_ref[pl.ds(r0, KT_I), :] = (key >> 16).astype(jnp.int16)
        return carry

    def index_trip(t, c):
        for s in range(n_sub):
            c = index_step(n_sub * t + s, c)
        return c

    lax.fori_loop(0, n_at, index_trip, 0)

    def count(pred_fn):
        def body(kt, accs):
            out = []
            for s in range(n_sub):
                r0 = pl.multiple_of(kt * KT_A + s * KT_I, KT_I)
                key = key_ref[pl.ds(r0, KT_I), :]
                s_pos = r0 + lax.broadcasted_iota(I32, (KT_I, 1), 0)
                m = jnp.where(pred_fn(key, s_pos), 1, 0).astype(I32)
                out.append(accs[s] + jnp.sum(m.reshape(KT_I // SUBLANES, SUBLANES, tq), axis=0))
            return tuple(out)
        accs = lax.fori_loop(0, n_at, body, tuple(jnp.zeros((SUBLANES, tq), I32) for _ in range(n_sub)))
        return jnp.sum(sum(accs[1:], accs[0]), axis=0, keepdims=True)

    def count16(ref16, pred_fn):
        def body(kt, accs):
            out = []
            for s in range(n_sub):
                r0 = pl.multiple_of(kt * KT_A + s * KT_I, KT_I)
                m = jnp.where(pred_fn(ref16[pl.ds(r0, KT_I), :]), jnp.int16(1), jnp.int16(0))
                m = m.reshape(KT_I // SUBLANES_16BIT, SUBLANES_16BIT, tq)
                acc = accs[s]
                for q in range(KT_I // SUBLANES_16BIT):
                    acc = acc + m[q]
                out.append(acc)
            return tuple(out)
        accs = lax.fori_loop(0, n_at, body, tuple(jnp.zeros((SUBLANES_16BIT, tq), jnp.int16) for _ in range(n_sub)))
        tot = sum((a.astype(I32) for a in accs[1:]), accs[0].astype(I32))
        return jnp.sum(tot, axis=0, keepdims=True)

    small = jnp.where(n_adm_row <= topk, 1, 0).astype(I32)

    def all_done(done):
        return jnp.min(done.astype(F32)) > 0.0

    def search(first_bit, last_bit, count_ge, carry, early_exit):
        def step(b, c):
            v, done, thr, c_rej = c
            cand_u = v | jnp.left_shift(jnp.int32(1), 31 - b)
            cand_s = cand_u ^ INT_MIN
            cnt = count_ge(cand_s)
            accept = cnt >= topk
            v = jnp.where(accept, cand_u, v)
            newly = (cnt == topk) & (done == 0)
            thr = jnp.where(newly, cand_s, thr)
            return v, jnp.where(newly, 1, done), thr, jnp.where(accept, c_rej, cnt)

        if not early_exit:
            return lax.fori_loop(first_bit, last_bit, step, carry)

        def cond(c):
            return (c[0] < last_bit) & jnp.logical_not(all_done(c[1][1]))

        def body(c):
            b, inner = c
            return b + 4, lax.fori_loop(b, b + 4, step, inner)

        assert (last_bit - first_bit) % 4 == 0
        return lax.while_loop(cond, body, (jnp.int32(first_bit), carry))[1]

    v0 = jnp.zeros((1, tq), I32)
    thr0 = jnp.full((1, tq), INT_MIN + 1, I32)
    carry = search(0, 16, lambda cand_s: count16(key16_ref, lambda k16: k16 >= (cand_s >> 16).astype(jnp.int16)),
                   (v0, small, thr0, jnp.zeros((1, tq), I32)), early_exit=False)
    n_above = carry[3]

    @pl.when(jnp.logical_not(all_done(carry[1])))
    def _():
        hi32 = (carry[0] ^ INT_MIN) >> 16

        def low_step(kt, c):
            for s in range(n_sub):
                r0 = pl.multiple_of(kt * KT_A + s * KT_I, KT_I)
                key = key_ref[pl.ds(r0, KT_I), :]
                low = jnp.where((key >> 16) == hi32, (key & 0xFFFF) - 32768, -32768)
                low16_ref[pl.ds(r0, KT_I), :] = low.astype(jnp.int16)
            return c

        lax.fori_loop(0, n_at, low_step, 0)

    def count_low(cand_s):
        c16 = ((cand_s & 0xFFFF) - 32768).astype(jnp.int16)
        return n_above + count16(low16_ref, lambda l16: l16 >= c16)

    v_u, done_i, thr, _ = search(16, 32, count_low, carry, early_exit=True)
    done = done_i != 0
    v_s = v_u ^ INT_MIN
    any_tie = jnp.logical_not(all_done(done_i))

    jlim_ref[...] = jnp.zeros(jlim_ref.shape, I32)

    @pl.when(any_tie)
    def _():
        n_gt = count(lambda key, s_pos: key > v_s)
        need = topk - n_gt
        pos_bits = max(int(math.ceil(math.log2(max(k_ref.shape[1], 2)))), 1) + 1

        def pos_step(b, jv):
            cand = jv | jnp.left_shift(jnp.int32(1), pos_bits - 1 - b)
            cnt = count(lambda key, s_pos: (key == v_s) & (s_pos < cand))
            return jnp.where(cnt <= need, cand, jv)

        jlim_ref[0:1, :] = lax.fori_loop(0, pos_bits, pos_step, jnp.zeros((1, tq), I32))

    j_lim = jlim_ref[0:1, :]

    v_eff = jnp.where(done, thr - 1, v_s)
    j_eff = jnp.where(done, 0, j_lim)

    def bias_loop(with_ties):
        def bias_step(kt, carry):
            for s in range(n_sub):
                r0 = pl.multiple_of(kt * KT_A + s * KT_I, KT_I)
                key = key_ref[pl.ds(r0, KT_I), :]
                sel = key > v_eff
                if with_ties:
                    s_pos = r0 + lax.broadcasted_iota(I32, (KT_I, 1), 0)
                    sel = sel | ((key == v_eff) & (s_pos < j_eff) & (key != INT_MIN))
                bias = jnp.where(sel, 0.0, NEG_BIG).astype(F32)
                key_ref[pl.ds(r0, KT_I), :] = lax.bitcast_convert_type(bias, I32)
            return carry

        lax.fori_loop(0, n_at, bias_step, 0)

    pl.when(any_tie)(lambda: bias_loop(True))
    pl.when(jnp.logical_not(any_tie))(lambda: bias_loop(False))

    gq = gq_ref[...]
    q_all = q_ref[...]
    for h in range(N_HEADS):
        sl = slice(h * HEAD_DIM, (h + 1) * HEAD_DIM)
        qh = q_all[:, sl]
        qh = qh * lax.rsqrt(jnp.mean(qh * qh, axis=-1, keepdims=True) + EPS) * gq * (HEAD_DIM ** -0.5 * LOG2_E)
        qt_ref[sl, :] = qh.T.astype(BF16)
    acc_ref[...] = jnp.zeros(acc_ref.shape, F32)

    heads = [slice(h * HEAD_DIM, (h + 1) * HEAD_DIM) for h in range(N_HEADS)]
    kt_h = KT_A // 2
    n_sub_h = kt_h // KT_I

    def issue_scores(kt, half, slot):
        r0 = pl.multiple_of(kt * KT_A + half * kt_h, kt_h)
        for h, sl in enumerate(heads):
            lg_ref[slot, h] = _dot(k_ref[0, pl.ds(r0, kt_h), sl], qt_ref[sl, :])

    def issue_values(kt, half, slot):
        return [_dot(vt_ref[0, kt, sl, half * kt_h:(half + 1) * kt_h], p_ref[slot, h])
                for h, sl in enumerate(heads)]

    def softmax_half(kt, half, slot, ms, ls):
        r0 = pl.multiple_of(kt * KT_A + half * kt_h, kt_h)
        new_ms, new_ls, alphas = [], [], []
        for h in range(N_HEADS):
            mx = None
            for s in range(n_sub_h):
                rs = slice(s * KT_I, (s + 1) * KT_I)
                bias = lax.bitcast_convert_type(key_ref[pl.ds(r0 + s * KT_I, KT_I), :], F32)
                lg = lg_ref[slot, h, rs, :] + bias
                lg_ref[slot, h, rs, :] = lg
                part = jnp.max(lg.reshape(KT_I // SUBLANES, SUBLANES, tq), axis=0)
                mx = part if mx is None else jnp.maximum(mx, part)
            m_new = jnp.maximum(ms[h], jnp.max(mx, axis=0, keepdims=True))
            alphas.append(jnp.exp2(ms[h] - m_new))
            new_ms.append(m_new)
        for h in range(N_HEADS):
            lsum = None
            for s in range(n_sub_h):
                rs = slice(s * KT_I, (s + 1) * KT_I)
                p = jnp.exp2(lg_ref[slot, h, rs, :] - new_ms[h])
                p_ref[slot, h, rs, :] = p.astype(BF16)
                part = jnp.sum(p.reshape(KT_I // SUBLANES, SUBLANES, tq), axis=0)
                lsum = part if lsum is None else lsum + part
            new_ls.append(alphas[h] * ls[h] + jnp.sum(lsum, axis=0, keepdims=True))
        return new_ms, new_ls, alphas

    def accumulate(alphas, pv):
        for h in range(N_HEADS):
            acc_ref[h] = alphas[h] * acc_ref[h] + pv[h]

    def att_step(kt, carry):
        ms, ls = carry
        issue_scores(kt, 0, 0)
        issue_scores(kt, 1, 1)
        ms, ls, alphas_a = softmax_half(kt, 0, 0, ms, ls)
        pv_a = issue_values(kt, 0, 0)
        ms, ls, alphas_b = softmax_half(kt, 1, 1, ms, ls)
        accumulate(alphas_a, pv_a)
        accumulate(alphas_b, issue_values(kt, 1, 1))
        return tuple(ms), tuple(ls)

    m0 = tuple(jnp.full((1, tq), NEG_BIG, F32) for _ in range(N_HEADS))
    l0 = tuple(jnp.zeros((1, tq), F32) for _ in range(N_HEADS))
    _, ls = lax.fori_loop(0, n_at, att_step, (m0, l0))
    z_all = z_ref[...]
    for h in range(N_HEADS):
        sl = slice(h * HEAD_DIM, (h + 1) * HEAD_DIM)
        oh = (acc_ref[h] / ls[h]).T
        o_ref[:, sl] = (oh * _silu(z_all[:, sl])).astype(o_ref.dtype)


def _dsa(proj, small, k_all16, vt16, ki16, gq_row, batch, q_len, tq, offset, s_valid):
    n = batch * q_len
    nq = q_len // tq
    s_pad = k_all16.shape[1]
    topk = min(TOPK_MAX, s_valid // 4)
    gw = GROUP_WIDTH
    qi_w = IDX_HEADS * IDX_DIM
    kern = functools.partial(_dsa_kernel, tq=tq, offset=offset, s_valid=s_valid, topk=topk)
    once = pl.Buffered(1)
    return pl.pallas_call(
        kern,
        grid=(batch, nq),
        in_specs=[
            pl.BlockSpec((tq, gw), lambda b, i: (b * nq + i, COL_QC // gw)),
            pl.BlockSpec((tq, qi_w), lambda b, i: (b * nq + i, COL_QI // qi_w)),
            pl.BlockSpec((tq, gw), lambda b, i: (b * nq + i, COL_ZC // gw)),
            pl.BlockSpec((tq, SMALL_COLS), lambda b, i: (b * nq + i, 0)),
            pl.BlockSpec((1, s_pad, gw), lambda b, i: (b, 0, 0), pipeline_mode=once),
            pl.BlockSpec((1, s_pad // KT_A, gw, KT_A), lambda b, i: (b, 0, 0, 0), pipeline_mode=once),
            pl.BlockSpec((1, s_pad, IDX_DIM), lambda b, i: (b, 0, 0), pipeline_mode=once),
            pl.BlockSpec((1, HEAD_DIM), lambda b, i: (0, 0)),
        ],
        out_specs=pl.BlockSpec((tq, gw), lambda b, i: (b * nq + i, 0)),
        out_shape=jax.ShapeDtypeStruct((n, gw), BF16),
        scratch_shapes=[pltpu.VMEM((s_pad, tq), I32),
                        pltpu.VMEM((qi_w, tq), BF16),
                        pltpu.VMEM((IDX_HEADS, SUBLANES, tq), F32),
                        pltpu.VMEM((SUBLANES, tq), I32),
                        pltpu.VMEM((gw, tq), BF16),
                        pltpu.VMEM((N_HEADS, HEAD_DIM, tq), F32),
                        pltpu.VMEM((2, N_HEADS, KT_A // 2, tq), F32),
                        pltpu.VMEM((2, N_HEADS, KT_A // 2, tq), BF16),
                        pltpu.VMEM((s_pad, tq), jnp.int16),
                        pltpu.VMEM((s_pad, tq), jnp.int16)],
        compiler_params=_cparams(("parallel", "arbitrary")),
        name="dsa",
    )(proj, proj, proj, small, k_all16, vt16, ki16, gq_row)


def _dsa_select_kernel(qi_ref, sm_ref, pki_ref, nki_ref, bias_ref, ki_ref, key_ref, jl_ref, *,
                       lq, offset, s_valid, topk):
    s_pad = ki_ref.shape[0]
    n_t = s_pad // KT_A
    ki_ref[0:offset, :] = pki_ref[0].astype(BF16)
    ki_ref[offset:s_pad, :] = jnp.zeros((s_pad - offset, IDX_DIM), BF16)
    ki_ref[offset:offset + lq, :] = nki_ref[...]
    qi = qi_ref[...]
    q2 = jnp.concatenate([qi[:, h * IDX_DIM:(h + 1) * IDX_DIM] for h in range(IDX_HEADS)], axis=0).astype(BF16)
    w = sm_ref[...] * (IDX_HEADS ** -0.5 * IDX_DIM ** -0.5)
    w_cols = [w[:, LANE_WI + h:LANE_WI + h + 1] for h in range(IDX_HEADS)]
    t_chunk = (offset + lax.broadcasted_iota(I32, (lq, 1), 0)) // CHUNK

    def index_step(j, carry):
        c0 = pl.multiple_of(j * KT_A, KT_A)
        sc = _dot_nt(q2, ki_ref[pl.ds(c0, KT_A), :])
        acc = jnp.zeros((lq, KT_A), F32)
        for h in range(IDX_HEADS):
            acc = acc + w_cols[h] * jnp.maximum(sc[h * lq:(h + 1) * lq, :], 0.0)
        bits = lax.bitcast_convert_type(acc, I32)
        key = bits ^ ((bits >> 31) & 0x7FFFFFFF)
        s_pos = c0 + lax.broadcasted_iota(I32, (1, KT_A), 1)
        adm = (s_pos // CHUNK <= t_chunk) & (s_pos < s_valid)
        key_ref[:, pl.ds(c0, KT_A)] = jnp.where(adm, key, INT_MIN)
        return carry

    lax.fori_loop(0, n_t, index_step, 0)

    def count(pred_fn):
        def body(j, acc):
            c0 = pl.multiple_of(j * KT_A, KT_A)
            s_pos = c0 + lax.broadcasted_iota(I32, (1, KT_A), 1)
            m = jnp.where(pred_fn(key_ref[:, pl.ds(c0, KT_A)], s_pos), 1, 0).astype(I32)
            for c in range(KT_A // LANES):
                acc = acc + m[:, c * LANES:(c + 1) * LANES]
            return acc
        acc = lax.fori_loop(0, n_t, body, jnp.zeros((lq, LANES), I32))
        return jnp.sum(acc, axis=1, keepdims=True)

    def bit_step(b, v):
        cand_u = v | jnp.left_shift(jnp.int32(1), 31 - b)
        cnt = count(lambda key, s_pos: key >= (cand_u ^ INT_MIN))
        return jnp.where(cnt >= topk, cand_u, v)

    v_s = lax.fori_loop(0, 32, bit_step, jnp.zeros((lq, 1), I32)) ^ INT_MIN
    n_gt = count(lambda key, s_pos: key > v_s)
    n_ge = count(lambda key, s_pos: key >= v_s)
    need = topk - n_gt
    tied = (n_ge != topk) & (v_s != INT_MIN)
    jl_ref[...] = jnp.full(jl_ref.shape, s_pad, I32)

    @pl.when(jnp.max(jnp.where(tied, 1.0, 0.0)) > 0.0)
    def _():
        pos_bits = max(int(math.ceil(math.log2(max(s_pad, 2)))), 1) + 1

        def pos_step(b, jv):
            cand = jv | jnp.left_shift(jnp.int32(1), pos_bits - 1 - b)
            cnt = count(lambda key, s_pos: (key == v_s) & (s_pos < cand))
            return jnp.where(cnt <= need, cand, jv)

        jl_ref[:, 0:1] = lax.fori_loop(0, pos_bits, pos_step, jnp.zeros((lq, 1), I32))

    j_lim = jnp.where(tied, jl_ref[:, 0:1], s_pad)

    def bias_step(j, carry):
        c0 = pl.multiple_of(j * KT_A, KT_A)
        key = key_ref[:, pl.ds(c0, KT_A)]
        s_pos = c0 + lax.broadcasted_iota(I32, (1, KT_A), 1)
        sel = (key > v_s) | ((key == v_s) & (s_pos < j_lim) & (key != INT_MIN))
        bias_ref[0, :, pl.ds(c0, KT_A)] = jnp.where(sel, 0.0, NEG_BIG).astype(F32)
        return carry

    lax.fori_loop(0, n_t, bias_step, 0)


def _dsa_select(proj, small, past_ki, ki16, batch, lq, offset, layer):
    s_valid = offset + lq
    s_pad = _round_up(s_valid, KT_A)
    topk = min(TOPK_MAX, s_valid // 4)
    qi_w = IDX_HEADS * IDX_DIM
    kern = functools.partial(_dsa_select_kernel, lq=lq, offset=offset, s_valid=s_valid, topk=topk)
    return pl.pallas_call(
        kern,
        grid=(batch,),
        in_specs=[
            pl.BlockSpec((lq, qi_w), lambda b: (b, COL_QI // qi_w)),
            pl.BlockSpec((lq, SMALL_COLS), lambda b: (b, 0)),
            pl.BlockSpec((None, 1, offset, IDX_DIM), lambda b: (layer, b, 0, 0)),
            pl.BlockSpec((lq, IDX_DIM), lambda b: (b, 0)),
        ],
        out_specs=pl.BlockSpec((1, lq, s_pad), lambda b: (b, 0, 0)),
        out_shape=jax.ShapeDtypeStruct((batch, lq, s_pad), F32),
        scratch_shapes=[pltpu.VMEM((s_pad, IDX_DIM), BF16), pltpu.VMEM((lq, s_pad), I32),
                        pltpu.VMEM((lq, LANES), I32)],
        compiler_params=_cparams(("parallel",)),
        name="dsa_select",
    )(proj, small, past_ki, ki16)


def _dsa_decode_kernel(q_ref, z_ref, bias_ref, pk_ref, pv_ref, nk_ref, nv_ref, gq_ref, o_ref,
                       qn_ref, m_ref, l_ref, acc_ref, *, lq, n_cache_tiles):
    j = pl.program_id(1)

    @pl.when(j == 0)
    def _():
        q_all = q_ref[...]
        gq = gq_ref[...]
        for h in range(N_HEADS):
            qh = q_all[:, h * HEAD_DIM:(h + 1) * HEAD_DIM]
            qh = qh * lax.rsqrt(jnp.mean(qh * qh, axis=-1, keepdims=True) + EPS) * gq * (HEAD_DIM ** -0.5 * LOG2_E)
            qn_ref[h] = qh.astype(BF16)
        m_ref[...] = jnp.full(m_ref.shape, NEG_BIG, F32)
        l_ref[...] = jnp.zeros(l_ref.shape, F32)
        acc_ref[...] = jnp.zeros(acc_ref.shape, F32)

    def attend(k_heads, v_heads, bias):
        for h in range(N_HEADS):
            logit = _dot_nt(qn_ref[h], k_heads[h]) + bias
            m_old = m_ref[h]
            m_new = jnp.maximum(m_old, jnp.max(logit, axis=1, keepdims=True))
            alpha = jnp.exp2(m_old - m_new)
            p = jnp.exp2(logit - m_new[:, 0:1])
            l_ref[h] = alpha * l_ref[h] + jnp.sum(p, axis=1, keepdims=True)
            acc_ref[h] = alpha * acc_ref[h] + _dot(p.astype(BF16), v_heads[h])
            m_ref[h] = m_new

    @pl.when(j < n_cache_tiles)
    def _():
        attend([pk_ref[0, pl.ds(h, KT_A, stride=N_HEADS), :].astype(BF16) for h in range(N_HEADS)],
               [pv_ref[0, pl.ds(h, KT_A, stride=N_HEADS), :].astype(BF16) for h in range(N_HEADS)], bias_ref[0])

    @pl.when(j == n_cache_tiles)
    def _():
        pad = jnp.zeros((LANES - lq, HEAD_DIM), BF16)
        nk = nk_ref[...]
        attend([jnp.concatenate([nk[:, h * HEAD_DIM:(h + 1) * HEAD_DIM], pad], axis=0) for h in range(N_HEADS)],
               [jnp.concatenate([nv_ref[pl.ds(h, lq, stride=N_HEADS), :].astype(BF16), pad], axis=0)
                for h in range(N_HEADS)],
               bias_ref[0][:, 0:LANES])
        z_all = z_ref[...]
        for h in range(N_HEADS):
            sl = slice(h * HEAD_DIM, (h + 1) * HEAD_DIM)
            o_ref[:, sl] = (acc_ref[h] / l_ref[h] * _silu(z_all[:, sl])).astype(o_ref.dtype)


def _dsa_decode(proj, bias, past_k, past_v, k16, v_new, gq_row, batch, lq, offset, layer):
    gw = GROUP_WIDTH
    n_cache_tiles = offset // KT_A
    kern = functools.partial(_dsa_decode_kernel, lq=lq, n_cache_tiles=n_cache_tiles)
    last = n_cache_tiles - 1

    def rows(c):
        return c.reshape(c.shape[0], c.shape[1], offset * N_HEADS, HEAD_DIM)
    return pl.pallas_call(
        kern,
        grid=(batch, n_cache_tiles + 1),
        in_specs=[
            pl.BlockSpec((lq, gw), lambda b, j: (b, COL_QC // gw)),
            pl.BlockSpec((lq, gw), lambda b, j: (b, COL_ZC // gw)),
            pl.BlockSpec((1, lq, KT_A), lambda b, j: (b, 0, j)),
            pl.BlockSpec((None, 1, KT_A * N_HEADS, HEAD_DIM), lambda b, j: (layer, b, jnp.minimum(j, last), 0)),
            pl.BlockSpec((None, 1, KT_A * N_HEADS, HEAD_DIM), lambda b, j: (layer, b, jnp.minimum(j, last), 0)),
            pl.BlockSpec((lq, gw), lambda b, j: (b, 0)),
            pl.BlockSpec((None, lq * N_HEADS, HEAD_DIM), lambda b, j: (layer, b, 0)),
            pl.BlockSpec((1, HEAD_DIM), lambda b, j: (0, 0)),
        ],
        out_specs=pl.BlockSpec((lq, gw), lambda b, j: (b, 0)),
        out_shape=jax.ShapeDtypeStruct((batch * lq, gw), BF16),
        scratch_shapes=[pltpu.VMEM((N_HEADS, lq, HEAD_DIM), BF16),
                        pltpu.VMEM((N_HEADS, lq, HEAD_DIM), F32),
                        pltpu.VMEM((N_HEADS, lq, HEAD_DIM), F32),
                        pltpu.VMEM((N_HEADS, lq, HEAD_DIM), F32)],
        compiler_params=_cparams(("parallel", "arbitrary")),
        name="dsa_decode",
    )(proj, proj, bias, rows(past_k), rows(past_v), k16, v_new, gq_row)


def _mem_kernel(q_ref, z_ref, mk_ref, mv_ref, gq_ref, o_ref):
    gq = gq_ref[...]
    q_all, z_all = q_ref[...], z_ref[...]
    mk = mk_ref[0].astype(BF16)
    mv = mv_ref[0].astype(BF16)
    for h in range(N_HEADS):
        sl = slice(h * HEAD_DIM, (h + 1) * HEAD_DIM)
        qh = q_all[:, sl]
        qh = qh * lax.rsqrt(jnp.mean(qh * qh, axis=-1, keepdims=True) + EPS) * gq * (HEAD_DIM ** -0.5)
        logit = _dot_nt(qh.astype(BF16), mk[:, sl])
        m = jnp.max(logit, axis=-1, keepdims=True)
        p = jnp.exp(logit - m)
        l = jnp.sum(p, axis=-1, keepdims=True)
        oh = _dot(p.astype(BF16), mv[:, sl]) / l
        o_ref[:, sl] = (oh * _silu(z_all[:, sl])).astype(o_ref.dtype)


def _mem_attend(proj, mk, mv, gq_row, batch, seqlen):
    n = batch * seqlen
    tm = min(512, seqlen)
    nj = seqlen // tm
    gw = GROUP_WIDTH
    n_mem = mk.shape[1]
    return pl.pallas_call(
        _mem_kernel,
        grid=(batch, nj),
        in_specs=[
            pl.BlockSpec((tm, gw), lambda b, j: (b * nj + j, COL_QD // gw)),
            pl.BlockSpec((tm, gw), lambda b, j: (b * nj + j, COL_ZD // gw)),
            pl.BlockSpec((1, n_mem, gw), lambda b, j: (b, 0, 0)),
            pl.BlockSpec((1, n_mem, gw), lambda b, j: (b, 0, 0)),
            pl.BlockSpec((1, HEAD_DIM), lambda b, j: (0, 0)),
        ],
        out_specs=pl.BlockSpec((tm, gw), lambda b, j: (b * nj + j, 0)),
        out_shape=jax.ShapeDtypeStruct((n, gw), BF16),
        compiler_params=_cparams(("parallel", "arbitrary")),
        name="mem_attend",
    )(proj, proj, mk, mv, gq_row)


def _outproj_kernel(x_ref, a_ref, b_ref, c_ref, d_ref, w_ref, y_ref):
    gw = GROUP_WIDTH
    acc = x_ref[...] + _dot(a_ref[...], w_ref[0:gw, :])
    acc = acc + _dot(b_ref[...], w_ref[gw:2 * gw, :])
    acc = acc + _dot(c_ref[...], w_ref[2 * gw:3 * gw, :])
    acc = acc + _dot(d_ref[...], w_ref[3 * gw:4 * gw, :])
    y_ref[...] = acc


def _out_proj(x2d, oa, ob, oc, od, w16):
    n, d = x2d.shape
    tm = min(512, n)
    gw = GROUP_WIDTH
    grp = pl.BlockSpec((tm, gw), lambda i: (i, 0))
    return pl.pallas_call(
        _outproj_kernel,
        grid=(n // tm,),
        in_specs=[pl.BlockSpec((tm, d), lambda i: (i, 0)), grp, grp, grp, grp,
                  pl.BlockSpec((4 * gw, d), lambda i: (0, 0))],
        out_specs=pl.BlockSpec((tm, d), lambda i: (i, 0)),
        out_shape=jax.ShapeDtypeStruct((n, d), F32),
        compiler_params=_cparams(("parallel",)),
        name="out_proj",
    )(x2d, oa, ob, oc, od, w16)


_W_A_END = 4 * GROUP_WIDTH
_W_B_START = _W_A_END + 2 * N_HEADS
_W_B_END = _W_B_START + 8 * GROUP_WIDTH + IDX_HEADS * IDX_DIM
_W_D_START = _W_B_END + IDX_DIM + IDX_HEADS
_W_COLS = _W_D_START + 2 * GROUP_WIDTH


def _w_in_kernel(w_ref, m_ref, s_ref):
    m_ref[:, 0:_W_A_END] = w_ref[:, 0:_W_A_END].astype(BF16)
    m_ref[:, _W_A_END:_W_A_END + _W_B_END - _W_B_START] = w_ref[:, _W_B_START:_W_B_END].astype(BF16)
    m_ref[:, MAIN_COLS - 2 * GROUP_WIDTH:MAIN_COLS] = w_ref[:, _W_D_START:_W_COLS].astype(BF16)
    n_kw = IDX_DIM + IDX_HEADS
    s_ref[:, 0:n_kw] = w_ref[:, _W_B_END:_W_D_START].astype(BF16)
    s_ref[:, n_kw:n_kw + 2 * N_HEADS] = w_ref[:, _W_A_END:_W_B_START].astype(BF16)
    s_ref[:, n_kw + 2 * N_HEADS:] = jnp.zeros((w_ref.shape[0], SMALL_COLS - n_kw - 2 * N_HEADS), BF16)


def _prep_w_in(w_in):
    depth, d, cols = w_in.shape
    assert cols == _W_COLS and MAIN_COLS == _W_A_END + (_W_B_END - _W_B_START) + 2 * GROUP_WIDTH
    tm = 128
    return pl.pallas_call(
        _w_in_kernel,
        grid=(depth, d // tm),
        in_specs=[pl.BlockSpec((None, tm, cols), lambda l, i: (l, i, 0))],
        out_specs=[pl.BlockSpec((None, tm, MAIN_COLS), lambda l, i: (l, i, 0)),
                   pl.BlockSpec((None, tm, SMALL_COLS), lambda l, i: (l, i, 0))],
        out_shape=[jax.ShapeDtypeStruct((depth, d, MAIN_COLS), BF16),
                   jax.ShapeDtypeStruct((depth, d, SMALL_COLS), BF16)],
        compiler_params=_cparams(("parallel", "parallel")),
        name="w_in_prep",
    )(w_in)


def _lane_row(vals, lane0):
    row = jnp.zeros((1, SMALL_COLS), F32)
    return row.at[0, lane0:lane0 + vals.shape[0]].set(vals.astype(F32))


def _rope_tables(pos):
    half = HEAD_DIM // 2
    inv = ROPE_THETA ** (-jnp.arange(half, dtype=F32) / half)
    ang = pos.astype(F32)[:, None] * inv[None, :]
    cos, sin = jnp.cos(ang), jnp.sin(ang)
    return jnp.concatenate([cos, cos], axis=-1), jnp.concatenate([-sin, sin], axis=-1)


def _round_up(x, m):
    return (x + m - 1) // m * m


def _mixer_layer(x, conv_buf, s_gdn, s_ret, past_k, past_v, past_ki, mem_k, mem_v, wts, layer, depth, prev_cache):
    b, l, d = x.shape
    n = b * l
    offset = 0 if past_k is None else past_k.shape[2]
    x2d = x.reshape(n, d)
    proj, small = _in_proj(x2d, wts["norm_g"], wts["w_main"], wts["w_small"], layer)

    o_a, conv_new, s_gdn_new = _gdn(proj, small, wts["conv_w"], conv_buf, s_gdn, wts["alog_row"],
                                    wts["dtb_row"], wts["gdn_norm_g"], b, l)

    cos_t, sin_t = wts["rope_p"] if past_k is None else wts["rope_s"]
    o_b, s_ret_new = _retention(proj, cos_t, sin_t, s_ret, wts["ret_norm_g"], wts["ret_norm_b"], b, l)

    prefill = past_k is None
    prep = _dsa_prep(proj, small, wts["dsa_k_norm_g"], wts["idx_k_norm_g"], layer, depth, prev_cache, prefill)
    cache = tuple(prep[:3])
    if prefill:
        assert l % KT_A == 0, "prefill length must be a multiple of the key tile"
        o_c = _dsa(proj, small, prep[3].reshape(b, l, GROUP_WIDTH), prep[5].reshape(b, l // KT_A, GROUP_WIDTH, KT_A),
                   prep[4].reshape(b, l, IDX_DIM), wts["dsa_q_norm_g"], b, l, 256, 0, l)
    else:
        assert offset % KT_A == 0 and l % 16 == 0 and l <= 128, "decode step shape not supported"
        bias = _dsa_select(proj, small, past_ki, prep[4], b, l, offset, layer)
        o_c = _dsa_decode(proj, bias, past_k, past_v, prep[3], cache[1], wts["dsa_q_norm_g"], b, l, offset, layer)

    o_d = _mem_attend(proj, mem_k, mem_v, wts["mem_q_norm_g"], b, l)

    y = _out_proj(x2d, o_a, o_b, o_c, o_d, wts["w_out"]).reshape(b, l, d)
    return y, (conv_new, s_gdn_new, s_ret_new), cache


def kernel(x_prompt, x_sample, cache_gdn_conv, state_gdn, state_ret, cache_dsa_k, cache_dsa_v, cache_idx_k, cache_mem_k, cache_mem_v, mem_prompt, norm_g, w_in, gdn_conv_w, gdn_a_log, gdn_dt_bias, gdn_norm_g, ret_norm_g, ret_norm_b, dsa_q_norm_g, dsa_k_norm_g, idx_k_norm_g, mem_norm_g, w_mem_kv, mem_q_norm_g, mem_k_norm_g, w_out):
    depth = w_in.shape[0]
    b = x_prompt.shape[0]
    n_mem = mem_prompt.shape[1]
    d = x_prompt.shape[-1]
    y_p, y_s = x_prompt, x_sample
    st_p, st_s, mem_p = [], [], []
    cache_p = cache_s = None
    w_main, w_small = _prep_w_in(w_in)
    rope_p = _rope_tables(jnp.arange(x_prompt.shape[1], dtype=I32))
    rope_s = _rope_tables(cache_dsa_k.shape[2] + jnp.arange(x_sample.shape[1], dtype=I32))
    for li in range(depth):
        wts = dict(
            norm_g=norm_g[li][None, :], w_main=w_main, w_small=w_small, rope_p=rope_p, rope_s=rope_s,
            conv_w=gdn_conv_w[li],
            alog_row=_lane_row(gdn_a_log[li], LANE_ALPHA), dtb_row=_lane_row(gdn_dt_bias[li], LANE_ALPHA),
            gdn_norm_g=gdn_norm_g[li][None, :], ret_norm_g=ret_norm_g[li][None, :],
            ret_norm_b=ret_norm_b[li][None, :], dsa_q_norm_g=dsa_q_norm_g[li][None, :],
            dsa_k_norm_g=dsa_k_norm_g[li][None, :], idx_k_norm_g=idx_k_norm_g[li][None, :],
            mem_q_norm_g=mem_q_norm_g[li][None, :], w_out=w_out[li].astype(BF16),
        )
        mk, mv = _memory_kv(mem_prompt.reshape(b * n_mem, d), mem_norm_g[li][None, :],
                            w_mem_kv[li].astype(BF16), mem_k_norm_g[li][None, :])
        mk = mk.reshape(b, n_mem, GROUP_WIDTH)
        mv = mv.reshape(b, n_mem, GROUP_WIDTH)
        conv0 = jnp.zeros((b, CONV_W - 1, 3 * GROUP_WIDTH), F32)
        s0 = jnp.zeros((b, N_HEADS, HEAD_DIM, HEAD_DIM), F32)
        y_p, sp, cache_p = _mixer_layer(y_p, conv0, s0, s0, None, None, None, mk, mv, wts, li, depth, cache_p)
        st_p.append(sp)
        mem_p.append((mk.reshape(b, n_mem, N_HEADS, HEAD_DIM), mv.reshape(b, n_mem, N_HEADS, HEAD_DIM)))
        bs = x_sample.shape[0]
        y_s, ss, cache_s = _mixer_layer(y_s, cache_gdn_conv[li], state_gdn[li], state_ret[li],
                                        cache_dsa_k, cache_dsa_v, cache_idx_k,
                                        cache_mem_k[li].reshape(bs, n_mem, GROUP_WIDTH),
                                        cache_mem_v[li].reshape(bs, n_mem, GROUP_WIDTH), wts, li, depth, cache_s)
        st_s.append(ss)

    def stack(lst, k):
        return jnp.stack([s[k] for s in lst])

    def caches(c, bb, ll):
        return (c[0].reshape(depth, bb, ll, N_HEADS, HEAD_DIM), c[1].reshape(depth, bb, ll, N_HEADS, HEAD_DIM),
                c[2].reshape(depth, bb, ll, IDX_DIM))

    return ((y_p, y_s, stack(st_p, 0), stack(st_p, 1), stack(st_p, 2))
            + caches(cache_p, b, x_prompt.shape[1])
            + (stack(mem_p, 0), stack(mem_p, 1), stack(st_s, 0), stack(st_s, 1), stack(st_s, 2))
            + caches(cache_s, x_sample.shape[0], x_sample.shape[1]))
```

```python
import functools
import math

import jax
import jax.numpy as jnp
from jax import lax
from jax.experimental import pallas as pl
from jax.experimental.pallas import tpu as pltpu

F32 = jnp.float32
BF16 = jnp.bfloat16
I32 = jnp.int32

HEAD_DIM = 128
N_HEADS = 4
GROUP_WIDTH = N_HEADS * HEAD_DIM
CHUNK = 64
CONV_W = 4
IDX_HEADS = 16
IDX_DIM = 64
TOPK_MAX = 256
ROPE_THETA = 10000.0
EPS = 1e-6

COL_QA, COL_KA, COL_VA, COL_ZA = 0, 512, 1024, 1536
COL_QB, COL_KB, COL_VB, COL_ZB = 2048, 2560, 3072, 3584
COL_QC, COL_KC, COL_VC, COL_ZC = 4096, 4608, 5120, 5632
COL_QI = 6144
COL_QD, COL_ZD = 7168, 7680
MAIN_COLS = 8192
LANE_KI, LANE_WI, LANE_BETA, LANE_ALPHA = 0, 64, 80, 84
SMALL_COLS = 128

INT_MIN = -2 ** 31
NEG_BIG = -1e30
LOG2_E = 1.4426950408889634
SUBLANES = 8
LANES = 128
SUBLANES_16BIT = 16
VMEM_LIMIT = 56 * 1024 * 1024
HI = lax.Precision.HIGHEST


def _cparams(sem):
    return pltpu.CompilerParams(dimension_semantics=sem, vmem_limit_bytes=VMEM_LIMIT)


def _dot(a, b):
    return jnp.dot(a, b, preferred_element_type=F32)


def _dot_nt(a, b):
    return lax.dot_general(a, b, (((1,), (1,)), ((), ())), preferred_element_type=F32)


def _dot3(a, b):
    a_hi = a.astype(BF16)
    b_hi = b.astype(BF16)
    a_lo = (a - a_hi.astype(F32)).astype(BF16)
    b_lo = (b - b_hi.astype(F32)).astype(BF16)
    return _dot(a_hi, b_hi) + (_dot(a_hi, b_lo) + _dot(a_lo, b_hi))


def _silu(x):
    return x * jax.nn.sigmoid(x)


def _inproj_kernel(x_ref, g_ref, w_ref, ws_ref, o_ref, os_ref, h_ref):
    @pl.when(pl.program_id(1) == 0)
    def _():
        x = x_ref[...]
        y = x * lax.rsqrt(jnp.mean(x * x, axis=-1, keepdims=True) + EPS) * g_ref[...]
        hb = y.astype(BF16)
        h_ref[...] = hb
        os_ref[...] = _dot(hb, ws_ref[...])

    o_ref[...] = _dot(h_ref[...], w_ref[...])


def _in_proj(x2d, g_row, w_main, w_small, layer):
    n, d = x2d.shape
    tm = min(1024, n)
    tn = 1024
    return pl.pallas_call(
        _inproj_kernel,
        grid=(n // tm, MAIN_COLS // tn),
        in_specs=[
            pl.BlockSpec((tm, d), lambda i, j: (i, 0)),
            pl.BlockSpec((1, d), lambda i, j: (0, 0)),
            pl.BlockSpec((None, d, tn), lambda i, j: (layer, 0, j)),
            pl.BlockSpec((None, d, SMALL_COLS), lambda i, j: (layer, 0, 0)),
        ],
        out_specs=[
            pl.BlockSpec((tm, tn), lambda i, j: (i, j)),
            pl.BlockSpec((tm, SMALL_COLS), lambda i, j: (i, 0)),
        ],
        out_shape=[jax.ShapeDtypeStruct((n, MAIN_COLS), F32),
                   jax.ShapeDtypeStruct((n, SMALL_COLS), F32)],
        scratch_shapes=[pltpu.VMEM((tm, d), BF16)],
        compiler_params=_cparams(("parallel", "arbitrary")),
        name="in_proj",
    )(x2d, g_row, w_main, w_small)


def _memkv_kernel(x_ref, g_ref, w_ref, gk_ref, mk_ref, mv_ref):
    x = x_ref[...]
    y = x * lax.rsqrt(jnp.mean(x * x, axis=-1, keepdims=True) + EPS) * g_ref[...]
    kv = _dot(y.astype(BF16), w_ref[...])
    gk = gk_ref[...]
    for h in range(N_HEADS):
        sl = slice(h * HEAD_DIM, (h + 1) * HEAD_DIM)
        kh = kv[:, sl]
        mk_ref[:, sl] = kh * lax.rsqrt(jnp.mean(kh * kh, axis=-1, keepdims=True) + EPS) * gk
    mv_ref[...] = kv[:, GROUP_WIDTH:]


def _memory_kv(mem2d, g_row, w_kv, gk_row):
    n, d = mem2d.shape
    tm = min(256, n)
    return pl.pallas_call(
        _memkv_kernel,
        grid=(n // tm,),
        in_specs=[
            pl.BlockSpec((tm, d), lambda i: (i, 0)),
            pl.BlockSpec((1, d), lambda i: (0, 0)),
            pl.BlockSpec((d, 2 * GROUP_WIDTH), lambda i: (0, 0)),
            pl.BlockSpec((1, HEAD_DIM), lambda i: (0, 0)),
        ],
        out_specs=[pl.BlockSpec((tm, GROUP_WIDTH), lambda i: (i, 0)),
                   pl.BlockSpec((tm, GROUP_WIDTH), lambda i: (i, 0))],
        out_shape=[jax.ShapeDtypeStruct((n, GROUP_WIDTH), F32),
                   jax.ShapeDtypeStruct((n, GROUP_WIDTH), F32)],
        compiler_params=_cparams(("parallel",)),
        name="memory_kv",
    )(mem2d, g_row, w_kv, gk_row)


def _gdn_kernel(qkv_ref, z_ref, sm_ref, cw_ref, cb_ref, s0_ref, alog_ref, dtb_ref, gn_ref,
                o_ref, conv_ref, st_ref, xbuf_ref, s_ref, *, t_blk, chunk):
    j = pl.program_id(1)
    nj = pl.num_programs(1)
    gw3 = 3 * GROUP_WIDTH

    hdr = SUBLANES
    tail = hdr - (CONV_W - 1)

    @pl.when(j == 0)
    def _():
        xbuf_ref[0:hdr, :] = jnp.zeros((hdr, gw3), F32)
        xbuf_ref[tail:hdr, :] = cb_ref[0]
        s_ref[...] = s0_ref[0]

    @pl.when(j > 0)
    def _():
        xbuf_ref[0:hdr, :] = xbuf_ref[t_blk:t_blk + hdr, :]

    xbuf_ref[hdr:hdr + t_blk, :] = qkv_ref[...]
    conv_ref[0] = xbuf_ref[t_blk + tail:t_blk + hdr, :]

    cw = cw_ref[...]
    x_all = xbuf_ref[...]
    x_prev = pltpu.roll(x_all, 1, 0)
    u = x_all * cw[3:4, :] + x_prev * cw[2:3, :]
    v = x_all * cw[1:2, :] + x_prev * cw[0:1, :]
    y = _silu((u + pltpu.roll(v, 2, 0))[hdr:hdr + t_blk, :])

    sm = sm_ref[...]
    lane = lax.broadcasted_iota(I32, sm.shape, 1)
    beta_all = jax.nn.sigmoid(sm)
    xs = sm + dtb_ref[...]
    softplus = jnp.maximum(xs, 0.0) + jnp.log1p(jnp.exp(-jnp.abs(xs)))
    g_all = -jnp.exp(alog_ref[...]) * softplus
    g_all = jnp.where((lane >= LANE_ALPHA) & (lane < LANE_ALPHA + N_HEADS), g_all, 0.0)
    ri = lax.broadcasted_iota(I32, (t_blk, t_blk), 0)
    ci = lax.broadcasted_iota(I32, (t_blk, t_blk), 1)
    tri = jnp.where((ri // chunk == ci // chunk) & (ci <= ri), 1.0, 0.0).astype(F32)
    gcum = jnp.dot(tri, g_all, preferred_element_type=F32, precision=HI)
    gcum_t = gcum.T

    blk = min(2 * chunk, t_blk)
    n_blk = t_blk // blk
    cpb = blk // chunk
    rb = lax.broadcasted_iota(I32, (blk, blk), 0)
    cb = lax.broadcasted_iota(I32, (blk, blk), 1)
    same = (rb // chunk) == (cb // chunk)
    causal = same & (cb <= rb)
    strict = same & (cb < rb)
    eye = jnp.where(rb == cb, 1.0, 0.0).astype(F32)
    n_dbl = max(int(math.log2(chunk)) - 1, 0)
    gn = gn_ref[...]
    z_all = z_ref[...]

    qn, kn, vv = [], [], []
    for h in range(N_HEADS):
        qh = y[:, COL_QA + h * HEAD_DIM:COL_QA + (h + 1) * HEAD_DIM]
        kh = y[:, COL_KA + h * HEAD_DIM:COL_KA + (h + 1) * HEAD_DIM]
        qn.append(qh * lax.rsqrt(jnp.sum(qh * qh, axis=-1, keepdims=True) + EPS) * (HEAD_DIM ** -0.5))
        kn.append(kh * lax.rsqrt(jnp.sum(kh * kh, axis=-1, keepdims=True) + EPS))
        vv.append(y[:, COL_VA + h * HEAD_DIM:COL_VA + (h + 1) * HEAD_DIM])

    qs, ks, gcols, p_mats, x_mats, qk_mats, vb_mats, kg_mats, u_mats, w_mats = ({} for _ in range(10))

    def wy_operands(b):
        rs = slice(b * blk, (b + 1) * blk)
        for h in range(N_HEADS):
            gcol = gcum[rs, LANE_ALPHA + h:LANE_ALPHA + h + 1]
            grow = gcum_t[LANE_ALPHA + h:LANE_ALPHA + h + 1, rs]
            bcol = beta_all[rs, LANE_BETA + h:LANE_BETA + h + 1]
            decay = jnp.where(causal, jnp.exp(jnp.where(causal, gcol - grow, 0.0)), 0.0)
            qc, kc, vc = qn[h][rs], kn[h][rs], vv[h][rs]
            kb = kc * bcol
            kc16 = kc.astype(BF16)
            a_mat = jnp.where(strict, _dot_nt(kb.astype(BF16), kc16) * decay, 0.0)
            qk_mats[h, b] = jnp.where(causal, _dot_nt(qc.astype(BF16), kc16) * decay, 0.0).astype(BF16)
            p_mats[h, b] = -a_mat
            x_mats[h, b] = eye - a_mat
            vb_mats[h, b] = (vc * bcol).astype(BF16)
            kg_mats[h, b] = (kb * jnp.exp(gcol)).astype(BF16)
            qs[h, b], ks[h, b], gcols[h, b] = qc, kc, gcol

    def square_p(b):
        for h in range(N_HEADS):
            p_mats[h, b] = _dot3(p_mats[h, b], p_mats[h, b])

    def extend_x(b):
        for h in range(N_HEADS):
            x_mats[h, b] = x_mats[h, b] + _dot3(x_mats[h, b], p_mats[h, b])

    def wy_finish(b):
        for h in range(N_HEADS):
            x16 = x_mats[h, b].astype(BF16)
            u_mats[h, b] = _dot(x16, vb_mats[h, b])
            w_mats[h, b] = _dot(x16, kg_mats[h, b]).astype(BF16)

    states = [s_ref[h] for h in range(N_HEADS)]

    def recurrence_tasks(b):
        o_state = {h: [] for h in range(N_HEADS)}
        v_new = {h: [] for h in range(N_HEADS)}

        def chunk_step(c):
            cs = slice(c * chunk, (c + 1) * chunk)
            for h in range(N_HEADS):
                s = states[h]
                s16 = s.astype(BF16)
                gcol = gcols[h, b][cs]
                glast = gcols[h, b][(c + 1) * chunk - 1:(c + 1) * chunk]
                vn = u_mats[h, b][cs] - _dot(w_mats[h, b][cs], s16)
                vn16 = vn.astype(BF16)
                kd_t = (ks[h, b][cs] * jnp.exp(glast - gcol)).T.astype(BF16)
                states[h] = s * jnp.exp(glast) + _dot(kd_t, vn16)
                o_state[h].append(_dot((qs[h, b][cs] * jnp.exp(gcol)).astype(BF16), s16))
                v_new[h].append(vn16)

        def emit():
            rs = slice(b * blk, (b + 1) * blk)
            for h in range(N_HEADS):
                sl = slice(h * HEAD_DIM, (h + 1) * HEAD_DIM)
                o = jnp.concatenate(o_state[h], axis=0) + _dot(qk_mats[h, b], jnp.concatenate(v_new[h], axis=0))
                on = o * lax.rsqrt(jnp.mean(o * o, axis=-1, keepdims=True) + EPS) * gn
                o_ref[rs, sl] = (on * _silu(z_all[rs, sl])).astype(o_ref.dtype)

        return [functools.partial(chunk_step, c) for c in range(cpb)] + [emit]

    blocks = range(n_blk)
    for b in blocks:
        wy_operands(b)
    for _ in range(n_dbl):
        for b in blocks:
            square_p(b)
        for b in blocks:
            extend_x(b)
    for b in blocks:
        wy_finish(b)
    for b in blocks:
        for task in recurrence_tasks(b):
            task()
    for h in range(N_HEADS):
        s_ref[h] = states[h]

    @pl.when(j == nj - 1)
    def _():
        st_ref[0] = s_ref[...]


def _gdn(proj, small, conv_w, conv_buf, s0, alog_row, dtb_row, gn_row, batch, seqlen):
    n = batch * seqlen
    t_blk = min(256, seqlen)
    chunk = min(CHUNK, seqlen)
    nj = seqlen // t_blk
    gw3 = 3 * GROUP_WIDTH
    kern = functools.partial(_gdn_kernel, t_blk=t_blk, chunk=chunk)
    return pl.pallas_call(
        kern,
        grid=(batch, nj),
        in_specs=[
            pl.BlockSpec((t_blk, gw3), lambda b, j: (b * nj + j, COL_QA // gw3)),
            pl.BlockSpec((t_blk, GROUP_WIDTH), lambda b, j: (b * nj + j, COL_ZA // GROUP_WIDTH)),
            pl.BlockSpec((t_blk, SMALL_COLS), lambda b, j: (b * nj + j, 0)),
            pl.BlockSpec((CONV_W, gw3), lambda b, j: (0, 0)),
            pl.BlockSpec((1, CONV_W - 1, gw3), lambda b, j: (b, 0, 0)),
            pl.BlockSpec((1, N_HEADS, HEAD_DIM, HEAD_DIM), lambda b, j: (b, 0, 0, 0)),
            pl.BlockSpec((1, SMALL_COLS), lambda b, j: (0, 0)),
            pl.BlockSpec((1, SMALL_COLS), lambda b, j: (0, 0)),
            pl.BlockSpec((1, HEAD_DIM), lambda b, j: (0, 0)),
        ],
        out_specs=[
            pl.BlockSpec((t_blk, GROUP_WIDTH), lambda b, j: (b * nj + j, 0)),
            pl.BlockSpec((1, CONV_W - 1, gw3), lambda b, j: (b, 0, 0)),
            pl.BlockSpec((1, N_HEADS, HEAD_DIM, HEAD_DIM), lambda b, j: (b, 0, 0, 0)),
        ],
        out_shape=[
            jax.ShapeDtypeStruct((n, GROUP_WIDTH), BF16),
            jax.ShapeDtypeStruct((batch, CONV_W - 1, gw3), F32),
            jax.ShapeDtypeStruct((batch, N_HEADS, HEAD_DIM, HEAD_DIM), F32),
        ],
        scratch_shapes=[pltpu.VMEM((t_blk + SUBLANES, gw3), F32),
                        pltpu.VMEM((N_HEADS, HEAD_DIM, HEAD_DIM), F32)],
        compiler_params=_cparams(("parallel", "arbitrary")),
        name="gdn",
    )(proj, proj, small, conv_w, conv_buf, s0, alog_row, dtb_row, gn_row)


def _ret_kernel(q_ref, k_ref, v_ref, z_ref, cos_ref, sin_ref, s0_ref, g_ref, b_ref,
                o_ref, st_ref, s_ref, dmat_ref, *, t_blk):
    j = pl.program_id(1)
    nj = pl.num_programs(1)
    log_gamma = [math.log(1.0 - 2.0 ** (-5.0 - h)) for h in range(N_HEADS)]

    @pl.when(j == 0)
    def _():
        s_ref[...] = s0_ref[0]
        ri = lax.broadcasted_iota(I32, (t_blk, t_blk), 0)
        ci = lax.broadcasted_iota(I32, (t_blk, t_blk), 1)
        causal = ci <= ri
        rel = jnp.where(causal, ri - ci, 0).astype(F32)
        for h in range(N_HEADS):
            dmat_ref[h] = jnp.where(causal, jnp.exp(log_gamma[h] * rel), 0.0)

    cos = cos_ref[...]
    sin = sin_ref[...]
    idx_col = lax.broadcasted_iota(I32, (t_blk, 1), 0).astype(F32)
    gamma_g = g_ref[...]
    gamma_b = b_ref[...]
    q_all, k_all, v_all, z_all = q_ref[...], k_ref[...], v_ref[...], z_ref[...]

    for h in range(N_HEADS):
        sl = slice(h * HEAD_DIM, (h + 1) * HEAD_DIM)
        lg = log_gamma[h]
        qh, kh, vh = q_all[:, sl], k_all[:, sl], v_all[:, sl]
        qh = qh * cos + pltpu.roll(qh, HEAD_DIM // 2, 1) * sin
        kh = (kh * cos + pltpu.roll(kh, HEAD_DIM // 2, 1) * sin) * (HEAD_DIM ** -0.5)
        q16, k16, v16 = qh.astype(BF16), kh.astype(BF16), vh.astype(BF16)
        o_intra = _dot((_dot_nt(q16, k16) * dmat_ref[h]).astype(BF16), v16)
        s = s_ref[h]
        o_cross = _dot(q16, s.astype(BF16)) * jnp.exp(lg * (idx_col + 1.0))
        kd_t = (kh * jnp.exp(lg * (t_blk - 1.0 - idx_col))).T.astype(BF16)
        s_ref[h] = s * math.exp(lg * t_blk) + _dot(kd_t, v16)
        o = o_intra + o_cross
        mu = jnp.mean(o, axis=-1, keepdims=True)
        oc = o - mu
        var = jnp.mean(oc * oc, axis=-1, keepdims=True)
        on = oc * lax.rsqrt(var + EPS) * gamma_g + gamma_b
        o_ref[:, sl] = (on * _silu(z_all[:, sl])).astype(o_ref.dtype)

    @pl.when(j == nj - 1)
    def _():
        st_ref[0] = s_ref[...]


def _retention(proj, cos_t, sin_t, s0, g_row, b_row, batch, seqlen):
    n = batch * seqlen
    t_blk = min(256, seqlen)
    nj = seqlen // t_blk
    gw = GROUP_WIDTH
    kern = functools.partial(_ret_kernel, t_blk=t_blk)

    def col(c):
        return pl.BlockSpec((t_blk, gw), lambda b, j: (b * nj + j, c // gw))

    return pl.pallas_call(
        kern,
        grid=(batch, nj),
        in_specs=[
            col(COL_QB), col(COL_KB), col(COL_VB), col(COL_ZB),
            pl.BlockSpec((t_blk, HEAD_DIM), lambda b, j: (j, 0)),
            pl.BlockSpec((t_blk, HEAD_DIM), lambda b, j: (j, 0)),
            pl.BlockSpec((1, N_HEADS, HEAD_DIM, HEAD_DIM), lambda b, j: (b, 0, 0, 0)),
            pl.BlockSpec((1, HEAD_DIM), lambda b, j: (0, 0)),
            pl.BlockSpec((1, HEAD_DIM), lambda b, j: (0, 0)),
        ],
        out_specs=[
            pl.BlockSpec((t_blk, gw), lambda b, j: (b * nj + j, 0)),
            pl.BlockSpec((1, N_HEADS, HEAD_DIM, HEAD_DIM), lambda b, j: (b, 0, 0, 0)),
        ],
        out_shape=[
            jax.ShapeDtypeStruct((n, gw), BF16),
            jax.ShapeDtypeStruct((batch, N_HEADS, HEAD_DIM, HEAD_DIM), F32),
        ],
        scratch_shapes=[pltpu.VMEM((N_HEADS, HEAD_DIM, HEAD_DIM), F32),
                        pltpu.VMEM((N_HEADS, t_blk, t_blk), F32)],
        compiler_params=_cparams(("parallel", "arbitrary")),
        name="retention",
    )(proj, proj, proj, proj, cos_t, sin_t, s0, g_row, b_row)


def _dsa_prep_kernel(k_ref, v_ref, sm_ref, gk_ref, gi_ref, *rest, with_vt):
    outs = rest[3:]
    ko_ref, vo_ref, kio_ref, k16_ref, ki16_ref = outs[:5]
    k = k_ref[...]
    tm = k.shape[0]
    gk = gk_ref[...]
    for h in range(N_HEADS):
        sl = slice(h * HEAD_DIM, (h + 1) * HEAD_DIM)
        kh = k[:, sl]
        kn = kh * lax.rsqrt(jnp.mean(kh * kh, axis=-1, keepdims=True) + EPS) * gk
        ko_ref[pl.ds(h, tm, stride=N_HEADS), :] = kn
        k16_ref[:, sl] = kn.astype(BF16)
    v = v_ref[...]
    for h in range(N_HEADS):
        vo_ref[pl.ds(h, tm, stride=N_HEADS), :] = v[:, h * HEAD_DIM:(h + 1) * HEAD_DIM]
    if with_vt:
        outs[5][...] = v.T.astype(BF16)
    ki = sm_ref[...][:, LANE_KI:LANE_KI + IDX_DIM]
    kin = ki * lax.rsqrt(jnp.mean(ki * ki, axis=-1, keepdims=True) + EPS) * gi_ref[...]
    kio_ref[...] = kin
    ki16_ref[...] = kin.astype(BF16)


def _dsa_prep(proj, small, gk_row, gi_row, layer, depth, prev, with_vt):
    n = proj.shape[0]
    tm = min(KT_A, n)
    gw = GROUP_WIDTH
    kern = functools.partial(_dsa_prep_kernel, with_vt=with_vt)
    in_specs = [
        pl.BlockSpec((tm, gw), lambda i: (i, COL_KC // gw)),
        pl.BlockSpec((tm, gw), lambda i: (i, COL_VC // gw)),
        pl.BlockSpec((tm, SMALL_COLS), lambda i: (i, 0)),
        pl.BlockSpec((1, HEAD_DIM), lambda i: (0, 0)),
        pl.BlockSpec((1, IDX_DIM), lambda i: (0, 0)),
    ] + [pl.BlockSpec(memory_space=pl.ANY)] * 3
    out_specs = [pl.BlockSpec((None, tm * N_HEADS, HEAD_DIM), lambda i: (layer, i, 0)),
                 pl.BlockSpec((None, tm * N_HEADS, HEAD_DIM), lambda i: (layer, i, 0)),
                 pl.BlockSpec((None, tm, IDX_DIM), lambda i: (layer, i, 0)),
                 pl.BlockSpec((tm, gw), lambda i: (i, 0)),
                 pl.BlockSpec((tm, IDX_DIM), lambda i: (i, 0))]
    out_shape = [jax.ShapeDtypeStruct((depth, n * N_HEADS, HEAD_DIM), F32),
                 jax.ShapeDtypeStruct((depth, n * N_HEADS, HEAD_DIM), F32),
                 jax.ShapeDtypeStruct((depth, n, IDX_DIM), F32),
                 jax.ShapeDtypeStruct((n, gw), BF16),
                 jax.ShapeDtypeStruct((n, IDX_DIM), BF16)]
    if with_vt:
        out_specs.append(pl.BlockSpec((None, gw, tm), lambda i: (i, 0, 0)))
        out_shape.append(jax.ShapeDtypeStruct((n // tm, gw, tm), BF16))
    if prev is None:
        prev = tuple(jnp.zeros(s.shape, s.dtype) for s in out_shape[:3])
    return pl.pallas_call(
        kern,
        grid=(n // tm,),
        in_specs=in_specs,
        out_specs=out_specs,
        out_shape=out_shape,
        input_output_aliases={5 + t: t for t in range(3)},
        compiler_params=_cparams(("parallel",)),
        name="dsa_prep",
    )(proj, proj, small, gk_row, gi_row, *prev)


KT_I = 128
KT_A = 512


def _dsa_kernel(q_ref, qi_ref, z_ref, sm_ref, k_ref, vt_ref, ki_ref, gq_ref, o_ref,
                key_ref, qit_ref, wt_ref, jlim_ref, qt_ref, acc_ref, lg_ref, p_ref, key16_ref, low16_ref, *, tq, offset, s_valid, topk):
    i = pl.program_id(1)
    pos0 = offset + i * tq
    t_pos = pos0 + lax.broadcasted_iota(I32, (1, tq), 1)
    t_chunk = t_pos // CHUNK
    n_adm_row = jnp.minimum((t_chunk + 1) * CHUNK, s_valid)
    n_keys = jnp.minimum(((pos0 + tq - 1) // CHUNK + 1) * CHUNK, s_valid)
    n_at = (n_keys + KT_A - 1) // KT_A
    n_sub = KT_A // KT_I

    qit_ref[...] = qi_ref[...].T.astype(BF16)
    w_t = sm_ref[...].T * (IDX_HEADS ** -0.5 * IDX_DIM ** -0.5)
    for h in range(IDX_HEADS):
        wt_ref[h] = jnp.broadcast_to(w_t[LANE_WI + h:LANE_WI + h + 1, :], (SUBLANES, tq))

    def index_step(kt, carry):
        r0 = pl.multiple_of(kt * KT_I, KT_I)
        ki_t = ki_ref[0, pl.ds(r0, KT_I), :]
        acc = jnp.zeros((KT_I // SUBLANES, SUBLANES, tq), F32)
        for h in range(IDX_HEADS):
            sc = _dot(ki_t, qit_ref[h * IDX_DIM:(h + 1) * IDX_DIM, :])
            acc = acc + wt_ref[h][None] * jnp.maximum(sc, 0.0).reshape(KT_I // SUBLANES, SUBLANES, tq)
        bits = lax.bitcast_convert_type(acc.reshape(KT_I, tq), I32)
        key = bits ^ ((bits >> 31) & 0x7FFFFFFF)
        s_pos = r0 + lax.broadcasted_iota(I32, (KT_I, 1), 0)
        adm = (s_pos // CHUNK <= t_chunk) & (s_pos < s_valid)
        key = jnp.where(adm, key, INT_MIN)
        key_ref[pl.ds(r0, KT_I), :] = key
        key16_ref[pl.ds(r0, KT_I), :] = (key >> 16).astype(jnp.int16)
        return carry

    def index_trip(t, c):
        for s in range(n_sub):
            c = index_step(n_sub * t + s, c)
        return c

    lax.fori_loop(0, n_at, index_trip, 0)

    def count(pred_fn):
        def body(kt, accs):
            out = []
            for s in range(n_sub):
                r0 = pl.multiple_of(kt * KT_A + s * KT_I, KT_I)
                key = key_ref[pl.ds(r0, KT_I), :]
                s_pos = r0 + lax.broadcasted_iota(I32, (KT_I, 1), 0)
                m = jnp.where(pred_fn(key, s_pos), 1, 0).astype(I32)
                out.append(accs[s] + jnp.sum(m.reshape(KT_I // SUBLANES, SUBLANES, tq), axis=0))
            return tuple(out)
        accs = lax.fori_loop(0, n_at, body, tuple(jnp.zeros((SUBLANES, tq), I32) for _ in range(n_sub)))
        return jnp.sum(sum(accs[1:], accs[0]), axis=0, keepdims=True)

    def count16(ref16, pred_fn):
        def body(kt, accs):
            out = []
            for s in range(n_sub):
                r0 = pl.multiple_of(kt * KT_A + s * KT_I, KT_I)
                m = jnp.where(pred_fn(ref16[pl.ds(r0, KT_I), :]), jnp.int16(1), jnp.int16(0))
                m = m.reshape(KT_I // SUBLANES_16BIT, SUBLANES_16BIT, tq)
                acc = accs[s]
                for q in range(KT_I // SUBLANES_16BIT):
                    acc = acc + m[q]
                out.append(acc)
            return tuple(out)
        accs = lax.fori_loop(0, n_at, body, tuple(jnp.zeros((SUBLANES_16BIT, tq), jnp.int16) for _ in range(n_sub)))
        tot = sum((a.astype(I32) for a in accs[1:]), accs[0].astype(I32))
        return jnp.sum(tot, axis=0, keepdims=True)

    small = jnp.where(n_adm_row <= topk, 1, 0).astype(I32)

    def all_done(done):
        return jnp.min(done.astype(F32)) > 0.0

    def search(first_bit, last_bit, count_ge, carry, early_exit):
        def step(b, c):
            v, done, thr, c_rej = c
            cand_u = v | jnp.left_shift(jnp.int32(1), 31 - b)
            cand_s = cand_u ^ INT_MIN
            cnt = count_ge(cand_s)
            accept = cnt >= topk
            v = jnp.where(accept, cand_u, v)
            newly = (cnt == topk) & (done == 0)
            thr = jnp.where(newly, cand_s, thr)
            return v, jnp.where(newly, 1, done), thr, jnp.where(accept, c_rej, cnt)

        if not early_exit:
            return lax.fori_loop(first_bit, last_bit, step, carry)

        def cond(c):
            return (c[0] < last_bit) & jnp.logical_not(all_done(c[1][1]))

        def body(c):
            b, inner = c
            return b + 4, lax.fori_loop(b, b + 4, step, inner)

        assert (last_bit - first_bit) % 4 == 0
        return lax.while_loop(cond, body, (jnp.int32(first_bit), carry))[1]

    v0 = jnp.zeros((1, tq), I32)
    thr0 = jnp.full((1, tq), INT_MIN + 1, I32)
    carry = search(0, 16, lambda cand_s: count16(key16_ref, lambda k16: k16 >= (cand_s >> 16).astype(jnp.int16)),
                   (v0, small, thr0, jnp.zeros((1, tq), I32)), early_exit=False)
    n_above = carry[3]

    @pl.when(jnp.logical_not(all_done(carry[1])))
    def _():
        hi32 = (carry[0] ^ INT_MIN) >> 16

        def low_step(kt, c):
            for s in range(n_sub):
                r0 = pl.multiple_of(kt * KT_A + s * KT_I, KT_I)
                key = key_ref[pl.ds(r0, KT_I), :]
                low = jnp.where((key >> 16) == hi32, (key & 0xFFFF) - 32768, -32768)
                low16_ref[pl.ds(r0, KT_I), :] = low.astype(jnp.int16)
            return c

        lax.fori_loop(0, n_at, low_step, 0)

    def count_low(cand_s):
        c16 = ((cand_s & 0xFFFF) - 32768).astype(jnp.int16)
        return n_above + count16(low16_ref, lambda l16: l16 >= c16)

    v_u, done_i, thr, _ = search(16, 32, count_low, carry, early_exit=True)
    done = done_i != 0
    v_s = v_u ^ INT_MIN
    any_tie = jnp.logical_not(all_done(done_i))

    jlim_ref[...] = jnp.zeros(jlim_ref.shape, I32)

    @pl.when(any_tie)
    def _():
        n_gt = count(lambda key, s_pos: key > v_s)
        need = topk - n_gt
        pos_bits = max(int(math.ceil(math.log2(max(k_ref.shape[1], 2)))), 1) + 1

        def pos_step(b, jv):
            cand = jv | jnp.left_shift(jnp.int32(1), pos_bits - 1 - b)
            cnt = count(lambda key, s_pos: (key == v_s) & (s_pos < cand))
            return jnp.where(cnt <= need, cand, jv)

        jlim_ref[0:1, :] = lax.fori_loop(0, pos_bits, pos_step, jnp.zeros((1, tq), I32))

    j_lim = jlim_ref[0:1, :]

    v_eff = jnp.where(done, thr - 1, v_s)
    j_eff = jnp.where(done, 0, j_lim)

    def bias_loop(with_ties):
        def bias_step(kt, carry):
            for s in range(n_sub):
                r0 = pl.multiple_of(kt * KT_A + s * KT_I, KT_I)
                key = key_ref[pl.ds(r0, KT_I), :]
                sel = key > v_eff
                if with_ties:
                    s_pos = r0 + lax.broadcasted_iota(I32, (KT_I, 1), 0)
                    sel = sel | ((key == v_eff) & (s_pos < j_eff) & (key != INT_MIN))
                bias = jnp.where(sel, 0.0, NEG_BIG).astype(F32)
                key_ref[pl.ds(r0, KT_I), :] = lax.bitcast_convert_type(bias, I32)
            return carry

        lax.fori_loop(0, n_at, bias_step, 0)

    pl.when(any_tie)(lambda: bias_loop(True))
    pl.when(jnp.logical_not(any_tie))(lambda: bias_loop(False))

    gq = gq_ref[...]
    q_all = q_ref[...]
    for h in range(N_HEADS):
        sl = slice(h * HEAD_DIM, (h + 1) * HEAD_DIM)
        qh = q_all[:, sl]
        qh = qh * lax.rsqrt(jnp.mean(qh * qh, axis=-1, keepdims=True) + EPS) * gq * (HEAD_DIM ** -0.5 * LOG2_E)
        qt_ref[sl, :] = qh.T.astype(BF16)
    acc_ref[...] = jnp.zeros(acc_ref.shape, F32)

    heads = [slice(h * HEAD_DIM, (h + 1) * HEAD_DIM) for h in range(N_HEADS)]
    kt_h = KT_A // 2
    n_sub_h = kt_h // KT_I

    def issue_scores(kt, half, slot):
        r0 = pl.multiple_of(kt * KT_A + half * kt_h, kt_h)
        for h, sl in enumerate(heads):
            lg_ref[slot, h] = _dot(k_ref[0, pl.ds(r0, kt_h), sl], qt_ref[sl, :])

    def issue_values(kt, half, slot):
        return [_dot(vt_ref[0, kt, sl, half * kt_h:(half + 1) * kt_h], p_ref[slot, h])
                for h, sl in enumerate(heads)]

    def softmax_half(kt, half, slot, ms, ls):
        r0 = pl.multiple_of(kt * KT_A + half * kt_h, kt_h)
        new_ms, new_ls, alphas = [], [], []
        for h in range(N_HEADS):
            mx = None
            for s in range(n_sub_h):
                rs = slice(s * KT_I, (s + 1) * KT_I)
                bias = lax.bitcast_convert_type(key_ref[pl.ds(r0 + s * KT_I, KT_I), :], F32)
                lg = lg_ref[slot, h, rs, :] + bias
                lg_ref[slot, h, rs, :] = lg
                part = jnp.max(lg.reshape(KT_I // SUBLANES, SUBLANES, tq), axis=0)
                mx = part if mx is None else jnp.maximum(mx, part)
            m_new = jnp.maximum(ms[h], jnp.max(mx, axis=0, keepdims=True))
            alphas.append(jnp.exp2(ms[h] - m_new))
            new_ms.append(m_new)
        for h in range(N_HEADS):
            lsum = None
            for s in range(n_sub_h):
                rs = slice(s * KT_I, (s + 1) * KT_I)
                p = jnp.exp2(lg_ref[slot, h, rs, :] - new_ms[h])
                p_ref[slot, h, rs, :] = p.astype(BF16)
                part = jnp.sum(p.reshape(KT_I // SUBLANES, SUBLANES, tq), axis=0)
                lsum = part if lsum is None else lsum + part
            new_ls.append(alphas[h] * ls[h] + jnp.sum(lsum, axis=0, keepdims=True))
        return new_ms, new_ls, alphas

    def accumulate(alphas, pv):
        for h in range(N_HEADS):
            acc_ref[h] = alphas[h] * acc_ref[h] + pv[h]

    def att_step(kt, carry):
        ms, ls = carry
        issue_scores(kt, 0, 0)
        issue_scores(kt, 1, 1)
        ms, ls, alphas_a = softmax_half(kt, 0, 0, ms, ls)
        pv_a = issue_values(kt, 0, 0)
        ms, ls, alphas_b = softmax_half(kt, 1, 1, ms, ls)
        accumulate(alphas_a, pv_a)
        accumulate(alphas_b, issue_values(kt, 1, 1))
        return tuple(ms), tuple(ls)

    m0 = tuple(jnp.full((1, tq), NEG_BIG, F32) for _ in range(N_HEADS))
    l0 = tuple(jnp.zeros((1, tq), F32) for _ in range(N_HEADS))
    _, ls = lax.fori_loop(0, n_at, att_step, (m0, l0))
    z_all = z_ref[...]
    for h in range(N_HEADS):
        sl = slice(h * HEAD_DIM, (h + 1) * HEAD_DIM)
        oh = (acc_ref[h] / ls[h]).T
        o_ref[:, sl] = (oh * _silu(z_all[:, sl])).astype(o_ref.dtype)


def _dsa(proj, small, k_all16, vt16, ki16, gq_row, batch, q_len, tq, offset, s_valid):
    n = batch * q_len
    nq = q_len // tq
    s_pad = k_all16.shape[1]
    topk = min(TOPK_MAX, s_valid // 4)
    gw = GROUP_WIDTH
    qi_w = IDX_HEADS * IDX_DIM
    kern = functools.partial(_dsa_kernel, tq=tq, offset=offset, s_valid=s_valid, topk=topk)
    once = pl.Buffered(1)
    return pl.pallas_call(
        kern,
        grid=(batch, nq),
        in_specs=[
            pl.BlockSpec((tq, gw), lambda b, i: (b * nq + i, COL_QC // gw)),
            pl.BlockSpec((tq, qi_w), lambda b, i: (b * nq + i, COL_QI // qi_w)),
            pl.BlockSpec((tq, gw), lambda b, i: (b * nq + i, COL_ZC // gw)),
            pl.BlockSpec((tq, SMALL_COLS), lambda b, i: (b * nq + i, 0)),
            pl.BlockSpec((1, s_pad, gw), lambda b, i: (b, 0, 0), pipeline_mode=once),
            pl.BlockSpec((1, s_pad // KT_A, gw, KT_A), lambda b, i: (b, 0, 0, 0), pipeline_mode=once),
            pl.BlockSpec((1, s_pad, IDX_DIM), lambda b, i: (b, 0, 0), pipeline_mode=once),
            pl.BlockSpec((1, HEAD_DIM), lambda b, i: (0, 0)),
        ],
        out_specs=pl.BlockSpec((tq, gw), lambda b, i: (b * nq + i, 0)),
        out_shape=jax.ShapeDtypeStruct((n, gw), BF16),
        scratch_shapes=[pltpu.VMEM((s_pad, tq), I32),
                        pltpu.VMEM((qi_w, tq), BF16),
                        pltpu.VMEM((IDX_HEADS, SUBLANES, tq), F32),
                        pltpu.VMEM((SUBLANES, tq), I32),
                        pltpu.VMEM((gw, tq), BF16),
                        pltpu.VMEM((N_HEADS, HEAD_DIM, tq), F32),
                        pltpu.VMEM((2, N_HEADS, KT_A // 2, tq), F32),
                        pltpu.VMEM((2, N_HEADS, KT_A // 2, tq), BF16),
                        pltpu.VMEM((s_pad, tq), jnp.int16),
                        pltpu.VMEM((s_pad, tq), jnp.int16)],
        compiler_params=_cparams(("parallel", "arbitrary")),
        name="dsa",
    )(proj, proj, proj, small, k_all16, vt16, ki16, gq_row)


def _dsa_select_kernel(qi_ref, sm_ref, pki_ref, nki_ref, bias_ref, ki_ref, key_ref, jl_ref, *,
                       lq, offset, s_valid, topk):
    s_pad = ki_ref.shape[0]
    n_t = s_pad // KT_A
    ki_ref[0:offset, :] = pki_ref[0].astype(BF16)
    ki_ref[offset:s_pad, :] = jnp.zeros((s_pad - offset, IDX_DIM), BF16)
    ki_ref[offset:offset + lq, :] = nki_ref[...]
    qi = qi_ref[...]
    q2 = jnp.concatenate([qi[:, h * IDX_DIM:(h + 1) * IDX_DIM] for h in range(IDX_HEADS)], axis=0).astype(BF16)
    w = sm_ref[...] * (IDX_HEADS ** -0.5 * IDX_DIM ** -0.5)
    w_cols = [w[:, LANE_WI + h:LANE_WI + h + 1] for h in range(IDX_HEADS)]
    t_chunk = (offset + lax.broadcasted_iota(I32, (lq, 1), 0)) // CHUNK

    def index_step(j, carry):
        c0 = pl.multiple_of(j * KT_A, KT_A)
        sc = _dot_nt(q2, ki_ref[pl.ds(c0, KT_A), :])
        acc = jnp.zeros((lq, KT_A), F32)
        for h in range(IDX_HEADS):
            acc = acc + w_cols[h] * jnp.maximum(sc[h * lq:(h + 1) * lq, :], 0.0)
        bits = lax.bitcast_convert_type(acc, I32)
        key = bits ^ ((bits >> 31) & 0x7FFFFFFF)
        s_pos = c0 + lax.broadcasted_iota(I32, (1, KT_A), 1)
        adm = (s_pos // CHUNK <= t_chunk) & (s_pos < s_valid)
        key_ref[:, pl.ds(c0, KT_A)] = jnp.where(adm, key, INT_MIN)
        return carry

    lax.fori_loop(0, n_t, index_step, 0)

    def count(pred_fn):
        def body(j, acc):
            c0 = pl.multiple_of(j * KT_A, KT_A)
            s_pos = c0 + lax.broadcasted_iota(I32, (1, KT_A), 1)
            m = jnp.where(pred_fn(key_ref[:, pl.ds(c0, KT_A)], s_pos), 1, 0).astype(I32)
            for c in range(KT_A // LANES):
                acc = acc + m[:, c * LANES:(c + 1) * LANES]
            return acc
        acc = lax.fori_loop(0, n_t, body, jnp.zeros((lq, LANES), I32))
        return jnp.sum(acc, axis=1, keepdims=True)

    def bit_step(b, v):
        cand_u = v | jnp.left_shift(jnp.int32(1), 31 - b)
        cnt = count(lambda key, s_pos: key >= (cand_u ^ INT_MIN))
        return jnp.where(cnt >= topk, cand_u, v)

    v_s = lax.fori_loop(0, 32, bit_step, jnp.zeros((lq, 1), I32)) ^ INT_MIN
    n_gt = count(lambda key, s_pos: key > v_s)
    n_ge = count(lambda key, s_pos: key >= v_s)
    need = topk - n_gt
    tied = (n_ge != topk) & (v_s != INT_MIN)
    jl_ref[...] = jnp.full(jl_ref.shape, s_pad, I32)

    @pl.when(jnp.max(jnp.where(tied, 1.0, 0.0)) > 0.0)
    def _():
        pos_bits = max(int(math.ceil(math.log2(max(s_pad, 2)))), 1) + 1

        def pos_step(b, jv):
            cand = jv | jnp.left_shift(jnp.int32(1), pos_bits - 1 - b)
            cnt = count(lambda key, s_pos: (key == v_s) & (s_pos < cand))
            return jnp.where(cnt <= need, cand, jv)

        jl_ref[:, 0:1] = lax.fori_loop(0, pos_bits, pos_step, jnp.zeros((lq, 1), I32))

    j_lim = jnp.where(tied, jl_ref[:, 0:1], s_pad)

    def bias_step(j, carry):
        c0 = pl.multiple_of(j * KT_A, KT_A)
        key = key_ref[:, pl.ds(c0, KT_A)]
        s_pos = c0 + lax.broadcasted_iota(I32, (1, KT_A), 1)
        sel = (key > v_s) | ((key == v_s) & (s_pos < j_lim) & (key != INT_MIN))
        bias_ref[0, :, pl.ds(c0, KT_A)] = jnp.where(sel, 0.0, NEG_BIG).astype(F32)
        return carry

    lax.fori_loop(0, n_t, bias_step, 0)


def _dsa_select(proj, small, past_ki, ki16, batch, lq, offset, layer):
    s_valid = offset + lq
    s_pad = _round_up(s_valid, KT_A)
    topk = min(TOPK_MAX, s_valid // 4)
    qi_w = IDX_HEADS * IDX_DIM
    kern = functools.partial(_dsa_select_kernel, lq=lq, offset=offset, s_valid=s_valid, topk=topk)
    return pl.pallas_call(
        kern,
        grid=(batch,),
        in_specs=[
            pl.BlockSpec((lq, qi_w), lambda b: (b, COL_QI // qi_w)),
            pl.BlockSpec((lq, SMALL_COLS), lambda b: (b, 0)),
            pl.BlockSpec((None, 1, offset, IDX_DIM), lambda b: (layer, b, 0, 0)),
            pl.BlockSpec((lq, IDX_DIM), lambda b: (b, 0)),
        ],
        out_specs=pl.BlockSpec((1, lq, s_pad), lambda b: (b, 0, 0)),
        out_shape=jax.ShapeDtypeStruct((batch, lq, s_pad), F32),
        scratch_shapes=[pltpu.VMEM((s_pad, IDX_DIM), BF16), pltpu.VMEM((lq, s_pad), I32),
                        pltpu.VMEM((lq, LANES), I32)],
        compiler_params=_cparams(("parallel",)),
        name="dsa_select",
    )(proj, small, past_ki, ki16)


def _dsa_decode_kernel(q_ref, z_ref, bias_ref, pk_ref, pv_ref, nk_ref, nv_ref, gq_ref, o_ref,
                       qn_ref, m_ref, l_ref, acc_ref, *, lq, n_cache_tiles):
    j = pl.program_id(1)

    @pl.when(j == 0)
    def _():
        q_all = q_ref[...]
        gq = gq_ref[...]
        for h in range(N_HEADS):
            qh = q_all[:, h * HEAD_DIM:(h + 1) * HEAD_DIM]
            qh = qh * lax.rsqrt(jnp.mean(qh * qh, axis=-1, keepdims=True) + EPS) * gq * (HEAD_DIM ** -0.5 * LOG2_E)
            qn_ref[h] = qh.astype(BF16)
        m_ref[...] = jnp.full(m_ref.shape, NEG_BIG, F32)
        l_ref[...] = jnp.zeros(l_ref.shape, F32)
        acc_ref[...] = jnp.zeros(acc_ref.shape, F32)

    def attend(k_heads, v_heads, bias):
        for h in range(N_HEADS):
            logit = _dot_nt(qn_ref[h], k_heads[h]) + bias
            m_old = m_ref[h]
            m_new = jnp.maximum(m_old, jnp.max(logit, axis=1, keepdims=True))
            alpha = jnp.exp2(m_old - m_new)
            p = jnp.exp2(logit - m_new[:, 0:1])
            l_ref[h] = alpha * l_ref[h] + jnp.sum(p, axis=1, keepdims=True)
            acc_ref[h] = alpha * acc_ref[h] + _dot(p.astype(BF16), v_heads[h])
            m_ref[h] = m_new

    @pl.when(j < n_cache_tiles)
    def _():
        attend([pk_ref[0, pl.ds(h, KT_A, stride=N_HEADS), :].astype(BF16) for h in range(N_HEADS)],
               [pv_ref[0, pl.ds(h, KT_A, stride=N_HEADS), :].astype(BF16) for h in range(N_HEADS)], bias_ref[0])

    @pl.when(j == n_cache_tiles)
    def _():
        pad = jnp.zeros((LANES - lq, HEAD_DIM), BF16)
        nk = nk_ref[...]
        attend([jnp.concatenate([nk[:, h * HEAD_DIM:(h + 1) * HEAD_DIM], pad], axis=0) for h in range(N_HEADS)],
               [jnp.concatenate([nv_ref[pl.ds(h, lq, stride=N_HEADS), :].astype(BF16), pad], axis=0)
                for h in range(N_HEADS)],
               bias_ref[0][:, 0:LANES])
        z_all = z_ref[...]
        for h in range(N_HEADS):
            sl = slice(h * HEAD_DIM, (h + 1) * HEAD_DIM)
            o_ref[:, sl] = (acc_ref[h] / l_ref[h] * _silu(z_all[:, sl])).astype(o_ref.dtype)


def _dsa_decode(proj, bias, past_k, past_v, k16, v_new, gq_row, batch, lq, offset, layer):
    gw = GROUP_WIDTH
    n_cache_tiles = offset // KT_A
    kern = functools.partial(_dsa_decode_kernel, lq=lq, n_cache_tiles=n_cache_tiles)
    last = n_cache_tiles - 1

    def rows(c):
        return c.reshape(c.shape[0], c.shape[1], offset * N_HEADS, HEAD_DIM)
    return pl.pallas_call(
        kern,
        grid=(batch, n_cache_tiles + 1),
        in_specs=[
            pl.BlockSpec((lq, gw), lambda b, j: (b, COL_QC // gw)),
            pl.BlockSpec((lq, gw), lambda b, j: (b, COL_ZC // gw)),
            pl.BlockSpec((1, lq, KT_A), lambda b, j: (b, 0, j)),
            pl.BlockSpec((None, 1, KT_A * N_HEADS, HEAD_DIM), lambda b, j: (layer, b, jnp.minimum(j, last), 0)),
            pl.BlockSpec((None, 1, KT_A * N_HEADS, HEAD_DIM), lambda b, j: (layer, b, jnp.minimum(j, last), 0)),
            pl.BlockSpec((lq, gw), lambda b, j: (b, 0)),
            pl.BlockSpec((None, lq * N_HEADS, HEAD_DIM), lambda b, j: (layer, b, 0)),
            pl.BlockSpec((1, HEAD_DIM), lambda b, j: (0, 0)),
        ],
        out_specs=pl.BlockSpec((lq, gw), lambda b, j: (b, 0)),
        out_shape=jax.ShapeDtypeStruct((batch * lq, gw), BF16),
        scratch_shapes=[pltpu.VMEM((N_HEADS, lq, HEAD_DIM), BF16),
                        pltpu.VMEM((N_HEADS, lq, HEAD_DIM), F32),
                        pltpu.VMEM((N_HEADS, lq, HEAD_DIM), F32),
                        pltpu.VMEM((N_HEADS, lq, HEAD_DIM), F32)],
        compiler_params=_cparams(("parallel", "arbitrary")),
        name="dsa_decode",
    )(proj, proj, bias, rows(past_k), rows(past_v), k16, v_new, gq_row)


def _mem_kernel(q_ref, z_ref, mk_ref, mv_ref, gq_ref, o_ref):
    gq = gq_ref[...]
    q_all, z_all = q_ref[...], z_ref[...]
    mk = mk_ref[0].astype(BF16)
    mv = mv_ref[0].astype(BF16)
    for h in range(N_HEADS):
        sl = slice(h * HEAD_DIM, (h + 1) * HEAD_DIM)
        qh = q_all[:, sl]
        qh = qh * lax.rsqrt(jnp.mean(qh * qh, axis=-1, keepdims=True) + EPS) * gq * (HEAD_DIM ** -0.5)
        logit = _dot_nt(qh.astype(BF16), mk[:, sl])
        m = jnp.max(logit, axis=-1, keepdims=True)
        p = jnp.exp(logit - m)
        l = jnp.sum(p, axis=-1, keepdims=True)
        oh = _dot(p.astype(BF16), mv[:, sl]) / l
        o_ref[:, sl] = (oh * _silu(z_all[:, sl])).astype(o_ref.dtype)


def _mem_attend(proj, mk, mv, gq_row, batch, seqlen):
    n = batch * seqlen
    tm = min(512, seqlen)
    nj = seqlen // tm
    gw = GROUP_WIDTH
    n_mem = mk.shape[1]
    return pl.pallas_call(
        _mem_kernel,
        grid=(batch, nj),
        in_specs=[
            pl.BlockSpec((tm, gw), lambda b, j: (b * nj + j, COL_QD // gw)),
            pl.BlockSpec((tm, gw), lambda b, j: (b * nj + j, COL_ZD // gw)),
            pl.BlockSpec((1, n_mem, gw), lambda b, j: (b, 0, 0)),
            pl.BlockSpec((1, n_mem, gw), lambda b, j: (b, 0, 0)),
            pl.BlockSpec((1, HEAD_DIM), lambda b, j: (0, 0)),
        ],
        out_specs=pl.BlockSpec((tm, gw), lambda b, j: (b * nj + j, 0)),
        out_shape=jax.ShapeDtypeStruct((n, gw), BF16),
        compiler_params=_cparams(("parallel", "arbitrary")),
        name="mem_attend",
    )(proj, proj, mk, mv, gq_row)


def _outproj_kernel(x_ref, a_ref, b_ref, c_ref, d_ref, w_ref, y_ref):
    gw = GROUP_WIDTH
    acc = x_ref[...] + _dot(a_ref[...], w_ref[0:gw, :])
    acc = acc + _dot(b_ref[...], w_ref[gw:2 * gw, :])
    acc = acc + _dot(c_ref[...], w_ref[2 * gw:3 * gw, :])
    acc = acc + _dot(d_ref[...], w_ref[3 * gw:4 * gw, :])
    y_ref[...] = acc


def _out_proj(x2d, oa, ob, oc, od, w16):
    n, d = x2d.shape
    tm = min(512, n)
    gw = GROUP_WIDTH
    grp = pl.BlockSpec((tm, gw), lambda i: (i, 0))
    return pl.pallas_call(
        _outproj_kernel,
        grid=(n // tm,),
        in_specs=[pl.BlockSpec((tm, d), lambda i: (i, 0)), grp, grp, grp, grp,
                  pl.BlockSpec((4 * gw, d), lambda i: (0, 0))],
        out_specs=pl.BlockSpec((tm, d), lambda i: (i, 0)),
        out_shape=jax.ShapeDtypeStruct((n, d), F32),
        compiler_params=_cparams(("parallel",)),
        name="out_proj",
    )(x2d, oa, ob, oc, od, w16)


_W_A_END = 4 * GROUP_WIDTH
_W_B_START = _W_A_END + 2 * N_HEADS
_W_B_END = _W_B_START + 8 * GROUP_WIDTH + IDX_HEADS * IDX_DIM
_W_D_START = _W_B_END + IDX_DIM + IDX_HEADS
_W_COLS = _W_D_START + 2 * GROUP_WIDTH


def _w_in_kernel(w_ref, m_ref, s_ref):
    m_ref[:, 0:_W_A_END] = w_ref[:, 0:_W_A_END].astype(BF16)
    m_ref[:, _W_A_END:_W_A_END + _W_B_END - _W_B_START] = w_ref[:, _W_B_START:_W_B_END].astype(BF16)
    m_ref[:, MAIN_COLS - 2 * GROUP_WIDTH:MAIN_COLS] = w_ref[:, _W_D_START:_W_COLS].astype(BF16)
    n_kw = IDX_DIM + IDX_HEADS
    s_ref[:, 0:n_kw] = w_ref[:, _W_B_END:_W_D_START].astype(BF16)
    s_ref[:, n_kw:n_kw + 2 * N_HEADS] = w_ref[:, _W_A_END:_W_B_START].astype(BF16)
    s_ref[:, n_kw + 2 * N_HEADS:] = jnp.zeros((w_ref.shape[0], SMALL_COLS - n_kw - 2 * N_HEADS), BF16)


def _prep_w_in(w_in):
    depth, d, cols = w_in.shape
    assert cols == _W_COLS and MAIN_COLS == _W_A_END + (_W_B_END - _W_B_START) + 2 * GROUP_WIDTH
    tm = 128
    return pl.pallas_call(
        _w_in_kernel,
        grid=(depth, d // tm),
        in_specs=[pl.BlockSpec((None, tm, cols), lambda l, i: (l, i, 0))],
        out_specs=[pl.BlockSpec((None, tm, MAIN_COLS), lambda l, i: (l, i, 0)),
                   pl.BlockSpec((None, tm, SMALL_COLS), lambda l, i: (l, i, 0))],
        out_shape=[jax.ShapeDtypeStruct((depth, d, MAIN_COLS), BF16),
                   jax.ShapeDtypeStruct((depth, d, SMALL_COLS), BF16)],
        compiler_params=_cparams(("parallel", "parallel")),
        name="w_in_prep",
    )(w_in)


def _lane_row(vals, lane0):
    row = jnp.zeros((1, SMALL_COLS), F32)
    return row.at[0, lane0:lane0 + vals.shape[0]].set(vals.astype(F32))


def _rope_tables(pos):
    half = HEAD_DIM // 2
    inv = ROPE_THETA ** (-jnp.arange(half, dtype=F32) / half)
    ang = pos.astype(F32)[:, None] * inv[None, :]
    cos, sin = jnp.cos(ang), jnp.sin(ang)
    return jnp.concatenate([cos, cos], axis=-1), jnp.concatenate([-sin, sin], axis=-1)


def _round_up(x, m):
    return (x + m - 1) // m * m


def _mixer_layer(x, conv_buf, s_gdn, s_ret, past_k, past_v, past_ki, mem_k, mem_v, wts, layer, depth, prev_cache):
    b, l, d = x.shape
    n = b * l
    offset = 0 if past_k is None else past_k.shape[2]
    x2d = x.reshape(n, d)
    proj, small = _in_proj(x2d, wts["norm_g"], wts["w_main"], wts["w_small"], layer)

    o_a, conv_new, s_gdn_new = _gdn(proj, small, wts["conv_w"], conv_buf, s_gdn, wts["alog_row"],
                                    wts["dtb_row"], wts["gdn_norm_g"], b, l)

    cos_t, sin_t = wts["rope_p"] if past_k is None else wts["rope_s"]
    o_b, s_ret_new = _retention(proj, cos_t, sin_t, s_ret, wts["ret_norm_g"], wts["ret_norm_b"], b, l)

    prefill = past_k is None
    prep = _dsa_prep(proj, small, wts["dsa_k_norm_g"], wts["idx_k_norm_g"], layer, depth, prev_cache, prefill)
    cache = tuple(prep[:3])
    if prefill:
        assert l % KT_A == 0, "prefill length must be a multiple of the key tile"
        o_c = _dsa(proj, small, prep[3].reshape(b, l, GROUP_WIDTH), prep[5].reshape(b, l // KT_A, GROUP_WIDTH, KT_A),
                   prep[4].reshape(b, l, IDX_DIM), wts["dsa_q_norm_g"], b, l, 256, 0, l)
    else:
        assert offset % KT_A == 0 and l % 16 == 0 and l <= 128, "decode step shape not supported"
        bias = _dsa_select(proj, small, past_ki, prep[4], b, l, offset, layer)
        o_c = _dsa_decode(proj, bias, past_k, past_v, prep[3], cache[1], wts["dsa_q_norm_g"], b, l, offset, layer)

    o_d = _mem_attend(proj, mem_k, mem_v, wts["mem_q_norm_g"], b, l)

    y = _out_proj(x2d, o_a, o_b, o_c, o_d, wts["w_out"]).reshape(b, l, d)
    return y, (conv_new, s_gdn_new, s_ret_new), cache


def kernel(x_prompt, x_sample, cache_gdn_conv, state_gdn, state_ret, cache_dsa_k, cache_dsa_v, cache_idx_k, cache_mem_k, cache_mem_v, mem_prompt, norm_g, w_in, gdn_conv_w, gdn_a_log, gdn_dt_bias, gdn_norm_g, ret_norm_g, ret_norm_b, dsa_q_norm_g, dsa_k_norm_g, idx_k_norm_g, mem_norm_g, w_mem_kv, mem_q_norm_g, mem_k_norm_g, w_out):
    depth = w_in.shape[0]
    b = x_prompt.shape[0]
    n_mem = mem_prompt.shape[1]
    d = x_prompt.shape[-1]
    y_p, y_s = x_prompt, x_sample
    st_p, st_s, mem_p = [], [], []
    cache_p = cache_s = None
    w_main, w_small = _prep_w_in(w_in)
    rope_p = _rope_tables(jnp.arange(x_prompt.shape[1], dtype=I32))
    rope_s = _rope_tables(cache_dsa_k.shape[2] + jnp.arange(x_sample.shape[1], dtype=I32))
    for li in range(depth):
        wts = dict(
            norm_g=norm_g[li][None, :], w_main=w_main, w_small=w_small, rope_p=rope_p, rope_s=rope_s,
            conv_w=gdn_conv_w[li],
            alog_row=_lane_row(gdn_a_log[li], LANE_ALPHA), dtb_row=_lane_row(gdn_dt_bias[li], LANE_ALPHA),
            gdn_norm_g=gdn_norm_g[li][None, :], ret_norm_g=ret_norm_g[li][None, :],
            ret_norm_b=ret_norm_b[li][None, :], dsa_q_norm_g=dsa_q_norm_g[li][None, :],
            dsa_k_norm_g=dsa_k_norm_g[li][None, :], idx_k_norm_g=idx_k_norm_g[li][None, :],
            mem_q_norm_g=mem_q_norm_g[li][None, :], w_out=w_out[li].astype(BF16),
        )
        mk, mv = _memory_kv(mem_prompt.reshape(b * n_mem, d), mem_norm_g[li][None, :],
                            w_mem_kv[li].astype(BF16), mem_k_norm_g[li][None, :])
        mk = mk.reshape(b, n_mem, GROUP_WIDTH)
        mv = mv.reshape(b, n_mem, GROUP_WIDTH)
        conv0 = jnp.zeros((b, CONV_W - 1, 3 * GROUP_WIDTH), F32)
        s0 = jnp.zeros((b, N_HEADS, HEAD_DIM, HEAD_DIM), F32)
        y_p, sp, cache_p = _mixer_layer(y_p, conv0, s0, s0, None, None, None, mk, mv, wts, li, depth, cache_p)
        st_p.append(sp)
        mem_p.append((mk.reshape(b, n_mem, N_HEADS, HEAD_DIM), mv.reshape(b, n_mem, N_HEADS, HEAD_DIM)))
        bs = x_sample.shape[0]
        y_s, ss, cache_s = _mixer_layer(y_s, cache_gdn_conv[li], state_gdn[li], state_ret[li],
                                        cache_dsa_k, cache_dsa_v, cache_idx_k,
                                        cache_mem_k[li].reshape(bs, n_mem, GROUP_WIDTH),
                                        cache_mem_v[li].reshape(bs, n_mem, GROUP_WIDTH), wts, li, depth, cache_s)
        st_s.append(ss)

    def stack(lst, k):
        return jnp.stack([s[k] for s in lst])

    def caches(c, bb, ll):
        return (c[0].reshape(depth, bb, ll, N_HEADS, HEAD_DIM), c[1].reshape(depth, bb, ll, N_HEADS, HEAD_DIM),
                c[2].reshape(depth, bb, ll, IDX_DIM))

    return ((y_p, y_s, stack(st_p, 0), stack(st_p, 1), stack(st_p, 2))
            + caches(cache_p, b, x_prompt.shape[1])
            + (stack(mem_p, 0), stack(mem_p, 1), stack(st_s, 0), stack(st_s, 1), stack(st_s, 2))
            + caches(cache_s, x_sample.shape[0], x_sample.shape[1]))
```
